```python
import math
import jax, jax.numpy as jnp
from jax import lax
import numpy as np

D_MODEL = 1024
BATCH = 16
SEQ = 2048
DEPTH = 2

PLE_DIM = 256
ATTN_HEADS = 16
HEAD_DIM = 64
ATTN_WIDTH = ATTN_HEADS * HEAD_DIM
POOL_WINDOWS = (2, 4, 8, 16)
N_POOL_GROUPS = len(POOL_WINDOWS)
POOL_GROUP_DIM = 256
POOL_WIDTH = N_POOL_GROUPS * POOL_GROUP_DIM
MIX_WIDTH = ATTN_WIDTH + POOL_WIDTH
Q_BLOCK = 128
EPS = 1e-6
SPLIT_SIZES = (ATTN_WIDTH, ATTN_WIDTH, ATTN_WIDTH, ATTN_WIDTH, ATTN_HEADS, POOL_WIDTH, POOL_WIDTH)
IN_COLS = sum(SPLIT_SIZES)
SPLIT_POINTS = tuple(int(v) for v in np.cumsum(SPLIT_SIZES)[:-1])

kernel_name = "hymba_fox_pool_hybrid"


def rms_norm(x, g):
    xf = x.astype(jnp.float32)
    y = xf * lax.rsqrt(jnp.mean(xf * xf, axis=-1, keepdims=True) + EPS)
    return (y * g.astype(jnp.float32)).astype(x.dtype)


def multi_scale_pool(u, w_pool, pool_scale):
    B, S, _ = u.shape
    ug = u.reshape(B, S, N_POOL_GROUPS, POOL_GROUP_DIM).astype(jnp.float32)
    cs = jnp.cumsum(ug, axis=1)
    pos = jnp.arange(1, S + 1, dtype=jnp.float32)
    outs = []
    for g, w in enumerate(POOL_WINDOWS):
        c = cs[:, :, g]
        shifted = jnp.pad(c, ((0, 0), (w, 0), (0, 0)))[:, :S]
        count = jnp.minimum(pos, float(w))[None, :, None]
        outs.append((c - shifted) / count - ug[:, :, g])
    pooled = jnp.stack(outs, axis=2).astype(u.dtype)
    mixed = jnp.einsum('bsgc,gcd->bsgd', pooled, w_pool)
    return mixed.reshape(B, S, POOL_WIDTH) * pool_scale


def forgetting_attention(q, k, v, log_f):
    B, S, H, Dh = q.shape
    scale = 1.0 / math.sqrt(Dh)
    c = jnp.cumsum(log_f, axis=1).transpose(0, 2, 1)
    qh = q.transpose(0, 2, 1, 3)
    kh = k.transpose(0, 2, 1, 3)
    vh = v.transpose(0, 2, 1, 3)
    tri = jnp.tril(jnp.ones((Q_BLOCK, Q_BLOCK), dtype=bool))
    outs = []
    for i in range(S // Q_BLOCK):
        q0 = i * Q_BLOCK
        end = q0 + Q_BLOCK
        qb = qh[:, :, q0:end]
        kb = kh[:, :, :end]
        vb = vh[:, :, :end]
        s = jnp.einsum('bhqd,bhkd->bhqk', qb, kb,
                       preferred_element_type=jnp.float32) * scale
        s = s + c[:, :, q0:end, None] - c[:, :, None, :end]
        mask = jnp.concatenate([jnp.ones((Q_BLOCK, q0), dtype=bool), tri], axis=1)
        s = jnp.where(mask[None, None], s, -jnp.inf)
        pr = jax.nn.softmax(s, axis=-1)
        outs.append(jnp.einsum('bhqk,bhkd->bhqd', pr.astype(v.dtype), vb))
    o = jnp.concatenate(outs, axis=2)
    return o.transpose(0, 2, 1, 3)


def _fwd_setup_inputs(seed: int = 0) -> dict:
    key = jax.random.key(seed)
    ks = jax.random.split(key, 12)
    f32 = jnp.float32
    x = jax.random.normal(ks[0], (BATCH, SEQ, D_MODEL), f32)
    p = jax.random.normal(ks[1], (DEPTH, BATCH, SEQ, PLE_DIM), f32)
    norm_pre = 1.0 + 0.05 * jax.random.normal(ks[2], (DEPTH, D_MODEL), f32)
    norm_post = 1.0 + 0.05 * jax.random.normal(ks[3], (DEPTH, D_MODEL), f32)
    w_in = jax.random.normal(ks[4], (DEPTH, D_MODEL, IN_COLS), f32) * D_MODEL ** -0.5
    b_f = 3.0 + 0.5 * jax.random.normal(ks[5], (DEPTH, ATTN_HEADS), f32)
    w_pool = jax.random.normal(ks[6], (DEPTH, N_POOL_GROUPS, POOL_GROUP_DIM, POOL_GROUP_DIM), f32) * POOL_GROUP_DIM ** -0.5
    pool_scale = 1.0 + 0.1 * jax.random.normal(ks[7], (DEPTH, POOL_WIDTH), f32)
    w_out = jax.random.normal(ks[8], (DEPTH, MIX_WIDTH, D_MODEL), f32) * MIX_WIDTH ** -0.5
    w_pg = jax.random.normal(ks[9], (DEPTH, D_MODEL, D_MODEL), f32) * D_MODEL ** -0.5
    w_pe = jax.random.normal(ks[10], (DEPTH, PLE_DIM, D_MODEL), f32) * (0.5 * PLE_DIM ** -0.5)
    return {"x": x, "p": p, "norm_pre": norm_pre, "norm_post": norm_post,
            "w_in": w_in, "b_f": b_f, "w_pool": w_pool, "pool_scale": pool_scale,
            "w_out": w_out, "w_pg": w_pg, "w_pe": w_pe}


def _fwd_reference(x, p, norm_pre, norm_post, w_in, b_f, w_pool, pool_scale, w_out, w_pg, w_pe):
    B, S, _ = x.shape
    h = x
    for i in range(DEPTH):
        hn = rms_norm(h, norm_pre[i])
        proj = hn @ w_in[i]
        q, k, v, z_attn, f_logit, u_pool, z_pool = jnp.split(proj, SPLIT_POINTS, axis=-1)
        log_f = jax.nn.log_sigmoid(f_logit.astype(jnp.float32) + b_f[i].astype(jnp.float32))
        attn = forgetting_attention(
            q.reshape(B, S, ATTN_HEADS, HEAD_DIM),
            k.reshape(B, S, ATTN_HEADS, HEAD_DIM),
            v.reshape(B, S, ATTN_HEADS, HEAD_DIM), log_f)
        attn = attn.reshape(B, S, ATTN_WIDTH) * jax.nn.silu(z_attn)
        pool = multi_scale_pool(u_pool, w_pool[i], pool_scale[i]) * jax.nn.silu(z_pool)
        mix = jnp.concatenate([attn, pool], axis=-1) @ w_out[i]
        h = h + rms_norm(mix, norm_post[i])
        gate = jax.nn.sigmoid(h @ w_pg[i])
        h = h + gate * (p[i] @ w_pe[i])
    return h


import jax as _jax
import jax.numpy as _jnp

TWIN_FORMAT = 'train_step'
FWD_PARAMS = ['x', 'p', 'norm_pre', 'norm_post', 'w_in', 'b_f', 'w_pool', 'pool_scale', 'w_out', 'w_pg', 'w_pe']
TWIN_WEIGHTS = ['norm_pre', 'norm_post', 'w_in', 'b_f', 'w_pool', 'pool_scale', 'w_out', 'w_pg', 'w_pe']
TWIN_DIFF_INPUT = 'x'
TWIN_INPUTS = ['x', 'p', 'norm_pre', 'norm_post', 'w_in', 'b_f', 'w_pool', 'pool_scale', 'w_out', 'w_pg', 'w_pe', 'loss_target', 'm_norm_pre', 'm_norm_post', 'm_w_in', 'm_b_f', 'm_w_pool', 'm_pool_scale', 'm_w_out', 'm_w_pg', 'm_w_pe', 'v_norm_pre', 'v_norm_post', 'v_w_in', 'v_b_f', 'v_w_pool', 'v_pool_scale', 'v_w_out', 'v_w_pg', 'v_w_pe']
TWIN_OUTPUTS = ['loss', 'grad_x', 'grad_norm_pre', 'grad_norm_post', 'grad_w_in', 'grad_b_f', 'grad_w_pool', 'grad_pool_scale', 'grad_w_out', 'grad_w_pg', 'grad_w_pe', 'delta_norm_pre', 'delta_norm_post', 'delta_w_in', 'delta_b_f', 'delta_w_pool', 'delta_pool_scale', 'delta_w_out', 'delta_w_pg', 'delta_w_pe', 'new_m_norm_pre', 'new_m_norm_post', 'new_m_w_in', 'new_m_b_f', 'new_m_w_pool', 'new_m_pool_scale', 'new_m_w_out', 'new_m_w_pg', 'new_m_w_pe', 'new_v_norm_pre', 'new_v_norm_post', 'new_v_w_in', 'new_v_b_f', 'new_v_w_pool', 'new_v_pool_scale', 'new_v_w_out', 'new_v_w_pg', 'new_v_w_pe']
TWIN_LEAF_KINDS = {'loss': 'loss', 'grad_x': 'grad_x', 'grad_norm_pre': 'grad_w', 'grad_norm_post': 'grad_w', 'grad_w_in': 'grad_w', 'grad_b_f': 'grad_w', 'grad_w_pool': 'grad_w', 'grad_pool_scale': 'grad_w', 'grad_w_out': 'grad_w', 'grad_w_pg': 'grad_w', 'grad_w_pe': 'grad_w', 'delta_norm_pre': 'delta_w', 'delta_norm_post': 'delta_w', 'delta_w_in': 'delta_w', 'delta_b_f': 'delta_w', 'delta_w_pool': 'delta_w', 'delta_pool_scale': 'delta_w', 'delta_w_out': 'delta_w', 'delta_w_pg': 'delta_w', 'delta_w_pe': 'delta_w', 'new_m_norm_pre': 'new_m', 'new_m_norm_post': 'new_m', 'new_m_w_in': 'new_m', 'new_m_b_f': 'new_m', 'new_m_w_pool': 'new_m', 'new_m_pool_scale': 'new_m', 'new_m_w_out': 'new_m', 'new_m_w_pg': 'new_m', 'new_m_w_pe': 'new_m', 'new_v_norm_pre': 'new_v', 'new_v_norm_post': 'new_v', 'new_v_w_in': 'new_v', 'new_v_b_f': 'new_v', 'new_v_w_pool': 'new_v', 'new_v_pool_scale': 'new_v', 'new_v_w_out': 'new_v', 'new_v_w_pg': 'new_v', 'new_v_w_pe': 'new_v'}


def _forward(args):
    return _fwd_reference(*[args[k] for k in FWD_PARAMS])


def _output_shape():
    out = _jax.eval_shape(lambda: _forward(_fwd_setup_inputs(0)))
    return out.shape, out.dtype

N_MICROBATCH = 1
ADAM_LR = 0.001
ADAM_B1 = 0.9
ADAM_B2 = 0.999
ADAM_EPS = 1e-08
ADAM_WD = 0.01
ADAM_STEP = 10
PER_EXAMPLE_BATCH_AXIS = {'x': 0, 'p': 1, 'loss_target': 0}
SHARED_INPUTS = []
_WEIGHT_DTYPES = {'norm_pre': _jnp.float32, 'norm_post': _jnp.float32, 'w_in': _jnp.float32, 'b_f': _jnp.float32, 'w_pool': _jnp.float32, 'pool_scale': _jnp.float32, 'w_out': _jnp.float32, 'w_pg': _jnp.float32, 'w_pe': _jnp.float32}
MOMENT_SCALE = {'norm_pre': 6.414080e-01, 'norm_post': 3.204787e+01, 'w_in': 2.533954e-01, 'b_f': 6.012361e-01, 'w_pool': 4.363067e-01, 'pool_scale': 4.697627e-01, 'w_out': 4.678998e-01, 'w_pg': 6.343538e-02, 'w_pe': 3.054865e-01}


def _to_microbatches(a, axis):
    t = _jnp.moveaxis(a, axis, 0)
    t = t.reshape((N_MICROBATCH, t.shape[0] // N_MICROBATCH) + t.shape[1:])
    return _jnp.moveaxis(t, 1, axis + 1)


def setup_inputs(seed: int = 0) -> dict:
    inp = _fwd_setup_inputs(seed)
    key = _jax.random.fold_in(_jax.random.key(seed), 7919)
    shape, _ = _output_shape()
    out = dict(inp)
    out["loss_target"] = _jax.random.normal(_jax.random.fold_in(key, 0), shape, _jnp.float32)
    for i, name in enumerate(TWIN_WEIGHTS):
        w = inp[name].astype(_jnp.float32)
        if MOMENT_SCALE is None:
            s = _jnp.sqrt(_jnp.mean(_jnp.square(w)) + 1e-30)
        else:
            s = MOMENT_SCALE[name]
        km, kv = _jax.random.split(_jax.random.fold_in(key, i + 1))
        out[name] = w
        out["m_" + name] = s * _jax.random.normal(km, w.shape, _jnp.float32)
        out["v_" + name] = (s * s) * _jax.random.uniform(kv, w.shape, _jnp.float32, 0.5, 1.5)
    if N_MICROBATCH > 1:
        for name, axis in PER_EXAMPLE_BATCH_AXIS.items():
            out[name] = _to_microbatches(out[name], axis)
    return {'x': out['x'], 'p': out['p'], 'norm_pre': out['norm_pre'], 'norm_post': out['norm_post'], 'w_in': out['w_in'], 'b_f': out['b_f'], 'w_pool': out['w_pool'], 'pool_scale': out['pool_scale'], 'w_out': out['w_out'], 'w_pg': out['w_pg'], 'w_pe': out['w_pe'], 'loss_target': out['loss_target'], 'm_norm_pre': out['m_norm_pre'], 'm_norm_post': out['m_norm_post'], 'm_w_in': out['m_w_in'], 'm_b_f': out['m_b_f'], 'm_w_pool': out['m_w_pool'], 'm_pool_scale': out['m_pool_scale'], 'm_w_out': out['m_w_out'], 'm_w_pg': out['m_w_pg'], 'm_w_pe': out['m_w_pe'], 'v_norm_pre': out['v_norm_pre'], 'v_norm_post': out['v_norm_post'], 'v_w_in': out['v_w_in'], 'v_b_f': out['v_b_f'], 'v_w_pool': out['v_w_pool'], 'v_pool_scale': out['v_pool_scale'], 'v_w_out': out['v_w_out'], 'v_w_pg': out['v_w_pg'], 'v_w_pe': out['v_w_pe']}


def _loss(weights, diff, rest, loss_target):
    with _jax.named_scope("forward"):
        args = {**rest, TWIN_DIFF_INPUT: diff, **{k: w.astype(_WEIGHT_DTYPES[k]) for k, w in weights.items()}}
        y = _forward(args)
    with _jax.named_scope("loss_head"):
        err = _jnp.square(y.astype(_jnp.float32) - loss_target)
        return 0.5 * _jnp.sum(_jnp.mean(err, axis=-1)) if err.ndim else 0.5 * err


def _adamw(w, g, m, v):
    m = ADAM_B1 * m + (1.0 - ADAM_B1) * g
    v = ADAM_B2 * v + (1.0 - ADAM_B2) * _jnp.square(g)
    m_hat = m / (1.0 - ADAM_B1 ** ADAM_STEP)
    v_hat = v / (1.0 - ADAM_B2 ** ADAM_STEP)
    delta = -ADAM_LR * (m_hat / (_jnp.sqrt(v_hat) + ADAM_EPS) + ADAM_WD * w)
    return delta, m, v


def reference(x, p, norm_pre, norm_post, w_in, b_f, w_pool, pool_scale, w_out, w_pg, w_pe, loss_target, m_norm_pre, m_norm_post, m_w_in, m_b_f, m_w_pool, m_pool_scale, m_w_out, m_w_pg, m_w_pe, v_norm_pre, v_norm_post, v_w_in, v_b_f, v_w_pool, v_pool_scale, v_w_out, v_w_pg, v_w_pe):
    given = dict(x=x, p=p, norm_pre=norm_pre, norm_post=norm_post, w_in=w_in, b_f=b_f, w_pool=w_pool, pool_scale=pool_scale, w_out=w_out, w_pg=w_pg, w_pe=w_pe, loss_target=loss_target, m_norm_pre=m_norm_pre, m_norm_post=m_norm_post, m_w_in=m_w_in, m_b_f=m_b_f, m_w_pool=m_w_pool, m_pool_scale=m_pool_scale, m_w_out=m_w_out, m_w_pg=m_w_pg, m_w_pe=m_w_pe, v_norm_pre=v_norm_pre, v_norm_post=v_norm_post, v_w_in=v_w_in, v_b_f=v_b_f, v_w_pool=v_w_pool, v_pool_scale=v_pool_scale, v_w_out=v_w_out, v_w_pg=v_w_pg, v_w_pe=v_w_pe)
    weights = {n: given[n] for n in TWIN_WEIGHTS}
    shared = {n: given[n] for n in SHARED_INPUTS}
    per_example = {n: given[n] for n in ['x', 'p']}
    grad_fn = _jax.value_and_grad(_loss, argnums=(0, 1))

    def one_microbatch(ex, loss_target):
        ex = dict(ex)
        diff = ex.pop(TWIN_DIFF_INPUT)
        return grad_fn(weights, diff, {**shared, **ex}, loss_target)

    if N_MICROBATCH == 1:
        loss, (grad_w, grad_x) = one_microbatch(per_example, given["loss_target"])
    else:
        def body(carry, xs):
            loss_sum, grad_sum = carry
            l_k, (gw_k, gx_k) = one_microbatch(xs[0], xs[1])
            with _jax.named_scope("update"):
                return (loss_sum + l_k, _jax.tree.map(_jnp.add, grad_sum, gw_k)), gx_k

        init = (_jnp.zeros((), _jnp.float32), _jax.tree.map(_jnp.zeros_like, weights))
        (loss, grad_w), grad_x = _jax.lax.scan(body, init, (per_example, given["loss_target"]))
    with _jax.named_scope("update"):
        delta_w, new_m, new_v = {}, {}, {}
        for n in TWIN_WEIGHTS:
            delta_w[n], new_m[n], new_v[n] = _adamw(weights[n], grad_w[n], given["m_" + n], given["v_" + n])
    return (loss, grad_x, *[grad_w[n] for n in TWIN_WEIGHTS], *[delta_w[n] for n in TWIN_WEIGHTS],
            *[new_m[n] for n in TWIN_WEIGHTS], *[new_v[n] for n in TWIN_WEIGHTS])
```

```python
import functools

import jax
import jax.numpy as jnp
from jax import lax
from jax.experimental import pallas as pl
from jax.experimental.pallas import tpu as pltpu

F32 = jnp.float32
BF16 = jnp.bfloat16

D_MODEL = 1024
N_HEADS = 16
HEAD_DIM = 64
PAIR = 2 * HEAD_DIM
N_PAIRS = N_HEADS // 2
N_GROUPS = 4
GROUP_DIM = 256
PLE_DIM = 256
MIX = 2 * D_MODEL
N_SEG = 6
F_OFF = 4 * D_MODEL
IN_COLS = N_SEG * D_MODEL + N_HEADS
LANES = 128
EPS = 1e-6
SCALE = 0.125
NEG = -1e30

ADAM_LR = 0.001
ADAM_B1 = 0.9
ADAM_B2 = 0.999
ADAM_EPS = 1e-08
ADAM_WD = 0.01
ADAM_STEP = 10

VMEM_LIMIT = 56 * 1024 * 1024
MESH = pl.DeviceIdType.MESH
ANY = pl.BlockSpec(memory_space=pl.ANY)

NT = (((1,), (1,)), ((), ()))
TN = (((0,), (0,)), ((), ()))
NN = (((1,), (0,)), ((), ()))


def _pcall(body, *, name, out_shape, grid=(), in_specs=None, out_specs=None, scratch_shapes=(), dims=None):
    kw = {}
    if in_specs is not None:
        kw["in_specs"] = in_specs
    if out_specs is not None:
        kw["out_specs"] = out_specs
    return pl.pallas_call(
        body, name=name, out_shape=out_shape, grid=grid, scratch_shapes=list(scratch_shapes),
        compiler_params=pltpu.CompilerParams(dimension_semantics=dims, vmem_limit_bytes=VMEM_LIMIT), **kw)


def _dot(a, b, dn=NN):
    return lax.dot_general(a, b, dn, preferred_element_type=F32)


def _sigmoid(x):
    return 1.0 / (1.0 + jnp.exp(-x))


def _matmul(a, b, *, ta=False, name, tm=512, tn=1024, tk=512):
    if ta:
        kdim, m = a.shape
    else:
        m, kdim = a.shape
    n = b.shape[1]
    tm, tn, tk = min(tm, m), min(tn, n), min(tk, kdim)
    nk = kdim // tk

    def body(a_ref, b_ref, o_ref, acc_ref):
        k = pl.program_id(2)

        @pl.when(k == 0)
        def _():
            acc_ref[...] = jnp.zeros_like(acc_ref)

        acc_ref[...] += _dot(a_ref[...].astype(BF16), b_ref[...].astype(BF16), TN if ta else NN)

        @pl.when(k == nk - 1)
        def _():
            o_ref[...] = acc_ref[...]

    a_spec = pl.BlockSpec((tk, tm), lambda i, j, k: (k, i)) if ta else pl.BlockSpec((tm, tk), lambda i, j, k: (i, k))
    return _pcall(
        body, name=name, out_shape=jax.ShapeDtypeStruct((m, n), F32), grid=(m // tm, n // tn, nk),
        in_specs=[a_spec, pl.BlockSpec((tk, tn), lambda i, j, k: (k, j))],
        out_specs=pl.BlockSpec((tm, tn), lambda i, j, k: (i, j)),
        scratch_shapes=[pltpu.VMEM((tm, tn), F32)], dims=("parallel", "parallel", "arbitrary"))(a, b)


def _norm_in_proj(h, g, wm, wf, *, tm=512):
    t = h.shape[0]

    def body(h_ref, g_ref, wm_ref, wf_ref, q_ref, k_ref, v_ref, za_ref, u_ref, zp_ref, fl_ref, hn_ref):
        j = pl.program_id(1)

        @pl.when(j == 0)
        def _():
            x = h_ref[...]
            r = lax.rsqrt(jnp.mean(x * x, axis=-1, keepdims=True) + EPS)
            hn = (x * r * g_ref[...]).astype(BF16)
            hn_ref[...] = hn
            fl_ref[...] = _dot(hn, wf_ref[...])

        y = _dot(hn_ref[...], wm_ref[...])
        for s, ref in enumerate((q_ref, k_ref, v_ref, za_ref, u_ref, zp_ref)):
            @pl.when(j == s)
            def _(ref=ref):
                ref[...] = y.astype(ref.dtype)

    row = lambda i, j: (i, 0)
    blk = pl.BlockSpec((tm, D_MODEL), row)
    sds = lambda dt: jax.ShapeDtypeStruct((t, D_MODEL), dt)
    return _pcall(
        body, name="norm_in_proj", grid=(t // tm, N_SEG),
        in_specs=[blk, pl.BlockSpec((1, D_MODEL), lambda i, j: (0, 0)),
                  pl.BlockSpec((D_MODEL, D_MODEL), lambda i, j: (0, j)),
                  pl.BlockSpec((D_MODEL, LANES), lambda i, j: (0, 0))],
        out_specs=[blk] * 6 + [pl.BlockSpec((tm, LANES), row), blk],
        out_shape=[sds(BF16)] * 3 + [sds(F32)] * 3 + [jax.ShapeDtypeStruct((t, LANES), F32), sds(BF16)],
        dims=("parallel", "arbitrary"))(h, g, wm, wf)


def _gate_cumsum(flt, bias):
    b, hh, s = flt.shape

    def body(fl_ref, b_ref, c_ref):
        x = fl_ref[...] + b_ref[...]
        acc = jnp.minimum(x, 0.0) - jnp.log(1.0 + jnp.exp(-jnp.abs(x)))
        idx = lax.broadcasted_iota(jnp.int32, x.shape, 1)
        sh = 1
        while sh < s:
            acc = acc + jnp.where(idx >= sh, pltpu.roll(acc, sh, 1), 0.0)
            sh *= 2
        c_ref[...] = acc

    return _pcall(
        body, name="gate_cumsum", grid=(b,),
        in_specs=[pl.BlockSpec((None, hh, s), lambda i: (i, 0, 0)), pl.BlockSpec((hh, 1), lambda i: (0, 0))],
        out_specs=pl.BlockSpec((None, hh, s), lambda i: (i, 0, 0)),
        out_shape=jax.ShapeDtypeStruct((b, hh, s), F32), dims=("parallel",))(flt, bias)


def _gate_cumsum_bwd(dcol, drow, flt, bias):
    b, hh, s = flt.shape

    def body(d_ref, dr_ref, fl_ref, b_ref, dfl_ref, db_ref):
        @pl.when(pl.program_id(0) == 0)
        def _():
            db_ref[...] = jnp.zeros_like(db_ref)

        x = fl_ref[...] + b_ref[...]
        acc = dr_ref[...] - d_ref[...]
        idx = lax.broadcasted_iota(jnp.int32, x.shape, 1)
        sh = 1
        while sh < s:
            acc = acc + jnp.where(idx + sh < s, pltpu.roll(acc, s - sh, 1), 0.0)
            sh *= 2
        e = jnp.exp(-jnp.abs(x))
        sig_neg = jnp.where(x >= 0.0, e, 1.0) / (1.0 + e)
        dfl = acc * sig_neg
        dfl_ref[...] = dfl
        db_ref[...] += jnp.sum(dfl, axis=1, keepdims=True)

    return _pcall(
        body, name="gate_cumsum_bwd", grid=(b,),
        in_specs=[pl.BlockSpec((None, hh, s), lambda i: (i, 0, 0))] * 3 + [pl.BlockSpec((hh, 1), lambda i: (0, 0))],
        out_specs=[pl.BlockSpec((None, hh, s), lambda i: (i, 0, 0)), pl.BlockSpec((hh, 1), lambda i: (0, 0))],
        out_shape=[jax.ShapeDtypeStruct((b, hh, s), F32), jax.ShapeDtypeStruct((hh, 1), F32)],
        dims=("arbitrary",))(dcol, drow, flt, bias)


def _attn_fwd(q, k, v, c4, *, nb, tq=256):
    t = q.shape[0]
    s = t // nb
    nq = s // tq
    tk = tq

    def body(q_ref, k_ref, v_ref, c_ref, o_ref, lse_ref):
        qi = pl.program_id(2)
        lo = lax.broadcasted_iota(jnp.int32, (tq, PAIR), 1) < HEAD_DIM
        q2 = q_ref[...] * SCALE
        zero = jnp.zeros_like(q2)
        qh = (jnp.where(lo, q2, zero), jnp.where(lo, zero, q2))
        tri = lax.broadcasted_iota(jnp.int32, (tq, tk), 0) >= lax.broadcasted_iota(jnp.int32, (tq, tk), 1)

        def step(kj, carry, masked):
            off = pl.multiple_of(kj * tk, tk)
            k2 = k_ref[pl.ds(off, tk), :]
            v2 = v_ref[pl.ds(off, tk), :]
            crow = c_ref[:, pl.ds(off, tk)]
            out = []
            for hd in range(2):
                m, l, acc = carry[hd]
                sc = _dot(qh[hd], k2, NT) - crow[hd:hd + 1, :]
                if masked:
                    sc = jnp.where(tri, sc, NEG)
                m_new = jnp.maximum(m, jnp.max(sc, axis=1, keepdims=True))
                alpha = jnp.exp(m - m_new)
                p = jnp.exp(sc - m_new)
                l = alpha * l + jnp.sum(p, axis=1, keepdims=True)
                acc = alpha * acc + _dot(p.astype(BF16), v2)
                out.append((m_new, l, acc))
            return tuple(out)

        init = tuple((jnp.full((tq, 1), NEG, F32), jnp.zeros((tq, 1), F32), jnp.zeros((tq, PAIR), F32))
                     for _ in range(2))
        carry = lax.fori_loop(0, qi, lambda kj, cr: step(kj, cr, False), init)
        (m0, l0, a0), (m1, l1, a1) = step(qi, carry, True)
        o_ref[...] = jnp.where(lo, a0 / l0, a1 / l1)
        lse_ref[...] = jnp.where(lo, m0 + jnp.log(l0), m1 + jnp.log(l1))

    qspec = pl.BlockSpec((tq, PAIR), lambda b, hp, i: (b * nq + i, hp))
    kvspec = pl.BlockSpec((s, PAIR), lambda b, hp, i: (b, hp))
    return _pcall(
        body, name="attn_fwd", grid=(nb, N_PAIRS, nq),
        in_specs=[qspec, kvspec, kvspec, pl.BlockSpec((None, None, 2, s), lambda b, hp, i: (b, hp, 0, 0))],
        out_specs=[qspec, qspec],
        out_shape=[jax.ShapeDtypeStruct((t, D_MODEL), F32)] * 2,
        dims=("parallel", "parallel", "arbitrary"))(q, k, v, c4)


def _attn_bwd(q, k, v, do, ccol, lse4, delta4, *, nb, tk=256):
    t = q.shape[0]
    s = t // nb
    nk = s // tk
    tq = tk

    def body(q_ref, do_ref, k_ref, v_ref, cc_ref, lse_ref, dl_ref, dq_ref, dk_ref, dv_ref, ds_ref, dr_ref):
        kj = pl.program_id(2)

        @pl.when(kj == 0)
        def _():
            dq_ref[...] = jnp.zeros_like(dq_ref)
            dr_ref[...] = jnp.zeros_like(dr_ref)

        lo = lax.broadcasted_iota(jnp.int32, (tk, PAIR), 1) < HEAD_DIM
        k2 = k_ref[...]
        v2 = v_ref[...]
        zero = jnp.zeros_like(k2)
        kh = (jnp.where(lo, k2, zero), jnp.where(lo, zero, k2))
        vh = (jnp.where(lo, v2, zero), jnp.where(lo, zero, v2))
        cc = cc_ref[...]
        ccs = (cc[:, 0:1], cc[:, HEAD_DIM:HEAD_DIM + 1])
        tri = lax.broadcasted_iota(jnp.int32, (tk, tq), 1) >= lax.broadcasted_iota(jnp.int32, (tk, tq), 0)

        def step(qi, carry, masked):
            off = pl.multiple_of(qi * tq, tq)
            q2 = q_ref[pl.ds(off, tq), :] * SCALE
            do2 = do_ref[pl.ds(off, tq), :]
            lse = lse_ref[:, pl.ds(off, tq)]
            dl = dl_ref[:, pl.ds(off, tq)]
            out = []
            dqs = []
            for hd in range(2):
                dk_a, dv_a, ds_a = carry[hd]
                st = _dot(kh[hd], q2, NT) - ccs[hd]
                pt = jnp.exp(st - lse[hd:hd + 1, :])
                if masked:
                    pt = jnp.where(tri, pt, 0.0)
                dpt = _dot(vh[hd], do2, NT)
                dst = pt * (dpt - dl[hd:hd + 1, :])
                dstb = dst.astype(BF16)
                dv_a = dv_a + _dot(pt.astype(BF16), do2)
                dk_a = dk_a + _dot(dstb, q2)
                ds_a = ds_a + jnp.sum(dst, axis=1, keepdims=True)
                dr_ref[hd:hd + 1, pl.ds(off, tq)] += jnp.sum(dst, axis=0, keepdims=True)
                dqs.append(_dot(dstb, k2, TN))
                out.append((dk_a, dv_a, ds_a))
            loq = lax.broadcasted_iota(jnp.int32, (tq, PAIR), 1) < HEAD_DIM
            dq_ref[pl.ds(off, tq), :] += jnp.where(loq, dqs[0], dqs[1]) * SCALE
            return tuple(out)

        init = tuple((jnp.zeros((tk, PAIR), F32), jnp.zeros((tk, PAIR), F32), jnp.zeros((tk, 1), F32))
                     for _ in range(2))
        carry = step(kj, init, True)
        (dk0, dv0, ds0), (dk1, dv1, ds1) = lax.fori_loop(kj + 1, nk, lambda qi, cr: step(qi, cr, False), carry)
        dk_ref[...] = jnp.where(lo, dk0, dk1).astype(BF16)
        dv_ref[...] = jnp.where(lo, dv0, dv1).astype(BF16)
        ds_ref[...] = jnp.where(lo, ds0, ds1)

    full = pl.BlockSpec((s, PAIR), lambda b, hp, j: (b, hp))
    kblk = pl.BlockSpec((tk, PAIR), lambda b, hp, j: (b * nk + j, hp))
    rows = pl.BlockSpec((None, None, 2, s), lambda b, hp, j: (b, hp, 0, 0))
    sds = lambda dt: jax.ShapeDtypeStruct((t, D_MODEL), dt)
    return _pcall(
        body, name="attn_bwd", grid=(nb, N_PAIRS, nk),
        in_specs=[full, full, kblk, kblk, kblk, rows, rows],
        out_specs=[full, kblk, kblk, kblk, rows],
        out_shape=[sds(F32), sds(BF16), sds(BF16), sds(F32), jax.ShapeDtypeStruct((nb, N_PAIRS, 2, s), F32)],
        dims=("parallel", "parallel", "arbitrary"))(q, do, k, v, ccol, lse4, delta4)


def _window_sum(x, g, s, *, ahead):
    row = lax.broadcasted_iota(jnp.int32, x.shape, 0)
    for step in range(N_GROUPS):
        sh = 1 << step
        if ahead:
            moved = jnp.where(row + sh < s, pltpu.roll(x, s - sh, 0), 0.0)
        else:
            moved = jnp.where(row >= sh, pltpu.roll(x, sh, 0), 0.0)
        x = jnp.where(step <= g, x + moved, x)
    return x


def _inv_count(g, s):
    pos = lax.broadcasted_iota(jnp.int32, (s, 1), 0) + 1
    return 1.0 / jnp.minimum(pos, 2 << g).astype(F32)


def _pool_fwd(u, zp, wpool, ps, *, nb):
    t = u.shape[0]
    s = t // nb

    def body(u_ref, zp_ref, w_ref, ps_ref, out_ref, pooled_ref, mixed_ref):
        g = pl.program_id(1)
        x = u_ref[...]
        pooled = (_window_sum(x, g, s, ahead=False) * _inv_count(g, s) - x).astype(BF16)
        mixed = _dot(pooled, w_ref[...])
        z = zp_ref[...]
        pooled_ref[...] = pooled
        mixed_ref[...] = mixed
        out_ref[...] = (mixed * ps_ref[...] * (z * _sigmoid(z))).astype(BF16)

    blk = pl.BlockSpec((s, GROUP_DIM), lambda b, g: (b, g))
    sds = lambda dt: jax.ShapeDtypeStruct((t, D_MODEL), dt)
    return _pcall(
        body, name="pool_fwd", grid=(nb, N_GROUPS),
        in_specs=[blk, blk, pl.BlockSpec((None, GROUP_DIM, GROUP_DIM), lambda b, g: (g, 0, 0)),
                  pl.BlockSpec((1, GROUP_DIM), lambda b, g: (0, g))],
        out_specs=[blk, blk, blk], out_shape=[sds(BF16), sds(BF16), sds(F32)],
        dims=("parallel", "parallel"))(u, zp, wpool, ps)


def _pool_bwd(dpool, mixed, zp, wpool, ps, *, nb):
    t = dpool.shape[0]
    s = t // nb

    def body(dp_ref, mx_ref, zp_ref, w_ref, ps_ref, dmx_ref, du_ref, dzp_ref, dps_ref):
        g = pl.program_id(0)

        @pl.when(pl.program_id(1) == 0)
        def _():
            dps_ref[...] = jnp.zeros_like(dps_ref)

        dp = dp_ref[...]
        mixed = mx_ref[...]
        z = zp_ref[...]
        scale = ps_ref[...]
        sg = _sigmoid(z)
        silu = z * sg
        dmixed = (dp * scale * silu).astype(BF16)
        dmx_ref[...] = dmixed
        dps_ref[...] += jnp.sum(dp * mixed * silu, axis=0, keepdims=True)
        dzp_ref[...] = (dp * mixed * scale * (sg * (1.0 + z * (1.0 - sg)))).astype(BF16)
        dpooled = _dot(dmixed, w_ref[...], NT)
        du = _window_sum(dpooled * _inv_count(g, s), g, s, ahead=True) - dpooled
        du_ref[...] = du.astype(BF16)

    blk = pl.BlockSpec((s, GROUP_DIM), lambda g, b: (b, g))
    sds = lambda dt: jax.ShapeDtypeStruct((t, D_MODEL), dt)
    return _pcall(
        body, name="pool_bwd", grid=(N_GROUPS, nb),
        in_specs=[blk, blk, blk, pl.BlockSpec((None, GROUP_DIM, GROUP_DIM), lambda g, b: (g, 0, 0)),
                  pl.BlockSpec((1, GROUP_DIM), lambda g, b: (0, g))],
        out_specs=[blk, blk, blk, pl.BlockSpec((1, GROUP_DIM), lambda g, b: (0, g))],
        out_shape=[sds(BF16), sds(BF16), sds(BF16), jax.ShapeDtypeStruct((1, D_MODEL), F32)],
        dims=("parallel", "arbitrary"))(dpool, mixed, zp, wpool, ps)


def _pool_wgrad(pooled, dmixed, *, tk=1024):
    t = pooled.shape[0]
    nk = t // tk

    def body(a_ref, b_ref, o_ref):
        @pl.when(pl.program_id(1) == 0)
        def _():
            o_ref[...] = jnp.zeros_like(o_ref)

        o_ref[...] += _dot(a_ref[...], b_ref[...], TN)

    blk = pl.BlockSpec((tk, GROUP_DIM), lambda g, k: (k, g))
    return _pcall(
        body, name="pool_wgrad", grid=(N_GROUPS, nk), in_specs=[blk, blk],
        out_specs=pl.BlockSpec((None, GROUP_DIM, GROUP_DIM), lambda g, k: (g, 0, 0)),
        out_shape=jax.ShapeDtypeStruct((N_GROUPS, GROUP_DIM, GROUP_DIM), F32),
        dims=("parallel", "arbitrary"))(pooled, dmixed)


def _out_block(h, o, za, pool, p, wo, wpg, wpe, gpost, *, tm=256):
    t = h.shape[0]

    def body(h_ref, o_ref, za_ref, pool_ref, p_ref, wo_ref, wpg_ref, wpe_ref, g_ref,
             h2_ref, mix_ref, h1_ref, gate_ref, pe_ref, cat_ref):
        z = za_ref[...]
        a = (o_ref[...] * (z * _sigmoid(z))).astype(BF16)
        pool_v = pool_ref[...]
        cat_ref[:, :D_MODEL] = a
        cat_ref[:, D_MODEL:] = pool_v
        mix = _dot(a, wo_ref[:D_MODEL, :]) + _dot(pool_v, wo_ref[D_MODEL:, :])
        r = lax.rsqrt(jnp.mean(mix * mix, axis=-1, keepdims=True) + EPS)
        h1 = h_ref[...] + mix * r * g_ref[...]
        gate = _sigmoid(_dot(h1.astype(BF16), wpg_ref[...]))
        pe = _dot(p_ref[...].astype(BF16), wpe_ref[...])
        mix_ref[...] = mix
        h1_ref[...] = h1
        gate_ref[...] = gate
        pe_ref[...] = pe
        h2_ref[...] = h1 + gate * pe

    row = lambda i: (i, 0)
    const = lambda i: (0, 0)
    blk = pl.BlockSpec((tm, D_MODEL), row)
    sds = jax.ShapeDtypeStruct((t, D_MODEL), F32)
    return _pcall(
        body, name="out_block", grid=(t // tm,),
        in_specs=[blk, blk, blk, blk, pl.BlockSpec((tm, PLE_DIM), row),
                  pl.BlockSpec((MIX, D_MODEL), const), pl.BlockSpec((D_MODEL, D_MODEL), const),
                  pl.BlockSpec((PLE_DIM, D_MODEL), const), pl.BlockSpec((1, D_MODEL), const)],
        out_specs=[blk] * 5 + [pl.BlockSpec((tm, MIX), row)],
        out_shape=[sds] * 5 + [jax.ShapeDtypeStruct((t, MIX), BF16)],
        dims=("parallel",))(h, o, za, pool, p, wo, wpg, wpe, gpost)


def _loss_grad(y, target, *, tm=512):
    t = y.shape[0]

    def body(y_ref, t_ref, dy_ref, loss_ref):
        @pl.when(pl.program_id(0) == 0)
        def _():
            loss_ref[...] = jnp.zeros_like(loss_ref)

        err = y_ref[...] - t_ref[...]
        dy_ref[...] = err * (1.0 / D_MODEL)
        part = jnp.sum(jnp.sum(err * err, axis=1, keepdims=True), axis=0, keepdims=True)
        loss_ref[...] += part * (0.5 / D_MODEL)

    blk = pl.BlockSpec((tm, D_MODEL), lambda i: (i, 0))
    return _pcall(
        body, name="loss_grad", grid=(t // tm,), in_specs=[blk, blk],
        out_specs=[blk, pl.BlockSpec((8, LANES), lambda i: (0, 0))],
        out_shape=[jax.ShapeDtypeStruct((t, D_MODEL), F32), jax.ShapeDtypeStruct((8, LANES), F32)],
        dims=("arbitrary",))(y, target)


def _out_block_bwd(dh2, gate, pe, mix, o, za, wpg, wo, gpost, gsum, *, tm=256):
    t = dh2.shape[0]

    def body(dh2_ref, gate_ref, pe_ref, mix_ref, o_ref, za_ref, wpg_ref, wo_ref, g_ref, gs_ref,
             dh1_ref, dgp_ref, dpe_ref, dmix_ref, do_ref, dza_ref, dpool_ref, delta_ref, dg_ref):
        @pl.when(pl.program_id(0) == 0)
        def _():
            dg_ref[...] = jnp.zeros_like(dg_ref)

        dh2 = dh2_ref[...]
        gate = gate_ref[...]
        dpe_ref[...] = (dh2 * gate).astype(BF16)
        dgp = (dh2 * pe_ref[...] * gate * (1.0 - gate)).astype(BF16)
        dgp_ref[...] = dgp
        dh1 = dh2 + _dot(dgp, wpg_ref[...], NT)
        dh1_ref[...] = dh1
        mix = mix_ref[...]
        r = lax.rsqrt(jnp.mean(mix * mix, axis=-1, keepdims=True) + EPS)
        dg_ref[...] += jnp.sum(dh1 * mix * r, axis=0, keepdims=True)
        a = dh1 * g_ref[...]
        dmix = (r * a - mix * (r * r * r) * jnp.mean(a * mix, axis=-1, keepdims=True)).astype(BF16)
        dmix_ref[...] = dmix
        dattn = _dot(dmix, wo_ref[:D_MODEL, :], NT)
        dpool_ref[...] = _dot(dmix, wo_ref[D_MODEL:, :], NT)
        z = za_ref[...]
        sg = _sigmoid(z)
        o = o_ref[...]
        do = (dattn * (z * sg)).astype(BF16)
        do_ref[...] = do
        dza_ref[...] = (dattn * o * (sg * (1.0 + z * (1.0 - sg)))).astype(BF16)
        prod = do.astype(F32) * o
        hi = prod.astype(BF16)
        rest = (prod - hi.astype(F32)).astype(BF16)
        delta_ref[...] = _dot(hi, gs_ref[...]) + _dot(rest, gs_ref[...])

    row = lambda i: (i, 0)
    const = lambda i: (0, 0)
    blk = pl.BlockSpec((tm, D_MODEL), row)
    sds = lambda dt: jax.ShapeDtypeStruct((t, D_MODEL), dt)
    return _pcall(
        body, name="out_block_bwd", grid=(t // tm,),
        in_specs=[blk] * 6 + [pl.BlockSpec((D_MODEL, D_MODEL), const), pl.BlockSpec((MIX, D_MODEL), const),
                              pl.BlockSpec((1, D_MODEL), const), pl.BlockSpec((D_MODEL, LANES), const)],
        out_specs=[blk] * 7 + [pl.BlockSpec((tm, LANES), row), pl.BlockSpec((1, D_MODEL), const)],
        out_shape=[sds(F32)] + [sds(BF16)] * 5 + [sds(F32), jax.ShapeDtypeStruct((t, LANES), F32),
                                                 jax.ShapeDtypeStruct((1, D_MODEL), F32)],
        dims=("arbitrary",))(dh2, gate, pe, mix, o, za, wpg, wo, gpost, gsum)


def _in_proj_bwd(dsegs, dfl, wm, wf, h, dh1, gpre, *, tm=512):
    t = h.shape[0]

    def body(*refs):
        seg_refs = refs[:N_SEG]
        dfl_ref, wm_ref, wf_ref, h_ref, dh1_ref, g_ref, dh_ref, dg_ref, acc_ref = refs[N_SEG:]
        i = pl.program_id(0)
        j = pl.program_id(1)

        @pl.when(jnp.logical_and(i == 0, j == 0))
        def _():
            dg_ref[...] = jnp.zeros_like(dg_ref)

        @pl.when(j == 0)
        def _():
            acc_ref[...] = _dot(dfl_ref[...].astype(BF16), wf_ref[...], NT)

        for sgi, ref in enumerate(seg_refs):
            @pl.when(j == sgi)
            def _(ref=ref):
                acc_ref[...] += _dot(ref[...].astype(BF16), wm_ref[...], NT)

        @pl.when(j == N_SEG - 1)
        def _():
            dhn = acc_ref[...]
            x = h_ref[...]
            r = lax.rsqrt(jnp.mean(x * x, axis=-1, keepdims=True) + EPS)
            dg_ref[...] += jnp.sum(dhn * x * r, axis=0, keepdims=True)
            a = dhn * g_ref[...]
            dh_ref[...] = dh1_ref[...] + r * a - x * (r * r * r) * jnp.mean(a * x, axis=-1, keepdims=True)

    row = lambda i, j: (i, 0)
    const = lambda i, j: (0, 0)
    blk = pl.BlockSpec((tm, D_MODEL), row)
    return _pcall(
        body, name="in_proj_bwd", grid=(t // tm, N_SEG),
        in_specs=[blk] * N_SEG + [pl.BlockSpec((tm, LANES), row), pl.BlockSpec((D_MODEL, D_MODEL), lambda i, j: (0, j)),
                                  pl.BlockSpec((D_MODEL, LANES), const), blk, blk, pl.BlockSpec((1, D_MODEL), const)],
        out_specs=[blk, pl.BlockSpec((1, D_MODEL), const)],
        out_shape=[jax.ShapeDtypeStruct((t, D_MODEL), F32), jax.ShapeDtypeStruct((1, D_MODEL), F32)],
        scratch_shapes=[pltpu.VMEM((tm, D_MODEL), F32)],
        dims=("arbitrary", "arbitrary"))(*dsegs, dfl, wm, wf, h, dh1, gpre)


def _chip_exchange(arrays, *, per_target, name):
    n = len(arrays)
    shapes = [a.shape[1:] if per_target else a.shape for a in arrays]

    def body(*refs):
        ins, outs = refs[:n], refs[n:2 * n]
        send, recv, loc = refs[2 * n:]
        x, y, c = lax.axis_index("x"), lax.axis_index("y"), lax.axis_index("c")
        me = 2 * x + y
        peers = [(1 - x, y), (x, 1 - y), (1 - x, 1 - y)]

        def src(kk, chip):
            return ins[kk].at[chip] if per_target else ins[kk]

        def remote(kk, pi, slot):
            px, py = peers[pi]
            return pltpu.make_async_remote_copy(
                src_ref=src(kk, 2 * px + py), dst_ref=outs[kk].at[slot], send_sem=send.at[kk * 3 + pi],
                recv_sem=recv.at[kk * 3 + pi], device_id=(px, py, c), device_id_type=MESH)

        local = [pltpu.make_async_copy(src(kk, me), outs[kk].at[me], loc.at[kk]) for kk in range(n)]
        for cp in local:
            cp.start()
        sends = [remote(kk, pi, me) for kk in range(n) for pi in range(3)]
        for cp in sends:
            cp.start()
        for kk in range(n):
            for pi, (px, py) in enumerate(peers):
                remote(kk, pi, 2 * px + py).wait_recv()
        for cp in sends:
            cp.wait_send()
        for cp in local:
            cp.wait()

    return _pcall(
        body, name=name, in_specs=[ANY] * n, out_specs=[ANY] * n,
        out_shape=[jax.ShapeDtypeStruct((4,) + tuple(sh), a.dtype) for sh, a in zip(shapes, arrays)],
        scratch_shapes=[pltpu.SemaphoreType.DMA((3 * n,)), pltpu.SemaphoreType.DMA((3 * n,)),
                        pltpu.SemaphoreType.DMA((n,))])(*arrays)


def _sibling_exchange(arrays, *, name):
    n = len(arrays)

    def body(*refs):
        ins, outs = refs[:n], refs[n:2 * n]
        send, recv = refs[2 * n:]
        sib = (lax.axis_index("x"), lax.axis_index("y"), 1 - lax.axis_index("c"))
        copies = [pltpu.make_async_remote_copy(src_ref=ins[kk], dst_ref=outs[kk], send_sem=send.at[kk],
                                               recv_sem=recv.at[kk], device_id=sib, device_id_type=MESH)
                  for kk in range(n)]
        for cp in copies:
            cp.start()
        for cp in copies:
            cp.wait_recv()
        for cp in copies:
            cp.wait_send()

    return _pcall(
        body, name=name, in_specs=[ANY] * n, out_specs=[ANY] * n,
        out_shape=[jax.ShapeDtypeStruct(a.shape, a.dtype) for a in arrays],
        scratch_shapes=[pltpu.SemaphoreType.DMA((n,)), pltpu.SemaphoreType.DMA((n,))])(*arrays)


def _all_gather_small(a, *, name):
    def body(a_ref, o_ref, send, recv, loc):
        x, y, c = lax.axis_index("x"), lax.axis_index("y"), lax.axis_index("c")
        me = 4 * x + 2 * y + c

        def peer(rel):
            fx, fy, fc = (rel >> 2) & 1, (rel >> 1) & 1, rel & 1
            px = (1 - x) if fx else x
            py = (1 - y) if fy else y
            pc = (1 - c) if fc else c
            return px, py, pc

        def remote(rel, slot):
            return pltpu.make_async_remote_copy(
                src_ref=a_ref, dst_ref=o_ref.at[slot], send_sem=send.at[rel - 1], recv_sem=recv.at[rel - 1],
                device_id=peer(rel), device_id_type=MESH)

        mine = pltpu.make_async_copy(a_ref, o_ref.at[me], loc)
        mine.start()
        sends = [remote(rel, me) for rel in range(1, 8)]
        for cp in sends:
            cp.start()
        for rel in range(1, 8):
            px, py, pc = peer(rel)
            remote(rel, 4 * px + 2 * py + pc).wait_recv()
        for cp in sends:
            cp.wait_send()
        mine.wait()

    return _pcall(
        body, name=name, in_specs=[ANY], out_specs=ANY,
        out_shape=jax.ShapeDtypeStruct((8,) + a.shape, a.dtype),
        scratch_shapes=[pltpu.SemaphoreType.DMA((7,)), pltpu.SemaphoreType.DMA((7,)), pltpu.SemaphoreType.DMA])(a)


def _sum_slots(r, *, name, tr=256):
    _, rows, cols = r.shape
    tr = min(tr, rows)

    def body(r_ref, o_ref):
        acc = r_ref[0].astype(F32)
        for slot in range(1, 4):
            acc = acc + r_ref[slot].astype(F32)
        o_ref[...] = acc

    return _pcall(
        body, name=name, grid=(rows // tr,), in_specs=[pl.BlockSpec((4, tr, cols), lambda i: (0, i, 0))],
        out_specs=pl.BlockSpec((tr, cols), lambda i: (i, 0)),
        out_shape=jax.ShapeDtypeStruct((rows, cols), F32), dims=("parallel",))(r)


def _adamw_math(w, g, m, v):
    m = ADAM_B1 * m + (1.0 - ADAM_B1) * g
    v = ADAM_B2 * v + (1.0 - ADAM_B2) * (g * g)
    m_hat = m / (1.0 - ADAM_B1 ** ADAM_STEP)
    v_hat = v / (1.0 - ADAM_B2 ** ADAM_STEP)
    delta = -ADAM_LR * (m_hat / (jnp.sqrt(v_hat) + ADAM_EPS) + ADAM_WD * w)
    return delta, m, v


def _adamw(w, m, v, g_mine, g_theirs, *, name, tr=128):
    rows, cols = w.shape
    tr = min(tr, rows)

    def body(w_ref, m_ref, v_ref, gs_ref, gn_ref, g_ref, d_ref, mo_ref, vo_ref):
        g = gs_ref[...] + gn_ref[...]
        delta, m_new, v_new = _adamw_math(w_ref[...], g, m_ref[...], v_ref[...])
        g_ref[...] = g
        d_ref[...] = delta
        mo_ref[...] = m_new
        vo_ref[...] = v_new

    blk = pl.BlockSpec((tr, cols), lambda i: (i, 0))
    return _pcall(
        body, name=name, grid=(rows // tr,), in_specs=[blk] * 5, out_specs=[blk] * 4,
        out_shape=[jax.ShapeDtypeStruct((rows, cols), F32)] * 4, dims=("parallel",))(w, m, v, g_mine, g_theirs)


def _adamw_small(w, m, v, parts, *, name):
    def body(w_ref, m_ref, v_ref, p_ref, g_ref, d_ref, mo_ref, vo_ref):
        g = p_ref[0]
        for dev in range(1, 8):
            g = g + p_ref[dev]
        delta, m_new, v_new = _adamw_math(w_ref[...], g, m_ref[...], v_ref[...])
        g_ref[...] = g
        d_ref[...] = delta
        mo_ref[...] = m_new
        vo_ref[...] = v_new

    return _pcall(body, name=name, out_shape=[jax.ShapeDtypeStruct(w.shape, F32)] * 4)(w, m, v, parts)


def _rows_from_heads(a, nb, s):
    return a.reshape(nb, s, N_HEADS).transpose(0, 2, 1).reshape(nb, N_PAIRS, 2, s)


def _layer_fwd(h, p_l, wts, nb):
    t = h.shape[0]
    s = t // nb
    q, k, v, za, u, zp, fl, hn = _norm_in_proj(h, wts["gpre"], wts["wm"], wts["wf"])
    flt = fl[:, :N_HEADS].reshape(nb, s, N_HEADS).transpose(0, 2, 1)
    c = _gate_cumsum(flt, wts["bf"])
    o, lse = _attn_fwd(q, k, v, c.reshape(nb, N_PAIRS, 2, s), nb=nb)
    pool, pooled, mixed = _pool_fwd(u, zp, wts["wpool"], wts["ps"], nb=nb)
    h2, mix, h1, gate, pe, cat = _out_block(h, o, za, pool, p_l, wts["wo"], wts["wpg"], wts["wpe"], wts["gpost"])
    saved = dict(h=h, hn=hn, q=q, k=k, v=v, za=za, zp=zp, flt=flt, c=c, o=o, lse=lse, pooled=pooled, mixed=mixed,
                 mix=mix, h1=h1, gate=gate, pe=pe, cat=cat, p=p_l)
    return h2, saved


def _layer_bwd(dh2, sv, wts, gsum, nb):
    t = dh2.shape[0]
    s = t // nb
    dh1, dgp, dpe, dmix, do, dza, dpool, delta, dgpost = _out_block_bwd(
        dh2, sv["gate"], sv["pe"], sv["mix"], sv["o"], sv["za"], wts["wpg"], wts["wo"], wts["gpost"], gsum)
    g_wpe = _matmul(sv["p"], dpe, ta=True, name="wgrad_pe")
    g_wpg = _matmul(sv["h1"], dgp, ta=True, name="wgrad_pg")
    g_wo = _matmul(sv["cat"], dmix, ta=True, name="wgrad_out")

    ccol = jnp.repeat(sv["c"].transpose(0, 2, 1).reshape(t, N_HEADS), HEAD_DIM, axis=1)
    lse4 = _rows_from_heads(sv["lse"][:, ::HEAD_DIM], nb, s)
    delta4 = _rows_from_heads(delta[:, :N_HEADS], nb, s)
    dq, dk, dv, dcol, drow = _attn_bwd(sv["q"], sv["k"], sv["v"], do, ccol, lse4, delta4, nb=nb)
    dcol_rows = dcol[:, ::HEAD_DIM].reshape(nb, s, N_HEADS).transpose(0, 2, 1)
    dflt, dbf = _gate_cumsum_bwd(dcol_rows, drow.reshape(nb, N_HEADS, s), sv["flt"], wts["bf"])
    dfl = jnp.pad(dflt.transpose(0, 2, 1).reshape(t, N_HEADS), ((0, 0), (0, LANES - N_HEADS)))

    dmixed, du, dzp, dps = _pool_bwd(dpool, sv["mixed"], sv["zp"], wts["wpool"], wts["ps"], nb=nb)
    g_wpool = _pool_wgrad(sv["pooled"], dmixed)

    dsegs = (dq, dk, dv, dza, du, dzp)
    dh, dgpre = _in_proj_bwd(dsegs, dfl, wts["wm"], wts["wf"], sv["h"], dh1, wts["gpre"])
    g_segs = [_matmul(sv["hn"], dx, ta=True, name="wgrad_in_f32" if dx.dtype == F32 else "wgrad_in")
              for dx in dsegs]
    g_wf = _matmul(sv["hn"], dfl, ta=True, name="wgrad_in_f")
    g_win = jnp.concatenate(g_segs[:4] + [g_wf[:, :N_HEADS]] + g_segs[4:], axis=1)
    grads = dict(w_in=g_win, w_out=g_wo, w_pg=g_wpg, w_pe=g_wpe, w_pool=g_wpool,
                 norm_pre=dgpre[0], norm_post=dgpost[0], pool_scale=dps[0], b_f=dbf[:, 0])
    return dh, grads


def kernel(x, p, norm_pre, norm_post, w_in, b_f, w_pool, pool_scale, w_out, w_pg, w_pe, loss_target, m_norm_pre, m_norm_post, m_w_in, m_b_f, m_w_pool, m_pool_scale, m_w_out, m_w_pg, m_w_pe, v_norm_pre, v_norm_post, v_w_in, v_b_f, v_w_pool, v_pool_scale, v_w_out, v_w_pg, v_w_pe):
    nb, s, _ = x.shape
    t = nb * s
    depth = w_in.shape[0]
    shard_cols = w_in.shape[2]
    big = dict(w_in=(w_in, m_w_in, v_w_in), w_out=(w_out, m_w_out, v_w_out), w_pg=(w_pg, m_w_pg, v_w_pg),
               w_pe=(w_pe, m_w_pe, v_w_pe), w_pool=(w_pool, m_w_pool, v_w_pool))
    names = list(big)

    gathered = dict(zip(names, _chip_exchange([big[n][0].astype(BF16) for n in names], per_target=False,
                                              name="gather_weights")))
    layers = []
    for l in range(depth):
        win = jnp.concatenate([gathered["w_in"][j, l] for j in range(4)], axis=1)
        wm = jnp.concatenate([win[:, :F_OFF], win[:, F_OFF + N_HEADS:]], axis=1)
        wf = jnp.pad(win[:, F_OFF:F_OFF + N_HEADS], ((0, 0), (0, LANES - N_HEADS)))
        layers.append(dict(
            wm=wm, wf=wf,
            wo=gathered["w_out"][:, l].reshape(MIX, D_MODEL),
            wpg=gathered["w_pg"][:, l].reshape(D_MODEL, D_MODEL),
            wpe=jnp.concatenate([gathered["w_pe"][j, l] for j in range(4)], axis=1),
            wpool=gathered["w_pool"][:, l].transpose(1, 0, 2, 3).reshape(N_GROUPS, GROUP_DIM, GROUP_DIM),
            gpre=norm_pre[l][None], gpost=norm_post[l][None], ps=pool_scale[l][None], bf=b_f[l][:, None]))

    h = x.reshape(t, D_MODEL)
    saved = []
    for l in range(depth):
        h, sv = _layer_fwd(h, p[l].reshape(t, PLE_DIM), layers[l], nb)
        saved.append(sv)
    dh, loss_blk = _loss_grad(h, loss_target.reshape(t, D_MODEL))
    loss = lax.psum(loss_blk[0, 0], ("x", "y", "c"))
    gsum = (jnp.arange(D_MODEL)[:, None] // HEAD_DIM == jnp.arange(LANES)[None, :]).astype(BF16)
    grads = [None] * depth
    for l in reversed(range(depth)):
        dh, grads[l] = _layer_bwd(dh, saved[l], layers[l], gsum, nb)
    grad_x = dh.reshape(nb, s, D_MODEL)

    def pieces(name):
        full = jnp.stack([grads[l][name] for l in range(depth)])
        if name == "w_in":
            return jnp.stack([full[:, :, j * shard_cols:(j + 1) * shard_cols] for j in range(4)])
        if name == "w_out":
            return full.reshape(depth, 4, MIX // 4, D_MODEL).transpose(1, 0, 2, 3)
        if name == "w_pg":
            return full.reshape(depth, 4, D_MODEL // 4, D_MODEL).transpose(1, 0, 2, 3)
        if name == "w_pe":
            return jnp.stack([full[:, :, j * (D_MODEL // 4):(j + 1) * (D_MODEL // 4)] for j in range(4)])
        return full.reshape(depth, N_GROUPS, 4, GROUP_DIM // 4, GROUP_DIM).transpose(2, 0, 1, 3, 4)

    received = _chip_exchange([pieces(n).astype(BF16) for n in names], per_target=True, name="scatter_grads")
    sums = []
    for n, r in zip(names, received):
        cols = r.shape[-1]
        sums.append(_sum_slots(r.reshape(4, -1, cols), name="sum_" + n))
    other = _sibling_exchange(sums, name="swap_sums")
    big_out = {}
    for n, mine, theirs in zip(names, sums, other):
        w, m, v = big[n]
        cols = w.shape[-1]
        outs = _adamw(w.reshape(-1, cols), m.reshape(-1, cols), v.reshape(-1, cols), mine, theirs, name="adamw_" + n)
        big_out[n] = [o.reshape(w.shape) for o in outs]

    small = dict(norm_pre=(norm_pre, m_norm_pre, v_norm_pre), norm_post=(norm_post, m_norm_post, v_norm_post),
                 pool_scale=(pool_scale, m_pool_scale, v_pool_scale), b_f=(b_f, m_b_f, v_b_f))

    def pack(get):
        rows = []
        for n in small:
            a = get(n)
            rows.append(jnp.pad(a, ((0, 0), (0, D_MODEL - a.shape[1]))))
        return jnp.concatenate(rows, axis=0)

    parts = _all_gather_small(pack(lambda n: jnp.stack([grads[l][n] for l in range(depth)])), name="gather_small")
    small_packed = _adamw_small(pack(lambda n: small[n][0]), pack(lambda n: small[n][1]), pack(lambda n: small[n][2]),
                                parts, name="adamw_small")
    small_out = {}
    for i, n in enumerate(small):
        width = small[n][0].shape[1]
        small_out[n] = [o[depth * i:depth * (i + 1), :width] for o in small_packed]

    order = ["norm_pre", "norm_post", "w_in", "b_f", "w_pool", "pool_scale", "w_out", "w_pg", "w_pe"]
    result = [loss, grad_x]
    for kind in range(4):
        for n in order:
            result.append(big_out[n][kind] if n in big_out else small_out[n][kind])
    return tuple(result)
```

```python
import functools

import jax
import jax.numpy as jnp
from jax import lax
from jax.experimental import pallas as pl
from jax.experimental.pallas import tpu as pltpu

F32 = jnp.float32
BF16 = jnp.bfloat16

D_MODEL = 1024
N_HEADS = 16
HEAD_DIM = 64
PAIR = 2 * HEAD_DIM
N_PAIRS = N_HEADS // 2
N_GROUPS = 4
GROUP_DIM = 256
PLE_DIM = 256
MIX = 2 * D_MODEL
N_SEG = 6
F_OFF = 4 * D_MODEL
IN_COLS = N_SEG * D_MODEL + N_HEADS
LANES = 128
EPS = 1e-6
SCALE = 0.125
NEG = -1e30

ADAM_LR = 0.001
ADAM_B1 = 0.9
ADAM_B2 = 0.999
ADAM_EPS = 1e-08
ADAM_WD = 0.01
ADAM_STEP = 10

VMEM_LIMIT = 56 * 1024 * 1024
MESH = pl.DeviceIdType.MESH
ANY = pl.BlockSpec(memory_space=pl.ANY)

NT = (((1,), (1,)), ((), ()))
TN = (((0,), (0,)), ((), ()))
NN = (((1,), (0,)), ((), ()))


def _pcall(body, *, name, out_shape, grid=(), in_specs=None, out_specs=None, scratch_shapes=(), dims=None):
    kw = {}
    if in_specs is not None:
        kw["in_specs"] = in_specs
    if out_specs is not None:
        kw["out_specs"] = out_specs
    return pl.pallas_call(
        body, name=name, out_shape=out_shape, grid=grid, scratch_shapes=list(scratch_shapes),
        compiler_params=pltpu.CompilerParams(dimension_semantics=dims, vmem_limit_bytes=VMEM_LIMIT), **kw)


def _dot(a, b, dn=NN):
    return lax.dot_general(a, b, dn, preferred_element_type=F32)


def _sigmoid(x):
    return 1.0 / (1.0 + jnp.exp(-x))


def _matmul(a, b, *, ta=False, name, tm=512, tn=1024, tk=512):
    if ta:
        kdim, m = a.shape
    else:
        m, kdim = a.shape
    n = b.shape[1]
    tm, tn, tk = min(tm, m), min(tn, n), min(tk, kdim)
    nk = kdim // tk

    def body(a_ref, b_ref, o_ref, acc_ref):
        k = pl.program_id(2)

        @pl.when(k == 0)
        def _():
            acc_ref[...] = jnp.zeros_like(acc_ref)

        acc_ref[...] += _dot(a_ref[...].astype(BF16), b_ref[...].astype(BF16), TN if ta else NN)

        @pl.when(k == nk - 1)
        def _():
            o_ref[...] = acc_ref[...]

    a_spec = pl.BlockSpec((tk, tm), lambda i, j, k: (k, i)) if ta else pl.BlockSpec((tm, tk), lambda i, j, k: (i, k))
    return _pcall(
        body, name=name, out_shape=jax.ShapeDtypeStruct((m, n), F32), grid=(m // tm, n // tn, nk),
        in_specs=[a_spec, pl.BlockSpec((tk, tn), lambda i, j, k: (k, j))],
        out_specs=pl.BlockSpec((tm, tn), lambda i, j, k: (i, j)),
        scratch_shapes=[pltpu.VMEM((tm, tn), F32)], dims=("parallel", "parallel", "arbitrary"))(a, b)


def _norm_in_proj(h, g, wm, wf, *, tm=512):
    t = h.shape[0]

    def body(h_ref, g_ref, wm_ref, wf_ref, q_ref, k_ref, v_ref, za_ref, u_ref, zp_ref, fl_ref, hn_ref):
        j = pl.program_id(1)

        @pl.when(j == 0)
        def _():
            x = h_ref[...]
            r = lax.rsqrt(jnp.mean(x * x, axis=-1, keepdims=True) + EPS)
            hn = (x * r * g_ref[...]).astype(BF16)
            hn_ref[...] = hn
            fl_ref[...] = _dot(hn, wf_ref[...])

        y = _dot(hn_ref[...], wm_ref[...])
        for s, ref in enumerate((q_ref, k_ref, v_ref, za_ref, u_ref, zp_ref)):
            @pl.when(j == s)
            def _(ref=ref):
                ref[...] = y.astype(ref.dtype)

    row = lambda i, j: (i, 0)
    blk = pl.BlockSpec((tm, D_MODEL), row)
    sds = lambda dt: jax.ShapeDtypeStruct((t, D_MODEL), dt)
    return _pcall(
        body, name="norm_in_proj", grid=(t // tm, N_SEG),
        in_specs=[blk, pl.BlockSpec((1, D_MODEL), lambda i, j: (0, 0)),
                  pl.BlockSpec((D_MODEL, D_MODEL), lambda i, j: (0, j)),
                  pl.BlockSpec((D_MODEL, LANES), lambda i, j: (0, 0))],
        out_specs=[blk] * 6 + [pl.BlockSpec((tm, LANES), row), blk],
        out_shape=[sds(BF16)] * 3 + [sds(F32)] * 3 + [jax.ShapeDtypeStruct((t, LANES), F32), sds(BF16)],
        dims=("parallel", "arbitrary"))(h, g, wm, wf)


def _gate_cumsum(flt, bias):
    b, hh, s = flt.shape

    def body(fl_ref, b_ref, c_ref):
        x = fl_ref[...] + b_ref[...]
        acc = jnp.minimum(x, 0.0) - jnp.log(1.0 + jnp.exp(-jnp.abs(x)))
        idx = lax.broadcasted_iota(jnp.int32, x.shape, 1)
        sh = 1
        while sh < s:
            acc = acc + jnp.where(idx >= sh, pltpu.roll(acc, sh, 1), 0.0)
            sh *= 2
        c_ref[...] = acc

    return _pcall(
        body, name="gate_cumsum", grid=(b,),
        in_specs=[pl.BlockSpec((None, hh, s), lambda i: (i, 0, 0)), pl.BlockSpec((hh, 1), lambda i: (0, 0))],
        out_specs=pl.BlockSpec((None, hh, s), lambda i: (i, 0, 0)),
        out_shape=jax.ShapeDtypeStruct((b, hh, s), F32), dims=("parallel",))(flt, bias)


def _gate_cumsum_bwd(dcol, drow, flt, bias):
    b, hh, s = flt.shape

    def body(d_ref, dr_ref, fl_ref, b_ref, dfl_ref, db_ref):
        @pl.when(pl.program_id(0) == 0)
        def _():
            db_ref[...] = jnp.zeros_like(db_ref)

        x = fl_ref[...] + b_ref[...]
        acc = dr_ref[...] - d_ref[...]
        idx = lax.broadcasted_iota(jnp.int32, x.shape, 1)
        sh = 1
        while sh < s:
            acc = acc + jnp.where(idx + sh < s, pltpu.roll(acc, s - sh, 1), 0.0)
            sh *= 2
        e = jnp.exp(-jnp.abs(x))
        sig_neg = jnp.where(x >= 0.0, e, 1.0) / (1.0 + e)
        dfl = acc * sig_neg
        dfl_ref[...] = dfl
        db_ref[...] += jnp.sum(dfl, axis=1, keepdims=True)

    return _pcall(
        body, name="gate_cumsum_bwd", grid=(b,),
        in_specs=[pl.BlockSpec((None, hh, s), lambda i: (i, 0, 0))] * 3 + [pl.BlockSpec((hh, 1), lambda i: (0, 0))],
        out_specs=[pl.BlockSpec((None, hh, s), lambda i: (i, 0, 0)), pl.BlockSpec((hh, 1), lambda i: (0, 0))],
        out_shape=[jax.ShapeDtypeStruct((b, hh, s), F32), jax.ShapeDtypeStruct((hh, 1), F32)],
        dims=("arbitrary",))(dcol, drow, flt, bias)


def _attn_fwd(q, k, v, c4, *, nb, tq=512, rc=32, pv_rows=512):
    t = q.shape[0]
    s = t // nb
    tq = min(tq, s)
    nq = s // tq
    tk = tq
    pv_rows = min(pv_rows, tq)

    def body(q_ref, k_ref, v_ref, c_ref, o_ref, lse_ref, qh_scr, s_scr, p_scr, m_scr, acc_scr):
        qi = pl.program_id(2)
        lo = lax.broadcasted_iota(jnp.int32, (tq, PAIR), 1) < HEAD_DIM
        q2 = q_ref[...] * SCALE
        zero = jnp.zeros_like(q2)
        qh_scr[0] = jnp.where(lo, q2, zero)
        qh_scr[1] = jnp.where(lo, zero, q2)
        m_scr[...] = jnp.full(m_scr.shape, NEG, F32)
        acc_scr[...] = jnp.zeros(acc_scr.shape, F32)
        row = lax.broadcasted_iota(jnp.int32, (rc, tk), 0)
        col = lax.broadcasted_iota(jnp.int32, (rc, tk), 1)

        def scores(kj, slot):
            k2 = k_ref[pl.ds(pl.multiple_of(kj * tk, tk), tk), :]
            for hd in range(2):
                s_scr[slot, hd] = _dot(qh_scr[hd], k2, NT)

        def softmax_pv(kj, slot, masked):
            off = pl.multiple_of(kj * tk, tk)
            v2 = v_ref[pl.ds(off, tk), :]
            one = jnp.ones_like(v2)
            va = (jnp.where(lo, v2, one), jnp.where(lo, one, v2))
            crow = c_ref[:, pl.ds(off, tk)]
            for hd in range(2):
                for r0 in range(0, tq, pv_rows):
                    for r in range(r0, r0 + pv_rows, rc):
                        sc = s_scr[slot, hd, r:r + rc, :] - crow[hd:hd + 1, :]
                        if masked:
                            sc = jnp.where(row + r >= col, sc, NEG)
                        m_old = m_scr[hd, r:r + rc, :]
                        m_new = jnp.maximum(m_old, jnp.max(sc, axis=1, keepdims=True))
                        for cb in range(0, tk, LANES):
                            p_scr[hd, r:r + rc, cb:cb + LANES] = jnp.exp(sc[:, cb:cb + LANES] - m_new).astype(BF16)
                        m_scr[hd, r:r + rc, :] = m_new
                        acc_scr[hd, r:r + rc, :] = acc_scr[hd, r:r + rc, :] * jnp.exp(m_old - m_new)
                    acc_scr[hd, r0:r0 + pv_rows, :] += _dot(p_scr[hd, r0:r0 + pv_rows, :], va[hd])

        def unmasked(kj, carry):
            scores(kj, 0)
            softmax_pv(kj, 0, False)
            return carry

        lax.fori_loop(0, qi, unmasked, 0)
        scores(qi, 0)
        softmax_pv(qi, 0, True)
        a0, a1 = acc_scr[0], acc_scr[1]
        den = jnp.where(lo, pltpu.roll(a0, HEAD_DIM, 1), pltpu.roll(a1, HEAD_DIM, 1))
        o_ref[...] = jnp.where(lo, a0, a1) / den
        lse_ref[...] = jnp.where(lo, m_scr[0], m_scr[1]) + jnp.log(den)

    qspec = pl.BlockSpec((tq, PAIR), lambda b, hp, i: (b * nq + i, hp))
    kvspec = pl.BlockSpec((s, PAIR), lambda b, hp, i: (b, hp))
    return _pcall(
        body, name="attn_fwd", grid=(nb, N_PAIRS, nq),
        in_specs=[qspec, kvspec, kvspec, pl.BlockSpec((None, None, 2, s), lambda b, hp, i: (b, hp, 0, 0))],
        out_specs=[qspec, qspec],
        out_shape=[jax.ShapeDtypeStruct((t, D_MODEL), F32)] * 2,
        scratch_shapes=[pltpu.VMEM((2, tq, PAIR), BF16), pltpu.VMEM((1, 2, tq, tk), F32), pltpu.VMEM((2, tq, tk), BF16),
                        pltpu.VMEM((2, tq, LANES), F32), pltpu.VMEM((2, tq, PAIR), F32)],
        dims=("parallel", "parallel", "arbitrary"))(q, k, v, c4)


def _attn_bwd(q, k, v, do, ccol, lse4, delta4, *, nb, tk=512, rc=16):
    t = q.shape[0]
    s = t // nb
    tk = min(tk, s)
    nk = s // tk
    tq = tk

    def body(q_ref, do_ref, k_ref, v_ref, cc_ref, lse_ref, dl_ref, dq_ref, dk_ref, dv_ref, dc_ref, dr_ref,
             kz_scr, vz_scr, ko_scr, crep_scr, st_scr, dp_scr, pt_scr, ds_scr, dk_scr, dv_scr):
        kj = pl.program_id(2)

        @pl.when(kj == 0)
        def _():
            dq_ref[...] = jnp.zeros_like(dq_ref)
            dr_ref[...] = jnp.zeros_like(dr_ref)

        lo = lax.broadcasted_iota(jnp.int32, (tk, PAIR), 1) < HEAD_DIM
        k2 = k_ref[...]
        v2 = v_ref[...]
        zero = jnp.zeros_like(k2)
        one = jnp.ones_like(k2)
        kz_scr[0] = jnp.where(lo, k2, zero)
        kz_scr[1] = jnp.where(lo, zero, k2)
        vz_scr[0] = jnp.where(lo, v2, zero)
        vz_scr[1] = jnp.where(lo, zero, v2)
        ko_scr[0] = jnp.where(lo, k2, one)
        ko_scr[1] = jnp.where(lo, one, k2)
        cc = cc_ref[...]
        cc_sw = pltpu.roll(cc, HEAD_DIM, 1)
        crep_scr[0] = jnp.where(lo, cc, cc_sw)
        crep_scr[1] = jnp.where(lo, cc_sw, cc)
        dk_scr[...] = jnp.zeros(dk_scr.shape, F32)
        dv_scr[...] = jnp.zeros(dv_scr.shape, F32)
        row = lax.broadcasted_iota(jnp.int32, (rc, LANES), 0)
        col = lax.broadcasted_iota(jnp.int32, (rc, LANES), 1)

        def step(qi, masked):
            off = pl.multiple_of(qi * tq, tq)
            q2 = q_ref[pl.ds(off, tq), :] * SCALE
            do2 = do_ref[pl.ds(off, tq), :]
            lse = lse_ref[:, pl.ds(off, tq)]
            dl = dl_ref[:, pl.ds(off, tq)]
            qo = (jnp.where(lo, q2, jnp.ones_like(q2)), jnp.where(lo, jnp.ones_like(q2), q2))
            for hd in range(2):
                st_scr[hd] = _dot(kz_scr[hd], q2, NT)
                dp_scr[hd] = _dot(vz_scr[hd], do2, NT)
            dqs = []
            for hd in range(2):
                for r in range(0, tk, rc):
                    c_rep = crep_scr[hd, r:r + rc, :]
                    for cb in range(0, tq, LANES):
                        pt = jnp.exp(st_scr[hd, r:r + rc, cb:cb + LANES] - c_rep - lse[hd:hd + 1, cb:cb + LANES])
                        if masked:
                            pt = jnp.where(col + cb >= row + r, pt, 0.0)
                        dst = pt * (dp_scr[hd, r:r + rc, cb:cb + LANES] - dl[hd:hd + 1, cb:cb + LANES])
                        pt_scr[hd, r:r + rc, cb:cb + LANES] = pt.astype(BF16)
                        ds_scr[hd, r:r + rc, cb:cb + LANES] = dst.astype(BF16)
                dv_scr[hd] += _dot(pt_scr[hd], do2)
                dk_scr[hd] += _dot(ds_scr[hd], qo[hd])
                dqs.append(_dot(ds_scr[hd], ko_scr[hd], TN))
            dq_ref[pl.ds(off, tq), :] += jnp.where(lo, dqs[0], dqs[1]) * SCALE
            dr_ref[pl.ds(off, tq), :] += jnp.where(lo, dqs[1], dqs[0])

        step(kj, True)

        def unmasked(qi, carry):
            step(qi, False)
            return carry

        lax.fori_loop(kj + 1, nk, unmasked, 0)
        dk_ref[...] = jnp.where(lo, dk_scr[0], dk_scr[1]).astype(BF16)
        dc_ref[...] = jnp.where(lo, dk_scr[1], dk_scr[0])
        dv_ref[...] = jnp.where(lo, dv_scr[0], dv_scr[1]).astype(BF16)

    full = pl.BlockSpec((s, PAIR), lambda b, hp, j: (b, hp))
    kblk = pl.BlockSpec((tk, PAIR), lambda b, hp, j: (b * nk + j, hp))
    rows = pl.BlockSpec((None, None, 2, s), lambda b, hp, j: (b, hp, 0, 0))
    sds = lambda dt: jax.ShapeDtypeStruct((t, D_MODEL), dt)
    pair_bf = pltpu.VMEM((2, tk, PAIR), BF16)
    pair_f = pltpu.VMEM((2, tk, PAIR), F32)
    return _pcall(
        body, name="attn_bwd", grid=(nb, N_PAIRS, nk),
        in_specs=[full, full, kblk, kblk, kblk, rows, rows],
        out_specs=[full, kblk, kblk, kblk, full],
        out_shape=[sds(F32), sds(BF16), sds(BF16), sds(F32), sds(F32)],
        scratch_shapes=[pair_bf, pair_bf, pair_bf, pair_f, pltpu.VMEM((2, tk, tq), F32), pltpu.VMEM((2, tk, tq), F32),
                        pltpu.VMEM((2, tk, tq), BF16), pltpu.VMEM((2, tk, tq), BF16), pair_f, pair_f],
        dims=("parallel", "parallel", "arbitrary"))(q, do, k, v, ccol, lse4, delta4)


def _window_sum(x, g, s, *, ahead):
    row = lax.broadcasted_iota(jnp.int32, x.shape, 0)
    for step in range(N_GROUPS):
        sh = 1 << step
        if ahead:
            moved = jnp.where(row + sh < s, pltpu.roll(x, s - sh, 0), 0.0)
        else:
            moved = jnp.where(row >= sh, pltpu.roll(x, sh, 0), 0.0)
        x = jnp.where(step <= g, x + moved, x)
    return x


def _inv_count(g, s):
    pos = lax.broadcasted_iota(jnp.int32, (s, 1), 0) + 1
    return 1.0 / jnp.minimum(pos, 2 << g).astype(F32)


def _pool_fwd(u, zp, wpool, ps, *, nb):
    t = u.shape[0]
    s = t // nb

    def body(u_ref, zp_ref, w_ref, ps_ref, out_ref, pooled_ref, mixed_ref):
        g = pl.program_id(1)
        x = u_ref[...]
        pooled = (_window_sum(x, g, s, ahead=False) * _inv_count(g, s) - x).astype(BF16)
        mixed = _dot(pooled, w_ref[...])
        z = zp_ref[...]
        pooled_ref[...] = pooled
        mixed_ref[...] = mixed
        out_ref[...] = (mixed * ps_ref[...] * (z * _sigmoid(z))).astype(BF16)

    blk = pl.BlockSpec((s, GROUP_DIM), lambda b, g: (b, g))
    sds = lambda dt: jax.ShapeDtypeStruct((t, D_MODEL), dt)
    return _pcall(
        body, name="pool_fwd", grid=(nb, N_GROUPS),
        in_specs=[blk, blk, pl.BlockSpec((None, GROUP_DIM, GROUP_DIM), lambda b, g: (g, 0, 0)),
                  pl.BlockSpec((1, GROUP_DIM), lambda b, g: (0, g))],
        out_specs=[blk, blk, blk], out_shape=[sds(BF16), sds(BF16), sds(F32)],
        dims=("parallel", "parallel"))(u, zp, wpool, ps)


def _pool_bwd(dpool, mixed, zp, wpool, ps, *, nb):
    t = dpool.shape[0]
    s = t // nb

    def body(dp_ref, mx_ref, zp_ref, w_ref, ps_ref, dmx_ref, du_ref, dzp_ref, dps_ref):
        g = pl.program_id(0)

        @pl.when(pl.program_id(1) == 0)
        def _():
            dps_ref[...] = jnp.zeros_like(dps_ref)

        dp = dp_ref[...]
        mixed = mx_ref[...]
        z = zp_ref[...]
        scale = ps_ref[...]
        sg = _sigmoid(z)
        silu = z * sg
        dmixed = (dp * scale * silu).astype(BF16)
        dmx_ref[...] = dmixed
        dps_ref[...] += jnp.sum(dp * mixed * silu, axis=0, keepdims=True)
        dzp_ref[...] = (dp * mixed * scale * (sg * (1.0 + z * (1.0 - sg)))).astype(BF16)
        dpooled = _dot(dmixed, w_ref[...], NT)
        du = _window_sum(dpooled * _inv_count(g, s), g, s, ahead=True) - dpooled
        du_ref[...] = du.astype(BF16)

    blk = pl.BlockSpec((s, GROUP_DIM), lambda g, b: (b, g))
    sds = lambda dt: jax.ShapeDtypeStruct((t, D_MODEL), dt)
    return _pcall(
        body, name="pool_bwd", grid=(N_GROUPS, nb),
        in_specs=[blk, blk, blk, pl.BlockSpec((None, GROUP_DIM, GROUP_DIM), lambda g, b: (g, 0, 0)),
                  pl.BlockSpec((1, GROUP_DIM), lambda g, b: (0, g))],
        out_specs=[blk, blk, blk, pl.BlockSpec((1, GROUP_DIM), lambda g, b: (0, g))],
        out_shape=[sds(BF16), sds(BF16), sds(BF16), jax.ShapeDtypeStruct((1, D_MODEL), F32)],
        dims=("parallel", "arbitrary"))(dpool, mixed, zp, wpool, ps)


def _pool_wgrad(pooled, dmixed, *, tk=1024):
    t = pooled.shape[0]
    nk = t // tk

    def body(a_ref, b_ref, o_ref):
        @pl.when(pl.program_id(1) == 0)
        def _():
            o_ref[...] = jnp.zeros_like(o_ref)

        o_ref[...] += _dot(a_ref[...], b_ref[...], TN)

    blk = pl.BlockSpec((tk, GROUP_DIM), lambda g, k: (k, g))
    return _pcall(
        body, name="pool_wgrad", grid=(N_GROUPS, nk), in_specs=[blk, blk],
        out_specs=pl.BlockSpec((None, GROUP_DIM, GROUP_DIM), lambda g, k: (g, 0, 0)),
        out_shape=jax.ShapeDtypeStruct((N_GROUPS, GROUP_DIM, GROUP_DIM), F32),
        dims=("parallel", "arbitrary"))(pooled, dmixed)


def _out_block(h, o, za, pool, p, wo, wpg, wpe, gpost, *, tm=256):
    t = h.shape[0]

    def body(h_ref, o_ref, za_ref, pool_ref, p_ref, wo_ref, wpg_ref, wpe_ref, g_ref,
             h2_ref, mix_ref, h1_ref, gate_ref, pe_ref, cat_ref):
        z = za_ref[...]
        a = (o_ref[...] * (z * _sigmoid(z))).astype(BF16)
        pool_v = pool_ref[...]
        cat_ref[:, :D_MODEL] = a
        cat_ref[:, D_MODEL:] = pool_v
        mix = _dot(a, wo_ref[:D_MODEL, :]) + _dot(pool_v, wo_ref[D_MODEL:, :])
        r = lax.rsqrt(jnp.mean(mix * mix, axis=-1, keepdims=True) + EPS)
        h1 = h_ref[...] + mix * r * g_ref[...]
        gate = _sigmoid(_dot(h1.astype(BF16), wpg_ref[...]))
        pe = _dot(p_ref[...].astype(BF16), wpe_ref[...])
        mix_ref[...] = mix
        h1_ref[...] = h1
        gate_ref[...] = gate
        pe_ref[...] = pe
        h2_ref[...] = h1 + gate * pe

    row = lambda i: (i, 0)
    const = lambda i: (0, 0)
    blk = pl.BlockSpec((tm, D_MODEL), row)
    sds = jax.ShapeDtypeStruct((t, D_MODEL), F32)
    return _pcall(
        body, name="out_block", grid=(t // tm,),
        in_specs=[blk, blk, blk, blk, pl.BlockSpec((tm, PLE_DIM), row),
                  pl.BlockSpec((MIX, D_MODEL), const), pl.BlockSpec((D_MODEL, D_MODEL), const),
                  pl.BlockSpec((PLE_DIM, D_MODEL), const), pl.BlockSpec((1, D_MODEL), const)],
        out_specs=[blk] * 5 + [pl.BlockSpec((tm, MIX), row)],
        out_shape=[sds] * 5 + [jax.ShapeDtypeStruct((t, MIX), BF16)],
        dims=("parallel",))(h, o, za, pool, p, wo, wpg, wpe, gpost)


def _loss_grad(y, target, *, tm=512):
    t = y.shape[0]

    def body(y_ref, t_ref, dy_ref, loss_ref):
        @pl.when(pl.program_id(0) == 0)
        def _():
            loss_ref[...] = jnp.zeros_like(loss_ref)

        err = y_ref[...] - t_ref[...]
        dy_ref[...] = err * (1.0 / D_MODEL)
        part = jnp.sum(jnp.sum(err * err, axis=1, keepdims=True), axis=0, keepdims=True)
        loss_ref[...] += part * (0.5 / D_MODEL)

    blk = pl.BlockSpec((tm, D_MODEL), lambda i: (i, 0))
    return _pcall(
        body, name="loss_grad", grid=(t // tm,), in_specs=[blk, blk],
        out_specs=[blk, pl.BlockSpec((8, LANES), lambda i: (0, 0))],
        out_shape=[jax.ShapeDtypeStruct((t, D_MODEL), F32), jax.ShapeDtypeStruct((8, LANES), F32)],
        dims=("arbitrary",))(y, target)


def _out_block_bwd(dh2, gate, pe, mix, o, za, wpg, wo, gpost, gsum, *, tm=256):
    t = dh2.shape[0]

    def body(dh2_ref, gate_ref, pe_ref, mix_ref, o_ref, za_ref, wpg_ref, wo_ref, g_ref, gs_ref,
             dh1_ref, dgp_ref, dpe_ref, dmix_ref, do_ref, dza_ref, dpool_ref, delta_ref, dg_ref):
        @pl.when(pl.program_id(0) == 0)
        def _():
            dg_ref[...] = jnp.zeros_like(dg_ref)

        dh2 = dh2_ref[...]
        gate = gate_ref[...]
        dpe_ref[...] = (dh2 * gate).astype(BF16)
        dgp = (dh2 * pe_ref[...] * gate * (1.0 - gate)).astype(BF16)
        dgp_ref[...] = dgp
        dh1 = dh2 + _dot(dgp, wpg_ref[...], NT)
        dh1_ref[...] = dh1
        mix = mix_ref[...]
        r = lax.rsqrt(jnp.mean(mix * mix, axis=-1, keepdims=True) + EPS)
        dg_ref[...] += jnp.sum(dh1 * mix * r, axis=0, keepdims=True)
        a = dh1 * g_ref[...]
        dmix = (r * a - mix * (r * r * r) * jnp.mean(a * mix, axis=-1, keepdims=True)).astype(BF16)
        dmix_ref[...] = dmix
        dattn = _dot(dmix, wo_ref[:D_MODEL, :], NT)
        dpool_ref[...] = _dot(dmix, wo_ref[D_MODEL:, :], NT)
        z = za_ref[...]
        sg = _sigmoid(z)
        o = o_ref[...]
        do = (dattn * (z * sg)).astype(BF16)
        do_ref[...] = do
        dza_ref[...] = (dattn * o * (sg * (1.0 + z * (1.0 - sg)))).astype(BF16)
        prod = do.astype(F32) * o
        hi = prod.astype(BF16)
        rest = (prod - hi.astype(F32)).astype(BF16)
        delta_ref[...] = _dot(hi, gs_ref[...]) + _dot(rest, gs_ref[...])

    row = lambda i: (i, 0)
    const = lambda i: (0, 0)
    blk = pl.BlockSpec((tm, D_MODEL), row)
    sds = lambda dt: jax.ShapeDtypeStruct((t, D_MODEL), dt)
    return _pcall(
        body, name="out_block_bwd", grid=(t // tm,),
        in_specs=[blk] * 6 + [pl.BlockSpec((D_MODEL, D_MODEL), const), pl.BlockSpec((MIX, D_MODEL), const),
                              pl.BlockSpec((1, D_MODEL), const), pl.BlockSpec((D_MODEL, LANES), const)],
        out_specs=[blk] * 7 + [pl.BlockSpec((tm, LANES), row), pl.BlockSpec((1, D_MODEL), const)],
        out_shape=[sds(F32)] + [sds(BF16)] * 5 + [sds(F32), jax.ShapeDtypeStruct((t, LANES), F32),
                                                 jax.ShapeDtypeStruct((1, D_MODEL), F32)],
        dims=("arbitrary",))(dh2, gate, pe, mix, o, za, wpg, wo, gpost, gsum)


def _in_proj_bwd(dsegs, dfl, wm, wf, h, dh1, gpre, *, tm=512):
    t = h.shape[0]

    def body(*refs):
        seg_refs = refs[:N_SEG]
        dfl_ref, wm_ref, wf_ref, h_ref, dh1_ref, g_ref, dh_ref, dg_ref, acc_ref = refs[N_SEG:]
        i = pl.program_id(0)
        j = pl.program_id(1)

        @pl.when(jnp.logical_and(i == 0, j == 0))
        def _():
            dg_ref[...] = jnp.zeros_like(dg_ref)

        @pl.when(j == 0)
        def _():
            acc_ref[...] = _dot(dfl_ref[...].astype(BF16), wf_ref[...], NT)

        for sgi, ref in enumerate(seg_refs):
            @pl.when(j == sgi)
            def _(ref=ref):
                acc_ref[...] += _dot(ref[...].astype(BF16), wm_ref[...], NT)

        @pl.when(j == N_SEG - 1)
        def _():
            dhn = acc_ref[...]
            x = h_ref[...]
            r = lax.rsqrt(jnp.mean(x * x, axis=-1, keepdims=True) + EPS)
            dg_ref[...] += jnp.sum(dhn * x * r, axis=0, keepdims=True)
            a = dhn * g_ref[...]
            dh_ref[...] = dh1_ref[...] + r * a - x * (r * r * r) * jnp.mean(a * x, axis=-1, keepdims=True)

    row = lambda i, j: (i, 0)
    const = lambda i, j: (0, 0)
    blk = pl.BlockSpec((tm, D_MODEL), row)
    return _pcall(
        body, name="in_proj_bwd", grid=(t // tm, N_SEG),
        in_specs=[blk] * N_SEG + [pl.BlockSpec((tm, LANES), row), pl.BlockSpec((D_MODEL, D_MODEL), lambda i, j: (0, j)),
                                  pl.BlockSpec((D_MODEL, LANES), const), blk, blk, pl.BlockSpec((1, D_MODEL), const)],
        out_specs=[blk, pl.BlockSpec((1, D_MODEL), const)],
        out_shape=[jax.ShapeDtypeStruct((t, D_MODEL), F32), jax.ShapeDtypeStruct((1, D_MODEL), F32)],
        scratch_shapes=[pltpu.VMEM((tm, D_MODEL), F32)],
        dims=("arbitrary", "arbitrary"))(*dsegs, dfl, wm, wf, h, dh1, gpre)


def _chip_exchange(arrays, *, per_target, name):
    n = len(arrays)
    shapes = [a.shape[1:] if per_target else a.shape for a in arrays]

    def body(*refs):
        ins, outs = refs[:n], refs[n:2 * n]
        send, recv, loc = refs[2 * n:]
        x, y, c = lax.axis_index("x"), lax.axis_index("y"), lax.axis_index("c")
        me = 2 * x + y
        peers = [(1 - x, y), (x, 1 - y), (1 - x, 1 - y)]

        def src(kk, chip):
            return ins[kk].at[chip] if per_target else ins[kk]

        def remote(kk, pi, slot):
            px, py = peers[pi]
            return pltpu.make_async_remote_copy(
                src_ref=src(kk, 2 * px + py), dst_ref=outs[kk].at[slot], send_sem=send.at[kk * 3 + pi],
                recv_sem=recv.at[kk * 3 + pi], device_id=(px, py, c), device_id_type=MESH)

        local = [pltpu.make_async_copy(src(kk, me), outs[kk].at[me], loc.at[kk]) for kk in range(n)]
        for cp in local:
            cp.start()
        sends = [remote(kk, pi, me) for kk in range(n) for pi in range(3)]
        for cp in sends:
            cp.start()
        for kk in range(n):
            for pi, (px, py) in enumerate(peers):
                remote(kk, pi, 2 * px + py).wait_recv()
        for cp in sends:
            cp.wait_send()
        for cp in local:
            cp.wait()

    return _pcall(
        body, name=name, in_specs=[ANY] * n, out_specs=[ANY] * n,
        out_shape=[jax.ShapeDtypeStruct((4,) + tuple(sh), a.dtype) for sh, a in zip(shapes, arrays)],
        scratch_shapes=[pltpu.SemaphoreType.DMA((3 * n,)), pltpu.SemaphoreType.DMA((3 * n,)),
                        pltpu.SemaphoreType.DMA((n,))])(*arrays)


def _sibling_exchange(arrays, *, name):
    n = len(arrays)

    def body(*refs):
        ins, outs = refs[:n], refs[n:2 * n]
        send, recv = refs[2 * n:]
        sib = (lax.axis_index("x"), lax.axis_index("y"), 1 - lax.axis_index("c"))
        copies = [pltpu.make_async_remote_copy(src_ref=ins[kk], dst_ref=outs[kk], send_sem=send.at[kk],
                                               recv_sem=recv.at[kk], device_id=sib, device_id_type=MESH)
                  for kk in range(n)]
        for cp in copies:
            cp.start()
        for cp in copies:
            cp.wait_recv()
        for cp in copies:
            cp.wait_send()

    return _pcall(
        body, name=name, in_specs=[ANY] * n, out_specs=[ANY] * n,
        out_shape=[jax.ShapeDtypeStruct(a.shape, a.dtype) for a in arrays],
        scratch_shapes=[pltpu.SemaphoreType.DMA((n,)), pltpu.SemaphoreType.DMA((n,))])(*arrays)


def _all_gather_small(a, *, name):
    def body(a_ref, o_ref, send, recv, loc):
        x, y, c = lax.axis_index("x"), lax.axis_index("y"), lax.axis_index("c")
        me = 4 * x + 2 * y + c

        def peer(rel):
            fx, fy, fc = (rel >> 2) & 1, (rel >> 1) & 1, rel & 1
            px = (1 - x) if fx else x
            py = (1 - y) if fy else y
            pc = (1 - c) if fc else c
            return px, py, pc

        def remote(rel, slot):
            return pltpu.make_async_remote_copy(
                src_ref=a_ref, dst_ref=o_ref.at[slot], send_sem=send.at[rel - 1], recv_sem=recv.at[rel - 1],
                device_id=peer(rel), device_id_type=MESH)

        mine = pltpu.make_async_copy(a_ref, o_ref.at[me], loc)
        mine.start()
        sends = [remote(rel, me) for rel in range(1, 8)]
        for cp in sends:
            cp.start()
        for rel in range(1, 8):
            px, py, pc = peer(rel)
            remote(rel, 4 * px + 2 * py + pc).wait_recv()
        for cp in sends:
            cp.wait_send()
        mine.wait()

    return _pcall(
        body, name=name, in_specs=[ANY], out_specs=ANY,
        out_shape=jax.ShapeDtypeStruct((8,) + a.shape, a.dtype),
        scratch_shapes=[pltpu.SemaphoreType.DMA((7,)), pltpu.SemaphoreType.DMA((7,)), pltpu.SemaphoreType.DMA])(a)


def _sum_slots(r, *, name, tr=256):
    _, rows, cols = r.shape
    tr = min(tr, rows)

    def body(r_ref, o_ref):
        acc = r_ref[0].astype(F32)
        for slot in range(1, 4):
            acc = acc + r_ref[slot].astype(F32)
        o_ref[...] = acc

    return _pcall(
        body, name=name, grid=(rows // tr,), in_specs=[pl.BlockSpec((4, tr, cols), lambda i: (0, i, 0))],
        out_specs=pl.BlockSpec((tr, cols), lambda i: (i, 0)),
        out_shape=jax.ShapeDtypeStruct((rows, cols), F32), dims=("parallel",))(r)


def _adamw_math(w, g, m, v):
    m = ADAM_B1 * m + (1.0 - ADAM_B1) * g
    v = ADAM_B2 * v + (1.0 - ADAM_B2) * (g * g)
    m_hat = m / (1.0 - ADAM_B1 ** ADAM_STEP)
    v_hat = v / (1.0 - ADAM_B2 ** ADAM_STEP)
    delta = -ADAM_LR * (m_hat / (jnp.sqrt(v_hat) + ADAM_EPS) + ADAM_WD * w)
    return delta, m, v


def _adamw(w, m, v, g_mine, g_theirs, *, name, tr=128):
    rows, cols = w.shape
    tr = min(tr, rows)

    def body(w_ref, m_ref, v_ref, gs_ref, gn_ref, g_ref, d_ref, mo_ref, vo_ref):
        g = gs_ref[...] + gn_ref[...]
        delta, m_new, v_new = _adamw_math(w_ref[...], g, m_ref[...], v_ref[...])
        g_ref[...] = g
        d_ref[...] = delta
        mo_ref[...] = m_new
        vo_ref[...] = v_new

    blk = pl.BlockSpec((tr, cols), lambda i: (i, 0))
    return _pcall(
        body, name=name, grid=(rows // tr,), in_specs=[blk] * 5, out_specs=[blk] * 4,
        out_shape=[jax.ShapeDtypeStruct((rows, cols), F32)] * 4, dims=("parallel",))(w, m, v, g_mine, g_theirs)


def _adamw_small(w, m, v, parts, *, name):
    def body(w_ref, m_ref, v_ref, p_ref, g_ref, d_ref, mo_ref, vo_ref):
        g = p_ref[0]
        for dev in range(1, 8):
            g = g + p_ref[dev]
        delta, m_new, v_new = _adamw_math(w_ref[...], g, m_ref[...], v_ref[...])
        g_ref[...] = g
        d_ref[...] = delta
        mo_ref[...] = m_new
        vo_ref[...] = v_new

    return _pcall(body, name=name, out_shape=[jax.ShapeDtypeStruct(w.shape, F32)] * 4)(w, m, v, parts)


def _rows_from_heads(a, nb, s):
    return a.reshape(nb, s, N_HEADS).transpose(0, 2, 1).reshape(nb, N_PAIRS, 2, s)


def _layer_fwd(h, p_l, wts, nb):
    t = h.shape[0]
    s = t // nb
    q, k, v, za, u, zp, fl, hn = _norm_in_proj(h, wts["gpre"], wts["wm"], wts["wf"])
    flt = fl[:, :N_HEADS].reshape(nb, s, N_HEADS).transpose(0, 2, 1)
    c = _gate_cumsum(flt, wts["bf"])
    o, lse = _attn_fwd(q, k, v, c.reshape(nb, N_PAIRS, 2, s), nb=nb)
    pool, pooled, mixed = _pool_fwd(u, zp, wts["wpool"], wts["ps"], nb=nb)
    h2, mix, h1, gate, pe, cat = _out_block(h, o, za, pool, p_l, wts["wo"], wts["wpg"], wts["wpe"], wts["gpost"])
    saved = dict(h=h, hn=hn, q=q, k=k, v=v, za=za, zp=zp, flt=flt, c=c, o=o, lse=lse, pooled=pooled, mixed=mixed,
                 mix=mix, h1=h1, gate=gate, pe=pe, cat=cat, p=p_l)
    return h2, saved


def _layer_bwd(dh2, sv, wts, gsum, nb):
    t = dh2.shape[0]
    s = t // nb
    dh1, dgp, dpe, dmix, do, dza, dpool, delta, dgpost = _out_block_bwd(
        dh2, sv["gate"], sv["pe"], sv["mix"], sv["o"], sv["za"], wts["wpg"], wts["wo"], wts["gpost"], gsum)
    g_wpe = _matmul(sv["p"], dpe, ta=True, name="wgrad_pe")
    g_wpg = _matmul(sv["h1"], dgp, ta=True, name="wgrad_pg")
    g_wo = _matmul(sv["cat"], dmix, ta=True, name="wgrad_out")

    ccol = jnp.repeat(sv["c"].transpose(0, 2, 1).reshape(t, N_HEADS), HEAD_DIM, axis=1)
    lse4 = _rows_from_heads(sv["lse"][:, ::HEAD_DIM], nb, s)
    delta4 = _rows_from_heads(delta[:, :N_HEADS], nb, s)
    dq, dk, dv, dcol, drow = _attn_bwd(sv["q"], sv["k"], sv["v"], do, ccol, lse4, delta4, nb=nb)

    def head_rows(a):
        a = a[:, ::HEAD_DIM].reshape(nb, s, N_PAIRS, 2)[..., ::-1]
        return a.reshape(nb, s, N_HEADS).transpose(0, 2, 1)

    dflt, dbf = _gate_cumsum_bwd(head_rows(dcol), head_rows(drow), sv["flt"], wts["bf"])
    dfl = jnp.pad(dflt.transpose(0, 2, 1).reshape(t, N_HEADS), ((0, 0), (0, LANES - N_HEADS)))

    dmixed, du, dzp, dps = _pool_bwd(dpool, sv["mixed"], sv["zp"], wts["wpool"], wts["ps"], nb=nb)
    g_wpool = _pool_wgrad(sv["pooled"], dmixed)

    dsegs = (dq, dk, dv, dza, du, dzp)
    dh, dgpre = _in_proj_bwd(dsegs, dfl, wts["wm"], wts["wf"], sv["h"], dh1, wts["gpre"])
    g_segs = [_matmul(sv["hn"], dx, ta=True, name="wgrad_in_f32" if dx.dtype == F32 else "wgrad_in")
              for dx in dsegs]
    g_wf = _matmul(sv["hn"], dfl, ta=True, name="wgrad_in_f")
    g_win = jnp.concatenate(g_segs[:4] + [g_wf[:, :N_HEADS]] + g_segs[4:], axis=1)
    grads = dict(w_in=g_win, w_out=g_wo, w_pg=g_wpg, w_pe=g_wpe, w_pool=g_wpool,
                 norm_pre=dgpre[0], norm_post=dgpost[0], pool_scale=dps[0], b_f=dbf[:, 0])
    return dh, grads


def kernel(x, p, norm_pre, norm_post, w_in, b_f, w_pool, pool_scale, w_out, w_pg, w_pe, loss_target, m_norm_pre, m_norm_post, m_w_in, m_b_f, m_w_pool, m_pool_scale, m_w_out, m_w_pg, m_w_pe, v_norm_pre, v_norm_post, v_w_in, v_b_f, v_w_pool, v_pool_scale, v_w_out, v_w_pg, v_w_pe):
    nb, s, _ = x.shape
    t = nb * s
    depth = w_in.shape[0]
    shard_cols = w_in.shape[2]
    big = dict(w_in=(w_in, m_w_in, v_w_in), w_out=(w_out, m_w_out, v_w_out), w_pg=(w_pg, m_w_pg, v_w_pg),
               w_pe=(w_pe, m_w_pe, v_w_pe), w_pool=(w_pool, m_w_pool, v_w_pool))
    names = list(big)

    gathered = dict(zip(names, _chip_exchange([big[n][0].astype(BF16) for n in names], per_target=False,
                                              name="gather_weights")))
    layers = []
    for l in range(depth):
        win = jnp.concatenate([gathered["w_in"][j, l] for j in range(4)], axis=1)
        wm = jnp.concatenate([win[:, :F_OFF], win[:, F_OFF + N_HEADS:]], axis=1)
        wf = jnp.pad(win[:, F_OFF:F_OFF + N_HEADS], ((0, 0), (0, LANES - N_HEADS)))
        layers.append(dict(
            wm=wm, wf=wf,
            wo=gathered["w_out"][:, l].reshape(MIX, D_MODEL),
            wpg=gathered["w_pg"][:, l].reshape(D_MODEL, D_MODEL),
            wpe=jnp.concatenate([gathered["w_pe"][j, l] for j in range(4)], axis=1),
            wpool=gathered["w_pool"][:, l].transpose(1, 0, 2, 3).reshape(N_GROUPS, GROUP_DIM, GROUP_DIM),
            gpre=norm_pre[l][None], gpost=norm_post[l][None], ps=pool_scale[l][None], bf=b_f[l][:, None]))

    h = x.reshape(t, D_MODEL)
    saved = []
    for l in range(depth):
        h, sv = _layer_fwd(h, p[l].reshape(t, PLE_DIM), layers[l], nb)
        saved.append(sv)
    dh, loss_blk = _loss_grad(h, loss_target.reshape(t, D_MODEL))
    loss = lax.psum(loss_blk[0, 0], ("x", "y", "c"))
    gsum = (jnp.arange(D_MODEL)[:, None] // HEAD_DIM == jnp.arange(LANES)[None, :]).astype(BF16)
    grads = [None] * depth
    for l in reversed(range(depth)):
        dh, grads[l] = _layer_bwd(dh, saved[l], layers[l], gsum, nb)
    grad_x = dh.reshape(nb, s, D_MODEL)

    def pieces(name):
        full = jnp.stack([grads[l][name] for l in range(depth)])
        if name == "w_in":
            return jnp.stack([full[:, :, j * shard_cols:(j + 1) * shard_cols] for j in range(4)])
        if name == "w_out":
            return full.reshape(depth, 4, MIX // 4, D_MODEL).transpose(1, 0, 2, 3)
        if name == "w_pg":
            return full.reshape(depth, 4, D_MODEL // 4, D_MODEL).transpose(1, 0, 2, 3)
        if name == "w_pe":
            return jnp.stack([full[:, :, j * (D_MODEL // 4):(j + 1) * (D_MODEL // 4)] for j in range(4)])
        return full.reshape(depth, N_GROUPS, 4, GROUP_DIM // 4, GROUP_DIM).transpose(2, 0, 1, 3, 4)

    received = _chip_exchange([pieces(n).astype(BF16) for n in names], per_target=True, name="scatter_grads")
    sums = []
    for n, r in zip(names, received):
        cols = r.shape[-1]
        sums.append(_sum_slots(r.reshape(4, -1, cols), name="sum_" + n))
    other = _sibling_exchange(sums, name="swap_sums")
    big_out = {}
    for n, mine, theirs in zip(names, sums, other):
        w, m, v = big[n]
        cols = w.shape[-1]
        outs = _adamw(w.reshape(-1, cols), m.reshape(-1, cols), v.reshape(-1, cols), mine, theirs, name="adamw_" + n)
        big_out[n] = [o.reshape(w.shape) for o in outs]

    small = dict(norm_pre=(norm_pre, m_norm_pre, v_norm_pre), norm_post=(norm_post, m_norm_post, v_norm_post),
                 pool_scale=(pool_scale, m_pool_scale, v_pool_scale), b_f=(b_f, m_b_f, v_b_f))

    def pack(get):
        rows = []
        for n in small:
            a = get(n)
            rows.append(jnp.pad(a, ((0, 0), (0, D_MODEL - a.shape[1]))))
        return jnp.concatenate(rows, axis=0)

    parts = _all_gather_small(pack(lambda n: jnp.stack([grads[l][n] for l in range(depth)])), name="gather_small")
    small_packed = _adamw_small(pack(lambda n: small[n][0]), pack(lambda n: small[n][1]), pack(lambda n: small[n][2]),
                                parts, name="adamw_small")
    small_out = {}
    for i, n in enumerate(small):
        width = small[n][0].shape[1]
        small_out[n] = [o[depth * i:depth * (i + 1), :width] for o in small_packed]

    order = ["norm_pre", "norm_post", "w_in", "b_f", "w_pool", "pool_scale", "w_out", "w_pg", "w_pe"]
    result = [loss, grad_x]
    for kind in range(4):
        for n in order:
            result.append(big_out[n][kind] if n in big_out else small_out[n][kind])
    return tuple(result)
```

```python
import functools

import jax
import jax.numpy as jnp
from jax import lax
from jax.experimental import pallas as pl
from jax.experimental.pallas import tpu as pltpu

F32 = jnp.float32
BF16 = jnp.bfloat16

D_MODEL = 1024
N_HEADS = 16
HEAD_DIM = 64
PAIR = 2 * HEAD_DIM
N_PAIRS = N_HEADS // 2
N_GROUPS = 4
GROUP_DIM = 256
PLE_DIM = 256
MIX = 2 * D_MODEL
N_SEG = 6
F_OFF = 4 * D_MODEL
IN_COLS = N_SEG * D_MODEL + N_HEADS
LANES = 128
EPS = 1e-6
SCALE = 0.125
NEG = -1e30

ADAM_LR = 0.001
ADAM_B1 = 0.9
ADAM_B2 = 0.999
ADAM_EPS = 1e-08
ADAM_WD = 0.01
ADAM_STEP = 10

VMEM_LIMIT = 56 * 1024 * 1024
MESH = pl.DeviceIdType.MESH
ANY = pl.BlockSpec(memory_space=pl.ANY)
HBM_SPEC = pl.BlockSpec(memory_space=pltpu.HBM)
SEM_SPEC = pl.BlockSpec(memory_space=pltpu.SEMAPHORE)
SIDE_EFFECT = pltpu.SideEffectType.DATAFLOW_SIDE_EFFECTING

NT = (((1,), (1,)), ((), ()))
TN = (((0,), (0,)), ((), ()))
NN = (((1,), (0,)), ((), ()))


def _pcall(body, *, name, out_shape, grid=(), in_specs=None, out_specs=None, scratch_shapes=(), dims=None):
    kw = {}
    if in_specs is not None:
        kw["in_specs"] = in_specs
    if out_specs is not None:
        kw["out_specs"] = out_specs
    return pl.pallas_call(
        body, name=name, out_shape=out_shape, grid=grid, scratch_shapes=list(scratch_shapes),
        compiler_params=pltpu.CompilerParams(dimension_semantics=dims, vmem_limit_bytes=VMEM_LIMIT), **kw)


def _dot(a, b, dn=NN):
    return lax.dot_general(a, b, dn, preferred_element_type=F32)


def _sigmoid(x):
    return 1.0 / (1.0 + jnp.exp(-x))


def _matmul(a, b, *, ta=False, name, tm=512, tn=1024, tk=512):
    if ta:
        kdim, m = a.shape
    else:
        m, kdim = a.shape
    n = b.shape[1]
    tm, tn, tk = min(tm, m), min(tn, n), min(tk, kdim)
    nk = kdim // tk

    def body(a_ref, b_ref, o_ref, acc_ref):
        k = pl.program_id(2)

        @pl.when(k == 0)
        def _():
            acc_ref[...] = jnp.zeros_like(acc_ref)

        acc_ref[...] += _dot(a_ref[...].astype(BF16), b_ref[...].astype(BF16), TN if ta else NN)

        @pl.when(k == nk - 1)
        def _():
            o_ref[...] = acc_ref[...]

    a_spec = pl.BlockSpec((tk, tm), lambda i, j, k: (k, i)) if ta else pl.BlockSpec((tm, tk), lambda i, j, k: (i, k))
    return _pcall(
        body, name=name, out_shape=jax.ShapeDtypeStruct((m, n), F32), grid=(m // tm, n // tn, nk),
        in_specs=[a_spec, pl.BlockSpec((tk, tn), lambda i, j, k: (k, j))],
        out_specs=pl.BlockSpec((tm, tn), lambda i, j, k: (i, j)),
        scratch_shapes=[pltpu.VMEM((tm, tn), F32)], dims=("parallel", "parallel", "arbitrary"))(a, b)


def _norm_in_proj(h, g, wm, wf, *, tm=512):
    t = h.shape[0]

    def body(h_ref, g_ref, wm_ref, wf_ref, q_ref, k_ref, v_ref, za_ref, u_ref, zp_ref, fl_ref, hn_ref):
        j = pl.program_id(1)

        @pl.when(j == 0)
        def _():
            x = h_ref[...]
            r = lax.rsqrt(jnp.mean(x * x, axis=-1, keepdims=True) + EPS)
            hn = (x * r * g_ref[...]).astype(BF16)
            hn_ref[...] = hn
            fl_ref[...] = _dot(hn, wf_ref[...])

        y = _dot(hn_ref[...], wm_ref[...])
        for s, ref in enumerate((q_ref, k_ref, v_ref, za_ref, u_ref, zp_ref)):
            @pl.when(j == s)
            def _(ref=ref):
                ref[...] = y.astype(ref.dtype)

    row = lambda i, j: (i, 0)
    blk = pl.BlockSpec((tm, D_MODEL), row)
    sds = lambda dt: jax.ShapeDtypeStruct((t, D_MODEL), dt)
    return _pcall(
        body, name="norm_in_proj", grid=(t // tm, N_SEG),
        in_specs=[blk, pl.BlockSpec((1, D_MODEL), lambda i, j: (0, 0)),
                  pl.BlockSpec((D_MODEL, D_MODEL), lambda i, j: (0, j)),
                  pl.BlockSpec((D_MODEL, LANES), lambda i, j: (0, 0))],
        out_specs=[blk] * 6 + [pl.BlockSpec((tm, LANES), row), blk],
        out_shape=[sds(BF16)] * 3 + [sds(F32)] * 3 + [jax.ShapeDtypeStruct((t, LANES), F32), sds(BF16)],
        dims=("parallel", "arbitrary"))(h, g, wm, wf)


def _gate_cumsum(flt, bias):
    b, hh, s = flt.shape

    def body(fl_ref, b_ref, c_ref):
        x = fl_ref[...] + b_ref[...]
        acc = jnp.minimum(x, 0.0) - jnp.log(1.0 + jnp.exp(-jnp.abs(x)))
        idx = lax.broadcasted_iota(jnp.int32, x.shape, 1)
        sh = 1
        while sh < s:
            acc = acc + jnp.where(idx >= sh, pltpu.roll(acc, sh, 1), 0.0)
            sh *= 2
        c_ref[...] = acc

    return _pcall(
        body, name="gate_cumsum", grid=(b,),
        in_specs=[pl.BlockSpec((None, hh, s), lambda i: (i, 0, 0)), pl.BlockSpec((hh, 1), lambda i: (0, 0))],
        out_specs=pl.BlockSpec((None, hh, s), lambda i: (i, 0, 0)),
        out_shape=jax.ShapeDtypeStruct((b, hh, s), F32), dims=("parallel",))(flt, bias)


def _gate_cumsum_bwd(dcol, drow, flt, bias):
    b, hh, s = flt.shape

    def body(d_ref, dr_ref, fl_ref, b_ref, dfl_ref, db_ref):
        @pl.when(pl.program_id(0) == 0)
        def _():
            db_ref[...] = jnp.zeros_like(db_ref)

        x = fl_ref[...] + b_ref[...]
        acc = dr_ref[...] - d_ref[...]
        idx = lax.broadcasted_iota(jnp.int32, x.shape, 1)
        sh = 1
        while sh < s:
            acc = acc + jnp.where(idx + sh < s, pltpu.roll(acc, s - sh, 1), 0.0)
            sh *= 2
        e = jnp.exp(-jnp.abs(x))
        sig_neg = jnp.where(x >= 0.0, e, 1.0) / (1.0 + e)
        dfl = acc * sig_neg
        dfl_ref[...] = dfl
        db_ref[...] += jnp.sum(dfl, axis=1, keepdims=True)

    return _pcall(
        body, name="gate_cumsum_bwd", grid=(b,),
        in_specs=[pl.BlockSpec((None, hh, s), lambda i: (i, 0, 0))] * 3 + [pl.BlockSpec((hh, 1), lambda i: (0, 0))],
        out_specs=[pl.BlockSpec((None, hh, s), lambda i: (i, 0, 0)), pl.BlockSpec((hh, 1), lambda i: (0, 0))],
        out_shape=[jax.ShapeDtypeStruct((b, hh, s), F32), jax.ShapeDtypeStruct((hh, 1), F32)],
        dims=("arbitrary",))(dcol, drow, flt, bias)


def _attn_fwd(q, k, v, c4, *, nb, tq=512, rc=32, pv_rows=512):
    t = q.shape[0]
    s = t // nb
    tq = min(tq, s)
    nq = s // tq
    tk = tq
    pv_rows = min(pv_rows, tq)

    def body(q_ref, k_ref, v_ref, c_ref, o_ref, lse_ref, qh_scr, s_scr, p_scr, m_scr, acc_scr):
        qi = pl.program_id(2)
        lo = lax.broadcasted_iota(jnp.int32, (tq, PAIR), 1) < HEAD_DIM
        q2 = q_ref[...] * SCALE
        zero = jnp.zeros_like(q2)
        qh_scr[0] = jnp.where(lo, q2, zero)
        qh_scr[1] = jnp.where(lo, zero, q2)
        m_scr[...] = jnp.full(m_scr.shape, NEG, F32)
        acc_scr[...] = jnp.zeros(acc_scr.shape, F32)
        row = lax.broadcasted_iota(jnp.int32, (rc, tk), 0)
        col = lax.broadcasted_iota(jnp.int32, (rc, tk), 1)

        def scores(kj, slot):
            k2 = k_ref[pl.ds(pl.multiple_of(kj * tk, tk), tk), :]
            for hd in range(2):
                s_scr[slot, hd] = _dot(qh_scr[hd], k2, NT)

        def softmax_pv(kj, slot, masked):
            off = pl.multiple_of(kj * tk, tk)
            v2 = v_ref[pl.ds(off, tk), :]
            one = jnp.ones_like(v2)
            va = (jnp.where(lo, v2, one), jnp.where(lo, one, v2))
            crow = c_ref[:, pl.ds(off, tk)]
            for hd in range(2):
                for r0 in range(0, tq, pv_rows):
                    for r in range(r0, r0 + pv_rows, rc):
                        sc = s_scr[slot, hd, r:r + rc, :] - crow[hd:hd + 1, :]
                        if masked:
                            sc = jnp.where(row + r >= col, sc, NEG)
                        m_old = m_scr[hd, r:r + rc, :]
                        m_new = jnp.maximum(m_old, jnp.max(sc, axis=1, keepdims=True))
                        for cb in range(0, tk, LANES):
                            p_scr[hd, r:r + rc, cb:cb + LANES] = jnp.exp(sc[:, cb:cb + LANES] - m_new).astype(BF16)
                        m_scr[hd, r:r + rc, :] = m_new
                        acc_scr[hd, r:r + rc, :] = acc_scr[hd, r:r + rc, :] * jnp.exp(m_old - m_new)
                    acc_scr[hd, r0:r0 + pv_rows, :] += _dot(p_scr[hd, r0:r0 + pv_rows, :], va[hd])

        def unmasked(kj, carry):
            scores(kj, 0)
            softmax_pv(kj, 0, False)
            return carry

        lax.fori_loop(0, qi, unmasked, 0)
        scores(qi, 0)
        softmax_pv(qi, 0, True)
        a0, a1 = acc_scr[0], acc_scr[1]
        den = jnp.where(lo, pltpu.roll(a0, HEAD_DIM, 1), pltpu.roll(a1, HEAD_DIM, 1))
        o_ref[...] = jnp.where(lo, a0, a1) / den
        lse_ref[...] = jnp.where(lo, m_scr[0], m_scr[1]) + jnp.log(den)

    qspec = pl.BlockSpec((tq, PAIR), lambda b, hp, i: (b * nq + i, hp))
    kvspec = pl.BlockSpec((s, PAIR), lambda b, hp, i: (b, hp))
    return _pcall(
        body, name="attn_fwd", grid=(nb, N_PAIRS, nq),
        in_specs=[qspec, kvspec, kvspec, pl.BlockSpec((None, None, 2, s), lambda b, hp, i: (b, hp, 0, 0))],
        out_specs=[qspec, qspec],
        out_shape=[jax.ShapeDtypeStruct((t, D_MODEL), F32)] * 2,
        scratch_shapes=[pltpu.VMEM((2, tq, PAIR), BF16), pltpu.VMEM((1, 2, tq, tk), F32), pltpu.VMEM((2, tq, tk), BF16),
                        pltpu.VMEM((2, tq, LANES), F32), pltpu.VMEM((2, tq, PAIR), F32)],
        dims=("parallel", "parallel", "arbitrary"))(q, k, v, c4)


def _attn_bwd(q, k, v, do, ccol, lse4, delta4, *, nb, tk=512, rc=16):
    t = q.shape[0]
    s = t // nb
    tk = min(tk, s)
    nk = s // tk
    tq = tk

    def body(q_ref, do_ref, k_ref, v_ref, cc_ref, lse_ref, dl_ref, dq_ref, dk_ref, dv_ref, dc_ref, dr_ref,
             kz_scr, vz_scr, ko_scr, crep_scr, st_scr, dp_scr, pt_scr, ds_scr, dk_scr, dv_scr):
        kj = pl.program_id(2)

        @pl.when(kj == 0)
        def _():
            dq_ref[...] = jnp.zeros_like(dq_ref)
            dr_ref[...] = jnp.zeros_like(dr_ref)

        lo = lax.broadcasted_iota(jnp.int32, (tk, PAIR), 1) < HEAD_DIM
        k2 = k_ref[...]
        v2 = v_ref[...]
        zero = jnp.zeros_like(k2)
        one = jnp.ones_like(k2)
        kz_scr[0] = jnp.where(lo, k2, zero)
        kz_scr[1] = jnp.where(lo, zero, k2)
        vz_scr[0] = jnp.where(lo, v2, zero)
        vz_scr[1] = jnp.where(lo, zero, v2)
        ko_scr[0] = jnp.where(lo, k2, one)
        ko_scr[1] = jnp.where(lo, one, k2)
        cc = cc_ref[...]
        cc_sw = pltpu.roll(cc, HEAD_DIM, 1)
        crep_scr[0] = jnp.where(lo, cc, cc_sw)
        crep_scr[1] = jnp.where(lo, cc_sw, cc)
        dk_scr[...] = jnp.zeros(dk_scr.shape, F32)
        dv_scr[...] = jnp.zeros(dv_scr.shape, F32)
        row = lax.broadcasted_iota(jnp.int32, (rc, LANES), 0)
        col = lax.broadcasted_iota(jnp.int32, (rc, LANES), 1)

        def step(qi, masked):
            off = pl.multiple_of(qi * tq, tq)
            q2 = q_ref[pl.ds(off, tq), :] * SCALE
            do2 = do_ref[pl.ds(off, tq), :]
            lse = lse_ref[:, pl.ds(off, tq)]
            dl = dl_ref[:, pl.ds(off, tq)]
            qo = (jnp.where(lo, q2, jnp.ones_like(q2)), jnp.where(lo, jnp.ones_like(q2), q2))
            for hd in range(2):
                st_scr[hd] = _dot(kz_scr[hd], q2, NT)
                dp_scr[hd] = _dot(vz_scr[hd], do2, NT)
            dqs = []
            for hd in range(2):
                for r in range(0, tk, rc):
                    c_rep = crep_scr[hd, r:r + rc, :]
                    for cb in range(0, tq, LANES):
                        pt = jnp.exp(st_scr[hd, r:r + rc, cb:cb + LANES] - c_rep - lse[hd:hd + 1, cb:cb + LANES])
                        if masked:
                            pt = jnp.where(col + cb >= row + r, pt, 0.0)
                        dst = pt * (dp_scr[hd, r:r + rc, cb:cb + LANES] - dl[hd:hd + 1, cb:cb + LANES])
                        pt_scr[hd, r:r + rc, cb:cb + LANES] = pt.astype(BF16)
                        ds_scr[hd, r:r + rc, cb:cb + LANES] = dst.astype(BF16)
                dv_scr[hd] += _dot(pt_scr[hd], do2)
                dk_scr[hd] += _dot(ds_scr[hd], qo[hd])
                dqs.append(_dot(ds_scr[hd], ko_scr[hd], TN))
            dq_ref[pl.ds(off, tq), :] += jnp.where(lo, dqs[0], dqs[1]) * SCALE
            dr_ref[pl.ds(off, tq), :] += jnp.where(lo, dqs[1], dqs[0])

        step(kj, True)

        def unmasked(qi, carry):
            step(qi, False)
            return carry

        lax.fori_loop(kj + 1, nk, unmasked, 0)
        dk_ref[...] = jnp.where(lo, dk_scr[0], dk_scr[1]).astype(BF16)
        dc_ref[...] = jnp.where(lo, dk_scr[1], dk_scr[0])
        dv_ref[...] = jnp.where(lo, dv_scr[0], dv_scr[1]).astype(BF16)

    full = pl.BlockSpec((s, PAIR), lambda b, hp, j: (b, hp))
    kblk = pl.BlockSpec((tk, PAIR), lambda b, hp, j: (b * nk + j, hp))
    rows = pl.BlockSpec((None, None, 2, s), lambda b, hp, j: (b, hp, 0, 0))
    sds = lambda dt: jax.ShapeDtypeStruct((t, D_MODEL), dt)
    pair_bf = pltpu.VMEM((2, tk, PAIR), BF16)
    pair_f = pltpu.VMEM((2, tk, PAIR), F32)
    return _pcall(
        body, name="attn_bwd", grid=(nb, N_PAIRS, nk),
        in_specs=[full, full, kblk, kblk, kblk, rows, rows],
        out_specs=[full, kblk, kblk, kblk, full],
        out_shape=[sds(F32), sds(BF16), sds(BF16), sds(F32), sds(F32)],
        scratch_shapes=[pair_bf, pair_bf, pair_bf, pair_f, pltpu.VMEM((2, tk, tq), F32), pltpu.VMEM((2, tk, tq), F32),
                        pltpu.VMEM((2, tk, tq), BF16), pltpu.VMEM((2, tk, tq), BF16), pair_f, pair_f],
        dims=("parallel", "parallel", "arbitrary"))(q, do, k, v, ccol, lse4, delta4)


def _window_sum(x, g, s, *, ahead):
    row = lax.broadcasted_iota(jnp.int32, x.shape, 0)
    for step in range(N_GROUPS):
        sh = 1 << step
        if ahead:
            moved = jnp.where(row + sh < s, pltpu.roll(x, s - sh, 0), 0.0)
        else:
            moved = jnp.where(row >= sh, pltpu.roll(x, sh, 0), 0.0)
        x = jnp.where(step <= g, x + moved, x)
    return x


def _inv_count(g, s):
    pos = lax.broadcasted_iota(jnp.int32, (s, 1), 0) + 1
    return 1.0 / jnp.minimum(pos, 2 << g).astype(F32)


def _pool_fwd(u, zp, wpool, ps, *, nb):
    t = u.shape[0]
    s = t // nb

    def body(u_ref, zp_ref, w_ref, ps_ref, out_ref, pooled_ref, mixed_ref):
        g = pl.program_id(1)
        x = u_ref[...]
        pooled = (_window_sum(x, g, s, ahead=False) * _inv_count(g, s) - x).astype(BF16)
        mixed = _dot(pooled, w_ref[...])
        z = zp_ref[...]
        pooled_ref[...] = pooled
        mixed_ref[...] = mixed
        out_ref[...] = (mixed * ps_ref[...] * (z * _sigmoid(z))).astype(BF16)

    blk = pl.BlockSpec((s, GROUP_DIM), lambda b, g: (b, g))
    sds = lambda dt: jax.ShapeDtypeStruct((t, D_MODEL), dt)
    return _pcall(
        body, name="pool_fwd", grid=(nb, N_GROUPS),
        in_specs=[blk, blk, pl.BlockSpec((None, GROUP_DIM, GROUP_DIM), lambda b, g: (g, 0, 0)),
                  pl.BlockSpec((1, GROUP_DIM), lambda b, g: (0, g))],
        out_specs=[blk, blk, blk], out_shape=[sds(BF16), sds(BF16), sds(F32)],
        dims=("parallel", "parallel"))(u, zp, wpool, ps)


def _pool_bwd(dpool, mixed, zp, wpool, ps, *, nb):
    t = dpool.shape[0]
    s = t // nb

    def body(dp_ref, mx_ref, zp_ref, w_ref, ps_ref, dmx_ref, du_ref, dzp_ref, dps_ref):
        g = pl.program_id(0)

        @pl.when(pl.program_id(1) == 0)
        def _():
            dps_ref[...] = jnp.zeros_like(dps_ref)

        dp = dp_ref[...]
        mixed = mx_ref[...]
        z = zp_ref[...]
        scale = ps_ref[...]
        sg = _sigmoid(z)
        silu = z * sg
        dmixed = (dp * scale * silu).astype(BF16)
        dmx_ref[...] = dmixed
        dps_ref[...] += jnp.sum(dp * mixed * silu, axis=0, keepdims=True)
        dzp_ref[...] = (dp * mixed * scale * (sg * (1.0 + z * (1.0 - sg)))).astype(BF16)
        dpooled = _dot(dmixed, w_ref[...], NT)
        du = _window_sum(dpooled * _inv_count(g, s), g, s, ahead=True) - dpooled
        du_ref[...] = du.astype(BF16)

    blk = pl.BlockSpec((s, GROUP_DIM), lambda g, b: (b, g))
    sds = lambda dt: jax.ShapeDtypeStruct((t, D_MODEL), dt)
    return _pcall(
        body, name="pool_bwd", grid=(N_GROUPS, nb),
        in_specs=[blk, blk, blk, pl.BlockSpec((None, GROUP_DIM, GROUP_DIM), lambda g, b: (g, 0, 0)),
                  pl.BlockSpec((1, GROUP_DIM), lambda g, b: (0, g))],
        out_specs=[blk, blk, blk, pl.BlockSpec((1, GROUP_DIM), lambda g, b: (0, g))],
        out_shape=[sds(BF16), sds(BF16), sds(BF16), jax.ShapeDtypeStruct((1, D_MODEL), F32)],
        dims=("parallel", "arbitrary"))(dpool, mixed, zp, wpool, ps)


def _pool_wgrad(pooled, dmixed, *, tk=1024):
    t = pooled.shape[0]
    nk = t // tk

    def body(a_ref, b_ref, o_ref):
        @pl.when(pl.program_id(1) == 0)
        def _():
            o_ref[...] = jnp.zeros_like(o_ref)

        o_ref[...] += _dot(a_ref[...], b_ref[...], TN)

    blk = pl.BlockSpec((tk, GROUP_DIM), lambda g, k: (k, g))
    return _pcall(
        body, name="pool_wgrad", grid=(N_GROUPS, nk), in_specs=[blk, blk],
        out_specs=pl.BlockSpec((None, GROUP_DIM, GROUP_DIM), lambda g, k: (g, 0, 0)),
        out_shape=jax.ShapeDtypeStruct((N_GROUPS, GROUP_DIM, GROUP_DIM), F32),
        dims=("parallel", "arbitrary"))(pooled, dmixed)


def _out_block(h, o, za, pool, p, wo, wpg, wpe, gpost, *, tm=256):
    t = h.shape[0]

    def body(h_ref, o_ref, za_ref, pool_ref, p_ref, wo_ref, wpg_ref, wpe_ref, g_ref,
             h2_ref, mix_ref, h1_ref, gate_ref, pe_ref, cat_ref):
        z = za_ref[...]
        a = (o_ref[...] * (z * _sigmoid(z))).astype(BF16)
        pool_v = pool_ref[...]
        cat_ref[:, :D_MODEL] = a
        cat_ref[:, D_MODEL:] = pool_v
        mix = _dot(a, wo_ref[:D_MODEL, :]) + _dot(pool_v, wo_ref[D_MODEL:, :])
        r = lax.rsqrt(jnp.mean(mix * mix, axis=-1, keepdims=True) + EPS)
        h1 = h_ref[...] + mix * r * g_ref[...]
        gate = _sigmoid(_dot(h1.astype(BF16), wpg_ref[...]))
        pe = _dot(p_ref[...].astype(BF16), wpe_ref[...])
        mix_ref[...] = mix
        h1_ref[...] = h1
        gate_ref[...] = gate
        pe_ref[...] = pe
        h2_ref[...] = h1 + gate * pe

    row = lambda i: (i, 0)
    const = lambda i: (0, 0)
    blk = pl.BlockSpec((tm, D_MODEL), row)
    sds = jax.ShapeDtypeStruct((t, D_MODEL), F32)
    return _pcall(
        body, name="out_block", grid=(t // tm,),
        in_specs=[blk, blk, blk, blk, pl.BlockSpec((tm, PLE_DIM), row),
                  pl.BlockSpec((MIX, D_MODEL), const), pl.BlockSpec((D_MODEL, D_MODEL), const),
                  pl.BlockSpec((PLE_DIM, D_MODEL), const), pl.BlockSpec((1, D_MODEL), const)],
        out_specs=[blk] * 5 + [pl.BlockSpec((tm, MIX), row)],
        out_shape=[sds] * 5 + [jax.ShapeDtypeStruct((t, MIX), BF16)],
        dims=("parallel",))(h, o, za, pool, p, wo, wpg, wpe, gpost)


def _loss_grad(y, target, *, tm=512):
    t = y.shape[0]

    def body(y_ref, t_ref, dy_ref, loss_ref):
        @pl.when(pl.program_id(0) == 0)
        def _():
            loss_ref[...] = jnp.zeros_like(loss_ref)

        err = y_ref[...] - t_ref[...]
        dy_ref[...] = err * (1.0 / D_MODEL)
        part = jnp.sum(jnp.sum(err * err, axis=1, keepdims=True), axis=0, keepdims=True)
        loss_ref[...] += part * (0.5 / D_MODEL)

    blk = pl.BlockSpec((tm, D_MODEL), lambda i: (i, 0))
    return _pcall(
        body, name="loss_grad", grid=(t // tm,), in_specs=[blk, blk],
        out_specs=[blk, pl.BlockSpec((8, LANES), lambda i: (0, 0))],
        out_shape=[jax.ShapeDtypeStruct((t, D_MODEL), F32), jax.ShapeDtypeStruct((8, LANES), F32)],
        dims=("arbitrary",))(y, target)


def _out_block_bwd(dh2, gate, pe, mix, o, za, wpg, wo, gpost, gsum, *, tm=256):
    t = dh2.shape[0]

    def body(dh2_ref, gate_ref, pe_ref, mix_ref, o_ref, za_ref, wpg_ref, wo_ref, g_ref, gs_ref,
             dh1_ref, dgp_ref, dpe_ref, dmix_ref, do_ref, dza_ref, dpool_ref, delta_ref, dg_ref):
        @pl.when(pl.program_id(0) == 0)
        def _():
            dg_ref[...] = jnp.zeros_like(dg_ref)

        dh2 = dh2_ref[...]
        gate = gate_ref[...]
        dpe_ref[...] = (dh2 * gate).astype(BF16)
        dgp = (dh2 * pe_ref[...] * gate * (1.0 - gate)).astype(BF16)
        dgp_ref[...] = dgp
        dh1 = dh2 + _dot(dgp, wpg_ref[...], NT)
        dh1_ref[...] = dh1
        mix = mix_ref[...]
        r = lax.rsqrt(jnp.mean(mix * mix, axis=-1, keepdims=True) + EPS)
        dg_ref[...] += jnp.sum(dh1 * mix * r, axis=0, keepdims=True)
        a = dh1 * g_ref[...]
        dmix = (r * a - mix * (r * r * r) * jnp.mean(a * mix, axis=-1, keepdims=True)).astype(BF16)
        dmix_ref[...] = dmix
        dattn = _dot(dmix, wo_ref[:D_MODEL, :], NT)
        dpool_ref[...] = _dot(dmix, wo_ref[D_MODEL:, :], NT)
        z = za_ref[...]
        sg = _sigmoid(z)
        o = o_ref[...]
        do = (dattn * (z * sg)).astype(BF16)
        do_ref[...] = do
        dza_ref[...] = (dattn * o * (sg * (1.0 + z * (1.0 - sg)))).astype(BF16)
        prod = do.astype(F32) * o
        hi = prod.astype(BF16)
        rest = (prod - hi.astype(F32)).astype(BF16)
        delta_ref[...] = _dot(hi, gs_ref[...]) + _dot(rest, gs_ref[...])

    row = lambda i: (i, 0)
    const = lambda i: (0, 0)
    blk = pl.BlockSpec((tm, D_MODEL), row)
    sds = lambda dt: jax.ShapeDtypeStruct((t, D_MODEL), dt)
    return _pcall(
        body, name="out_block_bwd", grid=(t // tm,),
        in_specs=[blk] * 6 + [pl.BlockSpec((D_MODEL, D_MODEL), const), pl.BlockSpec((MIX, D_MODEL), const),
                              pl.BlockSpec((1, D_MODEL), const), pl.BlockSpec((D_MODEL, LANES), const)],
        out_specs=[blk] * 7 + [pl.BlockSpec((tm, LANES), row), pl.BlockSpec((1, D_MODEL), const)],
        out_shape=[sds(F32)] + [sds(BF16)] * 5 + [sds(F32), jax.ShapeDtypeStruct((t, LANES), F32),
                                                 jax.ShapeDtypeStruct((1, D_MODEL), F32)],
        dims=("arbitrary",))(dh2, gate, pe, mix, o, za, wpg, wo, gpost, gsum)


def _in_proj_bwd(dsegs, dfl, wm, wf, h, dh1, gpre, *, tm=512):
    t = h.shape[0]

    def body(*refs):
        seg_refs = refs[:N_SEG]
        dfl_ref, wm_ref, wf_ref, h_ref, dh1_ref, g_ref, dh_ref, dg_ref, acc_ref = refs[N_SEG:]
        i = pl.program_id(0)
        j = pl.program_id(1)

        @pl.when(jnp.logical_and(i == 0, j == 0))
        def _():
            dg_ref[...] = jnp.zeros_like(dg_ref)

        @pl.when(j == 0)
        def _():
            acc_ref[...] = _dot(dfl_ref[...].astype(BF16), wf_ref[...], NT)

        for sgi, ref in enumerate(seg_refs):
            @pl.when(j == sgi)
            def _(ref=ref):
                acc_ref[...] += _dot(ref[...].astype(BF16), wm_ref[...], NT)

        @pl.when(j == N_SEG - 1)
        def _():
            dhn = acc_ref[...]
            x = h_ref[...]
            r = lax.rsqrt(jnp.mean(x * x, axis=-1, keepdims=True) + EPS)
            dg_ref[...] += jnp.sum(dhn * x * r, axis=0, keepdims=True)
            a = dhn * g_ref[...]
            dh_ref[...] = dh1_ref[...] + r * a - x * (r * r * r) * jnp.mean(a * x, axis=-1, keepdims=True)

    row = lambda i, j: (i, 0)
    const = lambda i, j: (0, 0)
    blk = pl.BlockSpec((tm, D_MODEL), row)
    return _pcall(
        body, name="in_proj_bwd", grid=(t // tm, N_SEG),
        in_specs=[blk] * N_SEG + [pl.BlockSpec((tm, LANES), row), pl.BlockSpec((D_MODEL, D_MODEL), lambda i, j: (0, j)),
                                  pl.BlockSpec((D_MODEL, LANES), const), blk, blk, pl.BlockSpec((1, D_MODEL), const)],
        out_specs=[blk, pl.BlockSpec((1, D_MODEL), const)],
        out_shape=[jax.ShapeDtypeStruct((t, D_MODEL), F32), jax.ShapeDtypeStruct((1, D_MODEL), F32)],
        scratch_shapes=[pltpu.VMEM((tm, D_MODEL), F32)],
        dims=("arbitrary", "arbitrary"))(*dsegs, dfl, wm, wf, h, dh1, gpre)


def _chip_peers(x, y):
    return [(1 - x, y), (x, 1 - y), (1 - x, 1 - y)]


def _ici_views(src, land, gather, x, y, c, px, py):
    me, peer = 2 * x + y, 2 * px + py
    if gather:
        return src.at[c], land.at[me, c], land.at[peer, c]
    return src.at[peer], land.at[me], land.at[peer]


def _ici_start(srcs, lands, after, *, gather, name):
    n = len(srcs)

    def body(*refs):
        src, land = refs[:n], refs[n:2 * n]
        send, recv = refs[2 * n + 1], refs[2 * n + 2]
        token = refs[-1]
        x, y, c = lax.axis_index("x"), lax.axis_index("y"), lax.axis_index("c")
        for kk in range(n):
            for pi, (px, py) in enumerate(_chip_peers(x, y)):
                mine, there, _ = _ici_views(src[kk], land[kk], gather, x, y, c, px, py)
                pltpu.make_async_remote_copy(src_ref=mine, dst_ref=there, send_sem=send.at[3 * kk + pi],
                                             recv_sem=recv.at[3 * kk + pi], device_id=(px, py, c),
                                             device_id_type=MESH).start()
        token[...] = jnp.zeros_like(token)

    hbm = lambda a: pltpu.with_memory_space_constraint(a, pltpu.HBM)
    outs = pl.pallas_call(
        body, name=name,
        out_shape=(pltpu.SemaphoreType.DMA((3 * n,)), pltpu.SemaphoreType.DMA((3 * n,)),
                   *[pltpu.HBM(a.shape, a.dtype) for a in srcs], *[pltpu.HBM(a.shape, a.dtype) for a in lands],
                   jax.ShapeDtypeStruct((8, LANES), F32)),
        in_specs=[HBM_SPEC] * (2 * n) + [ANY],
        out_specs=(SEM_SPEC, SEM_SPEC, *[HBM_SPEC] * (2 * n), pl.BlockSpec(memory_space=pltpu.VMEM)),
        input_output_aliases={i: 2 + i for i in range(2 * n)},
        compiler_params=pltpu.CompilerParams(has_side_effects=SIDE_EFFECT),
    )(*[hbm(a) for a in srcs], *[hbm(a) for a in lands], after)
    return outs[0], outs[1], list(outs[2:2 + n]), list(outs[2 + n:2 + 2 * n]), outs[-1]


def _ici_wait(send, recv, srcs, lands, after, *, gather, name):
    n = len(srcs)

    def body(*refs):
        src, land = refs[:n], refs[n:2 * n]
        send_ref, recv_ref = refs[2 * n], refs[2 * n + 1]
        x, y, c = lax.axis_index("x"), lax.axis_index("y"), lax.axis_index("c")
        for kk in range(n):
            for pi, (px, py) in enumerate(_chip_peers(x, y)):
                mine, _, here = _ici_views(src[kk], land[kk], gather, x, y, c, px, py)
                cp = pltpu.make_async_remote_copy(src_ref=mine, dst_ref=here, send_sem=send_ref.at[3 * kk + pi],
                                                  recv_sem=recv_ref.at[3 * kk + pi], device_id=(px, py, c),
                                                  device_id_type=MESH)
                cp.wait_send()
                cp.wait_recv()

    outs = pl.pallas_call(
        body, name=name,
        out_shape=[pltpu.HBM(a.shape, a.dtype) for a in srcs] + [pltpu.HBM(a.shape, a.dtype) for a in lands],
        in_specs=[HBM_SPEC] * (2 * n) + [SEM_SPEC, SEM_SPEC, ANY], out_specs=[HBM_SPEC] * (2 * n),
        input_output_aliases={i: i for i in range(2 * n)},
        compiler_params=pltpu.CompilerParams(has_side_effects=SIDE_EFFECT),
    )(*srcs, *lands, send, recv, after)
    return list(outs[:n]), list(outs[n:])


def _sibling_call(make_copies, ins, out_shape, *, n_remote, n_local, name):
    n_in = len(ins)
    n_out = len(out_shape)

    def body(*refs):
        in_refs, out_refs = refs[:n_in], refs[n_in:n_in + n_out]
        send, recv, loc = refs[n_in + n_out:]
        c = lax.axis_index("c")
        sib = (lax.axis_index("x"), lax.axis_index("y"), 1 - c)
        remote, local = make_copies(in_refs, out_refs, c)
        sends = [pltpu.make_async_remote_copy(src_ref=a, dst_ref=b, send_sem=send.at[i], recv_sem=recv.at[i],
                                              device_id=sib, device_id_type=MESH) for i, (a, b, _) in enumerate(remote)]
        recvs = [pltpu.make_async_remote_copy(src_ref=a, dst_ref=h, send_sem=send.at[i], recv_sem=recv.at[i],
                                              device_id=sib, device_id_type=MESH) for i, (a, _, h) in enumerate(remote)]
        locs = [pltpu.make_async_copy(a, b, loc.at[i]) for i, (a, b) in enumerate(local)]
        for cp in locs + sends:
            cp.start()
        for cp in recvs:
            cp.wait_recv()
        for cp in sends:
            cp.wait_send()
        for cp in locs:
            cp.wait()

    return pl.pallas_call(
        body, name=name, out_shape=out_shape, in_specs=[ANY] * n_in, out_specs=[ANY] * n_out,
        scratch_shapes=[pltpu.SemaphoreType.DMA((n_remote,)), pltpu.SemaphoreType.DMA((n_remote,)),
                        pltpu.SemaphoreType.DMA((n_local,))])(*ins)


def _gather_finish(lands, shards, *, name):
    n = len(lands)

    def make_copies(ins, outs, c):
        x, y = lax.axis_index("x"), lax.axis_index("y")
        remote, local = [], []
        for kk in range(n):
            local.append((ins[n + kk], outs[kk].at[2 * x + y]))
            for px, py in _chip_peers(x, y):
                chip = 2 * px + py
                local.append((ins[kk].at[chip, c], outs[kk].at[chip, c]))
                remote.append((ins[kk].at[chip, c], outs[kk].at[chip, c], outs[kk].at[chip, 1 - c]))
        return remote, local

    return _sibling_call(make_copies, list(lands) + list(shards),
                         [jax.ShapeDtypeStruct(a.shape, a.dtype) for a in lands], n_remote=3 * n, n_local=4 * n, name=name)


def _presum_exchange(pieces, *, name):
    n = len(pieces)

    def make_copies(ins, outs, c):
        remote = [(ins[kk].at[1 - c], outs[n + kk], outs[n + kk]) for kk in range(n)]
        local = [(ins[kk].at[c], outs[kk]) for kk in range(n)]
        return remote, local

    shapes = [jax.ShapeDtypeStruct(a.shape[1:], a.dtype) for a in pieces]
    outs = _sibling_call(make_copies, list(pieces), shapes + shapes, n_remote=n, n_local=n, name=name)
    return list(outs[:n]), list(outs[n:])


def _halves_exchange(halves, *, name):
    n = len(halves)

    def make_copies(ins, outs, c):
        remote = [(ins[kk], outs[kk].at[c], outs[kk].at[1 - c]) for kk in range(n)]
        local = [(ins[kk], outs[kk].at[c]) for kk in range(n)]
        return remote, local

    return _sibling_call(make_copies, list(halves), [jax.ShapeDtypeStruct((2,) + a.shape, a.dtype) for a in halves],
                         n_remote=n, n_local=n, name=name)


def _all_gather_small(a, *, name):
    def body(a_ref, o_ref, send, recv, loc):
        x, y, c = lax.axis_index("x"), lax.axis_index("y"), lax.axis_index("c")
        me = 4 * x + 2 * y + c

        def peer(rel):
            fx, fy, fc = (rel >> 2) & 1, (rel >> 1) & 1, rel & 1
            px = (1 - x) if fx else x
            py = (1 - y) if fy else y
            pc = (1 - c) if fc else c
            return px, py, pc

        def remote(rel, slot):
            return pltpu.make_async_remote_copy(
                src_ref=a_ref, dst_ref=o_ref.at[slot], send_sem=send.at[rel - 1], recv_sem=recv.at[rel - 1],
                device_id=peer(rel), device_id_type=MESH)

        mine = pltpu.make_async_copy(a_ref, o_ref.at[me], loc)
        mine.start()
        sends = [remote(rel, me) for rel in range(1, 8)]
        for cp in sends:
            cp.start()
        for rel in range(1, 8):
            px, py, pc = peer(rel)
            remote(rel, 4 * px + 2 * py + pc).wait_recv()
        for cp in sends:
            cp.wait_send()
        mine.wait()

    return _pcall(
        body, name=name, in_specs=[ANY], out_specs=ANY,
        out_shape=jax.ShapeDtypeStruct((8,) + a.shape, a.dtype),
        scratch_shapes=[pltpu.SemaphoreType.DMA((7,)), pltpu.SemaphoreType.DMA((7,)), pltpu.SemaphoreType.DMA])(a)


def _pair_sum(a, b, *, name, tr=256):
    _, rows, cols = a.shape
    tr = min(tr, rows)

    def body(a_ref, b_ref, o_ref):
        o_ref[...] = (a_ref[...].astype(F32) + b_ref[...].astype(F32)).astype(BF16)

    blk = pl.BlockSpec((None, tr, cols), lambda j, i: (j, i, 0))
    return _pcall(body, name=name, grid=(4, rows // tr), in_specs=[blk, blk], out_specs=blk,
                  out_shape=jax.ShapeDtypeStruct(a.shape, BF16), dims=("parallel", "parallel"))(a, b)


def _chip_sum(own, recv, chip, *, name, tr=256):
    _, rows, cols = own.shape
    tr = min(tr, rows)

    def body(chip_ref, own_ref, r1_ref, r2_ref, r3_ref, o_ref):
        acc = own_ref[...].astype(F32)
        for ref in (r1_ref, r2_ref, r3_ref):
            acc = acc + ref[...].astype(F32)
        o_ref[...] = acc

    def slot(step):
        return pl.BlockSpec((None, tr, cols), lambda i, chip_ref: ((chip_ref[0] + step) % 4, i, 0))

    return pl.pallas_call(
        body, name=name, out_shape=jax.ShapeDtypeStruct((rows, cols), F32),
        grid_spec=pltpu.PrefetchScalarGridSpec(
            num_scalar_prefetch=1, grid=(rows // tr,), in_specs=[slot(0), slot(1), slot(2), slot(3)],
            out_specs=pl.BlockSpec((tr, cols), lambda i, chip_ref: (i, 0))),
        compiler_params=pltpu.CompilerParams(dimension_semantics=("parallel",), vmem_limit_bytes=VMEM_LIMIT),
    )(chip, own, recv, recv, recv)


def _adamw_math(w, g, m, v):
    m = ADAM_B1 * m + (1.0 - ADAM_B1) * g
    v = ADAM_B2 * v + (1.0 - ADAM_B2) * (g * g)
    m_hat = m / (1.0 - ADAM_B1 ** ADAM_STEP)
    v_hat = v / (1.0 - ADAM_B2 ** ADAM_STEP)
    delta = -ADAM_LR * (m_hat / (jnp.sqrt(v_hat) + ADAM_EPS) + ADAM_WD * w)
    return delta, m, v


def _adamw(w, m, v, g_layers, *, name, tr=128):
    depth = len(g_layers)
    rows, cols = g_layers[0].shape
    tr = min(tr, rows)
    nblk = rows // tr

    def body(*refs):
        w_ref, m_ref, v_ref = refs[:3]
        g_refs = refs[3:3 + depth]
        g_ref, d_ref, mo_ref, vo_ref = refs[3 + depth:]
        layer = pl.program_id(0)
        g = g_refs[0][...]
        for l in range(1, depth):
            g = jnp.where(layer == l, g_refs[l][...], g)
        delta, m_new, v_new = _adamw_math(w_ref[...], g, m_ref[...], v_ref[...])
        g_ref[...] = g
        d_ref[...] = delta
        mo_ref[...] = m_new
        vo_ref[...] = v_new

    def g_spec(l):
        return pl.BlockSpec((tr, cols), lambda ll, i: (jnp.where(ll == l, i, jnp.where(ll < l, 0, nblk - 1)), 0))

    blk = pl.BlockSpec((tr, cols), lambda ll, i: (ll * nblk + i, 0))
    return _pcall(
        body, name=name, grid=(depth, nblk), in_specs=[blk] * 3 + [g_spec(l) for l in range(depth)], out_specs=[blk] * 4,
        out_shape=[jax.ShapeDtypeStruct(w.shape, F32)] * 4, dims=("arbitrary", "arbitrary"))(w, m, v, *g_layers)


def _adamw_small(w, m, v, parts, *, name):
    def body(w_ref, m_ref, v_ref, p_ref, g_ref, d_ref, mo_ref, vo_ref):
        g = p_ref[0]
        for dev in range(1, 8):
            g = g + p_ref[dev]
        delta, m_new, v_new = _adamw_math(w_ref[...], g, m_ref[...], v_ref[...])
        g_ref[...] = g
        d_ref[...] = delta
        mo_ref[...] = m_new
        vo_ref[...] = v_new

    return _pcall(body, name=name, out_shape=[jax.ShapeDtypeStruct(w.shape, F32)] * 4)(w, m, v, parts)


def _rows_from_heads(a, nb, s):
    return a.reshape(nb, s, N_HEADS).transpose(0, 2, 1).reshape(nb, N_PAIRS, 2, s)


def _layer_fwd(h, p_l, wts, nb):
    t = h.shape[0]
    s = t // nb
    q, k, v, za, u, zp, fl, hn = _norm_in_proj(h, wts["gpre"], wts["wm"], wts["wf"])
    flt = fl[:, :N_HEADS].reshape(nb, s, N_HEADS).transpose(0, 2, 1)
    c = _gate_cumsum(flt, wts["bf"])
    o, lse = _attn_fwd(q, k, v, c.reshape(nb, N_PAIRS, 2, s), nb=nb)
    pool, pooled, mixed = _pool_fwd(u, zp, wts["wpool"], wts["ps"], nb=nb)
    h2, mix, h1, gate, pe, cat = _out_block(h, o, za, pool, p_l, wts["wo"], wts["wpg"], wts["wpe"], wts["gpost"])
    saved = dict(h=h, hn=hn, q=q, k=k, v=v, za=za, zp=zp, flt=flt, c=c, o=o, lse=lse, pooled=pooled, mixed=mixed,
                 mix=mix, h1=h1, gate=gate, pe=pe, cat=cat, p=p_l)
    return h2, saved


def _layer_bwd(dh2, sv, wts, gsum, nb):
    t = dh2.shape[0]
    s = t // nb
    dh1, dgp, dpe, dmix, do, dza, dpool, delta, dgpost = _out_block_bwd(
        dh2, sv["gate"], sv["pe"], sv["mix"], sv["o"], sv["za"], wts["wpg"], wts["wo"], wts["gpost"], gsum)
    g_wpe = _matmul(sv["p"], dpe, ta=True, name="wgrad_pe")
    g_wpg = _matmul(sv["h1"], dgp, ta=True, name="wgrad_pg")
    g_wo = _matmul(sv["cat"], dmix, ta=True, name="wgrad_out")

    ccol = jnp.repeat(sv["c"].transpose(0, 2, 1).reshape(t, N_HEADS), HEAD_DIM, axis=1)
    lse4 = _rows_from_heads(sv["lse"][:, ::HEAD_DIM], nb, s)
    delta4 = _rows_from_heads(delta[:, :N_HEADS], nb, s)
    dq, dk, dv, dcol, drow = _attn_bwd(sv["q"], sv["k"], sv["v"], do, ccol, lse4, delta4, nb=nb)

    def head_rows(a):
        a = a[:, ::HEAD_DIM].reshape(nb, s, N_PAIRS, 2)[..., ::-1]
        return a.reshape(nb, s, N_HEADS).transpose(0, 2, 1)

    dflt, dbf = _gate_cumsum_bwd(head_rows(dcol), head_rows(drow), sv["flt"], wts["bf"])
    dfl = jnp.pad(dflt.transpose(0, 2, 1).reshape(t, N_HEADS), ((0, 0), (0, LANES - N_HEADS)))

    dmixed, du, dzp, dps = _pool_bwd(dpool, sv["mixed"], sv["zp"], wts["wpool"], wts["ps"], nb=nb)
    g_wpool = _pool_wgrad(sv["pooled"], dmixed)

    dsegs = (dq, dk, dv, dza, du, dzp)
    dh, dgpre = _in_proj_bwd(dsegs, dfl, wts["wm"], wts["wf"], sv["h"], dh1, wts["gpre"])
    g_segs = [_matmul(sv["hn"], dx, ta=True, name="wgrad_in_f32" if dx.dtype == F32 else "wgrad_in")
              for dx in dsegs]
    g_wf = _matmul(sv["hn"], dfl, ta=True, name="wgrad_in_f")
    g_win = jnp.concatenate(g_segs[:4] + [g_wf[:, :N_HEADS]] + g_segs[4:], axis=1)
    grads = dict(w_in=g_win, w_out=g_wo, w_pg=g_wpg, w_pe=g_wpe, w_pool=g_wpool,
                 norm_pre=dgpre[0], norm_post=dgpost[0], pool_scale=dps[0], b_f=dbf[:, 0])
    return dh, grads


def kernel(x, p, norm_pre, norm_post, w_in, b_f, w_pool, pool_scale, w_out, w_pg, w_pe, loss_target, m_norm_pre, m_norm_post, m_w_in, m_b_f, m_w_pool, m_pool_scale, m_w_out, m_w_pg, m_w_pe, v_norm_pre, v_norm_post, v_w_in, v_b_f, v_w_pool, v_pool_scale, v_w_out, v_w_pg, v_w_pe):
    nb, s, _ = x.shape
    t = nb * s
    depth = w_in.shape[0]
    big = dict(w_in=(w_in, m_w_in, v_w_in), w_out=(w_out, m_w_out, v_w_out), w_pg=(w_pg, m_w_pg, v_w_pg),
               w_pe=(w_pe, m_w_pe, v_w_pe), w_pool=(w_pool, m_w_pool, v_w_pool))
    names = list(big)
    cols = {n: big[n][0].shape[-1] for n in names}
    chip = (2 * lax.axis_index("x") + lax.axis_index("y")).astype(jnp.int32).reshape(1)

    gathers = []
    token = jnp.zeros((8, LANES), F32)
    for l in range(depth):
        halves = [big[n][0][l].reshape(2, -1, cols[n]).astype(BF16) for n in names]
        lands = [lax.empty((4,) + a.shape, a.dtype) for a in halves]
        send, recv, halves, lands, token = _ici_start(halves, lands, token, gather=True, name=f"gather_start_{l}")
        gathers.append((send, recv, halves, lands))

    def layer_weights(l, after, anchor):
        send, recv, halves, lands = gathers[l]
        halves, lands = _ici_wait(send, recv, halves, lands, after, gather=True, name=f"gather_wait_{l}")
        full = dict(zip(names, _gather_finish(lands, halves, name="gather_finish")))
        full = {n: a.reshape(4, -1, cols[n]) for n, a in full.items()}
        win = jnp.concatenate([full["w_in"][j] for j in range(4)], axis=1)
        wm = jnp.concatenate([win[:, :F_OFF], win[:, F_OFF + N_HEADS:]], axis=1)
        wf = jnp.pad(win[:, F_OFF:F_OFF + N_HEADS], ((0, 0), (0, LANES - N_HEADS)))
        return dict(
            wm=wm, wf=wf, wo=full["w_out"].reshape(MIX, D_MODEL), wpg=full["w_pg"].reshape(D_MODEL, D_MODEL),
            wpe=jnp.concatenate([full["w_pe"][j] for j in range(4)], axis=1),
            wpool=full["w_pool"].reshape(4, N_GROUPS, GROUP_DIM // 4, GROUP_DIM).transpose(1, 0, 2, 3).reshape(
                N_GROUPS, GROUP_DIM, GROUP_DIM),
            gpre=norm_pre[l][None] + anchor, gpost=norm_post[l][None], ps=pool_scale[l][None], bf=b_f[l][:, None])

    h = x.reshape(t, D_MODEL)
    saved, layers = [], []
    after = token
    for l in range(depth):
        layers.append(layer_weights(l, after, token[0, 0]))
        h, sv = _layer_fwd(h, p[l].reshape(t, PLE_DIM), layers[l], nb)
        saved.append(sv)
        after = h
    dh, loss_blk = _loss_grad(h, loss_target.reshape(t, D_MODEL))
    loss = lax.psum(loss_blk[0, 0], ("x", "y", "c"))
    gsum = (jnp.arange(D_MODEL)[:, None] // HEAD_DIM == jnp.arange(LANES)[None, :]).astype(BF16)

    def pieces(g, name):
        if name in ("w_in", "w_pe"):
            by_chip = jnp.stack([g[:, j * cols[name]:(j + 1) * cols[name]] for j in range(4)])
        elif name == "w_pool":
            by_chip = g.reshape(N_GROUPS, 4, GROUP_DIM // 4, GROUP_DIM).transpose(1, 0, 2, 3)
        else:
            by_chip = g
        return by_chip.reshape(4, 2, -1, cols[name]).transpose(1, 0, 2, 3).astype(BF16)

    def scatter_start(l, g):
        mine, theirs = _presum_exchange([pieces(g[n], n) for n in names], name="presum_exchange")
        sums = [_pair_sum(a, b, name="presum_" + n) for n, a, b in zip(names, mine, theirs)]
        lands = [lax.empty(a.shape, a.dtype) for a in sums]
        return _ici_start(sums, lands, token, gather=False, name=f"scatter_start_{l}")

    def scatter_finish(l, state, after):
        send, recv, sums, lands, _ = state
        sums, lands = _ici_wait(send, recv, sums, lands, after, gather=False, name=f"scatter_wait_{l}")
        reduced = [_chip_sum(a, r, chip, name="chip_sum_" + n) for n, a, r in zip(names, sums, lands)]
        both = _halves_exchange(reduced, name="halves_exchange")
        return {n: a.reshape(-1, cols[n]) for n, a in zip(names, both)}

    grads = [None] * depth
    reduced = [None] * depth
    pending = None
    for l in reversed(range(depth)):
        wts = layers[l]
        if pending is not None:
            wts = dict(wts, gpost=wts["gpost"] + pending[1][4][0, 0])
        dh, grads[l] = _layer_bwd(dh, saved[l], wts, gsum, nb)
        if pending is not None:
            reduced[pending[0]] = scatter_finish(pending[0], pending[1], dh)
        pending = (l, scatter_start(l, grads[l]))
    grad_x = dh.reshape(nb, s, D_MODEL)

    small = dict(norm_pre=(norm_pre, m_norm_pre, v_norm_pre), norm_post=(norm_post, m_norm_post, v_norm_post),
                 pool_scale=(pool_scale, m_pool_scale, v_pool_scale), b_f=(b_f, m_b_f, v_b_f))

    def pack(get):
        rows = []
        for n in small:
            a = get(n)
            rows.append(jnp.pad(a, ((0, 0), (0, D_MODEL - a.shape[1]))))
        return jnp.concatenate(rows, axis=0)

    parts = _all_gather_small(pack(lambda n: jnp.stack([grads[l][n] for l in range(depth)])), name="gather_small")
    small_packed = _adamw_small(pack(lambda n: small[n][0]), pack(lambda n: small[n][1]), pack(lambda n: small[n][2]),
                                parts, name="adamw_small")
    small_out = {}
    for i, n in enumerate(small):
        width = small[n][0].shape[1]
        small_out[n] = [o[depth * i:depth * (i + 1), :width] for o in small_packed]

    reduced[pending[0]] = scatter_finish(pending[0], pending[1], small_packed[0])
    big_out = {}
    for n in names:
        w, m, v = big[n]
        outs = _adamw(w.reshape(-1, cols[n]), m.reshape(-1, cols[n]), v.reshape(-1, cols[n]),
                      [reduced[l][n] for l in range(depth)], name="adamw_" + n)
        big_out[n] = [o.reshape(w.shape) for o in outs]

    order =["norm_pre", "norm_post", "w_in", "b_f", "w_pool", "pool_scale", "w_out", "w_pg", "w_pe"]
    result = [loss, grad_x]
    for kind in range(4):
        for n in order:
            result.append(big_out[n][kind] if n in big_out else small_out[n][kind])
    return tuple(result)
```

```python
import functools

import jax
import jax.numpy as jnp
from jax import lax
from jax.experimental import pallas as pl
from jax.experimental.pallas import tpu as pltpu

F32 = jnp.float32
BF16 = jnp.bfloat16

D_MODEL = 1024
N_HEADS = 16
HEAD_DIM = 64
PAIR = 2 * HEAD_DIM
N_PAIRS = N_HEADS // 2
N_GROUPS = 4
GROUP_DIM = 256
PLE_DIM = 256
MIX = 2 * D_MODEL
N_SEG = 6
F_OFF = 4 * D_MODEL
IN_COLS = N_SEG * D_MODEL + N_HEADS
LANES = 128
EPS = 1e-6
SCALE = 0.125
NEG = -1e30

ADAM_LR = 0.001
ADAM_B1 = 0.9
ADAM_B2 = 0.999
ADAM_EPS = 1e-08
ADAM_WD = 0.01
ADAM_STEP = 10

VMEM_LIMIT = 56 * 1024 * 1024
MESH = pl.DeviceIdType.MESH
ANY = pl.BlockSpec(memory_space=pl.ANY)
HBM_SPEC = pl.BlockSpec(memory_space=pltpu.HBM)
SEM_SPEC = pl.BlockSpec(memory_space=pltpu.SEMAPHORE)
SIDE_EFFECT = pltpu.SideEffectType.DATAFLOW_SIDE_EFFECTING

NT = (((1,), (1,)), ((), ()))
TN = (((0,), (0,)), ((), ()))
NN = (((1,), (0,)), ((), ()))


def _pcall(body, *, name, out_shape, grid=(), in_specs=None, out_specs=None, scratch_shapes=(), dims=None):
    kw = {}
    if in_specs is not None:
        kw["in_specs"] = in_specs
    if out_specs is not None:
        kw["out_specs"] = out_specs
    return pl.pallas_call(
        body, name=name, out_shape=out_shape, grid=grid, scratch_shapes=list(scratch_shapes),
        compiler_params=pltpu.CompilerParams(dimension_semantics=dims, vmem_limit_bytes=VMEM_LIMIT), **kw)


def _dot(a, b, dn=NN):
    return lax.dot_general(a, b, dn, preferred_element_type=F32)


def _sigmoid(x):
    return 1.0 / (1.0 + jnp.exp(-x))


def _matmul(a, b, *, ta=False, name, tm=512, tn=1024, tk=512):
    if ta:
        kdim, m = a.shape
    else:
        m, kdim = a.shape
    n = b.shape[1]
    tm, tn, tk = min(tm, m), min(tn, n), min(tk, kdim)
    nk = kdim // tk

    def body(a_ref, b_ref, o_ref, acc_ref):
        k = pl.program_id(2)

        @pl.when(k == 0)
        def _():
            acc_ref[...] = jnp.zeros_like(acc_ref)

        acc_ref[...] += _dot(a_ref[...].astype(BF16), b_ref[...].astype(BF16), TN if ta else NN)

        @pl.when(k == nk - 1)
        def _():
            o_ref[...] = acc_ref[...]

    a_spec = pl.BlockSpec((tk, tm), lambda i, j, k: (k, i)) if ta else pl.BlockSpec((tm, tk), lambda i, j, k: (i, k))
    return _pcall(
        body, name=name, out_shape=jax.ShapeDtypeStruct((m, n), F32), grid=(m // tm, n // tn, nk),
        in_specs=[a_spec, pl.BlockSpec((tk, tn), lambda i, j, k: (k, j))],
        out_specs=pl.BlockSpec((tm, tn), lambda i, j, k: (i, j)),
        scratch_shapes=[pltpu.VMEM((tm, tn), F32)], dims=("parallel", "parallel", "arbitrary"))(a, b)


def _norm_in_proj(h, g, wm, wf, *, tm=512):
    t = h.shape[0]

    def body(h_ref, g_ref, wm_ref, wf_ref, q_ref, k_ref, v_ref, za_ref, u_ref, zp_ref, fl_ref, hn_ref):
        j = pl.program_id(1)

        @pl.when(j == 0)
        def _():
            x = h_ref[...]
            r = lax.rsqrt(jnp.mean(x * x, axis=-1, keepdims=True) + EPS)
            hn = (x * r * g_ref[...]).astype(BF16)
            hn_ref[...] = hn
            fl_ref[...] = _dot(hn, wf_ref[...])

        y = _dot(hn_ref[...], wm_ref[...])
        for s, ref in enumerate((q_ref, k_ref, v_ref, za_ref, u_ref, zp_ref)):
            @pl.when(j == s)
            def _(ref=ref):
                ref[...] = y.astype(ref.dtype)

    row = lambda i, j: (i, 0)
    blk = pl.BlockSpec((tm, D_MODEL), row)
    sds = lambda dt: jax.ShapeDtypeStruct((t, D_MODEL), dt)
    return _pcall(
        body, name="norm_in_proj", grid=(t // tm, N_SEG),
        in_specs=[blk, pl.BlockSpec((1, D_MODEL), lambda i, j: (0, 0)),
                  pl.BlockSpec((D_MODEL, D_MODEL), lambda i, j: (0, j)),
                  pl.BlockSpec((D_MODEL, LANES), lambda i, j: (0, 0))],
        out_specs=[blk] * 6 + [pl.BlockSpec((tm, LANES), row), blk],
        out_shape=[sds(BF16)] * 3 + [sds(F32)] * 3 + [jax.ShapeDtypeStruct((t, LANES), F32), sds(BF16)],
        dims=("parallel", "arbitrary"))(h, g, wm, wf)


def _gate_cumsum(flt, bias):
    b, hh, s = flt.shape

    def body(fl_ref, b_ref, c_ref):
        x = fl_ref[...] + b_ref[...]
        acc = jnp.minimum(x, 0.0) - jnp.log(1.0 + jnp.exp(-jnp.abs(x)))
        idx = lax.broadcasted_iota(jnp.int32, x.shape, 1)
        sh = 1
        while sh < s:
            acc = acc + jnp.where(idx >= sh, pltpu.roll(acc, sh, 1), 0.0)
            sh *= 2
        c_ref[...] = acc

    return _pcall(
        body, name="gate_cumsum", grid=(b,),
        in_specs=[pl.BlockSpec((None, hh, s), lambda i: (i, 0, 0)), pl.BlockSpec((hh, 1), lambda i: (0, 0))],
        out_specs=pl.BlockSpec((None, hh, s), lambda i: (i, 0, 0)),
        out_shape=jax.ShapeDtypeStruct((b, hh, s), F32), dims=("parallel",))(flt, bias)


def _gate_cumsum_bwd(dcol, drow, flt, bias):
    b, hh, s = flt.shape

    def body(d_ref, dr_ref, fl_ref, b_ref, dfl_ref, db_ref):
        @pl.when(pl.program_id(0) == 0)
        def _():
            db_ref[...] = jnp.zeros_like(db_ref)

        x = fl_ref[...] + b_ref[...]
        acc = dr_ref[...] - d_ref[...]
        idx = lax.broadcasted_iota(jnp.int32, x.shape, 1)
        sh = 1
        while sh < s:
            acc = acc + jnp.where(idx + sh < s, pltpu.roll(acc, s - sh, 1), 0.0)
            sh *= 2
        e = jnp.exp(-jnp.abs(x))
        sig_neg = jnp.where(x >= 0.0, e, 1.0) / (1.0 + e)
        dfl = acc * sig_neg
        dfl_ref[...] = dfl
        db_ref[...] += jnp.sum(dfl, axis=1, keepdims=True)

    return _pcall(
        body, name="gate_cumsum_bwd", grid=(b,),
        in_specs=[pl.BlockSpec((None, hh, s), lambda i: (i, 0, 0))] * 3 + [pl.BlockSpec((hh, 1), lambda i: (0, 0))],
        out_specs=[pl.BlockSpec((None, hh, s), lambda i: (i, 0, 0)), pl.BlockSpec((hh, 1), lambda i: (0, 0))],
        out_shape=[jax.ShapeDtypeStruct((b, hh, s), F32), jax.ShapeDtypeStruct((hh, 1), F32)],
        dims=("arbitrary",))(dcol, drow, flt, bias)


def _attn_fwd(q, k, v, c4, *, nb, tq=512, rc=32, pv_rows=512):
    t = q.shape[0]
    s = t // nb
    tq = min(tq, s)
    nq = s // tq
    tk = tq
    pv_rows = min(pv_rows, tq)

    def body(q_ref, k_ref, v_ref, c_ref, o_ref, lse_ref, qh_scr, s_scr, p_scr, m_scr, acc_scr):
        qi = pl.program_id(2)
        lo = lax.broadcasted_iota(jnp.int32, (tq, PAIR), 1) < HEAD_DIM
        q2 = q_ref[...] * SCALE
        zero = jnp.zeros_like(q2)
        qh_scr[0] = jnp.where(lo, q2, zero)
        qh_scr[1] = jnp.where(lo, zero, q2)
        m_scr[...] = jnp.full(m_scr.shape, NEG, F32)
        acc_scr[...] = jnp.zeros(acc_scr.shape, F32)
        row = lax.broadcasted_iota(jnp.int32, (rc, tk), 0)
        col = lax.broadcasted_iota(jnp.int32, (rc, tk), 1)

        def scores(kj, slot):
            k2 = k_ref[pl.ds(pl.multiple_of(kj * tk, tk), tk), :]
            for hd in range(2):
                s_scr[slot, hd] = _dot(qh_scr[hd], k2, NT)

        def softmax_pv(kj, slot, masked):
            off = pl.multiple_of(kj * tk, tk)
            v2 = v_ref[pl.ds(off, tk), :]
            one = jnp.ones_like(v2)
            va = (jnp.where(lo, v2, one), jnp.where(lo, one, v2))
            crow = c_ref[:, pl.ds(off, tk)]
            for hd in range(2):
                for r0 in range(0, tq, pv_rows):
                    for r in range(r0, r0 + pv_rows, rc):
                        sc = s_scr[slot, hd, r:r + rc, :] - crow[hd:hd + 1, :]
                        if masked:
                            sc = jnp.where(row + r >= col, sc, NEG)
                        m_old = m_scr[hd, r:r + rc, :]
                        m_new = jnp.maximum(m_old, jnp.max(sc, axis=1, keepdims=True))
                        for cb in range(0, tk, LANES):
                            p_scr[hd, r:r + rc, cb:cb + LANES] = jnp.exp(sc[:, cb:cb + LANES] - m_new).astype(BF16)
                        m_scr[hd, r:r + rc, :] = m_new
                        acc_scr[hd, r:r + rc, :] = acc_scr[hd, r:r + rc, :] * jnp.exp(m_old - m_new)
                    acc_scr[hd, r0:r0 + pv_rows, :] += _dot(p_scr[hd, r0:r0 + pv_rows, :], va[hd])

        def unmasked(kj, carry):
            scores(kj, 0)
            softmax_pv(kj, 0, False)
            return carry

        lax.fori_loop(0, qi, unmasked, 0)
        scores(qi, 0)
        softmax_pv(qi, 0, True)
        a0, a1 = acc_scr[0], acc_scr[1]
        den = jnp.where(lo, pltpu.roll(a0, HEAD_DIM, 1), pltpu.roll(a1, HEAD_DIM, 1))
        o_ref[...] = jnp.where(lo, a0, a1) / den
        lse_ref[...] = jnp.where(lo, m_scr[0], m_scr[1]) + jnp.log(den)

    qspec = pl.BlockSpec((tq, PAIR), lambda b, hp, i: (b * nq + i, hp))
    kvspec = pl.BlockSpec((s, PAIR), lambda b, hp, i: (b, hp))
    return _pcall(
        body, name="attn_fwd", grid=(nb, N_PAIRS, nq),
        in_specs=[qspec, kvspec, kvspec, pl.BlockSpec((None, None, 2, s), lambda b, hp, i: (b, hp, 0, 0))],
        out_specs=[qspec, qspec],
        out_shape=[jax.ShapeDtypeStruct((t, D_MODEL), F32)] * 2,
        scratch_shapes=[pltpu.VMEM((2, tq, PAIR), BF16), pltpu.VMEM((1, 2, tq, tk), F32), pltpu.VMEM((2, tq, tk), BF16),
                        pltpu.VMEM((2, tq, LANES), F32), pltpu.VMEM((2, tq, PAIR), F32)],
        dims=("parallel", "parallel", "arbitrary"))(q, k, v, c4)


def _attn_bwd(q, k, v, do, ccol, lse4, delta4, *, nb, tk=512, rc=16):
    t = q.shape[0]
    s = t // nb
    tk = min(tk, s)
    nk = s // tk
    tq = tk

    def body(q_ref, do_ref, k_ref, v_ref, cc_ref, lse_ref, dl_ref, dq_ref, dk_ref, dv_ref, dc_ref, dr_ref,
             kz_scr, vz_scr, ko_scr, crep_scr, st_scr, dp_scr, pt_scr, ds_scr, dk_scr, dv_scr):
        kj = pl.program_id(2)

        @pl.when(kj == 0)
        def _():
            dq_ref[...] = jnp.zeros_like(dq_ref)
            dr_ref[...] = jnp.zeros_like(dr_ref)

        lo = lax.broadcasted_iota(jnp.int32, (tk, PAIR), 1) < HEAD_DIM
        k2 = k_ref[...]
        v2 = v_ref[...]
        zero = jnp.zeros_like(k2)
        one = jnp.ones_like(k2)
        kz_scr[0] = jnp.where(lo, k2, zero)
        kz_scr[1] = jnp.where(lo, zero, k2)
        vz_scr[0] = jnp.where(lo, v2, zero)
        vz_scr[1] = jnp.where(lo, zero, v2)
        ko_scr[0] = jnp.where(lo, k2, one)
        ko_scr[1] = jnp.where(lo, one, k2)
        cc = cc_ref[...]
        cc_sw = pltpu.roll(cc, HEAD_DIM, 1)
        crep_scr[0] = jnp.where(lo, cc, cc_sw)
        crep_scr[1] = jnp.where(lo, cc_sw, cc)
        dk_scr[...] = jnp.zeros(dk_scr.shape, F32)
        dv_scr[...] = jnp.zeros(dv_scr.shape, F32)
        row = lax.broadcasted_iota(jnp.int32, (rc, LANES), 0)
        col = lax.broadcasted_iota(jnp.int32, (rc, LANES), 1)

        def step(qi, masked):
            off = pl.multiple_of(qi * tq, tq)
            q2 = q_ref[pl.ds(off, tq), :] * SCALE
            do2 = do_ref[pl.ds(off, tq), :]
            lse = lse_ref[:, pl.ds(off, tq)]
            dl = dl_ref[:, pl.ds(off, tq)]
            qo = (jnp.where(lo, q2, jnp.ones_like(q2)), jnp.where(lo, jnp.ones_like(q2), q2))
            for hd in range(2):
                st_scr[hd] = _dot(kz_scr[hd], q2, NT)
                dp_scr[hd] = _dot(vz_scr[hd], do2, NT)
            dqs = []
            for hd in range(2):
                for r in range(0, tk, rc):
                    c_rep = crep_scr[hd, r:r + rc, :]
                    for cb in range(0, tq, LANES):
                        pt = jnp.exp(st_scr[hd, r:r + rc, cb:cb + LANES] - c_rep - lse[hd:hd + 1, cb:cb + LANES])
                        if masked:
                            pt = jnp.where(col + cb >= row + r, pt, 0.0)
                        dst = pt * (dp_scr[hd, r:r + rc, cb:cb + LANES] - dl[hd:hd + 1, cb:cb + LANES])
                        pt_scr[hd, r:r + rc, cb:cb + LANES] = pt.astype(BF16)
                        ds_scr[hd, r:r + rc, cb:cb + LANES] = dst.astype(BF16)
                dv_scr[hd] += _dot(pt_scr[hd], do2)
                dk_scr[hd] += _dot(ds_scr[hd], qo[hd])
                dqs.append(_dot(ds_scr[hd], ko_scr[hd], TN))
            dq_ref[pl.ds(off, tq), :] += jnp.where(lo, dqs[0], dqs[1]) * SCALE
            dr_ref[pl.ds(off, tq), :] += jnp.where(lo, dqs[1], dqs[0])

        step(kj, True)

        def unmasked(qi, carry):
            step(qi, False)
            return carry

        lax.fori_loop(kj + 1, nk, unmasked, 0)
        dk_ref[...] = jnp.where(lo, dk_scr[0], dk_scr[1]).astype(BF16)
        dc_ref[...] = jnp.where(lo, dk_scr[1], dk_scr[0])
        dv_ref[...] = jnp.where(lo, dv_scr[0], dv_scr[1]).astype(BF16)

    full = pl.BlockSpec((s, PAIR), lambda b, hp, j: (b, hp))
    kblk = pl.BlockSpec((tk, PAIR), lambda b, hp, j: (b * nk + j, hp))
    rows = pl.BlockSpec((None, None, 2, s), lambda b, hp, j: (b, hp, 0, 0))
    sds = lambda dt: jax.ShapeDtypeStruct((t, D_MODEL), dt)
    pair_bf = pltpu.VMEM((2, tk, PAIR), BF16)
    pair_f = pltpu.VMEM((2, tk, PAIR), F32)
    return _pcall(
        body, name="attn_bwd", grid=(nb, N_PAIRS, nk),
        in_specs=[full, full, kblk, kblk, kblk, rows, rows],
        out_specs=[full, kblk, kblk, kblk, full],
        out_shape=[sds(F32), sds(BF16), sds(BF16), sds(F32), sds(F32)],
        scratch_shapes=[pair_bf, pair_bf, pair_bf, pair_f, pltpu.VMEM((2, tk, tq), F32), pltpu.VMEM((2, tk, tq), F32),
                        pltpu.VMEM((2, tk, tq), BF16), pltpu.VMEM((2, tk, tq), BF16), pair_f, pair_f],
        dims=("parallel", "parallel", "arbitrary"))(q, do, k, v, ccol, lse4, delta4)


def _window_sum(x, g, s, *, ahead):
    row = lax.broadcasted_iota(jnp.int32, x.shape, 0)
    for step in range(N_GROUPS):
        sh = 1 << step
        if ahead:
            moved = jnp.where(row + sh < s, pltpu.roll(x, s - sh, 0), 0.0)
        else:
            moved = jnp.where(row >= sh, pltpu.roll(x, sh, 0), 0.0)
        x = jnp.where(step <= g, x + moved, x)
    return x


def _inv_count(g, s):
    pos = lax.broadcasted_iota(jnp.int32, (s, 1), 0) + 1
    return 1.0 / jnp.minimum(pos, 2 << g).astype(F32)


def _pool_fwd(u, zp, wpool, ps, *, nb):
    t = u.shape[0]
    s = t // nb

    def body(u_ref, zp_ref, w_ref, ps_ref, out_ref, pooled_ref, mixed_ref):
        g = pl.program_id(1)
        x = u_ref[...]
        pooled = (_window_sum(x, g, s, ahead=False) * _inv_count(g, s) - x).astype(BF16)
        mixed = _dot(pooled, w_ref[...])
        z = zp_ref[...]
        pooled_ref[...] = pooled
        mixed_ref[...] = mixed
        out_ref[...] = (mixed * ps_ref[...] * (z * _sigmoid(z))).astype(BF16)

    blk = pl.BlockSpec((s, GROUP_DIM), lambda b, g: (b, g))
    sds = lambda dt: jax.ShapeDtypeStruct((t, D_MODEL), dt)
    return _pcall(
        body, name="pool_fwd", grid=(nb, N_GROUPS),
        in_specs=[blk, blk, pl.BlockSpec((None, GROUP_DIM, GROUP_DIM), lambda b, g: (g, 0, 0)),
                  pl.BlockSpec((1, GROUP_DIM), lambda b, g: (0, g))],
        out_specs=[blk, blk, blk], out_shape=[sds(BF16), sds(BF16), sds(F32)],
        dims=("parallel", "parallel"))(u, zp, wpool, ps)


def _pool_bwd(dpool, mixed, zp, wpool, ps, *, nb):
    t = dpool.shape[0]
    s = t // nb

    def body(dp_ref, mx_ref, zp_ref, w_ref, ps_ref, dmx_ref, du_ref, dzp_ref, dps_ref):
        g = pl.program_id(0)

        @pl.when(pl.program_id(1) == 0)
        def _():
            dps_ref[...] = jnp.zeros_like(dps_ref)

        dp = dp_ref[...]
        mixed = mx_ref[...]
        z = zp_ref[...]
        scale = ps_ref[...]
        sg = _sigmoid(z)
        silu = z * sg
        dmixed = (dp * scale * silu).astype(BF16)
        dmx_ref[...] = dmixed
        dps_ref[...] += jnp.sum(dp * mixed * silu, axis=0, keepdims=True)
        dzp_ref[...] = (dp * mixed * scale * (sg * (1.0 + z * (1.0 - sg)))).astype(BF16)
        dpooled = _dot(dmixed, w_ref[...], NT)
        du = _window_sum(dpooled * _inv_count(g, s), g, s, ahead=True) - dpooled
        du_ref[...] = du.astype(BF16)

    blk = pl.BlockSpec((s, GROUP_DIM), lambda g, b: (b, g))
    sds = lambda dt: jax.ShapeDtypeStruct((t, D_MODEL), dt)
    return _pcall(
        body, name="pool_bwd", grid=(N_GROUPS, nb),
        in_specs=[blk, blk, blk, pl.BlockSpec((None, GROUP_DIM, GROUP_DIM), lambda g, b: (g, 0, 0)),
                  pl.BlockSpec((1, GROUP_DIM), lambda g, b: (0, g))],
        out_specs=[blk, blk, blk, pl.BlockSpec((1, GROUP_DIM), lambda g, b: (0, g))],
        out_shape=[sds(BF16), sds(BF16), sds(BF16), jax.ShapeDtypeStruct((1, D_MODEL), F32)],
        dims=("parallel", "arbitrary"))(dpool, mixed, zp, wpool, ps)


def _pool_wgrad(pooled, dmixed, *, tk=1024):
    t = pooled.shape[0]
    tk = min(tk, t)
    nk = t // tk

    def body(a_ref, b_ref, o_ref):
        @pl.when(pl.program_id(1) == 0)
        def _():
            o_ref[...] = jnp.zeros_like(o_ref)

        o_ref[...] += _dot(a_ref[...], b_ref[...], TN)

    blk = pl.BlockSpec((tk, GROUP_DIM), lambda g, k: (k, g))
    return _pcall(
        body, name="pool_wgrad", grid=(N_GROUPS, nk), in_specs=[blk, blk],
        out_specs=pl.BlockSpec((None, GROUP_DIM, GROUP_DIM), lambda g, k: (g, 0, 0)),
        out_shape=jax.ShapeDtypeStruct((N_GROUPS, GROUP_DIM, GROUP_DIM), F32),
        dims=("parallel", "arbitrary"))(pooled, dmixed)


def _out_block(h, o, za, pool, p, wo, wpg, wpe, gpost, *, tm=256):
    t = h.shape[0]

    def body(h_ref, o_ref, za_ref, pool_ref, p_ref, wo_ref, wpg_ref, wpe_ref, g_ref,
             h2_ref, mix_ref, h1_ref, gate_ref, pe_ref, cat_ref):
        z = za_ref[...]
        a = (o_ref[...] * (z * _sigmoid(z))).astype(BF16)
        pool_v = pool_ref[...]
        cat_ref[:, :D_MODEL] = a
        cat_ref[:, D_MODEL:] = pool_v
        mix = _dot(a, wo_ref[:D_MODEL, :]) + _dot(pool_v, wo_ref[D_MODEL:, :])
        r = lax.rsqrt(jnp.mean(mix * mix, axis=-1, keepdims=True) + EPS)
        h1 = h_ref[...] + mix * r * g_ref[...]
        gate = _sigmoid(_dot(h1.astype(BF16), wpg_ref[...]))
        pe = _dot(p_ref[...].astype(BF16), wpe_ref[...])
        mix_ref[...] = mix
        h1_ref[...] = h1
        gate_ref[...] = gate
        pe_ref[...] = pe
        h2_ref[...] = h1 + gate * pe

    row = lambda i: (i, 0)
    const = lambda i: (0, 0)
    blk = pl.BlockSpec((tm, D_MODEL), row)
    sds = jax.ShapeDtypeStruct((t, D_MODEL), F32)
    return _pcall(
        body, name="out_block", grid=(t // tm,),
        in_specs=[blk, blk, blk, blk, pl.BlockSpec((tm, PLE_DIM), row),
                  pl.BlockSpec((MIX, D_MODEL), const), pl.BlockSpec((D_MODEL, D_MODEL), const),
                  pl.BlockSpec((PLE_DIM, D_MODEL), const), pl.BlockSpec((1, D_MODEL), const)],
        out_specs=[blk] * 5 + [pl.BlockSpec((tm, MIX), row)],
        out_shape=[sds] * 5 + [jax.ShapeDtypeStruct((t, MIX), BF16)],
        dims=("parallel",))(h, o, za, pool, p, wo, wpg, wpe, gpost)


def _loss_grad(y, target, *, tm=512):
    t = y.shape[0]

    def body(y_ref, t_ref, dy_ref, loss_ref):
        @pl.when(pl.program_id(0) == 0)
        def _():
            loss_ref[...] = jnp.zeros_like(loss_ref)

        err = y_ref[...] - t_ref[...]
        dy_ref[...] = err * (1.0 / D_MODEL)
        part = jnp.sum(jnp.sum(err * err, axis=1, keepdims=True), axis=0, keepdims=True)
        loss_ref[...] += part * (0.5 / D_MODEL)

    blk = pl.BlockSpec((tm, D_MODEL), lambda i: (i, 0))
    return _pcall(
        body, name="loss_grad", grid=(t // tm,), in_specs=[blk, blk],
        out_specs=[blk, pl.BlockSpec((8, LANES), lambda i: (0, 0))],
        out_shape=[jax.ShapeDtypeStruct((t, D_MODEL), F32), jax.ShapeDtypeStruct((8, LANES), F32)],
        dims=("arbitrary",))(y, target)


def _out_block_bwd(dh2, gate, pe, mix, o, za, wpg, wo, gpost, gsum, *, tm=256):
    t = dh2.shape[0]

    def body(dh2_ref, gate_ref, pe_ref, mix_ref, o_ref, za_ref, wpg_ref, wo_ref, g_ref, gs_ref,
             dh1_ref, dgp_ref, dpe_ref, dmix_ref, do_ref, dza_ref, dpool_ref, delta_ref, dg_ref):
        @pl.when(pl.program_id(0) == 0)
        def _():
            dg_ref[...] = jnp.zeros_like(dg_ref)

        dh2 = dh2_ref[...]
        gate = gate_ref[...]
        dpe_ref[...] = (dh2 * gate).astype(BF16)
        dgp = (dh2 * pe_ref[...] * gate * (1.0 - gate)).astype(BF16)
        dgp_ref[...] = dgp
        dh1 = dh2 + _dot(dgp, wpg_ref[...], NT)
        dh1_ref[...] = dh1
        mix = mix_ref[...]
        r = lax.rsqrt(jnp.mean(mix * mix, axis=-1, keepdims=True) + EPS)
        dg_ref[...] += jnp.sum(dh1 * mix * r, axis=0, keepdims=True)
        a = dh1 * g_ref[...]
        dmix = (r * a - mix * (r * r * r) * jnp.mean(a * mix, axis=-1, keepdims=True)).astype(BF16)
        dmix_ref[...] = dmix
        dattn = _dot(dmix, wo_ref[:D_MODEL, :], NT)
        dpool_ref[...] = _dot(dmix, wo_ref[D_MODEL:, :], NT)
        z = za_ref[...]
        sg = _sigmoid(z)
        o = o_ref[...]
        do = (dattn * (z * sg)).astype(BF16)
        do_ref[...] = do
        dza_ref[...] = (dattn * o * (sg * (1.0 + z * (1.0 - sg)))).astype(BF16)
        prod = do.astype(F32) * o
        hi = prod.astype(BF16)
        rest = (prod - hi.astype(F32)).astype(BF16)
        delta_ref[...] = _dot(hi, gs_ref[...]) + _dot(rest, gs_ref[...])

    row = lambda i: (i, 0)
    const = lambda i: (0, 0)
    blk = pl.BlockSpec((tm, D_MODEL), row)
    sds = lambda dt: jax.ShapeDtypeStruct((t, D_MODEL), dt)
    return _pcall(
        body, name="out_block_bwd", grid=(t // tm,),
        in_specs=[blk] * 6 + [pl.BlockSpec((D_MODEL, D_MODEL), const), pl.BlockSpec((MIX, D_MODEL), const),
                              pl.BlockSpec((1, D_MODEL), const), pl.BlockSpec((D_MODEL, LANES), const)],
        out_specs=[blk] * 7 + [pl.BlockSpec((tm, LANES), row), pl.BlockSpec((1, D_MODEL), const)],
        out_shape=[sds(F32)] + [sds(BF16)] * 5 + [sds(F32), jax.ShapeDtypeStruct((t, LANES), F32),
                                                 jax.ShapeDtypeStruct((1, D_MODEL), F32)],
        dims=("arbitrary",))(dh2, gate, pe, mix, o, za, wpg, wo, gpost, gsum)


def _in_proj_bwd(dsegs, dfl, wm, wf, h, dh1, gpre, *, tm=512):
    t = h.shape[0]

    def body(*refs):
        seg_refs = refs[:N_SEG]
        dfl_ref, wm_ref, wf_ref, h_ref, dh1_ref, g_ref, dh_ref, dg_ref, acc_ref = refs[N_SEG:]
        i = pl.program_id(0)
        j = pl.program_id(1)

        @pl.when(jnp.logical_and(i == 0, j == 0))
        def _():
            dg_ref[...] = jnp.zeros_like(dg_ref)

        @pl.when(j == 0)
        def _():
            acc_ref[...] = _dot(dfl_ref[...].astype(BF16), wf_ref[...], NT)

        for sgi, ref in enumerate(seg_refs):
            @pl.when(j == sgi)
            def _(ref=ref):
                acc_ref[...] += _dot(ref[...].astype(BF16), wm_ref[...], NT)

        @pl.when(j == N_SEG - 1)
        def _():
            dhn = acc_ref[...]
            x = h_ref[...]
            r = lax.rsqrt(jnp.mean(x * x, axis=-1, keepdims=True) + EPS)
            dg_ref[...] += jnp.sum(dhn * x * r, axis=0, keepdims=True)
            a = dhn * g_ref[...]
            dh_ref[...] = dh1_ref[...] + r * a - x * (r * r * r) * jnp.mean(a * x, axis=-1, keepdims=True)

    row = lambda i, j: (i, 0)
    const = lambda i, j: (0, 0)
    blk = pl.BlockSpec((tm, D_MODEL), row)
    return _pcall(
        body, name="in_proj_bwd", grid=(t // tm, N_SEG),
        in_specs=[blk] * N_SEG + [pl.BlockSpec((tm, LANES), row), pl.BlockSpec((D_MODEL, D_MODEL), lambda i, j: (0, j)),
                                  pl.BlockSpec((D_MODEL, LANES), const), blk, blk, pl.BlockSpec((1, D_MODEL), const)],
        out_specs=[blk, pl.BlockSpec((1, D_MODEL), const)],
        out_shape=[jax.ShapeDtypeStruct((t, D_MODEL), F32), jax.ShapeDtypeStruct((1, D_MODEL), F32)],
        scratch_shapes=[pltpu.VMEM((tm, D_MODEL), F32)],
        dims=("arbitrary", "arbitrary"))(*dsegs, dfl, wm, wf, h, dh1, gpre)


def _chip_peers(x, y):
    return [(1 - x, y), (x, 1 - y), (1 - x, 1 - y)]


def _ici_views(src, land, gather, x, y, c, pi, px, py):
    if gather:
        return src.at[c], land.at[pi], land.at[pi]
    return src.at[2 * px + py], land.at[2 * x + y], land.at[2 * px + py]


def _ici_start(srcs, lands, after, *, gather, name):
    n = len(srcs)

    def body(*refs):
        src, land = refs[:n], refs[n:2 * n]
        send, recv = refs[2 * n + 1], refs[2 * n + 2]
        token = refs[-1]
        x, y, c = lax.axis_index("x"), lax.axis_index("y"), lax.axis_index("c")
        for kk in range(n):
            for pi, (px, py) in enumerate(_chip_peers(x, y)):
                mine, there, _ = _ici_views(src[kk], land[kk], gather, x, y, c, pi, px, py)
                pltpu.make_async_remote_copy(src_ref=mine, dst_ref=there, send_sem=send.at[3 * kk + pi],
                                             recv_sem=recv.at[3 * kk + pi], device_id=(px, py, c),
                                             device_id_type=MESH).start()
        token[...] = jnp.zeros_like(token)

    hbm = lambda a: pltpu.with_memory_space_constraint(a, pltpu.HBM)
    outs = pl.pallas_call(
        body, name=name,
        out_shape=(pltpu.SemaphoreType.DMA((3 * n,)), pltpu.SemaphoreType.DMA((3 * n,)),
                   *[pltpu.HBM(a.shape, a.dtype) for a in srcs], *[pltpu.HBM(a.shape, a.dtype) for a in lands],
                   jax.ShapeDtypeStruct((8, LANES), F32)),
        in_specs=[HBM_SPEC] * (2 * n) + [ANY],
        out_specs=(SEM_SPEC, SEM_SPEC, *[HBM_SPEC] * (2 * n), pl.BlockSpec(memory_space=pltpu.VMEM)),
        input_output_aliases={i: 2 + i for i in range(2 * n)},
        compiler_params=pltpu.CompilerParams(has_side_effects=SIDE_EFFECT),
    )(*[hbm(a) for a in srcs], *[hbm(a) for a in lands], after)
    return outs[0], outs[1], list(outs[2:2 + n]), list(outs[2 + n:2 + 2 * n]), outs[-1]


def _ici_wait(send, recv, srcs, lands, after, *, gather, name):
    n = len(srcs)

    def body(*refs):
        src, land = refs[:n], refs[n:2 * n]
        send_ref, recv_ref = refs[2 * n], refs[2 * n + 1]
        x, y, c = lax.axis_index("x"), lax.axis_index("y"), lax.axis_index("c")
        for kk in range(n):
            for pi, (px, py) in enumerate(_chip_peers(x, y)):
                mine, _, here = _ici_views(src[kk], land[kk], gather, x, y, c, pi, px, py)
                cp = pltpu.make_async_remote_copy(src_ref=mine, dst_ref=here, send_sem=send_ref.at[3 * kk + pi],
                                                  recv_sem=recv_ref.at[3 * kk + pi], device_id=(px, py, c),
                                                  device_id_type=MESH)
                cp.wait_send()
                cp.wait_recv()

    outs = pl.pallas_call(
        body, name=name,
        out_shape=[pltpu.HBM(a.shape, a.dtype) for a in srcs] + [pltpu.HBM(a.shape, a.dtype) for a in lands],
        in_specs=[HBM_SPEC] * (2 * n) + [SEM_SPEC, SEM_SPEC, ANY], out_specs=[HBM_SPEC] * (2 * n),
        input_output_aliases={i: i for i in range(2 * n)},
        compiler_params=pltpu.CompilerParams(has_side_effects=SIDE_EFFECT),
    )(*srcs, *lands, send, recv, after)
    return list(outs[:n]), list(outs[n:])


def _sibling_send(arrays, *, name):
    n = len(arrays)

    def body(*refs):
        ins, outs = refs[:n], refs[n:2 * n]
        send, recv = refs[2 * n:]
        sib = (lax.axis_index("x"), lax.axis_index("y"), 1 - lax.axis_index("c"))
        copies = [pltpu.make_async_remote_copy(src_ref=ins[kk], dst_ref=outs[kk], send_sem=send.at[kk],
                                               recv_sem=recv.at[kk], device_id=sib, device_id_type=MESH)
                  for kk in range(n)]
        for cp in copies:
            cp.start()
        for cp in copies:
            cp.wait_recv()
        for cp in copies:
            cp.wait_send()

    return _pcall(
        body, name=name, in_specs=[ANY] * n, out_specs=[ANY] * n,
        out_shape=[jax.ShapeDtypeStruct(a.shape, a.dtype) for a in arrays],
        scratch_shapes=[pltpu.SemaphoreType.DMA((n,)), pltpu.SemaphoreType.DMA((n,))])(*arrays)


def _all_gather_small(a, *, name):
    def body(a_ref, o_ref, send, recv, loc):
        x, y, c = lax.axis_index("x"), lax.axis_index("y"), lax.axis_index("c")
        me = 4 * x + 2 * y + c

        def peer(rel):
            fx, fy, fc = (rel >> 2) & 1, (rel >> 1) & 1, rel & 1
            px = (1 - x) if fx else x
            py = (1 - y) if fy else y
            pc = (1 - c) if fc else c
            return px, py, pc

        def remote(rel, slot):
            return pltpu.make_async_remote_copy(
                src_ref=a_ref, dst_ref=o_ref.at[slot], send_sem=send.at[rel - 1], recv_sem=recv.at[rel - 1],
                device_id=peer(rel), device_id_type=MESH)

        mine = pltpu.make_async_copy(a_ref, o_ref.at[me], loc)
        mine.start()
        sends = [remote(rel, me) for rel in range(1, 8)]
        for cp in sends:
            cp.start()
        for rel in range(1, 8):
            px, py, pc = peer(rel)
            remote(rel, 4 * px + 2 * py + pc).wait_recv()
        for cp in sends:
            cp.wait_send()
        mine.wait()

    return _pcall(
        body, name=name, in_specs=[ANY], out_specs=ANY,
        out_shape=jax.ShapeDtypeStruct((8,) + a.shape, a.dtype),
        scratch_shapes=[pltpu.SemaphoreType.DMA((7,)), pltpu.SemaphoreType.DMA((7,)), pltpu.SemaphoreType.DMA])(a)


def _pair_sum(a, b, *, name, tr=256):
    _, rows, cols = a.shape
    tr = min(tr, rows)

    def body(a_ref, b_ref, o_ref):
        o_ref[...] = (a_ref[...].astype(F32) + b_ref[...].astype(F32)).astype(BF16)

    blk = pl.BlockSpec((None, tr, cols), lambda j, i: (j, i, 0))
    return _pcall(body, name=name, grid=(4, rows // tr), in_specs=[blk, blk], out_specs=blk,
                  out_shape=jax.ShapeDtypeStruct(a.shape, BF16), dims=("parallel", "parallel"))(a, b)


def _chip_sum(own, recv, chip, *, name, tr=256):
    _, rows, cols = own.shape
    tr = min(tr, rows)

    def body(chip_ref, own_ref, r1_ref, r2_ref, r3_ref, o_ref):
        acc = own_ref[...].astype(F32)
        for ref in (r1_ref, r2_ref, r3_ref):
            acc = acc + ref[...].astype(F32)
        o_ref[...] = acc

    def slot(step):
        return pl.BlockSpec((None, tr, cols), lambda i, chip_ref: ((chip_ref[0] + step) % 4, i, 0))

    return pl.pallas_call(
        body, name=name, out_shape=jax.ShapeDtypeStruct((rows, cols), F32),
        grid_spec=pltpu.PrefetchScalarGridSpec(
            num_scalar_prefetch=1, grid=(rows // tr,), in_specs=[slot(0), slot(1), slot(2), slot(3)],
            out_specs=pl.BlockSpec((tr, cols), lambda i, chip_ref: (i, 0))),
        compiler_params=pltpu.CompilerParams(dimension_semantics=("parallel",), vmem_limit_bytes=VMEM_LIMIT),
    )(chip, own, recv, recv, recv)


def _adamw_math(w, g, m, v):
    m = ADAM_B1 * m + (1.0 - ADAM_B1) * g
    v = ADAM_B2 * v + (1.0 - ADAM_B2) * (g * g)
    m_hat = m / (1.0 - ADAM_B1 ** ADAM_STEP)
    v_hat = v / (1.0 - ADAM_B2 ** ADAM_STEP)
    delta = -ADAM_LR * (m_hat / (jnp.sqrt(v_hat) + ADAM_EPS) + ADAM_WD * w)
    return delta, m, v


def _adamw(w, m, v, g_layers, *, name, tr=128):
    depth = len(g_layers)
    rows, cols = g_layers[0].shape
    tr = min(tr, rows)
    nblk = rows // tr

    def body(*refs):
        w_ref, m_ref, v_ref = refs[:3]
        g_refs = refs[3:3 + depth]
        g_ref, d_ref, mo_ref, vo_ref = refs[3 + depth:]
        layer = pl.program_id(0)
        g = g_refs[0][...]
        for l in range(1, depth):
            g = jnp.where(layer == l, g_refs[l][...], g)
        delta, m_new, v_new = _adamw_math(w_ref[...], g, m_ref[...], v_ref[...])
        g_ref[...] = g
        d_ref[...] = delta
        mo_ref[...] = m_new
        vo_ref[...] = v_new

    def g_spec(l):
        return pl.BlockSpec((tr, cols), lambda ll, i: (jnp.where(ll == l, i, jnp.where(ll < l, 0, nblk - 1)), 0))

    blk = pl.BlockSpec((tr, cols), lambda ll, i: (ll * nblk + i, 0))
    return _pcall(
        body, name=name, grid=(depth, nblk), in_specs=[blk] * 3 + [g_spec(l) for l in range(depth)], out_specs=[blk] * 4,
        out_shape=[jax.ShapeDtypeStruct(w.shape, F32)] * 4, dims=("arbitrary", "arbitrary"))(w, m, v, *g_layers)


def _adamw_small(w, m, v, parts, *, name):
    def body(w_ref, m_ref, v_ref, p_ref, g_ref, d_ref, mo_ref, vo_ref):
        g = p_ref[0]
        for dev in range(1, 8):
            g = g + p_ref[dev]
        delta, m_new, v_new = _adamw_math(w_ref[...], g, m_ref[...], v_ref[...])
        g_ref[...] = g
        d_ref[...] = delta
        mo_ref[...] = m_new
        vo_ref[...] = v_new

    return _pcall(body, name=name, out_shape=[jax.ShapeDtypeStruct(w.shape, F32)] * 4)(w, m, v, parts)


def _rows_from_heads(a, nb, s):
    return a.reshape(nb, s, N_HEADS).transpose(0, 2, 1).reshape(nb, N_PAIRS, 2, s)


def _layer_fwd(h, p_l, wts, nb):
    t = h.shape[0]
    s = t // nb
    q, k, v, za, u, zp, fl, hn = _norm_in_proj(h, wts["gpre"], wts["wm"], wts["wf"])
    flt = fl[:, :N_HEADS].reshape(nb, s, N_HEADS).transpose(0, 2, 1)
    c = _gate_cumsum(flt, wts["bf"])
    o, lse = _attn_fwd(q, k, v, c.reshape(nb, N_PAIRS, 2, s), nb=nb)
    pool, pooled, mixed = _pool_fwd(u, zp, wts["wpool"], wts["ps"], nb=nb)
    h2, mix, h1, gate, pe, cat = _out_block(h, o, za, pool, p_l, wts["wo"], wts["wpg"], wts["wpe"], wts["gpost"])
    saved = dict(h=h, hn=hn, q=q, k=k, v=v, za=za, zp=zp, flt=flt, c=c, o=o, lse=lse, pooled=pooled, mixed=mixed,
                 mix=mix, h1=h1, gate=gate, pe=pe, cat=cat, p=p_l)
    return h2, saved


def _layer_bwd(dh2, sv, wts, gsum, nb):
    t = dh2.shape[0]
    s = t // nb
    dh1, dgp, dpe, dmix, do, dza, dpool, delta, dgpost = _out_block_bwd(
        dh2, sv["gate"], sv["pe"], sv["mix"], sv["o"], sv["za"], wts["wpg"], wts["wo"], wts["gpost"], gsum)
    g_wpe = _matmul(sv["p"], dpe, ta=True, name="wgrad_pe")
    g_wpg = _matmul(sv["h1"], dgp, ta=True, name="wgrad_pg")
    g_wo = _matmul(sv["cat"], dmix, ta=True, name="wgrad_out")

    ccol = jnp.repeat(sv["c"].transpose(0, 2, 1).reshape(t, N_HEADS), HEAD_DIM, axis=1)
    lse4 = _rows_from_heads(sv["lse"][:, ::HEAD_DIM], nb, s)
    delta4 = _rows_from_heads(delta[:, :N_HEADS], nb, s)
    dq, dk, dv, dcol, drow = _attn_bwd(sv["q"], sv["k"], sv["v"], do, ccol, lse4, delta4, nb=nb)

    def head_rows(a):
        a = a[:, ::HEAD_DIM].reshape(nb, s, N_PAIRS, 2)[..., ::-1]
        return a.reshape(nb, s, N_HEADS).transpose(0, 2, 1)

    dflt, dbf = _gate_cumsum_bwd(head_rows(dcol), head_rows(drow), sv["flt"], wts["bf"])
    dfl = jnp.pad(dflt.transpose(0, 2, 1).reshape(t, N_HEADS), ((0, 0), (0, LANES - N_HEADS)))

    dmixed, du, dzp, dps = _pool_bwd(dpool, sv["mixed"], sv["zp"], wts["wpool"], wts["ps"], nb=nb)
    g_wpool = _pool_wgrad(sv["pooled"], dmixed)

    dsegs = (dq, dk, dv, dza, du, dzp)
    dh, dgpre = _in_proj_bwd(dsegs, dfl, wts["wm"], wts["wf"], sv["h"], dh1, wts["gpre"])
    g_segs = [_matmul(sv["hn"], dx, ta=True, name="wgrad_in_f32" if dx.dtype == F32 else "wgrad_in")
              for dx in dsegs]
    g_wf = _matmul(sv["hn"], dfl, ta=True, name="wgrad_in_f")
    g_win = jnp.concatenate(g_segs[:4] + [g_wf[:, :N_HEADS]] + g_segs[4:], axis=1)
    grads = dict(w_in=g_win, w_out=g_wo, w_pg=g_wpg, w_pe=g_wpe, w_pool=g_wpool,
                 norm_pre=dgpre[0], norm_post=dgpost[0], pool_scale=dps[0], b_f=dbf[:, 0])
    return dh, grads


def kernel(x, p, norm_pre, norm_post, w_in, b_f, w_pool, pool_scale, w_out, w_pg, w_pe, loss_target, m_norm_pre, m_norm_post, m_w_in, m_b_f, m_w_pool, m_pool_scale, m_w_out, m_w_pg, m_w_pe, v_norm_pre, v_norm_post, v_w_in, v_b_f, v_w_pool, v_pool_scale, v_w_out, v_w_pg, v_w_pe):
    nb, s, _ = x.shape
    t = nb * s
    depth = w_in.shape[0]
    big = dict(w_in=(w_in, m_w_in, v_w_in), w_out=(w_out, m_w_out, v_w_out), w_pg=(w_pg, m_w_pg, v_w_pg),
               w_pe=(w_pe, m_w_pe, v_w_pe), w_pool=(w_pool, m_w_pool, v_w_pool))
    names = list(big)
    cols = {n: big[n][0].shape[-1] for n in names}
    chip = (2 * lax.axis_index("x") + lax.axis_index("y")).astype(jnp.int32).reshape(1)
    core = lax.axis_index("c")
    south = core == 0

    gathers = []
    token = jnp.zeros((8, LANES), F32)
    for l in range(depth):
        halves = [big[n][0][l].reshape(2, -1, cols[n]).astype(BF16) for n in names]
        lands = [lax.empty((3,) + a.shape[1:], a.dtype) for a in halves]
        send, recv, halves, lands, token = _ici_start(halves, lands, token, gather=True, name=f"gather_start_{l}")
        gathers.append((send, recv, halves, lands))

    def layer_weights(l, after, anchor):
        send, recv, halves, lands = gathers[l]
        halves, lands = _ici_wait(send, recv, halves, lands, after, gather=True, name=f"gather_wait_{l}")
        others = _sibling_send(lands, name="gather_pass_on")
        full = {}
        for n, own, land, other in zip(names, halves, lands, others):
            low = jnp.where(south, land, other)
            high = jnp.where(south, other, land)
            rel = [jnp.concatenate([low[pi], high[pi]], axis=0) for pi in range(3)]
            by_rel = jnp.stack([own.reshape(-1, cols[n]), rel[1], rel[0], rel[2]])
            full[n] = [lax.dynamic_index_in_dim(by_rel, j ^ chip[0], 0, keepdims=False) for j in range(4)]
        win = jnp.concatenate(full["w_in"], axis=1)
        wm = jnp.concatenate([win[:, :F_OFF], win[:, F_OFF + N_HEADS:]], axis=1)
        wf = jnp.pad(win[:, F_OFF:F_OFF + N_HEADS], ((0, 0), (0, LANES - N_HEADS)))
        return dict(
            wm=wm, wf=wf, wo=jnp.concatenate(full["w_out"], axis=0), wpg=jnp.concatenate(full["w_pg"], axis=0),
            wpe=jnp.concatenate(full["w_pe"], axis=1),
            wpool=jnp.stack(full["w_pool"]).reshape(4, N_GROUPS, GROUP_DIM // 4, GROUP_DIM).transpose(1, 0, 2, 3).reshape(
                N_GROUPS, GROUP_DIM, GROUP_DIM),
            gpre=norm_pre[l][None] + anchor, gpost=norm_post[l][None], ps=pool_scale[l][None], bf=b_f[l][:, None])

    h = x.reshape(t, D_MODEL)
    saved, layers = [], []
    after = token
    for l in range(depth):
        layers.append(layer_weights(l, after, token[0, 0]))
        h, sv = _layer_fwd(h, p[l].reshape(t, PLE_DIM), layers[l], nb)
        saved.append(sv)
        after = h
    dh, loss_blk = _loss_grad(h, loss_target.reshape(t, D_MODEL))
    loss = lax.psum(loss_blk[0, 0], ("x", "y", "c"))
    gsum = (jnp.arange(D_MODEL)[:, None] // HEAD_DIM == jnp.arange(LANES)[None, :]).astype(BF16)

    def pieces(g, name):
        if name in ("w_in", "w_pe"):
            by_chip = jnp.stack([g[:, j * cols[name]:(j + 1) * cols[name]] for j in range(4)])
        elif name == "w_pool":
            by_chip = g.reshape(N_GROUPS, 4, GROUP_DIM // 4, GROUP_DIM).transpose(1, 0, 2, 3)
        else:
            by_chip = g
        return by_chip.reshape(4, 2, -1, cols[name]).transpose(1, 0, 2, 3).astype(BF16)

    def scatter_start(l, g):
        mine, sent = [], []
        for n in names:
            halves = pieces(g[n], n)
            mine.append(lax.dynamic_index_in_dim(halves, core, 0, keepdims=False))
            sent.append(lax.dynamic_index_in_dim(halves, 1 - core, 0, keepdims=False))
        theirs = _sibling_send(sent, name="presum_exchange")
        sums = [_pair_sum(a, b, name="presum_" + n) for n, a, b in zip(names, mine, theirs)]
        lands = [lax.empty(a.shape, a.dtype) for a in sums]
        return _ici_start(sums, lands, token, gather=False, name=f"scatter_start_{l}")

    def scatter_finish(l, state, after):
        send, recv, sums, lands, _ = state
        sums, lands = _ici_wait(send, recv, sums, lands, after, gather=False, name=f"scatter_wait_{l}")
        mine = [_chip_sum(a, r, chip, name="chip_sum_" + n) for n, a, r in zip(names, sums, lands)]
        theirs = _sibling_send(mine, name="halves_exchange")
        return {n: jnp.concatenate([jnp.where(south, a, b), jnp.where(south, b, a)], axis=0)
                for n, a, b in zip(names, mine, theirs)}

    grads = [None] * depth
    reduced = [None] * depth
    pending = None
    for l in reversed(range(depth)):
        wts = layers[l]
        if pending is not None:
            wts = dict(wts, gpost=wts["gpost"] + pending[1][4][0, 0])
        dh, grads[l] = _layer_bwd(dh, saved[l], wts, gsum, nb)
        if pending is not None:
            reduced[pending[0]] = scatter_finish(pending[0], pending[1], dh)
        pending = (l, scatter_start(l, grads[l]))
    grad_x = dh.reshape(nb, s, D_MODEL)

    small = dict(norm_pre=(norm_pre, m_norm_pre, v_norm_pre), norm_post=(norm_post, m_norm_post, v_norm_post),
                 pool_scale=(pool_scale, m_pool_scale, v_pool_scale), b_f=(b_f, m_b_f, v_b_f))

    def pack(get):
        rows = []
        for n in small:
            a = get(n)
            rows.append(jnp.pad(a, ((0, 0), (0, D_MODEL - a.shape[1]))))
        return jnp.concatenate(rows, axis=0)

    parts = _all_gather_small(pack(lambda n: jnp.stack([grads[l][n] for l in range(depth)])), name="gather_small")
    small_packed = _adamw_small(pack(lambda n: small[n][0]), pack(lambda n: small[n][1]), pack(lambda n: small[n][2]),
                                parts, name="adamw_small")
    small_out = {}
    for i, n in enumerate(small):
        width = small[n][0].shape[1]
        small_out[n] = [o[depth * i:depth * (i + 1), :width] for o in small_packed]

    reduced[pending[0]] = scatter_finish(pending[0], pending[1], small_packed[0])
    big_out = {}
    for n in names:
        w, m, v = big[n]
        outs = _adamw(w.reshape(-1, cols[n]), m.reshape(-1, cols[n]), v.reshape(-1, cols[n]),
                      [reduced[l][n] for l in range(depth)], name="adamw_" + n)
        big_out[n] = [o.reshape(w.shape) for o in outs]

    order =["norm_pre", "norm_post", "w_in", "b_f", "w_pool", "pool_scale", "w_out", "w_pg", "w_pe"]
    result = [loss, grad_x]
    for kind in range(4):
        for n in order:
            result.append(big_out[n][kind] if n in big_out else small_out[n][kind])
    return tuple(result)
```

```python
import functools

import jax
import jax.numpy as jnp
from jax import lax
from jax.experimental import pallas as pl
from jax.experimental.pallas import tpu as pltpu

F32 = jnp.float32
BF16 = jnp.bfloat16

D_MODEL = 1024
N_HEADS = 16
HEAD_DIM = 64
PAIR = 2 * HEAD_DIM
N_PAIRS = N_HEADS // 2
N_GROUPS = 4
GROUP_DIM = 256
PLE_DIM = 256
MIX = 2 * D_MODEL
N_SEG = 6
F_OFF = 4 * D_MODEL
IN_COLS = N_SEG * D_MODEL + N_HEADS
LANES = 128
EPS = 1e-6
SCALE = 0.125
NEG = -1e30

ADAM_LR = 0.001
ADAM_B1 = 0.9
ADAM_B2 = 0.999
ADAM_EPS = 1e-08
ADAM_WD = 0.01
ADAM_STEP = 10

VMEM_LIMIT = 56 * 1024 * 1024
MESH = pl.DeviceIdType.MESH
ANY = pl.BlockSpec(memory_space=pl.ANY)
HBM_SPEC = pl.BlockSpec(memory_space=pltpu.HBM)
SEM_SPEC = pl.BlockSpec(memory_space=pltpu.SEMAPHORE)
SIDE_EFFECT = pltpu.SideEffectType.DATAFLOW_SIDE_EFFECTING

NT = (((1,), (1,)), ((), ()))
TN = (((0,), (0,)), ((), ()))
NN = (((1,), (0,)), ((), ()))


def _pcall(body, *, name, out_shape, grid=(), in_specs=None, out_specs=None, scratch_shapes=(), dims=None):
    kw = {}
    if in_specs is not None:
        kw["in_specs"] = in_specs
    if out_specs is not None:
        kw["out_specs"] = out_specs
    return pl.pallas_call(
        body, name=name, out_shape=out_shape, grid=grid, scratch_shapes=list(scratch_shapes),
        compiler_params=pltpu.CompilerParams(dimension_semantics=dims, vmem_limit_bytes=VMEM_LIMIT), **kw)


def _dot(a, b, dn=NN):
    return lax.dot_general(a, b, dn, preferred_element_type=F32)


def _sigmoid(x):
    return 1.0 / (1.0 + jnp.exp(-x))


def _matmul(a, b, *, ta=False, name, tm=512, tn=1024, tk=2048):
    if ta:
        kdim, m = a.shape
    else:
        m, kdim = a.shape
    n = b.shape[1]
    tm, tn, tk = min(tm, m), min(tn, n), min(tk, kdim)
    nk = kdim // tk

    def body(a_ref, b_ref, o_ref, acc_ref):
        k = pl.program_id(2)

        @pl.when(k == 0)
        def _():
            acc_ref[...] = jnp.zeros_like(acc_ref)

        acc_ref[...] += _dot(a_ref[...].astype(BF16), b_ref[...].astype(BF16), TN if ta else NN)

        @pl.when(k == nk - 1)
        def _():
            o_ref[...] = acc_ref[...].astype(BF16)

    a_spec = pl.BlockSpec((tk, tm), lambda i, j, k: (k, i)) if ta else pl.BlockSpec((tm, tk), lambda i, j, k: (i, k))
    return _pcall(
        body, name=name, out_shape=jax.ShapeDtypeStruct((m, n), BF16), grid=(m // tm, n // tn, nk),
        in_specs=[a_spec, pl.BlockSpec((tk, tn), lambda i, j, k: (k, j))],
        out_specs=pl.BlockSpec((tm, tn), lambda i, j, k: (i, j)),
        scratch_shapes=[pltpu.VMEM((tm, tn), F32)], dims=("parallel", "parallel", "arbitrary"))(a, b)


def _norm_in_proj(h, g, wm, wf, *, tm=512):
    t = h.shape[0]

    def body(h_ref, g_ref, wm_ref, wf_ref, q_ref, k_ref, v_ref, za_ref, u_ref, zp_ref, fl_ref, hn_ref):
        j = pl.program_id(1)

        @pl.when(j == 0)
        def _():
            x = h_ref[...]
            r = lax.rsqrt(jnp.mean(x * x, axis=-1, keepdims=True) + EPS)
            hn = (x * r * g_ref[...]).astype(BF16)
            hn_ref[...] = hn
            fl_ref[...] = _dot(hn, wf_ref[...])

        y = _dot(hn_ref[...], wm_ref[...])
        for s, ref in enumerate((q_ref, k_ref, v_ref, za_ref, u_ref, zp_ref)):
            @pl.when(j == s)
            def _(ref=ref):
                ref[...] = y.astype(ref.dtype)

    row = lambda i, j: (i, 0)
    blk = pl.BlockSpec((tm, D_MODEL), row)
    sds = lambda dt: jax.ShapeDtypeStruct((t, D_MODEL), dt)
    return _pcall(
        body, name="norm_in_proj", grid=(t // tm, N_SEG),
        in_specs=[blk, pl.BlockSpec((1, D_MODEL), lambda i, j: (0, 0)),
                  pl.BlockSpec((D_MODEL, D_MODEL), lambda i, j: (0, j)),
                  pl.BlockSpec((D_MODEL, LANES), lambda i, j: (0, 0))],
        out_specs=[blk] * 6 + [pl.BlockSpec((tm, LANES), row), blk],
        out_shape=[sds(BF16)] * 3 + [sds(F32)] * 3 + [jax.ShapeDtypeStruct((t, LANES), F32), sds(BF16)],
        dims=("parallel", "arbitrary"))(h, g, wm, wf)


def _gate_cumsum(flt, bias):
    b, hh, s = flt.shape

    def body(fl_ref, b_ref, c_ref):
        x = fl_ref[...] + b_ref[...]
        acc = jnp.minimum(x, 0.0) - jnp.log(1.0 + jnp.exp(-jnp.abs(x)))
        idx = lax.broadcasted_iota(jnp.int32, x.shape, 1)
        sh = 1
        while sh < s:
            acc = acc + jnp.where(idx >= sh, pltpu.roll(acc, sh, 1), 0.0)
            sh *= 2
        c_ref[...] = acc

    return _pcall(
        body, name="gate_cumsum", grid=(b,),
        in_specs=[pl.BlockSpec((None, hh, s), lambda i: (i, 0, 0)), pl.BlockSpec((hh, 1), lambda i: (0, 0))],
        out_specs=pl.BlockSpec((None, hh, s), lambda i: (i, 0, 0)),
        out_shape=jax.ShapeDtypeStruct((b, hh, s), F32), dims=("parallel",))(flt, bias)


def _gate_cumsum_bwd(dcol, drow, flt, bias):
    b, hh, s = flt.shape

    def body(d_ref, dr_ref, fl_ref, b_ref, dfl_ref, db_ref):
        @pl.when(pl.program_id(0) == 0)
        def _():
            db_ref[...] = jnp.zeros_like(db_ref)

        x = fl_ref[...] + b_ref[...]
        acc = dr_ref[...] - d_ref[...]
        idx = lax.broadcasted_iota(jnp.int32, x.shape, 1)
        sh = 1
        while sh < s:
            acc = acc + jnp.where(idx + sh < s, pltpu.roll(acc, s - sh, 1), 0.0)
            sh *= 2
        e = jnp.exp(-jnp.abs(x))
        sig_neg = jnp.where(x >= 0.0, e, 1.0) / (1.0 + e)
        dfl = acc * sig_neg
        dfl_ref[...] = dfl
        db_ref[...] += jnp.sum(dfl, axis=1, keepdims=True)

    return _pcall(
        body, name="gate_cumsum_bwd", grid=(b,),
        in_specs=[pl.BlockSpec((None, hh, s), lambda i: (i, 0, 0))] * 3 + [pl.BlockSpec((hh, 1), lambda i: (0, 0))],
        out_specs=[pl.BlockSpec((None, hh, s), lambda i: (i, 0, 0)), pl.BlockSpec((hh, 1), lambda i: (0, 0))],
        out_shape=[jax.ShapeDtypeStruct((b, hh, s), F32), jax.ShapeDtypeStruct((hh, 1), F32)],
        dims=("arbitrary",))(dcol, drow, flt, bias)


def _attn_fwd(q, k, v, c4, *, nb, tq=512, rc=32, pv_rows=512):
    t = q.shape[0]
    s = t // nb
    tq = min(tq, s)
    nq = s // tq
    tk = tq
    pv_rows = min(pv_rows, tq)

    def body(q_ref, k_ref, v_ref, c_ref, o_ref, lse_ref, qh_scr, s_scr, p_scr, m_scr, acc_scr):
        qi = pl.program_id(2)
        lo = lax.broadcasted_iota(jnp.int32, (tq, PAIR), 1) < HEAD_DIM
        q2 = q_ref[...] * SCALE
        zero = jnp.zeros_like(q2)
        qh_scr[0] = jnp.where(lo, q2, zero)
        qh_scr[1] = jnp.where(lo, zero, q2)
        m_scr[...] = jnp.full(m_scr.shape, NEG, F32)
        acc_scr[...] = jnp.zeros(acc_scr.shape, F32)
        row = lax.broadcasted_iota(jnp.int32, (rc, tk), 0)
        col = lax.broadcasted_iota(jnp.int32, (rc, tk), 1)

        def scores(kj, slot):
            k2 = k_ref[pl.ds(pl.multiple_of(kj * tk, tk), tk), :]
            for hd in range(2):
                s_scr[slot, hd] = _dot(qh_scr[hd], k2, NT)

        def softmax_pv(kj, slot, masked):
            off = pl.multiple_of(kj * tk, tk)
            v2 = v_ref[pl.ds(off, tk), :]
            one = jnp.ones_like(v2)
            va = (jnp.where(lo, v2, one), jnp.where(lo, one, v2))
            crow = c_ref[:, pl.ds(off, tk)]
            for hd in range(2):
                for r0 in range(0, tq, pv_rows):
                    for r in range(r0, r0 + pv_rows, rc):
                        sc = s_scr[slot, hd, r:r + rc, :] - crow[hd:hd + 1, :]
                        if masked:
                            sc = jnp.where(row + r >= col, sc, NEG)
                        m_old = m_scr[hd, r:r + rc, :]
                        m_new = jnp.maximum(m_old, jnp.max(sc, axis=1, keepdims=True))
                        for cb in range(0, tk, LANES):
                            p_scr[hd, r:r + rc, cb:cb + LANES] = jnp.exp(sc[:, cb:cb + LANES] - m_new).astype(BF16)
                        m_scr[hd, r:r + rc, :] = m_new
                        acc_scr[hd, r:r + rc, :] = acc_scr[hd, r:r + rc, :] * jnp.exp(m_old - m_new)
                    acc_scr[hd, r0:r0 + pv_rows, :] += _dot(p_scr[hd, r0:r0 + pv_rows, :], va[hd])

        def unmasked(kj, carry):
            scores(kj, 0)
            softmax_pv(kj, 0, False)
            return carry

        lax.fori_loop(0, qi, unmasked, 0)
        scores(qi, 0)
        softmax_pv(qi, 0, True)
        a0, a1 = acc_scr[0], acc_scr[1]
        den = jnp.where(lo, pltpu.roll(a0, HEAD_DIM, 1), pltpu.roll(a1, HEAD_DIM, 1))
        o_ref[...] = jnp.where(lo, a0, a1) / den
        lse_t = (jnp.where(lo, m_scr[0], m_scr[1]) + jnp.log(den)).T
        lse_ref[0:1, :] = lse_t[0:1, :]
        lse_ref[1:2, :] = lse_t[HEAD_DIM:HEAD_DIM + 1, :]

    qspec = pl.BlockSpec((tq, PAIR), lambda b, hp, i: (b * nq + i, hp))
    kvspec = pl.BlockSpec((s, PAIR), lambda b, hp, i: (b, hp))
    return _pcall(
        body, name="attn_fwd", grid=(nb, N_PAIRS, nq),
        in_specs=[qspec, kvspec, kvspec, pl.BlockSpec((None, None, 2, s), lambda b, hp, i: (b, hp, 0, 0))],
        out_specs=[qspec, pl.BlockSpec((None, None, 2, tq), lambda b, hp, i: (b, hp, 0, i))],
        out_shape=[jax.ShapeDtypeStruct((t, D_MODEL), F32), jax.ShapeDtypeStruct((nb, N_PAIRS, 2, s), F32)],
        scratch_shapes=[pltpu.VMEM((2, tq, PAIR), BF16), pltpu.VMEM((1, 2, tq, tk), F32), pltpu.VMEM((2, tq, tk), BF16),
                        pltpu.VMEM((2, tq, LANES), F32), pltpu.VMEM((2, tq, PAIR), F32)],
        dims=("parallel", "parallel", "arbitrary"))(q, k, v, c4)


def _attn_bwd(q, k, v, do, c4, lse4, delta4, *, nb, tk=512, rc=16):
    t = q.shape[0]
    s = t // nb
    tk = min(tk, s)
    nk = s // tk
    tq = tk

    def body(q_ref, do_ref, k_ref, v_ref, c_ref, lse_ref, dl_ref, dq_ref, dk_ref, dv_ref, dc_ref, dr_ref,
             kz_scr, vz_scr, ko_scr, crep_scr, st_scr, dp_scr, pt_scr, ds_scr, dk_scr, dv_scr, dr_scr):
        kj = pl.program_id(2)

        @pl.when(kj == 0)
        def _():
            dq_ref[...] = jnp.zeros_like(dq_ref)
            dr_scr[...] = jnp.zeros_like(dr_scr)

        lo = lax.broadcasted_iota(jnp.int32, (tk, PAIR), 1) < HEAD_DIM
        k2 = k_ref[...]
        v2 = v_ref[...]
        zero = jnp.zeros_like(k2)
        one = jnp.ones_like(k2)
        kz_scr[0] = jnp.where(lo, k2, zero)
        kz_scr[1] = jnp.where(lo, zero, k2)
        vz_scr[0] = jnp.where(lo, v2, zero)
        vz_scr[1] = jnp.where(lo, zero, v2)
        ko_scr[0] = jnp.where(lo, k2, one)
        ko_scr[1] = jnp.where(lo, one, k2)
        for hd in range(2):
            crep_scr[hd] = jnp.broadcast_to(c_ref[hd:hd + 1, :], (LANES, tk)).T
        dk_scr[...] = jnp.zeros(dk_scr.shape, F32)
        dv_scr[...] = jnp.zeros(dv_scr.shape, F32)
        row = lax.broadcasted_iota(jnp.int32, (rc, LANES), 0)
        col = lax.broadcasted_iota(jnp.int32, (rc, LANES), 1)

        def step(qi, masked):
            off = pl.multiple_of(qi * tq, tq)
            q2 = q_ref[pl.ds(off, tq), :] * SCALE
            do2 = do_ref[pl.ds(off, tq), :]
            lse = lse_ref[:, pl.ds(off, tq)]
            dl = dl_ref[:, pl.ds(off, tq)]
            qo = (jnp.where(lo, q2, jnp.ones_like(q2)), jnp.where(lo, jnp.ones_like(q2), q2))
            for hd in range(2):
                st_scr[hd] = _dot(kz_scr[hd], q2, NT)
                dp_scr[hd] = _dot(vz_scr[hd], do2, NT)
            dqs = []
            for hd in range(2):
                for r in range(0, tk, rc):
                    c_rep = crep_scr[hd, r:r + rc, :]
                    for cb in range(0, tq, LANES):
                        pt = jnp.exp(st_scr[hd, r:r + rc, cb:cb + LANES] - c_rep - lse[hd:hd + 1, cb:cb + LANES])
                        if masked:
                            pt = jnp.where(col + cb >= row + r, pt, 0.0)
                        dst = pt * (dp_scr[hd, r:r + rc, cb:cb + LANES] - dl[hd:hd + 1, cb:cb + LANES])
                        pt_scr[hd, r:r + rc, cb:cb + LANES] = pt.astype(BF16)
                        ds_scr[hd, r:r + rc, cb:cb + LANES] = dst.astype(BF16)
                dv_scr[hd] += _dot(pt_scr[hd], do2)
                dk_scr[hd] += _dot(ds_scr[hd], qo[hd])
                dqs.append(_dot(ds_scr[hd], ko_scr[hd], TN))
            dq_ref[pl.ds(off, tq), :] += jnp.where(lo, dqs[0], dqs[1]) * SCALE
            dr_scr[pl.ds(off, tq), :] += jnp.where(lo, dqs[1], dqs[0])

        step(kj, True)

        def unmasked(qi, carry):
            step(qi, False)
            return carry

        lax.fori_loop(kj + 1, nk, unmasked, 0)
        dk_ref[...] = jnp.where(lo, dk_scr[0], dk_scr[1]).astype(BF16)
        dv_ref[...] = jnp.where(lo, dv_scr[0], dv_scr[1]).astype(BF16)
        dc_t = jnp.where(lo, dk_scr[1], dk_scr[0]).T
        dc_ref[0:1, :] = dc_t[HEAD_DIM:HEAD_DIM + 1, :]
        dc_ref[1:2, :] = dc_t[0:1, :]

        @pl.when(kj == nk - 1)
        def _():
            for r in range(0, s, tq):
                dr_t = dr_scr[r:r + tq, :].T
                dr_ref[0:1, r:r + tq] = dr_t[HEAD_DIM:HEAD_DIM + 1, :]
                dr_ref[1:2, r:r + tq] = dr_t[0:1, :]

    full = pl.BlockSpec((s, PAIR), lambda b, hp, j: (b, hp))
    kblk = pl.BlockSpec((tk, PAIR), lambda b, hp, j: (b * nk + j, hp))
    rows = pl.BlockSpec((None, None, 2, s), lambda b, hp, j: (b, hp, 0, 0))
    krows = pl.BlockSpec((None, None, 2, tk), lambda b, hp, j: (b, hp, 0, j))
    sds = lambda dt: jax.ShapeDtypeStruct((t, D_MODEL), dt)
    rows_sds = jax.ShapeDtypeStruct((nb, N_PAIRS, 2, s), F32)
    pair_bf = pltpu.VMEM((2, tk, PAIR), BF16)
    pair_f = pltpu.VMEM((2, tk, PAIR), F32)
    return _pcall(
        body, name="attn_bwd", grid=(nb, N_PAIRS, nk),
        in_specs=[full, full, kblk, kblk, krows, rows, rows],
        out_specs=[full, kblk, kblk, krows, rows],
        out_shape=[sds(F32), sds(BF16), sds(BF16), rows_sds, rows_sds],
        scratch_shapes=[pair_bf, pair_bf, pair_bf, pair_f, pltpu.VMEM((2, tk, tq), F32), pltpu.VMEM((2, tk, tq), F32),
                        pltpu.VMEM((2, tk, tq), BF16), pltpu.VMEM((2, tk, tq), BF16), pair_f, pair_f,
                        pltpu.VMEM((s, PAIR), F32)],
        dims=("parallel", "parallel", "arbitrary"))(q, do, k, v, c4, lse4, delta4)


def _window_sum(x, g, s, *, ahead):
    row = lax.broadcasted_iota(jnp.int32, x.shape, 0)
    for step in range(N_GROUPS):
        sh = 1 << step
        if ahead:
            moved = jnp.where(row + sh < s, pltpu.roll(x, s - sh, 0), 0.0)
        else:
            moved = jnp.where(row >= sh, pltpu.roll(x, sh, 0), 0.0)
        x = jnp.where(step <= g, x + moved, x)
    return x


def _inv_count(g, s):
    pos = lax.broadcasted_iota(jnp.int32, (s, 1), 0) + 1
    return 1.0 / jnp.minimum(pos, 2 << g).astype(F32)


def _pool_fwd(u, zp, wpool, ps, *, nb):
    t = u.shape[0]
    s = t // nb

    def body(u_ref, zp_ref, w_ref, ps_ref, out_ref, pooled_ref, mixed_ref):
        g = pl.program_id(1)
        x = u_ref[...]
        pooled = (_window_sum(x, g, s, ahead=False) * _inv_count(g, s) - x).astype(BF16)
        mixed = _dot(pooled, w_ref[...])
        z = zp_ref[...]
        pooled_ref[...] = pooled
        mixed_ref[...] = mixed
        out_ref[...] = (mixed * ps_ref[...] * (z * _sigmoid(z))).astype(BF16)

    blk = pl.BlockSpec((s, GROUP_DIM), lambda b, g: (b, g))
    sds = lambda dt: jax.ShapeDtypeStruct((t, D_MODEL), dt)
    return _pcall(
        body, name="pool_fwd", grid=(nb, N_GROUPS),
        in_specs=[blk, blk, pl.BlockSpec((None, GROUP_DIM, GROUP_DIM), lambda b, g: (g, 0, 0)),
                  pl.BlockSpec((1, GROUP_DIM), lambda b, g: (0, g))],
        out_specs=[blk, blk, blk], out_shape=[sds(BF16), sds(BF16), sds(F32)],
        dims=("parallel", "parallel"))(u, zp, wpool, ps)


def _pool_bwd(dpool, mixed, zp, wpool, ps, *, nb):
    t = dpool.shape[0]
    s = t // nb

    def body(dp_ref, mx_ref, zp_ref, w_ref, ps_ref, dmx_ref, du_ref, dzp_ref, dps_ref):
        g = pl.program_id(0)

        @pl.when(pl.program_id(1) == 0)
        def _():
            dps_ref[...] = jnp.zeros_like(dps_ref)

        dp = dp_ref[...]
        mixed = mx_ref[...]
        z = zp_ref[...]
        scale = ps_ref[...]
        sg = _sigmoid(z)
        silu = z * sg
        dmixed = (dp * scale * silu).astype(BF16)
        dmx_ref[...] = dmixed
        dps_ref[...] += jnp.sum(dp * mixed * silu, axis=0, keepdims=True)
        dzp_ref[...] = (dp * mixed * scale * (sg * (1.0 + z * (1.0 - sg)))).astype(BF16)
        dpooled = _dot(dmixed, w_ref[...], NT)
        du = _window_sum(dpooled * _inv_count(g, s), g, s, ahead=True) - dpooled
        du_ref[...] = du.astype(BF16)

    blk = pl.BlockSpec((s, GROUP_DIM), lambda g, b: (b, g))
    sds = lambda dt: jax.ShapeDtypeStruct((t, D_MODEL), dt)
    return _pcall(
        body, name="pool_bwd", grid=(N_GROUPS, nb),
        in_specs=[blk, blk, blk, pl.BlockSpec((None, GROUP_DIM, GROUP_DIM), lambda g, b: (g, 0, 0)),
                  pl.BlockSpec((1, GROUP_DIM), lambda g, b: (0, g))],
        out_specs=[blk, blk, blk, pl.BlockSpec((1, GROUP_DIM), lambda g, b: (0, g))],
        out_shape=[sds(BF16), sds(BF16), sds(BF16), jax.ShapeDtypeStruct((1, D_MODEL), F32)],
        dims=("parallel", "arbitrary"))(dpool, mixed, zp, wpool, ps)


def _pool_wgrad(pooled, dmixed, *, tk=1024):
    t = pooled.shape[0]
    tk = min(tk, t)
    nk = t // tk

    def body(a_ref, b_ref, o_ref):
        @pl.when(pl.program_id(1) == 0)
        def _():
            o_ref[...] = jnp.zeros_like(o_ref)

        o_ref[...] += _dot(a_ref[...], b_ref[...], TN)

    blk = pl.BlockSpec((tk, GROUP_DIM), lambda g, k: (k, g))
    return _pcall(
        body, name="pool_wgrad", grid=(N_GROUPS, nk), in_specs=[blk, blk],
        out_specs=pl.BlockSpec((None, GROUP_DIM, GROUP_DIM), lambda g, k: (g, 0, 0)),
        out_shape=jax.ShapeDtypeStruct((N_GROUPS, GROUP_DIM, GROUP_DIM), F32),
        dims=("parallel", "arbitrary"))(pooled, dmixed)


def _out_block(h, o, za, pool, p, wo, wpg, wpe, gpost, *, tm=256):
    t = h.shape[0]

    def body(h_ref, o_ref, za_ref, pool_ref, p_ref, wo_ref, wpg_ref, wpe_ref, g_ref,
             h2_ref, mix_ref, h1_ref, gate_ref, pe_ref, cat_ref):
        z = za_ref[...]
        a = (o_ref[...] * (z * _sigmoid(z))).astype(BF16)
        pool_v = pool_ref[...]
        cat_ref[:, :D_MODEL] = a
        cat_ref[:, D_MODEL:] = pool_v
        mix = _dot(a, wo_ref[:D_MODEL, :]) + _dot(pool_v, wo_ref[D_MODEL:, :])
        r = lax.rsqrt(jnp.mean(mix * mix, axis=-1, keepdims=True) + EPS)
        h1 = h_ref[...] + mix * r * g_ref[...]
        gate = _sigmoid(_dot(h1.astype(BF16), wpg_ref[...]))
        pe = _dot(p_ref[...].astype(BF16), wpe_ref[...])
        mix_ref[...] = mix
        h1_ref[...] = h1
        gate_ref[...] = gate
        pe_ref[...] = pe
        h2_ref[...] = h1 + gate * pe

    row = lambda i: (i, 0)
    const = lambda i: (0, 0)
    blk = pl.BlockSpec((tm, D_MODEL), row)
    sds = jax.ShapeDtypeStruct((t, D_MODEL), F32)
    return _pcall(
        body, name="out_block", grid=(t // tm,),
        in_specs=[blk, blk, blk, blk, pl.BlockSpec((tm, PLE_DIM), row),
                  pl.BlockSpec((MIX, D_MODEL), const), pl.BlockSpec((D_MODEL, D_MODEL), const),
                  pl.BlockSpec((PLE_DIM, D_MODEL), const), pl.BlockSpec((1, D_MODEL), const)],
        out_specs=[blk] * 5 + [pl.BlockSpec((tm, MIX), row)],
        out_shape=[sds] * 5 + [jax.ShapeDtypeStruct((t, MIX), BF16)],
        dims=("parallel",))(h, o, za, pool, p, wo, wpg, wpe, gpost)


def _loss_grad(y, target, *, tm=512):
    t = y.shape[0]

    def body(y_ref, t_ref, dy_ref, loss_ref):
        @pl.when(pl.program_id(0) == 0)
        def _():
            loss_ref[...] = jnp.zeros_like(loss_ref)

        err = y_ref[...] - t_ref[...]
        dy_ref[...] = err * (1.0 / D_MODEL)
        part = jnp.sum(jnp.sum(err * err, axis=1, keepdims=True), axis=0, keepdims=True)
        loss_ref[...] += part * (0.5 / D_MODEL)

    blk = pl.BlockSpec((tm, D_MODEL), lambda i: (i, 0))
    return _pcall(
        body, name="loss_grad", grid=(t // tm,), in_specs=[blk, blk],
        out_specs=[blk, pl.BlockSpec((8, LANES), lambda i: (0, 0))],
        out_shape=[jax.ShapeDtypeStruct((t, D_MODEL), F32), jax.ShapeDtypeStruct((8, LANES), F32)],
        dims=("arbitrary",))(y, target)


def _out_block_bwd(dh2, gate, pe, mix, o, za, wpg, wo, gpost, gsum, *, tm=256):
    t = dh2.shape[0]

    def body(dh2_ref, gate_ref, pe_ref, mix_ref, o_ref, za_ref, wpg_ref, wo_ref, g_ref, gs_ref,
             dh1_ref, dgp_ref, dpe_ref, dmix_ref, do_ref, dza_ref, dpool_ref, delta_ref, dg_ref):
        @pl.when(pl.program_id(0) == 0)
        def _():
            dg_ref[...] = jnp.zeros_like(dg_ref)

        dh2 = dh2_ref[...]
        gate = gate_ref[...]
        dpe_ref[...] = (dh2 * gate).astype(BF16)
        dgp = (dh2 * pe_ref[...] * gate * (1.0 - gate)).astype(BF16)
        dgp_ref[...] = dgp
        dh1 = dh2 + _dot(dgp, wpg_ref[...], NT)
        dh1_ref[...] = dh1
        mix = mix_ref[...]
        r = lax.rsqrt(jnp.mean(mix * mix, axis=-1, keepdims=True) + EPS)
        dg_ref[...] += jnp.sum(dh1 * mix * r, axis=0, keepdims=True)
        a = dh1 * g_ref[...]
        dmix = (r * a - mix * (r * r * r) * jnp.mean(a * mix, axis=-1, keepdims=True)).astype(BF16)
        dmix_ref[...] = dmix
        dattn = _dot(dmix, wo_ref[:D_MODEL, :], NT)
        dpool_ref[...] = _dot(dmix, wo_ref[D_MODEL:, :], NT)
        z = za_ref[...]
        sg = _sigmoid(z)
        o = o_ref[...]
        do = (dattn * (z * sg)).astype(BF16)
        do_ref[...] = do
        dza_ref[...] = (dattn * o * (sg * (1.0 + z * (1.0 - sg)))).astype(BF16)
        prod = do.astype(F32) * o
        hi = prod.astype(BF16)
        rest = (prod - hi.astype(F32)).astype(BF16)
        delta_ref[...] = _dot(hi, gs_ref[...]) + _dot(rest, gs_ref[...])

    row = lambda i: (i, 0)
    const = lambda i: (0, 0)
    blk = pl.BlockSpec((tm, D_MODEL), row)
    sds = lambda dt: jax.ShapeDtypeStruct((t, D_MODEL), dt)
    return _pcall(
        body, name="out_block_bwd", grid=(t // tm,),
        in_specs=[blk] * 6 + [pl.BlockSpec((D_MODEL, D_MODEL), const), pl.BlockSpec((MIX, D_MODEL), const),
                              pl.BlockSpec((1, D_MODEL), const), pl.BlockSpec((D_MODEL, LANES), const)],
        out_specs=[blk] * 7 + [pl.BlockSpec((tm, LANES), row), pl.BlockSpec((1, D_MODEL), const)],
        out_shape=[sds(F32)] + [sds(BF16)] * 5 + [sds(F32), jax.ShapeDtypeStruct((t, LANES), F32),
                                                 jax.ShapeDtypeStruct((1, D_MODEL), F32)],
        dims=("arbitrary",))(dh2, gate, pe, mix, o, za, wpg, wo, gpost, gsum)


def _in_proj_bwd(dsegs, dfl, wm, wf, h, dh1, gpre, *, tm=512):
    t = h.shape[0]

    def body(*refs):
        seg_refs = refs[:N_SEG]
        dfl_ref, wm_ref, wf_ref, h_ref, dh1_ref, g_ref, dh_ref, dg_ref, acc_ref = refs[N_SEG:]
        i = pl.program_id(0)
        j = pl.program_id(1)

        @pl.when(jnp.logical_and(i == 0, j == 0))
        def _():
            dg_ref[...] = jnp.zeros_like(dg_ref)

        @pl.when(j == 0)
        def _():
            acc_ref[...] = _dot(dfl_ref[...].astype(BF16), wf_ref[...], NT)

        for sgi, ref in enumerate(seg_refs):
            @pl.when(j == sgi)
            def _(ref=ref):
                acc_ref[...] += _dot(ref[...].astype(BF16), wm_ref[...], NT)

        @pl.when(j == N_SEG - 1)
        def _():
            dhn = acc_ref[...]
            x = h_ref[...]
            r = lax.rsqrt(jnp.mean(x * x, axis=-1, keepdims=True) + EPS)
            dg_ref[...] += jnp.sum(dhn * x * r, axis=0, keepdims=True)
            a = dhn * g_ref[...]
            dh_ref[...] = dh1_ref[...] + r * a - x * (r * r * r) * jnp.mean(a * x, axis=-1, keepdims=True)

    row = lambda i, j: (i, 0)
    const = lambda i, j: (0, 0)
    blk = pl.BlockSpec((tm, D_MODEL), row)
    return _pcall(
        body, name="in_proj_bwd", grid=(t // tm, N_SEG),
        in_specs=[blk] * N_SEG + [pl.BlockSpec((tm, LANES), row), pl.BlockSpec((D_MODEL, D_MODEL), lambda i, j: (0, j)),
                                  pl.BlockSpec((D_MODEL, LANES), const), blk, blk, pl.BlockSpec((1, D_MODEL), const)],
        out_specs=[blk, pl.BlockSpec((1, D_MODEL), const)],
        out_shape=[jax.ShapeDtypeStruct((t, D_MODEL), F32), jax.ShapeDtypeStruct((1, D_MODEL), F32)],
        scratch_shapes=[pltpu.VMEM((tm, D_MODEL), F32)],
        dims=("arbitrary", "arbitrary"))(*dsegs, dfl, wm, wf, h, dh1, gpre)


def _chip_peers(x, y):
    return [(1 - x, y), (x, 1 - y), (1 - x, 1 - y)]


def _ici_views(src, land, gather, x, y, c, pi, px, py):
    if gather:
        return src.at[c], land.at[pi], land.at[pi]
    return src.at[2 * px + py], land.at[2 * x + y], land.at[2 * px + py]


def _ici_start(srcs, lands, after, *, gather, name):
    n = len(srcs)

    def body(*refs):
        src, land = refs[:n], refs[n:2 * n]
        send, recv = refs[2 * n + 1], refs[2 * n + 2]
        token = refs[-1]
        x, y, c = lax.axis_index("x"), lax.axis_index("y"), lax.axis_index("c")
        for kk in range(n):
            for pi, (px, py) in enumerate(_chip_peers(x, y)):
                mine, there, _ = _ici_views(src[kk], land[kk], gather, x, y, c, pi, px, py)
                pltpu.make_async_remote_copy(src_ref=mine, dst_ref=there, send_sem=send.at[3 * kk + pi],
                                             recv_sem=recv.at[3 * kk + pi], device_id=(px, py, c),
                                             device_id_type=MESH).start()
        token[...] = jnp.zeros_like(token)

    hbm = lambda a: pltpu.with_memory_space_constraint(a, pltpu.HBM)
    outs = pl.pallas_call(
        body, name=name,
        out_shape=(pltpu.SemaphoreType.DMA((3 * n,)), pltpu.SemaphoreType.DMA((3 * n,)),
                   *[pltpu.HBM(a.shape, a.dtype) for a in srcs], *[pltpu.HBM(a.shape, a.dtype) for a in lands],
                   jax.ShapeDtypeStruct((8, LANES), F32)),
        in_specs=[HBM_SPEC] * (2 * n) + [ANY],
        out_specs=(SEM_SPEC, SEM_SPEC, *[HBM_SPEC] * (2 * n), pl.BlockSpec(memory_space=pltpu.VMEM)),
        input_output_aliases={i: 2 + i for i in range(2 * n)},
        compiler_params=pltpu.CompilerParams(has_side_effects=SIDE_EFFECT),
    )(*[hbm(a) for a in srcs], *[hbm(a) for a in lands], after)
    return outs[0], outs[1], list(outs[2:2 + n]), list(outs[2 + n:2 + 2 * n]), outs[-1]


def _ici_wait(send, recv, srcs, lands, after, *, gather, name):
    n = len(srcs)

    def body(*refs):
        src, land = refs[:n], refs[n:2 * n]
        send_ref, recv_ref = refs[2 * n], refs[2 * n + 1]
        x, y, c = lax.axis_index("x"), lax.axis_index("y"), lax.axis_index("c")
        for kk in range(n):
            for pi, (px, py) in enumerate(_chip_peers(x, y)):
                mine, _, here = _ici_views(src[kk], land[kk], gather, x, y, c, pi, px, py)
                cp = pltpu.make_async_remote_copy(src_ref=mine, dst_ref=here, send_sem=send_ref.at[3 * kk + pi],
                                                  recv_sem=recv_ref.at[3 * kk + pi], device_id=(px, py, c),
                                                  device_id_type=MESH)
                cp.wait_send()
                cp.wait_recv()

    outs = pl.pallas_call(
        body, name=name,
        out_shape=[pltpu.HBM(a.shape, a.dtype) for a in srcs] + [pltpu.HBM(a.shape, a.dtype) for a in lands],
        in_specs=[HBM_SPEC] * (2 * n) + [SEM_SPEC, SEM_SPEC, ANY], out_specs=[HBM_SPEC] * (2 * n),
        input_output_aliases={i: i for i in range(2 * n)},
        compiler_params=pltpu.CompilerParams(has_side_effects=SIDE_EFFECT),
    )(*srcs, *lands, send, recv, after)
    return list(outs[:n]), list(outs[n:])


def _sibling_send(arrays, *, name):
    n = len(arrays)

    def body(*refs):
        ins, outs = refs[:n], refs[n:2 * n]
        send, recv = refs[2 * n:]
        sib = (lax.axis_index("x"), lax.axis_index("y"), 1 - lax.axis_index("c"))
        copies = [pltpu.make_async_remote_copy(src_ref=ins[kk], dst_ref=outs[kk], send_sem=send.at[kk],
                                               recv_sem=recv.at[kk], device_id=sib, device_id_type=MESH)
                  for kk in range(n)]
        for cp in copies:
            cp.start()
        for cp in copies:
            cp.wait_recv()
        for cp in copies:
            cp.wait_send()

    return _pcall(
        body, name=name, in_specs=[ANY] * n, out_specs=[ANY] * n,
        out_shape=[jax.ShapeDtypeStruct(a.shape, a.dtype) for a in arrays],
        scratch_shapes=[pltpu.SemaphoreType.DMA((n,)), pltpu.SemaphoreType.DMA((n,))])(*arrays)


def _all_gather_small(a, *, name):
    def body(a_ref, o_ref, send, recv, loc):
        x, y, c = lax.axis_index("x"), lax.axis_index("y"), lax.axis_index("c")
        me = 4 * x + 2 * y + c

        def peer(rel):
            fx, fy, fc = (rel >> 2) & 1, (rel >> 1) & 1, rel & 1
            px = (1 - x) if fx else x
            py = (1 - y) if fy else y
            pc = (1 - c) if fc else c
            return px, py, pc

        def remote(rel, slot):
            return pltpu.make_async_remote_copy(
                src_ref=a_ref, dst_ref=o_ref.at[slot], send_sem=send.at[rel - 1], recv_sem=recv.at[rel - 1],
                device_id=peer(rel), device_id_type=MESH)

        mine = pltpu.make_async_copy(a_ref, o_ref.at[me], loc)
        mine.start()
        sends = [remote(rel, me) for rel in range(1, 8)]
        for cp in sends:
            cp.start()
        for rel in range(1, 8):
            px, py, pc = peer(rel)
            remote(rel, 4 * px + 2 * py + pc).wait_recv()
        for cp in sends:
            cp.wait_send()
        mine.wait()

    return _pcall(
        body, name=name, in_specs=[ANY], out_specs=ANY,
        out_shape=jax.ShapeDtypeStruct((8,) + a.shape, a.dtype),
        scratch_shapes=[pltpu.SemaphoreType.DMA((7,)), pltpu.SemaphoreType.DMA((7,)), pltpu.SemaphoreType.DMA])(a)


def _pair_sum(a, b, *, name, tr=256):
    _, rows, cols = a.shape
    tr = min(tr, rows)

    def body(a_ref, b_ref, o_ref):
        o_ref[...] = (a_ref[...].astype(F32) + b_ref[...].astype(F32)).astype(BF16)

    blk = pl.BlockSpec((None, tr, cols), lambda j, i: (j, i, 0))
    return _pcall(body, name=name, grid=(4, rows // tr), in_specs=[blk, blk], out_specs=blk,
                  out_shape=jax.ShapeDtypeStruct(a.shape, BF16), dims=("parallel", "parallel"))(a, b)


def _chip_sum(own, recv, chip, *, name, tr=256):
    _, rows, cols = own.shape
    tr = min(tr, rows)

    def body(chip_ref, own_ref, r1_ref, r2_ref, r3_ref, o_ref):
        acc = own_ref[...].astype(F32)
        for ref in (r1_ref, r2_ref, r3_ref):
            acc = acc + ref[...].astype(F32)
        o_ref[...] = acc

    def slot(step):
        return pl.BlockSpec((None, tr, cols), lambda i, chip_ref: ((chip_ref[0] + step) % 4, i, 0))

    return pl.pallas_call(
        body, name=name, out_shape=jax.ShapeDtypeStruct((rows, cols), F32),
        grid_spec=pltpu.PrefetchScalarGridSpec(
            num_scalar_prefetch=1, grid=(rows // tr,), in_specs=[slot(0), slot(1), slot(2), slot(3)],
            out_specs=pl.BlockSpec((tr, cols), lambda i, chip_ref: (i, 0))),
        compiler_params=pltpu.CompilerParams(dimension_semantics=("parallel",), vmem_limit_bytes=VMEM_LIMIT),
    )(chip, own, recv, recv, recv)


def _adamw_math(w, g, m, v):
    m = ADAM_B1 * m + (1.0 - ADAM_B1) * g
    v = ADAM_B2 * v + (1.0 - ADAM_B2) * (g * g)
    m_hat = m / (1.0 - ADAM_B1 ** ADAM_STEP)
    v_hat = v / (1.0 - ADAM_B2 ** ADAM_STEP)
    delta = -ADAM_LR * (m_hat / (jnp.sqrt(v_hat) + ADAM_EPS) + ADAM_WD * w)
    return delta, m, v


def _adamw(w, m, v, g_layers, *, name, tr=128):
    depth = len(g_layers)
    rows, cols = g_layers[0].shape
    tr = min(tr, rows)
    nblk = rows // tr

    def body(*refs):
        w_ref, m_ref, v_ref = refs[:3]
        g_refs = refs[3:3 + depth]
        g_ref, d_ref, mo_ref, vo_ref = refs[3 + depth:]
        layer = pl.program_id(0)
        g = g_refs[0][...]
        for l in range(1, depth):
            g = jnp.where(layer == l, g_refs[l][...], g)
        delta, m_new, v_new = _adamw_math(w_ref[...], g, m_ref[...], v_ref[...])
        g_ref[...] = g
        d_ref[...] = delta
        mo_ref[...] = m_new
        vo_ref[...] = v_new

    def g_spec(l):
        return pl.BlockSpec((tr, cols), lambda ll, i: (jnp.where(ll == l, i, jnp.where(ll < l, 0, nblk - 1)), 0))

    blk = pl.BlockSpec((tr, cols), lambda ll, i: (ll * nblk + i, 0))
    return _pcall(
        body, name=name, grid=(depth, nblk), in_specs=[blk] * 3 + [g_spec(l) for l in range(depth)], out_specs=[blk] * 4,
        out_shape=[jax.ShapeDtypeStruct(w.shape, F32)] * 4, dims=("arbitrary", "arbitrary"))(w, m, v, *g_layers)


def _adamw_small(w, m, v, parts, *, name):
    def body(w_ref, m_ref, v_ref, p_ref, g_ref, d_ref, mo_ref, vo_ref):
        g = p_ref[0]
        for dev in range(1, 8):
            g = g + p_ref[dev]
        delta, m_new, v_new = _adamw_math(w_ref[...], g, m_ref[...], v_ref[...])
        g_ref[...] = g
        d_ref[...] = delta
        mo_ref[...] = m_new
        vo_ref[...] = v_new

    return _pcall(body, name=name, out_shape=[jax.ShapeDtypeStruct(w.shape, F32)] * 4)(w, m, v, parts)


def _rows_from_heads(a, nb, s):
    return a.reshape(nb, s, N_HEADS).transpose(0, 2, 1).reshape(nb, N_PAIRS, 2, s)


def _layer_fwd(h, p_l, wts, nb):
    t = h.shape[0]
    s = t // nb
    q, k, v, za, u, zp, fl, hn = _norm_in_proj(h, wts["gpre"], wts["wm"], wts["wf"])
    flt = fl[:, :N_HEADS].reshape(nb, s, N_HEADS).transpose(0, 2, 1)
    c = _gate_cumsum(flt, wts["bf"])
    o, lse = _attn_fwd(q, k, v, c.reshape(nb, N_PAIRS, 2, s), nb=nb)
    pool, pooled, mixed = _pool_fwd(u, zp, wts["wpool"], wts["ps"], nb=nb)
    h2, mix, h1, gate, pe, cat = _out_block(h, o, za, pool, p_l, wts["wo"], wts["wpg"], wts["wpe"], wts["gpost"])
    saved = dict(h=h, hn=hn, q=q, k=k, v=v, za=za, zp=zp, flt=flt, c=c, o=o, lse=lse, pooled=pooled, mixed=mixed,
                 mix=mix, h1=h1, gate=gate, pe=pe, cat=cat, p=p_l)
    return h2, saved


def _layer_bwd(dh2, sv, wts, gsum, nb):
    t = dh2.shape[0]
    s = t // nb
    dh1, dgp, dpe, dmix, do, dza, dpool, delta, dgpost = _out_block_bwd(
        dh2, sv["gate"], sv["pe"], sv["mix"], sv["o"], sv["za"], wts["wpg"], wts["wo"], wts["gpost"], gsum)
    g_wpe = _matmul(sv["p"], dpe, ta=True, name="wgrad_pe")
    g_wpg = _matmul(sv["h1"], dgp, ta=True, name="wgrad_pg")
    g_wo = _matmul(sv["cat"], dmix, ta=True, name="wgrad_out")

    delta4 = _rows_from_heads(delta[:, :N_HEADS], nb, s)
    dq, dk, dv, dcol, drow = _attn_bwd(sv["q"], sv["k"], sv["v"], do, sv["c"].reshape(nb, N_PAIRS, 2, s), sv["lse"],
                                       delta4, nb=nb)
    dflt, dbf = _gate_cumsum_bwd(dcol.reshape(nb, N_HEADS, s), drow.reshape(nb, N_HEADS, s), sv["flt"], wts["bf"])
    dfl = jnp.pad(dflt.transpose(0, 2, 1).reshape(t, N_HEADS), ((0, 0), (0, LANES - N_HEADS)))

    dmixed, du, dzp, dps = _pool_bwd(dpool, sv["mixed"], sv["zp"], wts["wpool"], wts["ps"], nb=nb)
    g_wpool = _pool_wgrad(sv["pooled"], dmixed)

    dsegs = (dq, dk, dv, dza, du, dzp)
    dh, dgpre = _in_proj_bwd(dsegs, dfl, wts["wm"], wts["wf"], sv["h"], dh1, wts["gpre"])
    g_segs = [_matmul(sv["hn"], dx, ta=True, name="wgrad_in_f32" if dx.dtype == F32 else "wgrad_in")
              for dx in dsegs]
    g_wf = _matmul(sv["hn"], dfl, ta=True, name="wgrad_in_f")
    g_win = jnp.concatenate(g_segs[:4] + [g_wf[:, :N_HEADS]] + g_segs[4:], axis=1)
    grads = dict(w_in=g_win, w_out=g_wo, w_pg=g_wpg, w_pe=g_wpe, w_pool=g_wpool,
                 norm_pre=dgpre[0], norm_post=dgpost[0], pool_scale=dps[0], b_f=dbf[:, 0])
    return dh, grads


def kernel(x, p, norm_pre, norm_post, w_in, b_f, w_pool, pool_scale, w_out, w_pg, w_pe, loss_target, m_norm_pre, m_norm_post, m_w_in, m_b_f, m_w_pool, m_pool_scale, m_w_out, m_w_pg, m_w_pe, v_norm_pre, v_norm_post, v_w_in, v_b_f, v_w_pool, v_pool_scale, v_w_out, v_w_pg, v_w_pe):
    nb, s, _ = x.shape
    t = nb * s
    depth = w_in.shape[0]
    big = dict(w_in=(w_in, m_w_in, v_w_in), w_out=(w_out, m_w_out, v_w_out), w_pg=(w_pg, m_w_pg, v_w_pg),
               w_pe=(w_pe, m_w_pe, v_w_pe), w_pool=(w_pool, m_w_pool, v_w_pool))
    names = list(big)
    cols = {n: big[n][0].shape[-1] for n in names}
    chip = (2 * lax.axis_index("x") + lax.axis_index("y")).astype(jnp.int32).reshape(1)
    core = lax.axis_index("c")
    south = core == 0

    gathers = []
    token = jnp.zeros((8, LANES), F32)
    for l in range(depth):
        halves = [big[n][0][l].reshape(2, -1, cols[n]).astype(BF16) for n in names]
        lands = [lax.empty((3,) + a.shape[1:], a.dtype) for a in halves]
        send, recv, halves, lands, token = _ici_start(halves, lands, token, gather=True, name=f"gather_start_{l}")
        gathers.append((send, recv, halves, lands))

    def layer_weights(l, after, anchor):
        send, recv, halves, lands = gathers[l]
        halves, lands = _ici_wait(send, recv, halves, lands, after, gather=True, name=f"gather_wait_{l}")
        others = _sibling_send(lands, name="gather_pass_on")
        full = {}
        for n, own, land, other in zip(names, halves, lands, others):
            low = jnp.where(south, land, other)
            high = jnp.where(south, other, land)
            rel = [jnp.concatenate([low[pi], high[pi]], axis=0) for pi in range(3)]
            by_rel = jnp.stack([own.reshape(-1, cols[n]), rel[1], rel[0], rel[2]])
            full[n] = [lax.dynamic_index_in_dim(by_rel, j ^ chip[0], 0, keepdims=False) for j in range(4)]
        win = jnp.concatenate(full["w_in"], axis=1)
        wm = jnp.concatenate([win[:, :F_OFF], win[:, F_OFF + N_HEADS:]], axis=1)
        wf = jnp.pad(win[:, F_OFF:F_OFF + N_HEADS], ((0, 0), (0, LANES - N_HEADS)))
        return dict(
            wm=wm, wf=wf, wo=jnp.concatenate(full["w_out"], axis=0), wpg=jnp.concatenate(full["w_pg"], axis=0),
            wpe=jnp.concatenate(full["w_pe"], axis=1),
            wpool=jnp.stack(full["w_pool"]).reshape(4, N_GROUPS, GROUP_DIM // 4, GROUP_DIM).transpose(1, 0, 2, 3).reshape(
                N_GROUPS, GROUP_DIM, GROUP_DIM),
            gpre=norm_pre[l][None] + anchor, gpost=norm_post[l][None], ps=pool_scale[l][None], bf=b_f[l][:, None])

    h = x.reshape(t, D_MODEL)
    saved, layers = [], []
    after = token
    for l in range(depth):
        layers.append(layer_weights(l, after, token[0, 0]))
        h, sv = _layer_fwd(h, p[l].reshape(t, PLE_DIM), layers[l], nb)
        saved.append(sv)
        after = h
    dh, loss_blk = _loss_grad(h, loss_target.reshape(t, D_MODEL))
    loss = lax.psum(loss_blk[0, 0], ("x", "y", "c"))
    gsum = (jnp.arange(D_MODEL)[:, None] // HEAD_DIM == jnp.arange(LANES)[None, :]).astype(BF16)

    def pieces(g, name):
        if name in ("w_in", "w_pe"):
            by_chip = jnp.stack([g[:, j * cols[name]:(j + 1) * cols[name]] for j in range(4)])
        elif name == "w_pool":
            by_chip = g.reshape(N_GROUPS, 4, GROUP_DIM // 4, GROUP_DIM).transpose(1, 0, 2, 3)
        else:
            by_chip = g
        return by_chip.reshape(4, 2, -1, cols[name]).transpose(1, 0, 2, 3).astype(BF16)

    def scatter_start(l, g):
        mine, sent = [], []
        for n in names:
            halves = pieces(g[n], n)
            mine.append(lax.dynamic_index_in_dim(halves, core, 0, keepdims=False))
            sent.append(lax.dynamic_index_in_dim(halves, 1 - core, 0, keepdims=False))
        theirs = _sibling_send(sent, name="presum_exchange")
        sums = [_pair_sum(a, b, name="presum_" + n) for n, a, b in zip(names, mine, theirs)]
        lands = [lax.empty(a.shape, a.dtype) for a in sums]
        return _ici_start(sums, lands, token, gather=False, name=f"scatter_start_{l}")

    def scatter_finish(l, state, after):
        send, recv, sums, lands, _ = state
        sums, lands = _ici_wait(send, recv, sums, lands, after, gather=False, name=f"scatter_wait_{l}")
        mine = [_chip_sum(a, r, chip, name="chip_sum_" + n) for n, a, r in zip(names, sums, lands)]
        theirs = _sibling_send(mine, name="halves_exchange")
        return {n: jnp.concatenate([jnp.where(south, a, b), jnp.where(south, b, a)], axis=0)
                for n, a, b in zip(names, mine, theirs)}

    grads = [None] * depth
    reduced = [None] * depth
    pending = None
    for l in reversed(range(depth)):
        wts = layers[l]
        if pending is not None:
            wts = dict(wts, gpost=wts["gpost"] + pending[1][4][0, 0])
        dh, grads[l] = _layer_bwd(dh, saved[l], wts, gsum, nb)
        if pending is not None:
            reduced[pending[0]] = scatter_finish(pending[0], pending[1], dh)
        pending = (l, scatter_start(l, grads[l]))
    grad_x = dh.reshape(nb, s, D_MODEL)

    small = dict(norm_pre=(norm_pre, m_norm_pre, v_norm_pre), norm_post=(norm_post, m_norm_post, v_norm_post),
                 pool_scale=(pool_scale, m_pool_scale, v_pool_scale), b_f=(b_f, m_b_f, v_b_f))

    def pack(get):
        rows = []
        for n in small:
            a = get(n)
            rows.append(jnp.pad(a, ((0, 0), (0, D_MODEL - a.shape[1]))))
        return jnp.concatenate(rows, axis=0)

    parts = _all_gather_small(pack(lambda n: jnp.stack([grads[l][n] for l in range(depth)])), name="gather_small")
    small_packed = _adamw_small(pack(lambda n: small[n][0]), pack(lambda n: small[n][1]), pack(lambda n: small[n][2]),
                                parts, name="adamw_small")
    small_out = {}
    for i, n in enumerate(small):
        width = small[n][0].shape[1]
        small_out[n] = [o[depth * i:depth * (i + 1), :width] for o in small_packed]

    reduced[pending[0]] = scatter_finish(pending[0], pending[1], small_packed[0])
    big_out = {}
    for n in names:
        w, m, v = big[n]
        outs = _adamw(w.reshape(-1, cols[n]), m.reshape(-1, cols[n]), v.reshape(-1, cols[n]),
                      [reduced[l][n] for l in range(depth)], name="adamw_" + n)
        big_out[n] = [o.reshape(w.shape) for o in outs]

    order =["norm_pre", "norm_post", "w_in", "b_f", "w_pool", "pool_scale", "w_out", "w_pg", "w_pe"]
    result = [loss, grad_x]
    for kind in range(4):
        for n in order:
            result.append(big_out[n][kind] if n in big_out else small_out[n][kind])
    return tuple(result)
```

```python
import functools

import jax
import jax.numpy as jnp
from jax import lax
from jax.experimental import pallas as pl
from jax.experimental.pallas import tpu as pltpu

F32 = jnp.float32
BF16 = jnp.bfloat16

D_MODEL = 1024
N_HEADS = 16
HEAD_DIM = 64
PAIR = 2 * HEAD_DIM
N_PAIRS = N_HEADS // 2
N_GROUPS = 4
GROUP_DIM = 256
PLE_DIM = 256
MIX = 2 * D_MODEL
N_SEG = 6
F_OFF = 4 * D_MODEL
IN_COLS = N_SEG * D_MODEL + N_HEADS
LANES = 128
EPS = 1e-6
SCALE = 0.125
NEG = -1e30

ADAM_LR = 0.001
ADAM_B1 = 0.9
ADAM_B2 = 0.999
ADAM_EPS = 1e-08
ADAM_WD = 0.01
ADAM_STEP = 10

VMEM_LIMIT = 56 * 1024 * 1024
MESH = pl.DeviceIdType.MESH
ANY = pl.BlockSpec(memory_space=pl.ANY)
HBM_SPEC = pl.BlockSpec(memory_space=pltpu.HBM)
SEM_SPEC = pl.BlockSpec(memory_space=pltpu.SEMAPHORE)
SIDE_EFFECT = pltpu.SideEffectType.DATAFLOW_SIDE_EFFECTING

NT = (((1,), (1,)), ((), ()))
TN = (((0,), (0,)), ((), ()))
NN = (((1,), (0,)), ((), ()))


def _pcall(body, *, name, out_shape, grid=(), in_specs=None, out_specs=None, scratch_shapes=(), dims=None):
    kw = {}
    if in_specs is not None:
        kw["in_specs"] = in_specs
    if out_specs is not None:
        kw["out_specs"] = out_specs
    return pl.pallas_call(
        body, name=name, out_shape=out_shape, grid=grid, scratch_shapes=list(scratch_shapes),
        compiler_params=pltpu.CompilerParams(dimension_semantics=dims, vmem_limit_bytes=VMEM_LIMIT), **kw)


def _dot(a, b, dn=NN):
    return lax.dot_general(a, b, dn, preferred_element_type=F32)


def _sigmoid(x):
    return 1.0 / (1.0 + jnp.exp(-x))


def _matmul(a, b, *, ta=False, name, tm=512, tn=1024, tk=2048):
    if ta:
        kdim, m = a.shape
    else:
        m, kdim = a.shape
    n = b.shape[1]
    tm, tn, tk = min(tm, m), min(tn, n), min(tk, kdim)
    nk = kdim // tk

    def body(a_ref, b_ref, o_ref, acc_ref):
        k = pl.program_id(2)

        @pl.when(k == 0)
        def _():
            acc_ref[...] = jnp.zeros_like(acc_ref)

        acc_ref[...] += _dot(a_ref[...].astype(BF16), b_ref[...].astype(BF16), TN if ta else NN)

        @pl.when(k == nk - 1)
        def _():
            o_ref[...] = acc_ref[...].astype(BF16)

    a_spec = pl.BlockSpec((tk, tm), lambda i, j, k: (k, i)) if ta else pl.BlockSpec((tm, tk), lambda i, j, k: (i, k))
    return _pcall(
        body, name=name, out_shape=jax.ShapeDtypeStruct((m, n), BF16), grid=(m // tm, n // tn, nk),
        in_specs=[a_spec, pl.BlockSpec((tk, tn), lambda i, j, k: (k, j))],
        out_specs=pl.BlockSpec((tm, tn), lambda i, j, k: (i, j)),
        scratch_shapes=[pltpu.VMEM((tm, tn), F32)], dims=("parallel", "parallel", "arbitrary"))(a, b)


def _norm_in_proj(h, g, wm, wf, *, tm=512):
    t = h.shape[0]

    def body(h_ref, g_ref, wm_ref, wf_ref, q_ref, k_ref, v_ref, za_ref, u_ref, zp_ref, fl_ref, hn_ref):
        j = pl.program_id(1)

        @pl.when(j == 0)
        def _():
            x = h_ref[...]
            r = lax.rsqrt(jnp.mean(x * x, axis=-1, keepdims=True) + EPS)
            hn = (x * r * g_ref[...]).astype(BF16)
            hn_ref[...] = hn
            fl_ref[...] = _dot(hn, wf_ref[...])

        y = _dot(hn_ref[...], wm_ref[...])
        for s, ref in enumerate((q_ref, k_ref, v_ref, za_ref, u_ref, zp_ref)):
            @pl.when(j == s)
            def _(ref=ref):
                ref[...] = y.astype(ref.dtype)

    row = lambda i, j: (i, 0)
    blk = pl.BlockSpec((tm, D_MODEL), row)
    sds = lambda dt: jax.ShapeDtypeStruct((t, D_MODEL), dt)
    return _pcall(
        body, name="norm_in_proj", grid=(t // tm, N_SEG),
        in_specs=[blk, pl.BlockSpec((1, D_MODEL), lambda i, j: (0, 0)),
                  pl.BlockSpec((D_MODEL, D_MODEL), lambda i, j: (0, j)),
                  pl.BlockSpec((D_MODEL, LANES), lambda i, j: (0, 0))],
        out_specs=[blk] * 6 + [pl.BlockSpec((tm, LANES), row), blk],
        out_shape=[sds(BF16)] * 3 + [sds(F32)] * 3 + [jax.ShapeDtypeStruct((t, LANES), F32), sds(BF16)],
        dims=("parallel", "arbitrary"))(h, g, wm, wf)


def _gate_cumsum(flt, bias):
    b, hh, s = flt.shape

    def body(fl_ref, b_ref, c_ref):
        x = fl_ref[...] + b_ref[...]
        acc = jnp.minimum(x, 0.0) - jnp.log(1.0 + jnp.exp(-jnp.abs(x)))
        idx = lax.broadcasted_iota(jnp.int32, x.shape, 1)
        sh = 1
        while sh < s:
            acc = acc + jnp.where(idx >= sh, pltpu.roll(acc, sh, 1), 0.0)
            sh *= 2
        c_ref[...] = acc

    return _pcall(
        body, name="gate_cumsum", grid=(b,),
        in_specs=[pl.BlockSpec((None, hh, s), lambda i: (i, 0, 0)), pl.BlockSpec((hh, 1), lambda i: (0, 0))],
        out_specs=pl.BlockSpec((None, hh, s), lambda i: (i, 0, 0)),
        out_shape=jax.ShapeDtypeStruct((b, hh, s), F32), dims=("parallel",))(flt, bias)


def _gate_cumsum_bwd(dcol, drow, flt, bias):
    b, hh, s = flt.shape

    def body(d_ref, dr_ref, fl_ref, b_ref, dfl_ref, db_ref):
        @pl.when(pl.program_id(0) == 0)
        def _():
            db_ref[...] = jnp.zeros_like(db_ref)

        x = fl_ref[...] + b_ref[...]
        acc = dr_ref[...] - d_ref[...]
        idx = lax.broadcasted_iota(jnp.int32, x.shape, 1)
        sh = 1
        while sh < s:
            acc = acc + jnp.where(idx + sh < s, pltpu.roll(acc, s - sh, 1), 0.0)
            sh *= 2
        e = jnp.exp(-jnp.abs(x))
        sig_neg = jnp.where(x >= 0.0, e, 1.0) / (1.0 + e)
        dfl = acc * sig_neg
        dfl_ref[...] = dfl
        db_ref[...] += jnp.sum(dfl, axis=1, keepdims=True)

    return _pcall(
        body, name="gate_cumsum_bwd", grid=(b,),
        in_specs=[pl.BlockSpec((None, hh, s), lambda i: (i, 0, 0))] * 3 + [pl.BlockSpec((hh, 1), lambda i: (0, 0))],
        out_specs=[pl.BlockSpec((None, hh, s), lambda i: (i, 0, 0)), pl.BlockSpec((hh, 1), lambda i: (0, 0))],
        out_shape=[jax.ShapeDtypeStruct((b, hh, s), F32), jax.ShapeDtypeStruct((hh, 1), F32)],
        dims=("arbitrary",))(dcol, drow, flt, bias)


def _attn_fwd(q, k, v, c4, *, nb, tq=512, rc=32, diag=256):
    t = q.shape[0]
    s = t // nb
    tq = min(tq, s)
    nq = s // tq
    tk = tq
    diag = min(diag, tq)

    def body(q_ref, k_ref, v_ref, c_ref, o_ref, lse_ref, qh_scr, s_scr, p_scr, m_scr, acc_scr):
        qi = pl.program_id(2)
        lo = lax.broadcasted_iota(jnp.int32, (tq, PAIR), 1) < HEAD_DIM
        q2 = q_ref[...] * SCALE
        zero = jnp.zeros_like(q2)
        qh_scr[0] = jnp.where(lo, q2, zero)
        qh_scr[1] = jnp.where(lo, zero, q2)
        m_scr[...] = jnp.full(m_scr.shape, NEG, F32)
        acc_scr[...] = jnp.zeros(acc_scr.shape, F32)
        whole = [(0, tq, tk, False)]
        diagonal = [(r0, r0 + diag, r0 + diag, True) for r0 in range(0, tq, diag)]

        def block(kj, bands):
            off = pl.multiple_of(kj * tk, tk)
            k2 = k_ref[pl.ds(off, tk), :]
            v2 = v_ref[pl.ds(off, tk), :]
            one = jnp.ones_like(v2)
            va = (jnp.where(lo, v2, one), jnp.where(lo, one, v2))
            crow = c_ref[:, pl.ds(off, tk)]
            for hd in range(2):
                for r0, r1, nc, _ in bands:
                    s_scr[hd, r0:r1, :nc] = _dot(qh_scr[hd, r0:r1, :], k2[:nc], NT)
            for hd in range(2):
                for r0, r1, nc, masked in bands:
                    row = lax.broadcasted_iota(jnp.int32, (rc, nc), 0)
                    col = lax.broadcasted_iota(jnp.int32, (rc, nc), 1)
                    for r in range(r0, r1, rc):
                        sc = s_scr[hd, r:r + rc, :nc] - crow[hd:hd + 1, :nc]
                        if masked:
                            sc = jnp.where(row + r >= col, sc, NEG)
                        m_old = m_scr[hd, r:r + rc, :]
                        m_new = jnp.maximum(m_old, jnp.max(sc, axis=1, keepdims=True))
                        for cb in range(0, nc, LANES):
                            p_scr[hd, r:r + rc, cb:cb + LANES] = jnp.exp(sc[:, cb:cb + LANES] - m_new).astype(BF16)
                        m_scr[hd, r:r + rc, :] = m_new
                        acc_scr[hd, r:r + rc, :] = acc_scr[hd, r:r + rc, :] * jnp.exp(m_old - m_new)
                    acc_scr[hd, r0:r1, :] += _dot(p_scr[hd, r0:r1, :nc], va[hd][:nc])

        def below(kj, carry):
            block(kj, whole)
            return carry

        lax.fori_loop(0, qi, below, 0)
        block(qi, diagonal)
        a0, a1 = acc_scr[0], acc_scr[1]
        den = jnp.where(lo, pltpu.roll(a0, HEAD_DIM, 1), pltpu.roll(a1, HEAD_DIM, 1))
        o_ref[...] = jnp.where(lo, a0, a1) / den
        lse_t = (jnp.where(lo, m_scr[0], m_scr[1]) + jnp.log(den)).T
        lse_ref[0:1, :] = lse_t[0:1, :]
        lse_ref[1:2, :] = lse_t[HEAD_DIM:HEAD_DIM + 1, :]

    qspec = pl.BlockSpec((tq, PAIR), lambda b, hp, i: (b * nq + i, hp))
    kvspec = pl.BlockSpec((s, PAIR), lambda b, hp, i: (b, hp))
    return _pcall(
        body, name="attn_fwd", grid=(nb, N_PAIRS, nq),
        in_specs=[qspec, kvspec, kvspec, pl.BlockSpec((None, None, 2, s), lambda b, hp, i: (b, hp, 0, 0))],
        out_specs=[qspec, pl.BlockSpec((None, None, 2, tq), lambda b, hp, i: (b, hp, 0, i))],
        out_shape=[jax.ShapeDtypeStruct((t, D_MODEL), F32), jax.ShapeDtypeStruct((nb, N_PAIRS, 2, s), F32)],
        scratch_shapes=[pltpu.VMEM((2, tq, PAIR), BF16), pltpu.VMEM((2, tq, tk), F32), pltpu.VMEM((2, tq, tk), BF16),
                        pltpu.VMEM((2, tq, LANES), F32), pltpu.VMEM((2, tq, PAIR), F32)],
        dims=("parallel", "parallel", "arbitrary"))(q, k, v, c4)


def _attn_bwd(q, k, v, do, c4, lse4, delta4, *, nb, tk=512, rc=16, diag=256):
    t = q.shape[0]
    s = t // nb
    tk = min(tk, s)
    nk = s // tk
    tq = tk
    diag = min(diag, tk)

    def body(q_ref, do_ref, k_ref, v_ref, c_ref, lse_ref, dl_ref, dq_ref, dk_ref, dv_ref, dc_ref, dr_ref,
             kz_scr, vz_scr, ko_scr, crep_scr, st_scr, dp_scr, pt_scr, ds_scr, dk_scr, dv_scr, dr_scr):
        kj = pl.program_id(2)

        @pl.when(kj == 0)
        def _():
            dq_ref[...] = jnp.zeros_like(dq_ref)
            dr_scr[...] = jnp.zeros_like(dr_scr)

        lo = lax.broadcasted_iota(jnp.int32, (tk, PAIR), 1) < HEAD_DIM
        k2 = k_ref[...]
        v2 = v_ref[...]
        zero = jnp.zeros_like(k2)
        one = jnp.ones_like(k2)
        kz_scr[0] = jnp.where(lo, k2, zero)
        kz_scr[1] = jnp.where(lo, zero, k2)
        vz_scr[0] = jnp.where(lo, v2, zero)
        vz_scr[1] = jnp.where(lo, zero, v2)
        ko_scr[0] = jnp.where(lo, k2, one)
        ko_scr[1] = jnp.where(lo, one, k2)
        for hd in range(2):
            crep_scr[hd] = jnp.broadcast_to(c_ref[hd:hd + 1, :], (LANES, tk)).T
        dk_scr[...] = jnp.zeros(dk_scr.shape, F32)
        dv_scr[...] = jnp.zeros(dv_scr.shape, F32)
        row = lax.broadcasted_iota(jnp.int32, (rc, LANES), 0)
        col = lax.broadcasted_iota(jnp.int32, (rc, LANES), 1)

        whole = [(0, tk, 0, False)]
        diagonal = [(r0, r0 + diag, r0, True) for r0 in range(0, tk, diag)]

        def block(qi, bands):
            off = pl.multiple_of(qi * tq, tq)
            q2 = q_ref[pl.ds(off, tq), :] * SCALE
            do2 = do_ref[pl.ds(off, tq), :]
            lse = lse_ref[:, pl.ds(off, tq)]
            dl = dl_ref[:, pl.ds(off, tq)]
            qo = (jnp.where(lo, q2, jnp.ones_like(q2)), jnp.where(lo, jnp.ones_like(q2), q2))
            for hd in range(2):
                for r0, r1, c0, _ in bands:
                    st_scr[hd, r0:r1, c0:] = _dot(kz_scr[hd, r0:r1, :], q2[c0:], NT)
                    dp_scr[hd, r0:r1, c0:] = _dot(vz_scr[hd, r0:r1, :], do2[c0:], NT)
            for r0, r1, c0, masked in bands:
                dqs = []
                for hd in range(2):
                    for r in range(r0, r1, rc):
                        c_rep = crep_scr[hd, r:r + rc, :]
                        for cb in range(c0, tq, LANES):
                            pt = jnp.exp(st_scr[hd, r:r + rc, cb:cb + LANES] - c_rep - lse[hd:hd + 1, cb:cb + LANES])
                            if masked and cb < c0 + diag:
                                pt = jnp.where(col + cb >= row + r, pt, 0.0)
                            dst = pt * (dp_scr[hd, r:r + rc, cb:cb + LANES] - dl[hd:hd + 1, cb:cb + LANES])
                            pt_scr[hd, r:r + rc, cb:cb + LANES] = pt.astype(BF16)
                            ds_scr[hd, r:r + rc, cb:cb + LANES] = dst.astype(BF16)
                    dv_scr[hd, r0:r1, :] += _dot(pt_scr[hd, r0:r1, c0:], do2[c0:])
                    dk_scr[hd, r0:r1, :] += _dot(ds_scr[hd, r0:r1, c0:], qo[hd][c0:])
                    dqs.append(_dot(ds_scr[hd, r0:r1, c0:], ko_scr[hd, r0:r1, :], TN))
                dq_ref[pl.ds(off + c0, tq - c0), :] += jnp.where(lo[c0:], dqs[0], dqs[1]) * SCALE
                dr_scr[pl.ds(off + c0, tq - c0), :] += jnp.where(lo[c0:], dqs[1], dqs[0])

        block(kj, diagonal)

        def below(qi, carry):
            block(qi, whole)
            return carry

        lax.fori_loop(kj + 1, nk, below, 0)
        dk_ref[...] = jnp.where(lo, dk_scr[0], dk_scr[1]).astype(BF16)
        dv_ref[...] = jnp.where(lo, dv_scr[0], dv_scr[1]).astype(BF16)
        dc_t = jnp.where(lo, dk_scr[1], dk_scr[0]).T
        dc_ref[0:1, :] = dc_t[HEAD_DIM:HEAD_DIM + 1, :]
        dc_ref[1:2, :] = dc_t[0:1, :]

        @pl.when(kj == nk - 1)
        def _():
            for r in range(0, s, tq):
                dr_t = dr_scr[r:r + tq, :].T
                dr_ref[0:1, r:r + tq] = dr_t[HEAD_DIM:HEAD_DIM + 1, :]
                dr_ref[1:2, r:r + tq] = dr_t[0:1, :]

    full = pl.BlockSpec((s, PAIR), lambda b, hp, j: (b, hp))
    kblk = pl.BlockSpec((tk, PAIR), lambda b, hp, j: (b * nk + j, hp))
    rows = pl.BlockSpec((None, None, 2, s), lambda b, hp, j: (b, hp, 0, 0))
    krows = pl.BlockSpec((None, None, 2, tk), lambda b, hp, j: (b, hp, 0, j))
    sds = lambda dt: jax.ShapeDtypeStruct((t, D_MODEL), dt)
    rows_sds = jax.ShapeDtypeStruct((nb, N_PAIRS, 2, s), F32)
    pair_bf = pltpu.VMEM((2, tk, PAIR), BF16)
    pair_f = pltpu.VMEM((2, tk, PAIR), F32)
    return _pcall(
        body, name="attn_bwd", grid=(nb, N_PAIRS, nk),
        in_specs=[full, full, kblk, kblk, krows, rows, rows],
        out_specs=[full, kblk, kblk, krows, rows],
        out_shape=[sds(F32), sds(BF16), sds(BF16), rows_sds, rows_sds],
        scratch_shapes=[pair_bf, pair_bf, pair_bf, pair_f, pltpu.VMEM((2, tk, tq), F32), pltpu.VMEM((2, tk, tq), F32),
                        pltpu.VMEM((2, tk, tq), BF16), pltpu.VMEM((2, tk, tq), BF16), pair_f, pair_f,
                        pltpu.VMEM((s, PAIR), F32)],
        dims=("parallel", "parallel", "arbitrary"))(q, do, k, v, c4, lse4, delta4)


def _window_sum(x, g, s, *, ahead):
    row = lax.broadcasted_iota(jnp.int32, x.shape, 0)
    for step in range(N_GROUPS):
        sh = 1 << step
        if ahead:
            moved = jnp.where(row + sh < s, pltpu.roll(x, s - sh, 0), 0.0)
        else:
            moved = jnp.where(row >= sh, pltpu.roll(x, sh, 0), 0.0)
        x = jnp.where(step <= g, x + moved, x)
    return x


def _inv_count(g, s):
    pos = lax.broadcasted_iota(jnp.int32, (s, 1), 0) + 1
    return 1.0 / jnp.minimum(pos, 2 << g).astype(F32)


def _pool_fwd(u, zp, wpool, ps, *, nb):
    t = u.shape[0]
    s = t // nb

    def body(u_ref, zp_ref, w_ref, ps_ref, out_ref, pooled_ref, mixed_ref):
        g = pl.program_id(1)
        x = u_ref[...]
        pooled = (_window_sum(x, g, s, ahead=False) * _inv_count(g, s) - x).astype(BF16)
        mixed = _dot(pooled, w_ref[...])
        z = zp_ref[...]
        pooled_ref[...] = pooled
        mixed_ref[...] = mixed
        out_ref[...] = (mixed * ps_ref[...] * (z * _sigmoid(z))).astype(BF16)

    blk = pl.BlockSpec((s, GROUP_DIM), lambda b, g: (b, g))
    sds = lambda dt: jax.ShapeDtypeStruct((t, D_MODEL), dt)
    return _pcall(
        body, name="pool_fwd", grid=(nb, N_GROUPS),
        in_specs=[blk, blk, pl.BlockSpec((None, GROUP_DIM, GROUP_DIM), lambda b, g: (g, 0, 0)),
                  pl.BlockSpec((1, GROUP_DIM), lambda b, g: (0, g))],
        out_specs=[blk, blk, blk], out_shape=[sds(BF16), sds(BF16), sds(F32)],
        dims=("parallel", "parallel"))(u, zp, wpool, ps)


def _pool_bwd(dpool, mixed, zp, wpool, ps, *, nb):
    t = dpool.shape[0]
    s = t // nb

    def body(dp_ref, mx_ref, zp_ref, w_ref, ps_ref, dmx_ref, du_ref, dzp_ref, dps_ref):
        g = pl.program_id(0)

        @pl.when(pl.program_id(1) == 0)
        def _():
            dps_ref[...] = jnp.zeros_like(dps_ref)

        dp = dp_ref[...]
        mixed = mx_ref[...]
        z = zp_ref[...]
        scale = ps_ref[...]
        sg = _sigmoid(z)
        silu = z * sg
        dmixed = (dp * scale * silu).astype(BF16)
        dmx_ref[...] = dmixed
        dps_ref[...] += jnp.sum(dp * mixed * silu, axis=0, keepdims=True)
        dzp_ref[...] = (dp * mixed * scale * (sg * (1.0 + z * (1.0 - sg)))).astype(BF16)
        dpooled = _dot(dmixed, w_ref[...], NT)
        du = _window_sum(dpooled * _inv_count(g, s), g, s, ahead=True) - dpooled
        du_ref[...] = du.astype(BF16)

    blk = pl.BlockSpec((s, GROUP_DIM), lambda g, b: (b, g))
    sds = lambda dt: jax.ShapeDtypeStruct((t, D_MODEL), dt)
    return _pcall(
        body, name="pool_bwd", grid=(N_GROUPS, nb),
        in_specs=[blk, blk, blk, pl.BlockSpec((None, GROUP_DIM, GROUP_DIM), lambda g, b: (g, 0, 0)),
                  pl.BlockSpec((1, GROUP_DIM), lambda g, b: (0, g))],
        out_specs=[blk, blk, blk, pl.BlockSpec((1, GROUP_DIM), lambda g, b: (0, g))],
        out_shape=[sds(BF16), sds(BF16), sds(BF16), jax.ShapeDtypeStruct((1, D_MODEL), F32)],
        dims=("parallel", "arbitrary"))(dpool, mixed, zp, wpool, ps)


def _pool_wgrad(pooled, dmixed, *, tk=1024):
    t = pooled.shape[0]
    tk = min(tk, t)
    nk = t // tk

    def body(a_ref, b_ref, o_ref):
        @pl.when(pl.program_id(1) == 0)
        def _():
            o_ref[...] = jnp.zeros_like(o_ref)

        o_ref[...] += _dot(a_ref[...], b_ref[...], TN)

    blk = pl.BlockSpec((tk, GROUP_DIM), lambda g, k: (k, g))
    return _pcall(
        body, name="pool_wgrad", grid=(N_GROUPS, nk), in_specs=[blk, blk],
        out_specs=pl.BlockSpec((None, GROUP_DIM, GROUP_DIM), lambda g, k: (g, 0, 0)),
        out_shape=jax.ShapeDtypeStruct((N_GROUPS, GROUP_DIM, GROUP_DIM), F32),
        dims=("parallel", "arbitrary"))(pooled, dmixed)


def _out_block(h, o, za, pool, p, wo, wpg, wpe, gpost, *, tm=256):
    t = h.shape[0]

    def body(h_ref, o_ref, za_ref, pool_ref, p_ref, wo_ref, wpg_ref, wpe_ref, g_ref,
             h2_ref, mix_ref, h1_ref, gate_ref, pe_ref, cat_ref):
        z = za_ref[...]
        a = (o_ref[...] * (z * _sigmoid(z))).astype(BF16)
        pool_v = pool_ref[...]
        cat_ref[:, :D_MODEL] = a
        cat_ref[:, D_MODEL:] = pool_v
        mix = _dot(a, wo_ref[:D_MODEL, :]) + _dot(pool_v, wo_ref[D_MODEL:, :])
        r = lax.rsqrt(jnp.mean(mix * mix, axis=-1, keepdims=True) + EPS)
        h1 = h_ref[...] + mix * r * g_ref[...]
        gate = _sigmoid(_dot(h1.astype(BF16), wpg_ref[...]))
        pe = _dot(p_ref[...].astype(BF16), wpe_ref[...])
        mix_ref[...] = mix
        h1_ref[...] = h1
        gate_ref[...] = gate
        pe_ref[...] = pe
        h2_ref[...] = h1 + gate * pe

    row = lambda i: (i, 0)
    const = lambda i: (0, 0)
    blk = pl.BlockSpec((tm, D_MODEL), row)
    sds = jax.ShapeDtypeStruct((t, D_MODEL), F32)
    return _pcall(
        body, name="out_block", grid=(t // tm,),
        in_specs=[blk, blk, blk, blk, pl.BlockSpec((tm, PLE_DIM), row),
                  pl.BlockSpec((MIX, D_MODEL), const), pl.BlockSpec((D_MODEL, D_MODEL), const),
                  pl.BlockSpec((PLE_DIM, D_MODEL), const), pl.BlockSpec((1, D_MODEL), const)],
        out_specs=[blk] * 5 + [pl.BlockSpec((tm, MIX), row)],
        out_shape=[sds] * 5 + [jax.ShapeDtypeStruct((t, MIX), BF16)],
        dims=("parallel",))(h, o, za, pool, p, wo, wpg, wpe, gpost)


def _loss_grad(y, target, *, tm=512):
    t = y.shape[0]

    def body(y_ref, t_ref, dy_ref, loss_ref):
        @pl.when(pl.program_id(0) == 0)
        def _():
            loss_ref[...] = jnp.zeros_like(loss_ref)

        err = y_ref[...] - t_ref[...]
        dy_ref[...] = err * (1.0 / D_MODEL)
        part = jnp.sum(jnp.sum(err * err, axis=1, keepdims=True), axis=0, keepdims=True)
        loss_ref[...] += part * (0.5 / D_MODEL)

    blk = pl.BlockSpec((tm, D_MODEL), lambda i: (i, 0))
    return _pcall(
        body, name="loss_grad", grid=(t // tm,), in_specs=[blk, blk],
        out_specs=[blk, pl.BlockSpec((8, LANES), lambda i: (0, 0))],
        out_shape=[jax.ShapeDtypeStruct((t, D_MODEL), F32), jax.ShapeDtypeStruct((8, LANES), F32)],
        dims=("arbitrary",))(y, target)


def _out_block_bwd(dh2, gate, pe, mix, o, za, wpg, wo, gpost, gsum, *, tm=256):
    t = dh2.shape[0]

    def body(dh2_ref, gate_ref, pe_ref, mix_ref, o_ref, za_ref, wpg_ref, wo_ref, g_ref, gs_ref,
             dh1_ref, dgp_ref, dpe_ref, dmix_ref, do_ref, dza_ref, dpool_ref, delta_ref, dg_ref):
        @pl.when(pl.program_id(0) == 0)
        def _():
            dg_ref[...] = jnp.zeros_like(dg_ref)

        dh2 = dh2_ref[...]
        gate = gate_ref[...]
        dpe_ref[...] = (dh2 * gate).astype(BF16)
        dgp = (dh2 * pe_ref[...] * gate * (1.0 - gate)).astype(BF16)
        dgp_ref[...] = dgp
        dh1 = dh2 + _dot(dgp, wpg_ref[...], NT)
        dh1_ref[...] = dh1
        mix = mix_ref[...]
        r = lax.rsqrt(jnp.mean(mix * mix, axis=-1, keepdims=True) + EPS)
        dg_ref[...] += jnp.sum(dh1 * mix * r, axis=0, keepdims=True)
        a = dh1 * g_ref[...]
        dmix = (r * a - mix * (r * r * r) * jnp.mean(a * mix, axis=-1, keepdims=True)).astype(BF16)
        dmix_ref[...] = dmix
        dattn = _dot(dmix, wo_ref[:D_MODEL, :], NT)
        dpool_ref[...] = _dot(dmix, wo_ref[D_MODEL:, :], NT)
        z = za_ref[...]
        sg = _sigmoid(z)
        o = o_ref[...]
        do = (dattn * (z * sg)).astype(BF16)
        do_ref[...] = do
        dza_ref[...] = (dattn * o * (sg * (1.0 + z * (1.0 - sg)))).astype(BF16)
        prod = do.astype(F32) * o
        hi = prod.astype(BF16)
        rest = (prod - hi.astype(F32)).astype(BF16)
        delta_ref[...] = _dot(hi, gs_ref[...]) + _dot(rest, gs_ref[...])

    row = lambda i: (i, 0)
    const = lambda i: (0, 0)
    blk = pl.BlockSpec((tm, D_MODEL), row)
    sds = lambda dt: jax.ShapeDtypeStruct((t, D_MODEL), dt)
    return _pcall(
        body, name="out_block_bwd", grid=(t // tm,),
        in_specs=[blk] * 6 + [pl.BlockSpec((D_MODEL, D_MODEL), const), pl.BlockSpec((MIX, D_MODEL), const),
                              pl.BlockSpec((1, D_MODEL), const), pl.BlockSpec((D_MODEL, LANES), const)],
        out_specs=[blk] * 7 + [pl.BlockSpec((tm, LANES), row), pl.BlockSpec((1, D_MODEL), const)],
        out_shape=[sds(F32)] + [sds(BF16)] * 5 + [sds(F32), jax.ShapeDtypeStruct((t, LANES), F32),
                                                 jax.ShapeDtypeStruct((1, D_MODEL), F32)],
        dims=("arbitrary",))(dh2, gate, pe, mix, o, za, wpg, wo, gpost, gsum)


def _in_proj_bwd(dsegs, dfl, wm, wf, h, dh1, gpre, *, tm=512):
    t = h.shape[0]

    def body(*refs):
        seg_refs = refs[:N_SEG]
        dfl_ref, wm_ref, wf_ref, h_ref, dh1_ref, g_ref, dh_ref, dg_ref, acc_ref = refs[N_SEG:]
        i = pl.program_id(0)
        j = pl.program_id(1)

        @pl.when(jnp.logical_and(i == 0, j == 0))
        def _():
            dg_ref[...] = jnp.zeros_like(dg_ref)

        @pl.when(j == 0)
        def _():
            acc_ref[...] = _dot(dfl_ref[...].astype(BF16), wf_ref[...], NT)

        for sgi, ref in enumerate(seg_refs):
            @pl.when(j == sgi)
            def _(ref=ref):
                acc_ref[...] += _dot(ref[...].astype(BF16), wm_ref[...], NT)

        @pl.when(j == N_SEG - 1)
        def _():
            dhn = acc_ref[...]
            x = h_ref[...]
            r = lax.rsqrt(jnp.mean(x * x, axis=-1, keepdims=True) + EPS)
            dg_ref[...] += jnp.sum(dhn * x * r, axis=0, keepdims=True)
            a = dhn * g_ref[...]
            dh_ref[...] = dh1_ref[...] + r * a - x * (r * r * r) * jnp.mean(a * x, axis=-1, keepdims=True)

    row = lambda i, j: (i, 0)
    const = lambda i, j: (0, 0)
    blk = pl.BlockSpec((tm, D_MODEL), row)
    return _pcall(
        body, name="in_proj_bwd", grid=(t // tm, N_SEG),
        in_specs=[blk] * N_SEG + [pl.BlockSpec((tm, LANES), row), pl.BlockSpec((D_MODEL, D_MODEL), lambda i, j: (0, j)),
                                  pl.BlockSpec((D_MODEL, LANES), const), blk, blk, pl.BlockSpec((1, D_MODEL), const)],
        out_specs=[blk, pl.BlockSpec((1, D_MODEL), const)],
        out_shape=[jax.ShapeDtypeStruct((t, D_MODEL), F32), jax.ShapeDtypeStruct((1, D_MODEL), F32)],
        scratch_shapes=[pltpu.VMEM((tm, D_MODEL), F32)],
        dims=("arbitrary", "arbitrary"))(*dsegs, dfl, wm, wf, h, dh1, gpre)


def _chip_peers(x, y):
    return [(1 - x, y), (x, 1 - y), (1 - x, 1 - y)]


def _ici_views(src, land, gather, x, y, c, pi, px, py):
    if gather:
        return src.at[c], land.at[pi], land.at[pi]
    return src.at[2 * px + py], land.at[2 * x + y], land.at[2 * px + py]


def _ici_start(srcs, lands, after, *, gather, name):
    n = len(srcs)

    def body(*refs):
        src, land = refs[:n], refs[n:2 * n]
        send, recv = refs[2 * n + 1], refs[2 * n + 2]
        token = refs[-1]
        x, y, c = lax.axis_index("x"), lax.axis_index("y"), lax.axis_index("c")
        for kk in range(n):
            for pi, (px, py) in enumerate(_chip_peers(x, y)):
                mine, there, _ = _ici_views(src[kk], land[kk], gather, x, y, c, pi, px, py)
                pltpu.make_async_remote_copy(src_ref=mine, dst_ref=there, send_sem=send.at[3 * kk + pi],
                                             recv_sem=recv.at[3 * kk + pi], device_id=(px, py, c),
                                             device_id_type=MESH).start()
        token[...] = jnp.zeros_like(token)

    hbm = lambda a: pltpu.with_memory_space_constraint(a, pltpu.HBM)
    outs = pl.pallas_call(
        body, name=name,
        out_shape=(pltpu.SemaphoreType.DMA((3 * n,)), pltpu.SemaphoreType.DMA((3 * n,)),
                   *[pltpu.HBM(a.shape, a.dtype) for a in srcs], *[pltpu.HBM(a.shape, a.dtype) for a in lands],
                   jax.ShapeDtypeStruct((8, LANES), F32)),
        in_specs=[HBM_SPEC] * (2 * n) + [ANY],
        out_specs=(SEM_SPEC, SEM_SPEC, *[HBM_SPEC] * (2 * n), pl.BlockSpec(memory_space=pltpu.VMEM)),
        input_output_aliases={i: 2 + i for i in range(2 * n)},
        compiler_params=pltpu.CompilerParams(has_side_effects=SIDE_EFFECT),
    )(*[hbm(a) for a in srcs], *[hbm(a) for a in lands], after)
    return outs[0], outs[1], list(outs[2:2 + n]), list(outs[2 + n:2 + 2 * n]), outs[-1]


def _ici_wait(send, recv, srcs, lands, after, *, gather, name):
    n = len(srcs)

    def body(*refs):
        src, land = refs[:n], refs[n:2 * n]
        send_ref, recv_ref = refs[2 * n], refs[2 * n + 1]
        x, y, c = lax.axis_index("x"), lax.axis_index("y"), lax.axis_index("c")
        for kk in range(n):
            for pi, (px, py) in enumerate(_chip_peers(x, y)):
                mine, _, here = _ici_views(src[kk], land[kk], gather, x, y, c, pi, px, py)
                cp = pltpu.make_async_remote_copy(src_ref=mine, dst_ref=here, send_sem=send_ref.at[3 * kk + pi],
                                                  recv_sem=recv_ref.at[3 * kk + pi], device_id=(px, py, c),
                                                  device_id_type=MESH)
                cp.wait_send()
                cp.wait_recv()

    outs = pl.pallas_call(
        body, name=name,
        out_shape=[pltpu.HBM(a.shape, a.dtype) for a in srcs] + [pltpu.HBM(a.shape, a.dtype) for a in lands],
        in_specs=[HBM_SPEC] * (2 * n) + [SEM_SPEC, SEM_SPEC, ANY], out_specs=[HBM_SPEC] * (2 * n),
        input_output_aliases={i: i for i in range(2 * n)},
        compiler_params=pltpu.CompilerParams(has_side_effects=SIDE_EFFECT),
    )(*srcs, *lands, send, recv, after)
    return list(outs[:n]), list(outs[n:])


def _sibling_send(arrays, *, name):
    n = len(arrays)

    def body(*refs):
        ins, outs = refs[:n], refs[n:2 * n]
        send, recv = refs[2 * n:]
        sib = (lax.axis_index("x"), lax.axis_index("y"), 1 - lax.axis_index("c"))
        copies = [pltpu.make_async_remote_copy(src_ref=ins[kk], dst_ref=outs[kk], send_sem=send.at[kk],
                                               recv_sem=recv.at[kk], device_id=sib, device_id_type=MESH)
                  for kk in range(n)]
        for cp in copies:
            cp.start()
        for cp in copies:
            cp.wait_recv()
        for cp in copies:
            cp.wait_send()

    return _pcall(
        body, name=name, in_specs=[ANY] * n, out_specs=[ANY] * n,
        out_shape=[jax.ShapeDtypeStruct(a.shape, a.dtype) for a in arrays],
        scratch_shapes=[pltpu.SemaphoreType.DMA((n,)), pltpu.SemaphoreType.DMA((n,))])(*arrays)


def _all_gather_small(a, *, name):
    def body(a_ref, o_ref, send, recv, loc):
        x, y, c = lax.axis_index("x"), lax.axis_index("y"), lax.axis_index("c")
        me = 4 * x + 2 * y + c

        def peer(rel):
            fx, fy, fc = (rel >> 2) & 1, (rel >> 1) & 1, rel & 1
            px = (1 - x) if fx else x
            py = (1 - y) if fy else y
            pc = (1 - c) if fc else c
            return px, py, pc

        def remote(rel, slot):
            return pltpu.make_async_remote_copy(
                src_ref=a_ref, dst_ref=o_ref.at[slot], send_sem=send.at[rel - 1], recv_sem=recv.at[rel - 1],
                device_id=peer(rel), device_id_type=MESH)

        mine = pltpu.make_async_copy(a_ref, o_ref.at[me], loc)
        mine.start()
        sends = [remote(rel, me) for rel in range(1, 8)]
        for cp in sends:
            cp.start()
        for rel in range(1, 8):
            px, py, pc = peer(rel)
            remote(rel, 4 * px + 2 * py + pc).wait_recv()
        for cp in sends:
            cp.wait_send()
        mine.wait()

    return _pcall(
        body, name=name, in_specs=[ANY], out_specs=ANY,
        out_shape=jax.ShapeDtypeStruct((8,) + a.shape, a.dtype),
        scratch_shapes=[pltpu.SemaphoreType.DMA((7,)), pltpu.SemaphoreType.DMA((7,)), pltpu.SemaphoreType.DMA])(a)


def _pair_sum(a, b, *, name, tr=256):
    _, rows, cols = a.shape
    tr = min(tr, rows)

    def body(a_ref, b_ref, o_ref):
        o_ref[...] = (a_ref[...].astype(F32) + b_ref[...].astype(F32)).astype(BF16)

    blk = pl.BlockSpec((None, tr, cols), lambda j, i: (j, i, 0))
    return _pcall(body, name=name, grid=(4, rows // tr), in_specs=[blk, blk], out_specs=blk,
                  out_shape=jax.ShapeDtypeStruct(a.shape, BF16), dims=("parallel", "parallel"))(a, b)


def _chip_sum(own, recv, chip, *, name, tr=256):
    _, rows, cols = own.shape
    tr = min(tr, rows)

    def body(chip_ref, own_ref, r1_ref, r2_ref, r3_ref, o_ref):
        acc = own_ref[...].astype(F32)
        for ref in (r1_ref, r2_ref, r3_ref):
            acc = acc + ref[...].astype(F32)
        o_ref[...] = acc

    def slot(step):
        return pl.BlockSpec((None, tr, cols), lambda i, chip_ref: ((chip_ref[0] + step) % 4, i, 0))

    return pl.pallas_call(
        body, name=name, out_shape=jax.ShapeDtypeStruct((rows, cols), F32),
        grid_spec=pltpu.PrefetchScalarGridSpec(
            num_scalar_prefetch=1, grid=(rows // tr,), in_specs=[slot(0), slot(1), slot(2), slot(3)],
            out_specs=pl.BlockSpec((tr, cols), lambda i, chip_ref: (i, 0))),
        compiler_params=pltpu.CompilerParams(dimension_semantics=("parallel",), vmem_limit_bytes=VMEM_LIMIT),
    )(chip, own, recv, recv, recv)


def _adamw_math(w, g, m, v):
    m = ADAM_B1 * m + (1.0 - ADAM_B1) * g
    v = ADAM_B2 * v + (1.0 - ADAM_B2) * (g * g)
    m_hat = m / (1.0 - ADAM_B1 ** ADAM_STEP)
    v_hat = v / (1.0 - ADAM_B2 ** ADAM_STEP)
    delta = -ADAM_LR * (m_hat / (jnp.sqrt(v_hat) + ADAM_EPS) + ADAM_WD * w)
    return delta, m, v


def _adamw(w, m, v, g, layer, prev, *, name, tr=128):
    rows, cols = g.shape
    tr = min(tr, rows)
    nblk = rows // tr

    def body(w_ref, m_ref, v_ref, gin_ref, *refs):
        g_ref, d_ref, mo_ref, vo_ref = refs[-4:]
        grad = gin_ref[...]
        delta, m_new, v_new = _adamw_math(w_ref[...], grad, m_ref[...], v_ref[...])
        g_ref[...] = grad
        d_ref[...] = delta
        mo_ref[...] = m_new
        vo_ref[...] = v_new

    blk = pl.BlockSpec((tr, cols), lambda i: (layer * nblk + i, 0))
    carried = [] if prev is None else list(prev)
    return pl.pallas_call(
        body, name=name, grid=(nblk,), out_shape=[jax.ShapeDtypeStruct(w.shape, F32)] * 4,
        in_specs=[blk] * 3 + [pl.BlockSpec((tr, cols), lambda i: (i, 0))] + [ANY] * len(carried), out_specs=[blk] * 4,
        input_output_aliases={4 + i: i for i in range(len(carried))},
        compiler_params=pltpu.CompilerParams(dimension_semantics=("parallel",), vmem_limit_bytes=VMEM_LIMIT),
    )(w, m, v, g, *carried)


def _adamw_small(w, m, v, parts, *, name):
    def body(w_ref, m_ref, v_ref, p_ref, g_ref, d_ref, mo_ref, vo_ref):
        g = p_ref[0]
        for dev in range(1, 8):
            g = g + p_ref[dev]
        delta, m_new, v_new = _adamw_math(w_ref[...], g, m_ref[...], v_ref[...])
        g_ref[...] = g
        d_ref[...] = delta
        mo_ref[...] = m_new
        vo_ref[...] = v_new

    return _pcall(body, name=name, out_shape=[jax.ShapeDtypeStruct(w.shape, F32)] * 4)(w, m, v, parts)


def _rows_from_heads(a, nb, s):
    return a.reshape(nb, s, N_HEADS).transpose(0, 2, 1).reshape(nb, N_PAIRS, 2, s)


def _layer_fwd(h, p_l, wts, nb):
    t = h.shape[0]
    s = t // nb
    q, k, v, za, u, zp, fl, hn = _norm_in_proj(h, wts["gpre"], wts["wm"], wts["wf"])
    flt = fl[:, :N_HEADS].reshape(nb, s, N_HEADS).transpose(0, 2, 1)
    c = _gate_cumsum(flt, wts["bf"])
    o, lse = _attn_fwd(q, k, v, c.reshape(nb, N_PAIRS, 2, s), nb=nb)
    pool, pooled, mixed = _pool_fwd(u, zp, wts["wpool"], wts["ps"], nb=nb)
    h2, mix, h1, gate, pe, cat = _out_block(h, o, za, pool, p_l, wts["wo"], wts["wpg"], wts["wpe"], wts["gpost"])
    saved = dict(h=h, hn=hn, q=q, k=k, v=v, za=za, zp=zp, flt=flt, c=c, o=o, lse=lse, pooled=pooled, mixed=mixed,
                 mix=mix, h1=h1, gate=gate, pe=pe, cat=cat, p=p_l)
    return h2, saved


def _layer_bwd(dh2, sv, wts, gsum, nb):
    t = dh2.shape[0]
    s = t // nb
    dh1, dgp, dpe, dmix, do, dza, dpool, delta, dgpost = _out_block_bwd(
        dh2, sv["gate"], sv["pe"], sv["mix"], sv["o"], sv["za"], wts["wpg"], wts["wo"], wts["gpost"], gsum)
    g_wpe = _matmul(sv["p"], dpe, ta=True, name="wgrad_pe")
    g_wpg = _matmul(sv["h1"], dgp, ta=True, name="wgrad_pg")
    g_wo = _matmul(sv["cat"], dmix, ta=True, name="wgrad_out")

    delta4 = _rows_from_heads(delta[:, :N_HEADS], nb, s)
    dq, dk, dv, dcol, drow = _attn_bwd(sv["q"], sv["k"], sv["v"], do, sv["c"].reshape(nb, N_PAIRS, 2, s), sv["lse"],
                                       delta4, nb=nb)
    dflt, dbf = _gate_cumsum_bwd(dcol.reshape(nb, N_HEADS, s), drow.reshape(nb, N_HEADS, s), sv["flt"], wts["bf"])
    dfl = jnp.pad(dflt.transpose(0, 2, 1).reshape(t, N_HEADS), ((0, 0), (0, LANES - N_HEADS)))

    dmixed, du, dzp, dps = _pool_bwd(dpool, sv["mixed"], sv["zp"], wts["wpool"], wts["ps"], nb=nb)
    g_wpool = _pool_wgrad(sv["pooled"], dmixed)

    dsegs = (dq, dk, dv, dza, du, dzp)
    dh, dgpre = _in_proj_bwd(dsegs, dfl, wts["wm"], wts["wf"], sv["h"], dh1, wts["gpre"])
    g_segs = [_matmul(sv["hn"], dx, ta=True, name="wgrad_in_f32" if dx.dtype == F32 else "wgrad_in")
              for dx in dsegs]
    g_wf = _matmul(sv["hn"], dfl, ta=True, name="wgrad_in_f")
    g_win = g_segs[:4] + [g_wf[:, :N_HEADS]] + g_segs[4:]
    grads = dict(w_in=g_win, w_out=g_wo, w_pg=g_wpg, w_pe=g_wpe, w_pool=g_wpool,
                 norm_pre=dgpre[0], norm_post=dgpost[0], pool_scale=dps[0], b_f=dbf[:, 0])
    return dh, grads


def kernel(x, p, norm_pre, norm_post, w_in, b_f, w_pool, pool_scale, w_out, w_pg, w_pe, loss_target, m_norm_pre, m_norm_post, m_w_in, m_b_f, m_w_pool, m_pool_scale, m_w_out, m_w_pg, m_w_pe, v_norm_pre, v_norm_post, v_w_in, v_b_f, v_w_pool, v_pool_scale, v_w_out, v_w_pg, v_w_pe):
    nb, s, _ = x.shape
    t = nb * s
    depth = w_in.shape[0]
    big = dict(w_in=(w_in, m_w_in, v_w_in), w_out=(w_out, m_w_out, v_w_out), w_pg=(w_pg, m_w_pg, v_w_pg),
               w_pe=(w_pe, m_w_pe, v_w_pe), w_pool=(w_pool, m_w_pool, v_w_pool))
    names = list(big)
    cols = {n: big[n][0].shape[-1] for n in names}
    chip = (2 * lax.axis_index("x") + lax.axis_index("y")).astype(jnp.int32).reshape(1)
    core = lax.axis_index("c")
    south = core == 0

    gathers = []
    token = jnp.zeros((8, LANES), F32)
    for l in range(depth):
        halves = [big[n][0][l].reshape(2, -1, cols[n]).astype(BF16) for n in names]
        lands = [lax.empty((3,) + a.shape[1:], a.dtype) for a in halves]
        send, recv, halves, lands, token = _ici_start(halves, lands, token, gather=True, name=f"gather_start_{l}")
        gathers.append((send, recv, halves, lands))

    def layer_weights(l, after, anchor):
        send, recv, halves, lands = gathers[l]
        halves, lands = _ici_wait(send, recv, halves, lands, after, gather=True, name=f"gather_wait_{l}")
        others = _sibling_send(lands, name="gather_pass_on")
        full = {}
        for n, own, land, other in zip(names, halves, lands, others):
            low = jnp.where(south, land, other)
            high = jnp.where(south, other, land)
            rel = [jnp.concatenate([low[pi], high[pi]], axis=0) for pi in range(3)]
            by_rel = jnp.stack([own.reshape(-1, cols[n]), rel[1], rel[0], rel[2]])
            full[n] = [lax.dynamic_index_in_dim(by_rel, j ^ chip[0], 0, keepdims=False) for j in range(4)]
        win = jnp.concatenate(full["w_in"], axis=1)
        wm = jnp.concatenate([win[:, :F_OFF], win[:, F_OFF + N_HEADS:]], axis=1)
        wf = jnp.pad(win[:, F_OFF:F_OFF + N_HEADS], ((0, 0), (0, LANES - N_HEADS)))
        return dict(
            wm=wm, wf=wf, wo=jnp.concatenate(full["w_out"], axis=0), wpg=jnp.concatenate(full["w_pg"], axis=0),
            wpe=jnp.concatenate(full["w_pe"], axis=1),
            wpool=jnp.stack(full["w_pool"]).reshape(4, N_GROUPS, GROUP_DIM // 4, GROUP_DIM).transpose(1, 0, 2, 3).reshape(
                N_GROUPS, GROUP_DIM, GROUP_DIM),
            gpre=norm_pre[l][None] + anchor, gpost=norm_post[l][None], ps=pool_scale[l][None], bf=b_f[l][:, None])

    h = x.reshape(t, D_MODEL)
    saved, layers = [], []
    after = token
    for l in range(depth):
        layers.append(layer_weights(l, after, token[0, 0]))
        h, sv = _layer_fwd(h, p[l].reshape(t, PLE_DIM), layers[l], nb)
        saved.append(sv)
        after = h
    dh, loss_blk = _loss_grad(h, loss_target.reshape(t, D_MODEL))
    loss = lax.psum(loss_blk[0, 0], ("x", "y", "c"))
    gsum = (jnp.arange(D_MODEL)[:, None] // HEAD_DIM == jnp.arange(LANES)[None, :]).astype(BF16)

    def column_range(segments, lo, hi):
        parts, start = [], 0
        for seg in segments:
            a, b = max(lo, start), min(hi, start + seg.shape[1])
            if a < b:
                parts.append(seg[:, a - start:b - start])
            start += seg.shape[1]
        return jnp.concatenate(parts, axis=1)

    def pieces(g, name):
        if name == "w_in":
            by_chip = jnp.stack([column_range(g, j * cols[name], (j + 1) * cols[name]) for j in range(4)])
        elif name == "w_pe":
            by_chip = jnp.stack([g[:, j * cols[name]:(j + 1) * cols[name]] for j in range(4)])
        elif name == "w_pool":
            by_chip = g.reshape(N_GROUPS, 4, GROUP_DIM // 4, GROUP_DIM).transpose(1, 0, 2, 3)
        else:
            by_chip = g
        return by_chip.reshape(4, 2, -1, cols[name]).transpose(1, 0, 2, 3).astype(BF16)

    def scatter_start(l, g):
        mine, sent = [], []
        for n in names:
            halves = pieces(g[n], n)
            mine.append(lax.dynamic_index_in_dim(halves, core, 0, keepdims=False))
            sent.append(lax.dynamic_index_in_dim(halves, 1 - core, 0, keepdims=False))
        theirs = _sibling_send(sent, name="presum_exchange")
        sums = [_pair_sum(a, b, name="presum_" + n) for n, a, b in zip(names, mine, theirs)]
        lands = [lax.empty(a.shape, a.dtype) for a in sums]
        return _ici_start(sums, lands, token, gather=False, name=f"scatter_start_{l}")

    def scatter_finish(l, state, after):
        send, recv, sums, lands, _ = state
        sums, lands = _ici_wait(send, recv, sums, lands, after, gather=False, name=f"scatter_wait_{l}")
        mine = [_chip_sum(a, r, chip, name="chip_sum_" + n) for n, a, r in zip(names, sums, lands)]
        theirs = _sibling_send(mine, name="halves_exchange")
        return {n: jnp.concatenate([jnp.where(south, a, b), jnp.where(south, b, a)], axis=0)
                for n, a, b in zip(names, mine, theirs)}

    grads = [None] * depth
    big_out = {n: None for n in names}

    def update(l, state, after):
        reduced = scatter_finish(l, state, after)
        for n in names:
            w, m, v = big[n]
            big_out[n] = _adamw(w.reshape(-1, cols[n]), m.reshape(-1, cols[n]), v.reshape(-1, cols[n]), reduced[n], l,
                                big_out[n], name="adamw_" + n)

    pending = None
    for l in reversed(range(depth)):
        wts = layers[l]
        if pending is not None:
            wts = dict(wts, gpost=wts["gpost"] + pending[1][4][0, 0])
        dh, grads[l] = _layer_bwd(dh, saved[l], wts, gsum, nb)
        state = scatter_start(l, grads[l])
        if pending is not None:
            update(pending[0], pending[1], state[4])
        pending = (l, state)
    grad_x = dh.reshape(nb, s, D_MODEL)

    small = dict(norm_pre=(norm_pre, m_norm_pre, v_norm_pre), norm_post=(norm_post, m_norm_post, v_norm_post),
                 pool_scale=(pool_scale, m_pool_scale, v_pool_scale), b_f=(b_f, m_b_f, v_b_f))

    def pack(get):
        rows = []
        for n in small:
            a = get(n)
            rows.append(jnp.pad(a, ((0, 0), (0, D_MODEL - a.shape[1]))))
        return jnp.concatenate(rows, axis=0)

    parts = _all_gather_small(pack(lambda n: jnp.stack([grads[l][n] for l in range(depth)])), name="gather_small")
    small_packed = _adamw_small(pack(lambda n: small[n][0]), pack(lambda n: small[n][1]), pack(lambda n: small[n][2]),
                                parts, name="adamw_small")
    small_out = {}
    for i, n in enumerate(small):
        width = small[n][0].shape[1]
        small_out[n] = [o[depth * i:depth * (i + 1), :width] for o in small_packed]

    update(pending[0], pending[1], big_out[names[-1]][0] if depth > 1 else small_packed[0])
    big_out = {n: [o.reshape(big[n][0].shape) for o in outs] for n, outs in big_out.items()}

    order =["norm_pre", "norm_post", "w_in", "b_f", "w_pool", "pool_scale", "w_out", "w_pg", "w_pe"]
    result = [loss, grad_x]
    for kind in range(4):
        for n in order:
            result.append(big_out[n][kind] if n in big_out else small_out[n][kind])
    return tuple(result)
```

```python
import functools

import jax
import jax.numpy as jnp
from jax import lax
from jax.experimental import pallas as pl
from jax.experimental.pallas import tpu as pltpu

F32 = jnp.float32
BF16 = jnp.bfloat16

D_MODEL = 1024
N_HEADS = 16
HEAD_DIM = 64
PAIR = 2 * HEAD_DIM
N_PAIRS = N_HEADS // 2
N_GROUPS = 4
GROUP_DIM = 256
PLE_DIM = 256
MIX = 2 * D_MODEL
N_SEG = 6
F_OFF = 4 * D_MODEL
IN_COLS = N_SEG * D_MODEL + N_HEADS
LANES = 128
EPS = 1e-6
SCALE = 0.125
NEG = -1e30

ADAM_LR = 0.001
ADAM_B1 = 0.9
ADAM_B2 = 0.999
ADAM_EPS = 1e-08
ADAM_WD = 0.01
ADAM_STEP = 10

VMEM_LIMIT = 56 * 1024 * 1024
MESH = pl.DeviceIdType.MESH
ANY = pl.BlockSpec(memory_space=pl.ANY)
HBM_SPEC = pl.BlockSpec(memory_space=pltpu.HBM)
SEM_SPEC = pl.BlockSpec(memory_space=pltpu.SEMAPHORE)
SIDE_EFFECT = pltpu.SideEffectType.DATAFLOW_SIDE_EFFECTING

NT = (((1,), (1,)), ((), ()))
TN = (((0,), (0,)), ((), ()))
NN = (((1,), (0,)), ((), ()))


def _pcall(body, *, name, out_shape, grid=(), in_specs=None, out_specs=None, scratch_shapes=(), dims=None):
    kw = {}
    if in_specs is not None:
        kw["in_specs"] = in_specs
    if out_specs is not None:
        kw["out_specs"] = out_specs
    return pl.pallas_call(
        body, name=name, out_shape=out_shape, grid=grid, scratch_shapes=list(scratch_shapes),
        compiler_params=pltpu.CompilerParams(dimension_semantics=dims, vmem_limit_bytes=VMEM_LIMIT), **kw)


def _dot(a, b, dn=NN):
    return lax.dot_general(a, b, dn, preferred_element_type=F32)


def _sigmoid(x):
    return 1.0 / (1.0 + jnp.exp(-x))


def _matmul(a, b, *, ta=False, name, tm=512, tn=1024, tk=2048):
    if ta:
        kdim, m = a.shape
    else:
        m, kdim = a.shape
    n = b.shape[1]
    tm, tn, tk = min(tm, m), min(tn, n), min(tk, kdim)
    nk = kdim // tk

    def body(a_ref, b_ref, o_ref, acc_ref):
        k = pl.program_id(2)

        @pl.when(k == 0)
        def _():
            acc_ref[...] = jnp.zeros_like(acc_ref)

        acc_ref[...] += _dot(a_ref[...].astype(BF16), b_ref[...].astype(BF16), TN if ta else NN)

        @pl.when(k == nk - 1)
        def _():
            o_ref[...] = acc_ref[...].astype(BF16)

    a_spec = pl.BlockSpec((tk, tm), lambda i, j, k: (k, i)) if ta else pl.BlockSpec((tm, tk), lambda i, j, k: (i, k))
    return _pcall(
        body, name=name, out_shape=jax.ShapeDtypeStruct((m, n), BF16), grid=(m // tm, n // tn, nk),
        in_specs=[a_spec, pl.BlockSpec((tk, tn), lambda i, j, k: (k, j))],
        out_specs=pl.BlockSpec((tm, tn), lambda i, j, k: (i, j)),
        scratch_shapes=[pltpu.VMEM((tm, tn), F32)], dims=("parallel", "parallel", "arbitrary"))(a, b)


def _norm_in_proj(h, g, wm, wf, *, tm=512):
    t = h.shape[0]

    def body(h_ref, g_ref, wm_ref, wf_ref, q_ref, k_ref, v_ref, za_ref, u_ref, zp_ref, fl_ref, hn_ref):
        j = pl.program_id(1)

        @pl.when(j == 0)
        def _():
            x = h_ref[...]
            r = lax.rsqrt(jnp.mean(x * x, axis=-1, keepdims=True) + EPS)
            hn = (x * r * g_ref[...]).astype(BF16)
            hn_ref[...] = hn
            fl_ref[...] = _dot(hn, wf_ref[...])

        y = _dot(hn_ref[...], wm_ref[...])
        for s, ref in enumerate((q_ref, k_ref, v_ref, za_ref, u_ref, zp_ref)):
            @pl.when(j == s)
            def _(ref=ref):
                ref[...] = y.astype(ref.dtype)

    row = lambda i, j: (i, 0)
    blk = pl.BlockSpec((tm, D_MODEL), row)
    sds = lambda dt: jax.ShapeDtypeStruct((t, D_MODEL), dt)
    return _pcall(
        body, name="norm_in_proj", grid=(t // tm, N_SEG),
        in_specs=[blk, pl.BlockSpec((1, D_MODEL), lambda i, j: (0, 0)),
                  pl.BlockSpec((D_MODEL, D_MODEL), lambda i, j: (0, j)),
                  pl.BlockSpec((D_MODEL, LANES), lambda i, j: (0, 0))],
        out_specs=[blk] * 6 + [pl.BlockSpec((tm, LANES), row), blk],
        out_shape=[sds(BF16)] * 3 + [sds(F32)] * 3 + [jax.ShapeDtypeStruct((t, LANES), F32), sds(BF16)],
        dims=("parallel", "arbitrary"))(h, g, wm, wf)


def _gate_cumsum(flt, bias):
    b, hh, s = flt.shape

    def body(fl_ref, b_ref, c_ref):
        x = fl_ref[...] + b_ref[...]
        acc = jnp.minimum(x, 0.0) - jnp.log(1.0 + jnp.exp(-jnp.abs(x)))
        idx = lax.broadcasted_iota(jnp.int32, x.shape, 1)
        sh = 1
        while sh < s:
            acc = acc + jnp.where(idx >= sh, pltpu.roll(acc, sh, 1), 0.0)
            sh *= 2
        c_ref[...] = acc

    return _pcall(
        body, name="gate_cumsum", grid=(b,),
        in_specs=[pl.BlockSpec((None, hh, s), lambda i: (i, 0, 0)), pl.BlockSpec((hh, 1), lambda i: (0, 0))],
        out_specs=pl.BlockSpec((None, hh, s), lambda i: (i, 0, 0)),
        out_shape=jax.ShapeDtypeStruct((b, hh, s), F32), dims=("parallel",))(flt, bias)


def _gate_cumsum_bwd(dcol, drow, flt, bias):
    b, hh, s = flt.shape

    def body(d_ref, dr_ref, fl_ref, b_ref, dfl_ref, db_ref):
        @pl.when(pl.program_id(0) == 0)
        def _():
            db_ref[...] = jnp.zeros_like(db_ref)

        x = fl_ref[...] + b_ref[...]
        acc = dr_ref[...] - d_ref[...]
        idx = lax.broadcasted_iota(jnp.int32, x.shape, 1)
        sh = 1
        while sh < s:
            acc = acc + jnp.where(idx + sh < s, pltpu.roll(acc, s - sh, 1), 0.0)
            sh *= 2
        e = jnp.exp(-jnp.abs(x))
        sig_neg = jnp.where(x >= 0.0, e, 1.0) / (1.0 + e)
        dfl = acc * sig_neg
        dfl_ref[...] = dfl
        db_ref[...] += jnp.sum(dfl, axis=1, keepdims=True)

    return _pcall(
        body, name="gate_cumsum_bwd", grid=(b,),
        in_specs=[pl.BlockSpec((None, hh, s), lambda i: (i, 0, 0))] * 3 + [pl.BlockSpec((hh, 1), lambda i: (0, 0))],
        out_specs=[pl.BlockSpec((None, hh, s), lambda i: (i, 0, 0)), pl.BlockSpec((hh, 1), lambda i: (0, 0))],
        out_shape=[jax.ShapeDtypeStruct((b, hh, s), F32), jax.ShapeDtypeStruct((hh, 1), F32)],
        dims=("arbitrary",))(dcol, drow, flt, bias)


def _attn_fwd(q, k, v, c4, *, nb, tq=512, rc=32, diag=256):
    t = q.shape[0]
    s = t // nb
    tq = min(tq, s)
    nq = s // tq
    tk = tq
    diag = min(diag, tq)

    def body(q_ref, k_ref, v_ref, c_ref, o_ref, lse_ref, qh_scr, s_scr, p_scr, m_scr, acc_scr):
        qi = pl.program_id(2)
        lo = lax.broadcasted_iota(jnp.int32, (tq, PAIR), 1) < HEAD_DIM
        q2 = q_ref[...] * SCALE
        zero = jnp.zeros_like(q2)
        qh_scr[0] = jnp.where(lo, q2, zero)
        qh_scr[1] = jnp.where(lo, zero, q2)
        m_scr[...] = jnp.full(m_scr.shape, NEG, F32)
        acc_scr[...] = jnp.zeros(acc_scr.shape, F32)
        whole = [(0, tq, tk, False)]
        diagonal = [(r0, r0 + diag, r0 + diag, True) for r0 in range(0, tq, diag)]

        def block(kj, bands):
            off = pl.multiple_of(kj * tk, tk)
            k2 = k_ref[pl.ds(off, tk), :]
            v2 = v_ref[pl.ds(off, tk), :]
            one = jnp.ones_like(v2)
            va = (jnp.where(lo, v2, one), jnp.where(lo, one, v2))
            crow = c_ref[:, pl.ds(off, tk)]
            for hd in range(2):
                for r0, r1, nc, _ in bands:
                    s_scr[hd, r0:r1, :nc] = _dot(qh_scr[hd, r0:r1, :], k2[:nc], NT)
            for hd in range(2):
                for r0, r1, nc, masked in bands:
                    row = lax.broadcasted_iota(jnp.int32, (rc, nc), 0)
                    col = lax.broadcasted_iota(jnp.int32, (rc, nc), 1)
                    for r in range(r0, r1, rc):
                        sc = s_scr[hd, r:r + rc, :nc] - crow[hd:hd + 1, :nc]
                        if masked:
                            sc = jnp.where(row + r >= col, sc, NEG)
                        m_old = m_scr[hd, r:r + rc, :]
                        m_new = jnp.maximum(m_old, jnp.max(sc, axis=1, keepdims=True))
                        for cb in range(0, nc, LANES):
                            p_scr[hd, r:r + rc, cb:cb + LANES] = jnp.exp(sc[:, cb:cb + LANES] - m_new).astype(BF16)
                        m_scr[hd, r:r + rc, :] = m_new
                        acc_scr[hd, r:r + rc, :] = acc_scr[hd, r:r + rc, :] * jnp.exp(m_old - m_new)
                    acc_scr[hd, r0:r1, :] += _dot(p_scr[hd, r0:r1, :nc], va[hd][:nc])

        def below(kj, carry):
            block(kj, whole)
            return carry

        lax.fori_loop(0, qi, below, 0)
        block(qi, diagonal)
        a0, a1 = acc_scr[0], acc_scr[1]
        den = jnp.where(lo, pltpu.roll(a0, HEAD_DIM, 1), pltpu.roll(a1, HEAD_DIM, 1))
        o_ref[...] = jnp.where(lo, a0, a1) / den
        lse_t = (jnp.where(lo, m_scr[0], m_scr[1]) + jnp.log(den)).T
        lse_ref[0:1, :] = lse_t[0:1, :]
        lse_ref[1:2, :] = lse_t[HEAD_DIM:HEAD_DIM + 1, :]

    qspec = pl.BlockSpec((tq, PAIR), lambda b, hp, i: (b * nq + i, hp))
    kvspec = pl.BlockSpec((s, PAIR), lambda b, hp, i: (b, hp))
    return _pcall(
        body, name="attn_fwd", grid=(nb, N_PAIRS, nq),
        in_specs=[qspec, kvspec, kvspec, pl.BlockSpec((None, None, 2, s), lambda b, hp, i: (b, hp, 0, 0))],
        out_specs=[qspec, pl.BlockSpec((None, None, 2, tq), lambda b, hp, i: (b, hp, 0, i))],
        out_shape=[jax.ShapeDtypeStruct((t, D_MODEL), F32), jax.ShapeDtypeStruct((nb, N_PAIRS, 2, s), F32)],
        scratch_shapes=[pltpu.VMEM((2, tq, PAIR), BF16), pltpu.VMEM((2, tq, tk), F32), pltpu.VMEM((2, tq, tk), BF16),
                        pltpu.VMEM((2, tq, LANES), F32), pltpu.VMEM((2, tq, PAIR), F32)],
        dims=("parallel", "parallel", "arbitrary"))(q, k, v, c4)


def _attn_bwd(q, k, v, do, c4, lse4, delta4, *, nb, tk=512, rc=16, diag=256):
    t = q.shape[0]
    s = t // nb
    tk = min(tk, s)
    nk = s // tk
    tq = tk
    diag = min(diag, tk)

    def body(q_ref, do_ref, k_ref, v_ref, c_ref, lse_ref, dl_ref, dq_ref, dk_ref, dv_ref, dc_ref, dr_ref,
             kz_scr, vz_scr, ko_scr, crep_scr, st_scr, dp_scr, pt_scr, ds_scr, dk_scr, dv_scr, dq_scr, dr_scr):
        kj = pl.program_id(2)

        @pl.when(kj == 0)
        def _():
            dq_scr[...] = jnp.zeros_like(dq_scr)
            dr_scr[...] = jnp.zeros_like(dr_scr)

        lo = lax.broadcasted_iota(jnp.int32, (tk, PAIR), 1) < HEAD_DIM
        k2 = k_ref[...]
        v2 = v_ref[...]
        zero = jnp.zeros_like(k2)
        one = jnp.ones_like(k2)
        kz_scr[0] = jnp.where(lo, k2, zero)
        kz_scr[1] = jnp.where(lo, zero, k2)
        vz_scr[0] = jnp.where(lo, v2, zero)
        vz_scr[1] = jnp.where(lo, zero, v2)
        ko_scr[0] = jnp.where(lo, k2, one)
        ko_scr[1] = jnp.where(lo, one, k2)
        for hd in range(2):
            crep_scr[hd] = jnp.broadcast_to(c_ref[hd:hd + 1, :], (LANES, tk)).T
        dk_scr[...] = jnp.zeros(dk_scr.shape, F32)
        dv_scr[...] = jnp.zeros(dv_scr.shape, F32)
        row = lax.broadcasted_iota(jnp.int32, (rc, LANES), 0)
        col = lax.broadcasted_iota(jnp.int32, (rc, LANES), 1)

        whole = [(0, tk, 0, False)]
        diagonal = [(r0, r0 + diag, r0, True) for r0 in range(0, tk, diag)]

        def block(qi, bands):
            off = pl.multiple_of(qi * tq, tq)
            q2 = q_ref[pl.ds(off, tq), :] * SCALE
            do2 = do_ref[pl.ds(off, tq), :]
            lse = lse_ref[:, pl.ds(off, tq)]
            dl = dl_ref[:, pl.ds(off, tq)]
            qo = (jnp.where(lo, q2, jnp.ones_like(q2)), jnp.where(lo, jnp.ones_like(q2), q2))
            for hd in range(2):
                for r0, r1, c0, _ in bands:
                    st_scr[hd, r0:r1, c0:] = _dot(kz_scr[hd, r0:r1, :], q2[c0:], NT)
                    dp_scr[hd, r0:r1, c0:] = _dot(vz_scr[hd, r0:r1, :], do2[c0:], NT)
            for r0, r1, c0, masked in bands:
                dqs = []
                for hd in range(2):
                    for r in range(r0, r1, rc):
                        c_rep = crep_scr[hd, r:r + rc, :]
                        for cb in range(c0, tq, LANES):
                            pt = jnp.exp(st_scr[hd, r:r + rc, cb:cb + LANES] - c_rep - lse[hd:hd + 1, cb:cb + LANES])
                            if masked and cb < c0 + diag:
                                pt = jnp.where(col + cb >= row + r, pt, 0.0)
                            dst = pt * (dp_scr[hd, r:r + rc, cb:cb + LANES] - dl[hd:hd + 1, cb:cb + LANES])
                            pt_scr[hd, r:r + rc, cb:cb + LANES] = pt.astype(BF16)
                            ds_scr[hd, r:r + rc, cb:cb + LANES] = dst.astype(BF16)
                    dv_scr[hd, r0:r1, :] += _dot(pt_scr[hd, r0:r1, c0:], do2[c0:])
                    dk_scr[hd, r0:r1, :] += _dot(ds_scr[hd, r0:r1, c0:], qo[hd][c0:])
                    dqs.append(_dot(ds_scr[hd, r0:r1, c0:], ko_scr[hd, r0:r1, :], TN))
                dq_scr[pl.ds(off + c0, tq - c0), :] += jnp.where(lo[c0:], dqs[0], dqs[1]) * SCALE
                dr_scr[pl.ds(off + c0, tq - c0), :] += jnp.where(lo[c0:], dqs[1], dqs[0])

        block(kj, diagonal)

        def below(qi, carry):
            block(qi, whole)
            return carry

        lax.fori_loop(kj + 1, nk, below, 0)
        dk_ref[...] = jnp.where(lo, dk_scr[0], dk_scr[1]).astype(BF16)
        dv_ref[...] = jnp.where(lo, dv_scr[0], dv_scr[1]).astype(BF16)
        dc_t = jnp.where(lo, dk_scr[1], dk_scr[0]).T
        dc_ref[0:1, :] = dc_t[HEAD_DIM:HEAD_DIM + 1, :]
        dc_ref[1:2, :] = dc_t[0:1, :]

        @pl.when(kj == nk - 1)
        def _():
            dq_ref[...] = dq_scr[...].astype(BF16)
            for r in range(0, s, tq):
                dr_t = dr_scr[r:r + tq, :].T
                dr_ref[0:1, r:r + tq] = dr_t[HEAD_DIM:HEAD_DIM + 1, :]
                dr_ref[1:2, r:r + tq] = dr_t[0:1, :]

    full = pl.BlockSpec((s, PAIR), lambda b, hp, j: (b, hp))
    kblk = pl.BlockSpec((tk, PAIR), lambda b, hp, j: (b * nk + j, hp))
    rows = pl.BlockSpec((None, None, 2, s), lambda b, hp, j: (b, hp, 0, 0))
    krows = pl.BlockSpec((None, None, 2, tk), lambda b, hp, j: (b, hp, 0, j))
    sds = lambda dt: jax.ShapeDtypeStruct((t, D_MODEL), dt)
    rows_sds = jax.ShapeDtypeStruct((nb, N_PAIRS, 2, s), F32)
    pair_bf = pltpu.VMEM((2, tk, PAIR), BF16)
    pair_f = pltpu.VMEM((2, tk, PAIR), F32)
    return _pcall(
        body, name="attn_bwd", grid=(nb, N_PAIRS, nk),
        in_specs=[full, full, kblk, kblk, krows, rows, rows],
        out_specs=[full, kblk, kblk, krows, rows],
        out_shape=[sds(BF16), sds(BF16), sds(BF16), rows_sds, rows_sds],
        scratch_shapes=[pair_bf, pair_bf, pair_bf, pair_f, pltpu.VMEM((2, tk, tq), F32), pltpu.VMEM((2, tk, tq), F32),
                        pltpu.VMEM((2, tk, tq), BF16), pltpu.VMEM((2, tk, tq), BF16), pair_f, pair_f,
                        pltpu.VMEM((s, PAIR), F32), pltpu.VMEM((s, PAIR), F32)],
        dims=("parallel", "parallel", "arbitrary"))(q, do, k, v, c4, lse4, delta4)


def _window_sum(x, g, s, *, ahead):
    row = lax.broadcasted_iota(jnp.int32, x.shape, 0)
    for step in range(N_GROUPS):
        sh = 1 << step
        if ahead:
            moved = jnp.where(row + sh < s, pltpu.roll(x, s - sh, 0), 0.0)
        else:
            moved = jnp.where(row >= sh, pltpu.roll(x, sh, 0), 0.0)
        x = jnp.where(step <= g, x + moved, x)
    return x


def _inv_count(g, s):
    pos = lax.broadcasted_iota(jnp.int32, (s, 1), 0) + 1
    return 1.0 / jnp.minimum(pos, 2 << g).astype(F32)


def _pool_fwd(u, zp, wpool, ps, *, nb):
    t = u.shape[0]
    s = t // nb

    def body(u_ref, zp_ref, w_ref, ps_ref, out_ref, pooled_ref, mixed_ref):
        g = pl.program_id(1)
        x = u_ref[...]
        pooled = (_window_sum(x, g, s, ahead=False) * _inv_count(g, s) - x).astype(BF16)
        mixed = _dot(pooled, w_ref[...])
        z = zp_ref[...]
        pooled_ref[...] = pooled
        mixed_ref[...] = mixed
        out_ref[...] = (mixed * ps_ref[...] * (z * _sigmoid(z))).astype(BF16)

    blk = pl.BlockSpec((s, GROUP_DIM), lambda b, g: (b, g))
    sds = lambda dt: jax.ShapeDtypeStruct((t, D_MODEL), dt)
    return _pcall(
        body, name="pool_fwd", grid=(nb, N_GROUPS),
        in_specs=[blk, blk, pl.BlockSpec((None, GROUP_DIM, GROUP_DIM), lambda b, g: (g, 0, 0)),
                  pl.BlockSpec((1, GROUP_DIM), lambda b, g: (0, g))],
        out_specs=[blk, blk, blk], out_shape=[sds(BF16), sds(BF16), sds(F32)],
        dims=("parallel", "parallel"))(u, zp, wpool, ps)


def _pool_bwd(dpool, mixed, zp, wpool, ps, *, nb):
    t = dpool.shape[0]
    s = t // nb

    def body(dp_ref, mx_ref, zp_ref, w_ref, ps_ref, dmx_ref, du_ref, dzp_ref, dps_ref):
        g = pl.program_id(0)

        @pl.when(pl.program_id(1) == 0)
        def _():
            dps_ref[...] = jnp.zeros_like(dps_ref)

        dp = dp_ref[...]
        mixed = mx_ref[...]
        z = zp_ref[...]
        scale = ps_ref[...]
        sg = _sigmoid(z)
        silu = z * sg
        dmixed = (dp * scale * silu).astype(BF16)
        dmx_ref[...] = dmixed
        dps_ref[...] += jnp.sum(dp * mixed * silu, axis=0, keepdims=True)
        dzp_ref[...] = (dp * mixed * scale * (sg * (1.0 + z * (1.0 - sg)))).astype(BF16)
        dpooled = _dot(dmixed, w_ref[...], NT)
        du = _window_sum(dpooled * _inv_count(g, s), g, s, ahead=True) - dpooled
        du_ref[...] = du.astype(BF16)

    blk = pl.BlockSpec((s, GROUP_DIM), lambda g, b: (b, g))
    sds = lambda dt: jax.ShapeDtypeStruct((t, D_MODEL), dt)
    return _pcall(
        body, name="pool_bwd", grid=(N_GROUPS, nb),
        in_specs=[blk, blk, blk, pl.BlockSpec((None, GROUP_DIM, GROUP_DIM), lambda g, b: (g, 0, 0)),
                  pl.BlockSpec((1, GROUP_DIM), lambda g, b: (0, g))],
        out_specs=[blk, blk, blk, pl.BlockSpec((1, GROUP_DIM), lambda g, b: (0, g))],
        out_shape=[sds(BF16), sds(BF16), sds(BF16), jax.ShapeDtypeStruct((1, D_MODEL), F32)],
        dims=("parallel", "arbitrary"))(dpool, mixed, zp, wpool, ps)


def _pool_wgrad(pooled, dmixed, *, tk=1024):
    t = pooled.shape[0]
    tk = min(tk, t)
    nk = t // tk

    def body(a_ref, b_ref, o_ref):
        @pl.when(pl.program_id(1) == 0)
        def _():
            o_ref[...] = jnp.zeros_like(o_ref)

        o_ref[...] += _dot(a_ref[...], b_ref[...], TN)

    blk = pl.BlockSpec((tk, GROUP_DIM), lambda g, k: (k, g))
    return _pcall(
        body, name="pool_wgrad", grid=(N_GROUPS, nk), in_specs=[blk, blk],
        out_specs=pl.BlockSpec((None, GROUP_DIM, GROUP_DIM), lambda g, k: (g, 0, 0)),
        out_shape=jax.ShapeDtypeStruct((N_GROUPS, GROUP_DIM, GROUP_DIM), F32),
        dims=("parallel", "arbitrary"))(pooled, dmixed)


def _out_block(h, o, za, pool, p, wo, wpg, wpe, gpost, *, tm=256):
    t = h.shape[0]

    def body(h_ref, o_ref, za_ref, pool_ref, p_ref, wo_ref, wpg_ref, wpe_ref, g_ref,
             h2_ref, mix_ref, h1_ref, gate_ref, pe_ref, cat_ref):
        z = za_ref[...]
        a = (o_ref[...] * (z * _sigmoid(z))).astype(BF16)
        pool_v = pool_ref[...]
        cat_ref[:, :D_MODEL] = a
        cat_ref[:, D_MODEL:] = pool_v
        mix = _dot(a, wo_ref[:D_MODEL, :]) + _dot(pool_v, wo_ref[D_MODEL:, :])
        r = lax.rsqrt(jnp.mean(mix * mix, axis=-1, keepdims=True) + EPS)
        h1 = h_ref[...] + mix * r * g_ref[...]
        gate = _sigmoid(_dot(h1.astype(BF16), wpg_ref[...]))
        pe = _dot(p_ref[...].astype(BF16), wpe_ref[...])
        mix_ref[...] = mix
        h1_ref[...] = h1
        gate_ref[...] = gate
        pe_ref[...] = pe
        h2_ref[...] = h1 + gate * pe

    row = lambda i: (i, 0)
    const = lambda i: (0, 0)
    blk = pl.BlockSpec((tm, D_MODEL), row)
    sds = jax.ShapeDtypeStruct((t, D_MODEL), F32)
    return _pcall(
        body, name="out_block", grid=(t // tm,),
        in_specs=[blk, blk, blk, blk, pl.BlockSpec((tm, PLE_DIM), row),
                  pl.BlockSpec((MIX, D_MODEL), const), pl.BlockSpec((D_MODEL, D_MODEL), const),
                  pl.BlockSpec((PLE_DIM, D_MODEL), const), pl.BlockSpec((1, D_MODEL), const)],
        out_specs=[blk] * 5 + [pl.BlockSpec((tm, MIX), row)],
        out_shape=[sds] * 5 + [jax.ShapeDtypeStruct((t, MIX), BF16)],
        dims=("parallel",))(h, o, za, pool, p, wo, wpg, wpe, gpost)


def _loss_grad(y, target, *, tm=512):
    t = y.shape[0]

    def body(y_ref, t_ref, dy_ref, loss_ref):
        @pl.when(pl.program_id(0) == 0)
        def _():
            loss_ref[...] = jnp.zeros_like(loss_ref)

        err = y_ref[...] - t_ref[...]
        dy_ref[...] = err * (1.0 / D_MODEL)
        part = jnp.sum(jnp.sum(err * err, axis=1, keepdims=True), axis=0, keepdims=True)
        loss_ref[...] += part * (0.5 / D_MODEL)

    blk = pl.BlockSpec((tm, D_MODEL), lambda i: (i, 0))
    return _pcall(
        body, name="loss_grad", grid=(t // tm,), in_specs=[blk, blk],
        out_specs=[blk, pl.BlockSpec((8, LANES), lambda i: (0, 0))],
        out_shape=[jax.ShapeDtypeStruct((t, D_MODEL), F32), jax.ShapeDtypeStruct((8, LANES), F32)],
        dims=("arbitrary",))(y, target)


def _out_block_bwd(dh2, gate, pe, mix, o, za, wpg, wo, gpost, gsum, *, tm=256):
    t = dh2.shape[0]

    def body(dh2_ref, gate_ref, pe_ref, mix_ref, o_ref, za_ref, wpg_ref, wo_ref, g_ref, gs_ref,
             dh1_ref, dgp_ref, dpe_ref, dmix_ref, do_ref, dza_ref, dpool_ref, delta_ref, dg_ref):
        @pl.when(pl.program_id(0) == 0)
        def _():
            dg_ref[...] = jnp.zeros_like(dg_ref)

        dh2 = dh2_ref[...]
        gate = gate_ref[...]
        dpe_ref[...] = (dh2 * gate).astype(BF16)
        dgp = (dh2 * pe_ref[...] * gate * (1.0 - gate)).astype(BF16)
        dgp_ref[...] = dgp
        dh1 = dh2 + _dot(dgp, wpg_ref[...], NT)
        dh1_ref[...] = dh1
        mix = mix_ref[...]
        r = lax.rsqrt(jnp.mean(mix * mix, axis=-1, keepdims=True) + EPS)
        dg_ref[...] += jnp.sum(dh1 * mix * r, axis=0, keepdims=True)
        a = dh1 * g_ref[...]
        dmix = (r * a - mix * (r * r * r) * jnp.mean(a * mix, axis=-1, keepdims=True)).astype(BF16)
        dmix_ref[...] = dmix
        dattn = _dot(dmix, wo_ref[:D_MODEL, :], NT)
        dpool_ref[...] = _dot(dmix, wo_ref[D_MODEL:, :], NT)
        z = za_ref[...]
        sg = _sigmoid(z)
        o = o_ref[...]
        do = (dattn * (z * sg)).astype(BF16)
        do_ref[...] = do
        dza_ref[...] = (dattn * o * (sg * (1.0 + z * (1.0 - sg)))).astype(BF16)
        prod = do.astype(F32) * o
        hi = prod.astype(BF16)
        rest = (prod - hi.astype(F32)).astype(BF16)
        delta_ref[...] = _dot(hi, gs_ref[...]) + _dot(rest, gs_ref[...])

    row = lambda i: (i, 0)
    const = lambda i: (0, 0)
    blk = pl.BlockSpec((tm, D_MODEL), row)
    sds = lambda dt: jax.ShapeDtypeStruct((t, D_MODEL), dt)
    return _pcall(
        body, name="out_block_bwd", grid=(t // tm,),
        in_specs=[blk] * 6 + [pl.BlockSpec((D_MODEL, D_MODEL), const), pl.BlockSpec((MIX, D_MODEL), const),
                              pl.BlockSpec((1, D_MODEL), const), pl.BlockSpec((D_MODEL, LANES), const)],
        out_specs=[blk] * 7 + [pl.BlockSpec((tm, LANES), row), pl.BlockSpec((1, D_MODEL), const)],
        out_shape=[sds(F32)] + [sds(BF16)] * 5 + [sds(F32), jax.ShapeDtypeStruct((t, LANES), F32),
                                                 jax.ShapeDtypeStruct((1, D_MODEL), F32)],
        dims=("arbitrary",))(dh2, gate, pe, mix, o, za, wpg, wo, gpost, gsum)


def _in_proj_bwd(dsegs, dfl, wm, wf, h, dh1, gpre, *, tm=512):
    t = h.shape[0]

    def body(*refs):
        seg_refs = refs[:N_SEG]
        dfl_ref, wm_ref, wf_ref, h_ref, dh1_ref, g_ref, dh_ref, dg_ref, acc_ref = refs[N_SEG:]
        i = pl.program_id(0)
        j = pl.program_id(1)

        @pl.when(jnp.logical_and(i == 0, j == 0))
        def _():
            dg_ref[...] = jnp.zeros_like(dg_ref)

        @pl.when(j == 0)
        def _():
            acc_ref[...] = _dot(dfl_ref[...].astype(BF16), wf_ref[...], NT)

        for sgi, ref in enumerate(seg_refs):
            @pl.when(j == sgi)
            def _(ref=ref):
                acc_ref[...] += _dot(ref[...].astype(BF16), wm_ref[...], NT)

        @pl.when(j == N_SEG - 1)
        def _():
            dhn = acc_ref[...]
            x = h_ref[...]
            r = lax.rsqrt(jnp.mean(x * x, axis=-1, keepdims=True) + EPS)
            dg_ref[...] += jnp.sum(dhn * x * r, axis=0, keepdims=True)
            a = dhn * g_ref[...]
            dh_ref[...] = dh1_ref[...] + r * a - x * (r * r * r) * jnp.mean(a * x, axis=-1, keepdims=True)

    row = lambda i, j: (i, 0)
    const = lambda i, j: (0, 0)
    blk = pl.BlockSpec((tm, D_MODEL), row)
    return _pcall(
        body, name="in_proj_bwd", grid=(t // tm, N_SEG),
        in_specs=[blk] * N_SEG + [pl.BlockSpec((tm, LANES), row), pl.BlockSpec((D_MODEL, D_MODEL), lambda i, j: (0, j)),
                                  pl.BlockSpec((D_MODEL, LANES), const), blk, blk, pl.BlockSpec((1, D_MODEL), const)],
        out_specs=[blk, pl.BlockSpec((1, D_MODEL), const)],
        out_shape=[jax.ShapeDtypeStruct((t, D_MODEL), F32), jax.ShapeDtypeStruct((1, D_MODEL), F32)],
        scratch_shapes=[pltpu.VMEM((tm, D_MODEL), F32)],
        dims=("arbitrary", "arbitrary"))(*dsegs, dfl, wm, wf, h, dh1, gpre)


def _chip_peers(x, y):
    return [(1 - x, y), (x, 1 - y), (1 - x, 1 - y)]


def _ici_views(src, land, gather, x, y, c, pi, px, py):
    if gather:
        return src.at[c], land.at[pi], land.at[pi]
    return src.at[2 * px + py], land.at[2 * x + y], land.at[2 * px + py]


def _ici_start(srcs, lands, after, *, gather, name):
    n = len(srcs)

    def body(*refs):
        src, land = refs[:n], refs[n:2 * n]
        send, recv = refs[2 * n + 1], refs[2 * n + 2]
        token = refs[-1]
        x, y, c = lax.axis_index("x"), lax.axis_index("y"), lax.axis_index("c")
        for kk in range(n):
            for pi, (px, py) in enumerate(_chip_peers(x, y)):
                mine, there, _ = _ici_views(src[kk], land[kk], gather, x, y, c, pi, px, py)
                pltpu.make_async_remote_copy(src_ref=mine, dst_ref=there, send_sem=send.at[3 * kk + pi],
                                             recv_sem=recv.at[3 * kk + pi], device_id=(px, py, c),
                                             device_id_type=MESH).start()
        token[...] = jnp.zeros_like(token)

    hbm = lambda a: pltpu.with_memory_space_constraint(a, pltpu.HBM)
    outs = pl.pallas_call(
        body, name=name,
        out_shape=(pltpu.SemaphoreType.DMA((3 * n,)), pltpu.SemaphoreType.DMA((3 * n,)),
                   *[pltpu.HBM(a.shape, a.dtype) for a in srcs], *[pltpu.HBM(a.shape, a.dtype) for a in lands],
                   jax.ShapeDtypeStruct((8, LANES), F32)),
        in_specs=[HBM_SPEC] * (2 * n) + [ANY],
        out_specs=(SEM_SPEC, SEM_SPEC, *[HBM_SPEC] * (2 * n), pl.BlockSpec(memory_space=pltpu.VMEM)),
        input_output_aliases={i: 2 + i for i in range(2 * n)},
        compiler_params=pltpu.CompilerParams(has_side_effects=SIDE_EFFECT),
    )(*[hbm(a) for a in srcs], *[hbm(a) for a in lands], after)
    return outs[0], outs[1], list(outs[2:2 + n]), list(outs[2 + n:2 + 2 * n]), outs[-1]


def _ici_wait(send, recv, srcs, lands, after, *, gather, name):
    n = len(srcs)

    def body(*refs):
        src, land = refs[:n], refs[n:2 * n]
        send_ref, recv_ref = refs[2 * n], refs[2 * n + 1]
        x, y, c = lax.axis_index("x"), lax.axis_index("y"), lax.axis_index("c")
        for kk in range(n):
            for pi, (px, py) in enumerate(_chip_peers(x, y)):
                mine, _, here = _ici_views(src[kk], land[kk], gather, x, y, c, pi, px, py)
                cp = pltpu.make_async_remote_copy(src_ref=mine, dst_ref=here, send_sem=send_ref.at[3 * kk + pi],
                                                  recv_sem=recv_ref.at[3 * kk + pi], device_id=(px, py, c),
                                                  device_id_type=MESH)
                cp.wait_send()
                cp.wait_recv()

    outs = pl.pallas_call(
        body, name=name,
        out_shape=[pltpu.HBM(a.shape, a.dtype) for a in srcs] + [pltpu.HBM(a.shape, a.dtype) for a in lands],
        in_specs=[HBM_SPEC] * (2 * n) + [SEM_SPEC, SEM_SPEC, ANY], out_specs=[HBM_SPEC] * (2 * n),
        input_output_aliases={i: i for i in range(2 * n)},
        compiler_params=pltpu.CompilerParams(has_side_effects=SIDE_EFFECT),
    )(*srcs, *lands, send, recv, after)
    return list(outs[:n]), list(outs[n:])


def _sibling_send(arrays, *, name):
    n = len(arrays)

    def body(*refs):
        ins, outs = refs[:n], refs[n:2 * n]
        send, recv = refs[2 * n:]
        sib = (lax.axis_index("x"), lax.axis_index("y"), 1 - lax.axis_index("c"))
        copies = [pltpu.make_async_remote_copy(src_ref=ins[kk], dst_ref=outs[kk], send_sem=send.at[kk],
                                               recv_sem=recv.at[kk], device_id=sib, device_id_type=MESH)
                  for kk in range(n)]
        for cp in copies:
            cp.start()
        for cp in copies:
            cp.wait_recv()
        for cp in copies:
            cp.wait_send()

    return _pcall(
        body, name=name, in_specs=[ANY] * n, out_specs=[ANY] * n,
        out_shape=[jax.ShapeDtypeStruct(a.shape, a.dtype) for a in arrays],
        scratch_shapes=[pltpu.SemaphoreType.DMA((n,)), pltpu.SemaphoreType.DMA((n,))])(*arrays)


def _all_gather_small(a, *, name):
    def body(a_ref, o_ref, send, recv, loc):
        x, y, c = lax.axis_index("x"), lax.axis_index("y"), lax.axis_index("c")
        me = 4 * x + 2 * y + c

        def peer(rel):
            fx, fy, fc = (rel >> 2) & 1, (rel >> 1) & 1, rel & 1
            px = (1 - x) if fx else x
            py = (1 - y) if fy else y
            pc = (1 - c) if fc else c
            return px, py, pc

        def remote(rel, slot):
            return pltpu.make_async_remote_copy(
                src_ref=a_ref, dst_ref=o_ref.at[slot], send_sem=send.at[rel - 1], recv_sem=recv.at[rel - 1],
                device_id=peer(rel), device_id_type=MESH)

        mine = pltpu.make_async_copy(a_ref, o_ref.at[me], loc)
        mine.start()
        sends = [remote(rel, me) for rel in range(1, 8)]
        for cp in sends:
            cp.start()
        for rel in range(1, 8):
            px, py, pc = peer(rel)
            remote(rel, 4 * px + 2 * py + pc).wait_recv()
        for cp in sends:
            cp.wait_send()
        mine.wait()

    return _pcall(
        body, name=name, in_specs=[ANY], out_specs=ANY,
        out_shape=jax.ShapeDtypeStruct((8,) + a.shape, a.dtype),
        scratch_shapes=[pltpu.SemaphoreType.DMA((7,)), pltpu.SemaphoreType.DMA((7,)), pltpu.SemaphoreType.DMA])(a)


def _to_bf16(a, *, name, tr=256):
    rows, cols = a.shape

    def body(a_ref, o_ref):
        o_ref[...] = a_ref[...].astype(BF16)

    blk = pl.BlockSpec((tr, cols), lambda i: (i, 0))
    return _pcall(body, name=name, grid=(rows // tr,), in_specs=[blk], out_specs=blk,
                  out_shape=jax.ShapeDtypeStruct(a.shape, BF16), dims=("parallel",))(a)


def _pair_sum(a, b, *, name, tr=256):
    _, rows, cols = a.shape
    tr = min(tr, rows)

    def body(a_ref, b_ref, o_ref):
        o_ref[...] = (a_ref[...].astype(F32) + b_ref[...].astype(F32)).astype(BF16)

    blk = pl.BlockSpec((None, tr, cols), lambda j, i: (j, i, 0))
    return _pcall(body, name=name, grid=(4, rows // tr), in_specs=[blk, blk], out_specs=blk,
                  out_shape=jax.ShapeDtypeStruct(a.shape, BF16), dims=("parallel", "parallel"))(a, b)


def _chip_sum(own, recv, chip, *, name, tr=256):
    _, rows, cols = own.shape
    tr = min(tr, rows)

    def body(chip_ref, own_ref, r1_ref, r2_ref, r3_ref, o_ref):
        acc = own_ref[...].astype(F32)
        for ref in (r1_ref, r2_ref, r3_ref):
            acc = acc + ref[...].astype(F32)
        o_ref[...] = acc

    def slot(step):
        return pl.BlockSpec((None, tr, cols), lambda i, chip_ref: ((chip_ref[0] + step) % 4, i, 0))

    return pl.pallas_call(
        body, name=name, out_shape=jax.ShapeDtypeStruct((rows, cols), F32),
        grid_spec=pltpu.PrefetchScalarGridSpec(
            num_scalar_prefetch=1, grid=(rows // tr,), in_specs=[slot(0), slot(1), slot(2), slot(3)],
            out_specs=pl.BlockSpec((tr, cols), lambda i, chip_ref: (i, 0))),
        compiler_params=pltpu.CompilerParams(dimension_semantics=("parallel",), vmem_limit_bytes=VMEM_LIMIT),
    )(chip, own, recv, recv, recv)


def _adamw_math(w, g, m, v):
    m = ADAM_B1 * m + (1.0 - ADAM_B1) * g
    v = ADAM_B2 * v + (1.0 - ADAM_B2) * (g * g)
    m_hat = m / (1.0 - ADAM_B1 ** ADAM_STEP)
    v_hat = v / (1.0 - ADAM_B2 ** ADAM_STEP)
    delta = -ADAM_LR * (m_hat / (jnp.sqrt(v_hat) + ADAM_EPS) + ADAM_WD * w)
    return delta, m, v


def _adamw(w, m, v, g, layer, prev, *, name, tr=128):
    rows, cols = g.shape
    tr = min(tr, rows)
    nblk = rows // tr

    def body(w_ref, m_ref, v_ref, gin_ref, *refs):
        g_ref, d_ref, mo_ref, vo_ref = refs[-4:]
        grad = gin_ref[...]
        delta, m_new, v_new = _adamw_math(w_ref[...], grad, m_ref[...], v_ref[...])
        g_ref[...] = grad
        d_ref[...] = delta
        mo_ref[...] = m_new
        vo_ref[...] = v_new

    blk = pl.BlockSpec((tr, cols), lambda i: (layer * nblk + i, 0))
    carried = [] if prev is None else list(prev)
    return pl.pallas_call(
        body, name=name, grid=(nblk,), out_shape=[jax.ShapeDtypeStruct(w.shape, F32)] * 4,
        in_specs=[blk] * 3 + [pl.BlockSpec((tr, cols), lambda i: (i, 0))] + [ANY] * len(carried), out_specs=[blk] * 4,
        input_output_aliases={4 + i: i for i in range(len(carried))},
        compiler_params=pltpu.CompilerParams(dimension_semantics=("parallel",), vmem_limit_bytes=VMEM_LIMIT),
    )(w, m, v, g, *carried)


def _adamw_small(w, m, v, parts, *, name):
    def body(w_ref, m_ref, v_ref, p_ref, g_ref, d_ref, mo_ref, vo_ref):
        g = p_ref[0]
        for dev in range(1, 8):
            g = g + p_ref[dev]
        delta, m_new, v_new = _adamw_math(w_ref[...], g, m_ref[...], v_ref[...])
        g_ref[...] = g
        d_ref[...] = delta
        mo_ref[...] = m_new
        vo_ref[...] = v_new

    return _pcall(body, name=name, out_shape=[jax.ShapeDtypeStruct(w.shape, F32)] * 4)(w, m, v, parts)


def _column_parts(segments, lo, hi):
    parts, start = [], 0
    for seg in segments:
        a, b = max(lo, start), min(hi, start + seg.shape[1])
        if a < b:
            parts.append(seg[:, a - start:b - start])
        start += seg.shape[1]
    return parts


def _rows_from_heads(a, nb, s):
    return a.reshape(nb, s, N_HEADS).transpose(0, 2, 1).reshape(nb, N_PAIRS, 2, s)


def _layer_fwd(h, p_l, wts, nb):
    t = h.shape[0]
    s = t // nb
    q, k, v, za, u, zp, fl, hn = _norm_in_proj(h, wts["gpre"], wts["wm"], wts["wf"])
    flt = fl[:, :N_HEADS].reshape(nb, s, N_HEADS).transpose(0, 2, 1)
    c = _gate_cumsum(flt, wts["bf"])
    o, lse = _attn_fwd(q, k, v, c.reshape(nb, N_PAIRS, 2, s), nb=nb)
    pool, pooled, mixed = _pool_fwd(u, zp, wts["wpool"], wts["ps"], nb=nb)
    h2, mix, h1, gate, pe, cat = _out_block(h, o, za, pool, p_l, wts["wo"], wts["wpg"], wts["wpe"], wts["gpost"])
    saved = dict(h=h, hn=hn, q=q, k=k, v=v, za=za, zp=zp, flt=flt, c=c, o=o, lse=lse, pooled=pooled, mixed=mixed,
                 mix=mix, h1=h1, gate=gate, pe=pe, cat=cat, p=p_l)
    return h2, saved


def _layer_bwd(dh2, sv, wts, gsum, nb):
    t = dh2.shape[0]
    s = t // nb
    dh1, dgp, dpe, dmix, do, dza, dpool, delta, dgpost = _out_block_bwd(
        dh2, sv["gate"], sv["pe"], sv["mix"], sv["o"], sv["za"], wts["wpg"], wts["wo"], wts["gpost"], gsum)
    g_wpe = _matmul(sv["p"], dpe, ta=True, name="wgrad_pe")
    g_wpg = _matmul(sv["h1"], dgp, ta=True, name="wgrad_pg")
    g_wo = _matmul(sv["cat"], dmix, ta=True, name="wgrad_out")

    delta4 = _rows_from_heads(delta[:, :N_HEADS], nb, s)
    dq, dk, dv, dcol, drow = _attn_bwd(sv["q"], sv["k"], sv["v"], do, sv["c"].reshape(nb, N_PAIRS, 2, s), sv["lse"],
                                       delta4, nb=nb)
    dflt, dbf = _gate_cumsum_bwd(dcol.reshape(nb, N_HEADS, s), drow.reshape(nb, N_HEADS, s), sv["flt"], wts["bf"])
    dfl = jnp.pad(dflt.transpose(0, 2, 1).reshape(t, N_HEADS), ((0, 0), (0, LANES - N_HEADS)))

    dmixed, du, dzp, dps = _pool_bwd(dpool, sv["mixed"], sv["zp"], wts["wpool"], wts["ps"], nb=nb)
    g_wpool = _pool_wgrad(sv["pooled"], dmixed)

    dsegs = (dq, dk, dv, dza, du, dzp)
    dh, dgpre = _in_proj_bwd(dsegs, dfl, wts["wm"], wts["wf"], sv["h"], dh1, wts["gpre"])
    g_segs = [_matmul(sv["hn"], dx, ta=True, name="wgrad_in") for dx in dsegs]
    g_wf = _matmul(sv["hn"], dfl, ta=True, name="wgrad_in_f")
    g_win = g_segs[:4] + [g_wf[:, :N_HEADS]] + g_segs[4:]
    grads = dict(w_in=g_win, w_out=g_wo, w_pg=g_wpg, w_pe=g_wpe, w_pool=g_wpool,
                 norm_pre=dgpre[0], norm_post=dgpost[0], pool_scale=dps[0], b_f=dbf[:, 0])
    return dh, grads


def kernel(x, p, norm_pre, norm_post, w_in, b_f, w_pool, pool_scale, w_out, w_pg, w_pe, loss_target, m_norm_pre, m_norm_post, m_w_in, m_b_f, m_w_pool, m_pool_scale, m_w_out, m_w_pg, m_w_pe, v_norm_pre, v_norm_post, v_w_in, v_b_f, v_w_pool, v_pool_scale, v_w_out, v_w_pg, v_w_pe):
    nb, s, _ = x.shape
    t = nb * s
    depth = w_in.shape[0]
    big = dict(w_in=(w_in, m_w_in, v_w_in), w_out=(w_out, m_w_out, v_w_out), w_pg=(w_pg, m_w_pg, v_w_pg),
               w_pe=(w_pe, m_w_pe, v_w_pe), w_pool=(w_pool, m_w_pool, v_w_pool))
    names = list(big)
    cols = {n: big[n][0].shape[-1] for n in names}
    chip = (2 * lax.axis_index("x") + lax.axis_index("y")).astype(jnp.int32).reshape(1)
    core = lax.axis_index("c")
    south = core == 0

    gathers = []
    token = jnp.zeros((8, LANES), F32)
    narrow = {n: big[n][0].astype(BF16) for n in names if n != "w_in"}
    narrow["w_in"] = _to_bf16(w_in.reshape(-1, cols["w_in"]), name="w_in_to_bf16").reshape(w_in.shape)
    for l in range(depth):
        halves = [narrow[n][l].reshape(2, -1, cols[n]) for n in names]
        lands = [lax.empty((3,) + a.shape[1:], a.dtype) for a in halves]
        send, recv, halves, lands, token = _ici_start(halves, lands, token, gather=True, name=f"gather_start_{l}")
        gathers.append((send, recv, halves, lands))

    def layer_weights(l, after, anchor):
        send, recv, halves, lands = gathers[l]
        halves, lands = _ici_wait(send, recv, halves, lands, after, gather=True, name=f"gather_wait_{l}")
        others = _sibling_send(lands, name="gather_pass_on")
        full = {}
        for n, own, land, other in zip(names, halves, lands, others):
            low = jnp.where(south, land, other)
            high = jnp.where(south, other, land)
            rel = [jnp.concatenate([low[pi], high[pi]], axis=0) for pi in range(3)]
            by_rel = jnp.stack([own.reshape(-1, cols[n]), rel[1], rel[0], rel[2]])
            full[n] = [lax.dynamic_index_in_dim(by_rel, j ^ chip[0], 0, keepdims=False) for j in range(4)]
        win = full["w_in"]
        wm = jnp.concatenate(_column_parts(win, 0, F_OFF) + _column_parts(win, F_OFF + N_HEADS, IN_COLS), axis=1)
        wf = jnp.pad(jnp.concatenate(_column_parts(win, F_OFF, F_OFF + N_HEADS), axis=1), ((0, 0), (0, LANES - N_HEADS)))
        return dict(
            wm=wm, wf=wf, wo=jnp.concatenate(full["w_out"], axis=0), wpg=jnp.concatenate(full["w_pg"], axis=0),
            wpe=jnp.concatenate(full["w_pe"], axis=1),
            wpool=jnp.stack(full["w_pool"]).reshape(4, N_GROUPS, GROUP_DIM // 4, GROUP_DIM).transpose(1, 0, 2, 3).reshape(
                N_GROUPS, GROUP_DIM, GROUP_DIM),
            gpre=norm_pre[l][None] + anchor, gpost=norm_post[l][None], ps=pool_scale[l][None], bf=b_f[l][:, None])

    h = x.reshape(t, D_MODEL)
    saved, layers = [], []
    after = token
    for l in range(depth):
        layers.append(layer_weights(l, after, token[0, 0]))
        h, sv = _layer_fwd(h, p[l].reshape(t, PLE_DIM), layers[l], nb)
        saved.append(sv)
        after = h
    dh, loss_blk = _loss_grad(h, loss_target.reshape(t, D_MODEL))
    loss = lax.psum(loss_blk[0, 0], ("x", "y", "c"))
    gsum = (jnp.arange(D_MODEL)[:, None] // HEAD_DIM == jnp.arange(LANES)[None, :]).astype(BF16)

    def pieces(g, name):
        if name == "w_in":
            by_chip = jnp.stack([jnp.concatenate(_column_parts(g, j * cols[name], (j + 1) * cols[name]), axis=1)
                                 for j in range(4)])
        elif name == "w_pe":
            by_chip = jnp.stack([g[:, j * cols[name]:(j + 1) * cols[name]] for j in range(4)])
        elif name == "w_pool":
            by_chip = g.reshape(N_GROUPS, 4, GROUP_DIM // 4, GROUP_DIM).transpose(1, 0, 2, 3)
        else:
            by_chip = g
        return by_chip.reshape(4, 2, -1, cols[name]).transpose(1, 0, 2, 3).astype(BF16)

    def scatter_start(l, g):
        mine, sent = [], []
        for n in names:
            halves = pieces(g[n], n)
            mine.append(lax.dynamic_index_in_dim(halves, core, 0, keepdims=False))
            sent.append(lax.dynamic_index_in_dim(halves, 1 - core, 0, keepdims=False))
        theirs = _sibling_send(sent, name="presum_exchange")
        sums = [_pair_sum(a, b, name="presum_" + n) for n, a, b in zip(names, mine, theirs)]
        lands = [lax.empty(a.shape, a.dtype) for a in sums]
        return _ici_start(sums, lands, token, gather=False, name=f"scatter_start_{l}")

    def scatter_finish(l, state, after):
        send, recv, sums, lands, _ = state
        sums, lands = _ici_wait(send, recv, sums, lands, after, gather=False, name=f"scatter_wait_{l}")
        mine = [_chip_sum(a, r, chip, name="chip_sum_" + n) for n, a, r in zip(names, sums, lands)]
        theirs = _sibling_send(mine, name="halves_exchange")
        return {n: jnp.concatenate([jnp.where(south, a, b), jnp.where(south, b, a)], axis=0)
                for n, a, b in zip(names, mine, theirs)}

    grads = [None] * depth
    big_out = {n: None for n in names}

    def update(l, state, after):
        reduced = scatter_finish(l, state, after)
        for n in names:
            w, m, v = big[n]
            big_out[n] = _adamw(w.reshape(-1, cols[n]), m.reshape(-1, cols[n]), v.reshape(-1, cols[n]), reduced[n], l,
                                big_out[n], name="adamw_" + n)

    pending = None
    for l in reversed(range(depth)):
        wts = layers[l]
        if pending is not None:
            wts = dict(wts, gpost=wts["gpost"] + pending[1][4][0, 0])
        dh, grads[l] = _layer_bwd(dh, saved[l], wts, gsum, nb)
        state = scatter_start(l, grads[l])
        if pending is not None:
            update(pending[0], pending[1], state[4])
        pending = (l, state)
    grad_x = dh.reshape(nb, s, D_MODEL)

    small = dict(norm_pre=(norm_pre, m_norm_pre, v_norm_pre), norm_post=(norm_post, m_norm_post, v_norm_post),
                 pool_scale=(pool_scale, m_pool_scale, v_pool_scale), b_f=(b_f, m_b_f, v_b_f))

    def pack(get):
        rows = []
        for n in small:
            a = get(n)
            rows.append(jnp.pad(a, ((0, 0), (0, D_MODEL - a.shape[1]))))
        return jnp.concatenate(rows, axis=0)

    parts = _all_gather_small(pack(lambda n: jnp.stack([grads[l][n] for l in range(depth)])), name="gather_small")
    small_packed = _adamw_small(pack(lambda n: small[n][0]), pack(lambda n: small[n][1]), pack(lambda n: small[n][2]),
                                parts, name="adamw_small")
    small_out = {}
    for i, n in enumerate(small):
        width = small[n][0].shape[1]
        small_out[n] = [o[depth * i:depth * (i + 1), :width] for o in small_packed]

    update(pending[0], pending[1], big_out[names[-1]][0] if depth > 1 else small_packed[0])
    big_out = {n: [o.reshape(big[n][0].shape) for o in outs] for n, outs in big_out.items()}

    order =["norm_pre", "norm_post", "w_in", "b_f", "w_pool", "pool_scale", "w_out", "w_pg", "w_pe"]
    result = [loss, grad_x]
    for kind in range(4):
        for n in order:
            result.append(big_out[n][kind] if n in big_out else small_out[n][kind])
    return tuple(result)
```

```python
import functools

import jax
import jax.numpy as jnp
from jax import lax
from jax.experimental import pallas as pl
from jax.experimental.pallas import tpu as pltpu

F32 = jnp.float32
BF16 = jnp.bfloat16

D_MODEL = 1024
N_HEADS = 16
HEAD_DIM = 64
PAIR = 2 * HEAD_DIM
N_PAIRS = N_HEADS // 2
N_GROUPS = 4
GROUP_DIM = 256
PLE_DIM = 256
MIX = 2 * D_MODEL
N_SEG = 6
F_OFF = 4 * D_MODEL
IN_COLS = N_SEG * D_MODEL + N_HEADS
LANES = 128
EPS = 1e-6
SCALE = 0.125
NEG = -1e30

ADAM_LR = 0.001
ADAM_B1 = 0.9
ADAM_B2 = 0.999
ADAM_EPS = 1e-08
ADAM_WD = 0.01
ADAM_STEP = 10

VMEM_LIMIT = 56 * 1024 * 1024
MESH = pl.DeviceIdType.MESH
ANY = pl.BlockSpec(memory_space=pl.ANY)
HBM_SPEC = pl.BlockSpec(memory_space=pltpu.HBM)
SEM_SPEC = pl.BlockSpec(memory_space=pltpu.SEMAPHORE)
SIDE_EFFECT = pltpu.SideEffectType.DATAFLOW_SIDE_EFFECTING

NT = (((1,), (1,)), ((), ()))
TN = (((0,), (0,)), ((), ()))
NN = (((1,), (0,)), ((), ()))


def _pcall(body, *, name, out_shape, grid=(), in_specs=None, out_specs=None, scratch_shapes=(), dims=None):
    kw = {}
    if in_specs is not None:
        kw["in_specs"] = in_specs
    if out_specs is not None:
        kw["out_specs"] = out_specs
    return pl.pallas_call(
        body, name=name, out_shape=out_shape, grid=grid, scratch_shapes=list(scratch_shapes),
        compiler_params=pltpu.CompilerParams(dimension_semantics=dims, vmem_limit_bytes=VMEM_LIMIT), **kw)


def _dot(a, b, dn=NN):
    return lax.dot_general(a, b, dn, preferred_element_type=F32)


def _sigmoid(x):
    return 1.0 / (1.0 + jnp.exp(-x))


def _matmul(a, b, *, ta=False, name, tm=512, tn=1024, tk=2048):
    if ta:
        kdim, m = a.shape
    else:
        m, kdim = a.shape
    n = b.shape[1]
    tm, tn, tk = min(tm, m), min(tn, n), min(tk, kdim)
    nk = kdim // tk

    def body(a_ref, b_ref, o_ref, acc_ref):
        k = pl.program_id(2)

        @pl.when(k == 0)
        def _():
            acc_ref[...] = jnp.zeros_like(acc_ref)

        acc_ref[...] += _dot(a_ref[...].astype(BF16), b_ref[...].astype(BF16), TN if ta else NN)

        @pl.when(k == nk - 1)
        def _():
            o_ref[...] = acc_ref[...].astype(BF16)

    a_spec = pl.BlockSpec((tk, tm), lambda i, j, k: (k, i)) if ta else pl.BlockSpec((tm, tk), lambda i, j, k: (i, k))
    return _pcall(
        body, name=name, out_shape=jax.ShapeDtypeStruct((m, n), BF16), grid=(m // tm, n // tn, nk),
        in_specs=[a_spec, pl.BlockSpec((tk, tn), lambda i, j, k: (k, j))],
        out_specs=pl.BlockSpec((tm, tn), lambda i, j, k: (i, j)),
        scratch_shapes=[pltpu.VMEM((tm, tn), F32)], dims=("parallel", "parallel", "arbitrary"))(a, b)


def _norm_in_proj(h, g, wm, wf, *, tm=512):
    t = h.shape[0]

    def body(h_ref, g_ref, wm_ref, wf_ref, q_ref, k_ref, v_ref, za_ref, u_ref, zp_ref, fl_ref, hn_ref):
        j = pl.program_id(1)

        @pl.when(j == 0)
        def _():
            x = h_ref[...]
            r = lax.rsqrt(jnp.mean(x * x, axis=-1, keepdims=True) + EPS)
            hn = (x * r * g_ref[...]).astype(BF16)
            hn_ref[...] = hn
            fl_ref[...] = _dot(hn, wf_ref[...])

        y = _dot(hn_ref[...], wm_ref[...])
        for s, (ref_a, ref_b) in enumerate(((q_ref, k_ref), (v_ref, za_ref), (u_ref, zp_ref))):
            @pl.when(j == s)
            def _(ref_a=ref_a, ref_b=ref_b):
                ref_a[...] = y[:, :D_MODEL].astype(ref_a.dtype)
                ref_b[...] = y[:, D_MODEL:].astype(ref_b.dtype)

    row = lambda i, j: (i, 0)
    blk = pl.BlockSpec((tm, D_MODEL), row)
    sds = lambda dt: jax.ShapeDtypeStruct((t, D_MODEL), dt)
    return _pcall(
        body, name="norm_in_proj", grid=(t // tm, N_SEG // 2),
        in_specs=[blk, pl.BlockSpec((1, D_MODEL), lambda i, j: (0, 0)),
                  pl.BlockSpec((D_MODEL, 2 * D_MODEL), lambda i, j: (0, j)),
                  pl.BlockSpec((D_MODEL, LANES), lambda i, j: (0, 0))],
        out_specs=[blk] * 6 + [pl.BlockSpec((tm, LANES), row), blk],
        out_shape=[sds(BF16)] * 3 + [sds(F32)] * 3 + [jax.ShapeDtypeStruct((t, LANES), F32), sds(BF16)],
        dims=("parallel", "arbitrary"))(h, g, wm, wf)


def _gate_cumsum(flt, bias):
    b, hh, s = flt.shape

    def body(fl_ref, b_ref, c_ref):
        x = fl_ref[...] + b_ref[...]
        acc = jnp.minimum(x, 0.0) - jnp.log(1.0 + jnp.exp(-jnp.abs(x)))
        idx = lax.broadcasted_iota(jnp.int32, x.shape, 1)
        sh = 1
        while sh < s:
            acc = acc + jnp.where(idx >= sh, pltpu.roll(acc, sh, 1), 0.0)
            sh *= 2
        c_ref[...] = acc

    return _pcall(
        body, name="gate_cumsum", grid=(b,),
        in_specs=[pl.BlockSpec((None, hh, s), lambda i: (i, 0, 0)), pl.BlockSpec((hh, 1), lambda i: (0, 0))],
        out_specs=pl.BlockSpec((None, hh, s), lambda i: (i, 0, 0)),
        out_shape=jax.ShapeDtypeStruct((b, hh, s), F32), dims=("parallel",))(flt, bias)


def _gate_cumsum_bwd(dcol, drow, flt, bias):
    b, hh, s = flt.shape

    def body(d_ref, dr_ref, fl_ref, b_ref, dfl_ref, db_ref):
        @pl.when(pl.program_id(0) == 0)
        def _():
            db_ref[...] = jnp.zeros_like(db_ref)

        x = fl_ref[...] + b_ref[...]
        acc = dr_ref[...] - d_ref[...]
        idx = lax.broadcasted_iota(jnp.int32, x.shape, 1)
        sh = 1
        while sh < s:
            acc = acc + jnp.where(idx + sh < s, pltpu.roll(acc, s - sh, 1), 0.0)
            sh *= 2
        e = jnp.exp(-jnp.abs(x))
        sig_neg = jnp.where(x >= 0.0, e, 1.0) / (1.0 + e)
        dfl = acc * sig_neg
        dfl_ref[...] = dfl
        db_ref[...] += jnp.sum(dfl, axis=1, keepdims=True)

    return _pcall(
        body, name="gate_cumsum_bwd", grid=(b,),
        in_specs=[pl.BlockSpec((None, hh, s), lambda i: (i, 0, 0))] * 3 + [pl.BlockSpec((hh, 1), lambda i: (0, 0))],
        out_specs=[pl.BlockSpec((None, hh, s), lambda i: (i, 0, 0)), pl.BlockSpec((hh, 1), lambda i: (0, 0))],
        out_shape=[jax.ShapeDtypeStruct((b, hh, s), F32), jax.ShapeDtypeStruct((hh, 1), F32)],
        dims=("arbitrary",))(dcol, drow, flt, bias)


def _attn_fwd(q, k, v, c4, *, nb, tq=512, rc=32, diag=256):
    t = q.shape[0]
    s = t // nb
    tq = min(tq, s)
    nq = s // tq
    tk = tq
    diag = min(diag, tq)

    def body(q_ref, k_ref, v_ref, c_ref, o_ref, lse_ref, qh_scr, s_scr, p_scr, m_scr, acc_scr):
        qi = pl.program_id(2)
        lo = lax.broadcasted_iota(jnp.int32, (tq, PAIR), 1) < HEAD_DIM
        q2 = q_ref[...] * SCALE
        zero = jnp.zeros_like(q2)
        qh_scr[0] = jnp.where(lo, q2, zero)
        qh_scr[1] = jnp.where(lo, zero, q2)
        m_scr[...] = jnp.full(m_scr.shape, NEG, F32)
        acc_scr[...] = jnp.zeros(acc_scr.shape, F32)
        whole = [(0, tq, tk, False)]
        diagonal = [(r0, r0 + diag, r0 + diag, True) for r0 in range(0, tq, diag)]

        def block(kj, bands):
            off = pl.multiple_of(kj * tk, tk)
            k2 = k_ref[pl.ds(off, tk), :]
            v2 = v_ref[pl.ds(off, tk), :]
            one = jnp.ones_like(v2)
            va = (jnp.where(lo, v2, one), jnp.where(lo, one, v2))
            crow = c_ref[:, pl.ds(off, tk)]
            for hd in range(2):
                for r0, r1, nc, _ in bands:
                    s_scr[hd, r0:r1, :nc] = _dot(qh_scr[hd, r0:r1, :], k2[:nc], NT)
            for hd in range(2):
                for r0, r1, nc, masked in bands:
                    row = lax.broadcasted_iota(jnp.int32, (rc, nc), 0)
                    col = lax.broadcasted_iota(jnp.int32, (rc, nc), 1)
                    for r in range(r0, r1, rc):
                        sc = s_scr[hd, r:r + rc, :nc] - crow[hd:hd + 1, :nc]
                        if masked:
                            sc = jnp.where(row + r >= col, sc, NEG)
                        m_old = m_scr[hd, r:r + rc, :]
                        m_new = jnp.maximum(m_old, jnp.max(sc, axis=1, keepdims=True))
                        for cb in range(0, nc, LANES):
                            p_scr[hd, r:r + rc, cb:cb + LANES] = jnp.exp(sc[:, cb:cb + LANES] - m_new).astype(BF16)
                        m_scr[hd, r:r + rc, :] = m_new
                        acc_scr[hd, r:r + rc, :] = acc_scr[hd, r:r + rc, :] * jnp.exp(m_old - m_new)
                    acc_scr[hd, r0:r1, :] += _dot(p_scr[hd, r0:r1, :nc], va[hd][:nc])

        def below(kj, carry):
            block(kj, whole)
            return carry

        lax.fori_loop(0, qi, below, 0)
        block(qi, diagonal)
        a0, a1 = acc_scr[0], acc_scr[1]
        den = jnp.where(lo, pltpu.roll(a0, HEAD_DIM, 1), pltpu.roll(a1, HEAD_DIM, 1))
        o_ref[...] = jnp.where(lo, a0, a1) / den
        lse_t = (jnp.where(lo, m_scr[0], m_scr[1]) + jnp.log(den)).T
        lse_ref[0:1, :] = lse_t[0:1, :]
        lse_ref[1:2, :] = lse_t[HEAD_DIM:HEAD_DIM + 1, :]

    qspec = pl.BlockSpec((tq, PAIR), lambda b, hp, i: (b * nq + i, hp))
    kvspec = pl.BlockSpec((s, PAIR), lambda b, hp, i: (b, hp))
    return _pcall(
        body, name="attn_fwd", grid=(nb, N_PAIRS, nq),
        in_specs=[qspec, kvspec, kvspec, pl.BlockSpec((None, None, 2, s), lambda b, hp, i: (b, hp, 0, 0))],
        out_specs=[qspec, pl.BlockSpec((None, None, 2, tq), lambda b, hp, i: (b, hp, 0, i))],
        out_shape=[jax.ShapeDtypeStruct((t, D_MODEL), F32), jax.ShapeDtypeStruct((nb, N_PAIRS, 2, s), F32)],
        scratch_shapes=[pltpu.VMEM((2, tq, PAIR), BF16), pltpu.VMEM((2, tq, tk), F32), pltpu.VMEM((2, tq, tk), BF16),
                        pltpu.VMEM((2, tq, LANES), F32), pltpu.VMEM((2, tq, PAIR), F32)],
        dims=("parallel", "parallel", "arbitrary"))(q, k, v, c4)


def _attn_bwd(q, k, v, do, c4, lse4, delta4, *, nb, tk=512, rc=16, diag=256):
    t = q.shape[0]
    s = t // nb
    tk = min(tk, s)
    nk = s // tk
    tq = tk
    diag = min(diag, tk)

    def body(q_ref, do_ref, k_ref, v_ref, c_ref, lse_ref, dl_ref, dq_ref, dk_ref, dv_ref, dc_ref, dr_ref,
             kz_scr, vz_scr, ko_scr, crep_scr, st_scr, dp_scr, pt_scr, ds_scr, dk_scr, dv_scr, dq_scr, dr_scr):
        kj = pl.program_id(2)

        @pl.when(kj == 0)
        def _():
            dq_scr[...] = jnp.zeros_like(dq_scr)
            dr_scr[...] = jnp.zeros_like(dr_scr)

        lo = lax.broadcasted_iota(jnp.int32, (tk, PAIR), 1) < HEAD_DIM
        k2 = k_ref[...]
        v2 = v_ref[...]
        zero = jnp.zeros_like(k2)
        one = jnp.ones_like(k2)
        kz_scr[0] = jnp.where(lo, k2, zero)
        kz_scr[1] = jnp.where(lo, zero, k2)
        vz_scr[0] = jnp.where(lo, v2, zero)
        vz_scr[1] = jnp.where(lo, zero, v2)
        ko_scr[0] = jnp.where(lo, k2, one)
        ko_scr[1] = jnp.where(lo, one, k2)
        for hd in range(2):
            crep_scr[hd] = jnp.broadcast_to(c_ref[hd:hd + 1, :], (LANES, tk)).T
        dk_scr[...] = jnp.zeros(dk_scr.shape, F32)
        dv_scr[...] = jnp.zeros(dv_scr.shape, F32)
        row = lax.broadcasted_iota(jnp.int32, (rc, LANES), 0)
        col = lax.broadcasted_iota(jnp.int32, (rc, LANES), 1)

        whole = [(0, tk, 0, False)]
        diagonal = [(r0, r0 + diag, r0, True) for r0 in range(0, tk, diag)]

        def block(qi, bands):
            off = pl.multiple_of(qi * tq, tq)
            q2 = q_ref[pl.ds(off, tq), :] * SCALE
            do2 = do_ref[pl.ds(off, tq), :]
            lse = lse_ref[:, pl.ds(off, tq)]
            dl = dl_ref[:, pl.ds(off, tq)]
            qo = (jnp.where(lo, q2, jnp.ones_like(q2)), jnp.where(lo, jnp.ones_like(q2), q2))
            for hd in range(2):
                for r0, r1, c0, _ in bands:
                    st_scr[hd, r0:r1, c0:] = _dot(kz_scr[hd, r0:r1, :], q2[c0:], NT)
                    dp_scr[hd, r0:r1, c0:] = _dot(vz_scr[hd, r0:r1, :], do2[c0:], NT)
            for r0, r1, c0, masked in bands:
                dqs = []
                for hd in range(2):
                    for r in range(r0, r1, rc):
                        c_rep = crep_scr[hd, r:r + rc, :]
                        for cb in range(c0, tq, LANES):
                            pt = jnp.exp(st_scr[hd, r:r + rc, cb:cb + LANES] - c_rep - lse[hd:hd + 1, cb:cb + LANES])
                            if masked and cb < c0 + diag:
                                pt = jnp.where(col + cb >= row + r, pt, 0.0)
                            dst = pt * (dp_scr[hd, r:r + rc, cb:cb + LANES] - dl[hd:hd + 1, cb:cb + LANES])
                            pt_scr[hd, r:r + rc, cb:cb + LANES] = pt.astype(BF16)
                            ds_scr[hd, r:r + rc, cb:cb + LANES] = dst.astype(BF16)
                    dv_scr[hd, r0:r1, :] += _dot(pt_scr[hd, r0:r1, c0:], do2[c0:])
                    dk_scr[hd, r0:r1, :] += _dot(ds_scr[hd, r0:r1, c0:], qo[hd][c0:])
                    dqs.append(_dot(ds_scr[hd, r0:r1, c0:], ko_scr[hd, r0:r1, :], TN))
                dq_scr[pl.ds(off + c0, tq - c0), :] += jnp.where(lo[c0:], dqs[0], dqs[1]) * SCALE
                dr_scr[pl.ds(off + c0, tq - c0), :] += jnp.where(lo[c0:], dqs[1], dqs[0])

        block(kj, diagonal)

        def below(qi, carry):
            block(qi, whole)
            return carry

        lax.fori_loop(kj + 1, nk, below, 0)
        dk_ref[...] = jnp.where(lo, dk_scr[0], dk_scr[1]).astype(BF16)
        dv_ref[...] = jnp.where(lo, dv_scr[0], dv_scr[1]).astype(BF16)
        dc_t = jnp.where(lo, dk_scr[1], dk_scr[0]).T
        dc_ref[0:1, :] = dc_t[HEAD_DIM:HEAD_DIM + 1, :]
        dc_ref[1:2, :] = dc_t[0:1, :]

        @pl.when(kj == nk - 1)
        def _():
            dq_ref[...] = dq_scr[...].astype(BF16)
            for r in range(0, s, tq):
                dr_t = dr_scr[r:r + tq, :].T
                dr_ref[0:1, r:r + tq] = dr_t[HEAD_DIM:HEAD_DIM + 1, :]
                dr_ref[1:2, r:r + tq] = dr_t[0:1, :]

    full = pl.BlockSpec((s, PAIR), lambda b, hp, j: (b, hp))
    kblk = pl.BlockSpec((tk, PAIR), lambda b, hp, j: (b * nk + j, hp))
    rows = pl.BlockSpec((None, None, 2, s), lambda b, hp, j: (b, hp, 0, 0))
    krows = pl.BlockSpec((None, None, 2, tk), lambda b, hp, j: (b, hp, 0, j))
    sds = lambda dt: jax.ShapeDtypeStruct((t, D_MODEL), dt)
    rows_sds = jax.ShapeDtypeStruct((nb, N_PAIRS, 2, s), F32)
    pair_bf = pltpu.VMEM((2, tk, PAIR), BF16)
    pair_f = pltpu.VMEM((2, tk, PAIR), F32)
    return _pcall(
        body, name="attn_bwd", grid=(nb, N_PAIRS, nk),
        in_specs=[full, full, kblk, kblk, krows, rows, rows],
        out_specs=[full, kblk, kblk, krows, rows],
        out_shape=[sds(BF16), sds(BF16), sds(BF16), rows_sds, rows_sds],
        scratch_shapes=[pair_bf, pair_bf, pair_bf, pair_f, pltpu.VMEM((2, tk, tq), F32), pltpu.VMEM((2, tk, tq), F32),
                        pltpu.VMEM((2, tk, tq), BF16), pltpu.VMEM((2, tk, tq), BF16), pair_f, pair_f,
                        pltpu.VMEM((s, PAIR), F32), pltpu.VMEM((s, PAIR), F32)],
        dims=("parallel", "parallel", "arbitrary"))(q, do, k, v, c4, lse4, delta4)


def _window_sum(x, g, s, *, ahead):
    row = lax.broadcasted_iota(jnp.int32, x.shape, 0)
    for step in range(N_GROUPS):
        sh = 1 << step
        if ahead:
            moved = jnp.where(row + sh < s, pltpu.roll(x, s - sh, 0), 0.0)
        else:
            moved = jnp.where(row >= sh, pltpu.roll(x, sh, 0), 0.0)
        x = jnp.where(step <= g, x + moved, x)
    return x


def _inv_count(g, s):
    pos = lax.broadcasted_iota(jnp.int32, (s, 1), 0) + 1
    return 1.0 / jnp.minimum(pos, 2 << g).astype(F32)


def _pool_fwd(u, zp, wpool, ps, *, nb):
    t = u.shape[0]
    s = t // nb

    def body(u_ref, zp_ref, w_ref, ps_ref, out_ref, pooled_ref, mixed_ref):
        g = pl.program_id(1)
        x = u_ref[...]
        pooled = (_window_sum(x, g, s, ahead=False) * _inv_count(g, s) - x).astype(BF16)
        mixed = _dot(pooled, w_ref[...])
        z = zp_ref[...]
        pooled_ref[...] = pooled
        mixed_ref[...] = mixed
        out_ref[...] = (mixed * ps_ref[...] * (z * _sigmoid(z))).astype(BF16)

    blk = pl.BlockSpec((s, GROUP_DIM), lambda b, g: (b, g))
    sds = lambda dt: jax.ShapeDtypeStruct((t, D_MODEL), dt)
    return _pcall(
        body, name="pool_fwd", grid=(nb, N_GROUPS),
        in_specs=[blk, blk, pl.BlockSpec((None, GROUP_DIM, GROUP_DIM), lambda b, g: (g, 0, 0)),
                  pl.BlockSpec((1, GROUP_DIM), lambda b, g: (0, g))],
        out_specs=[blk, blk, blk], out_shape=[sds(BF16), sds(BF16), sds(F32)],
        dims=("parallel", "parallel"))(u, zp, wpool, ps)


def _pool_bwd(dpool, mixed, zp, wpool, ps, *, nb):
    t = dpool.shape[0]
    s = t // nb

    def body(dp_ref, mx_ref, zp_ref, w_ref, ps_ref, dmx_ref, du_ref, dzp_ref, dps_ref):
        g = pl.program_id(0)

        @pl.when(pl.program_id(1) == 0)
        def _():
            dps_ref[...] = jnp.zeros_like(dps_ref)

        dp = dp_ref[...]
        mixed = mx_ref[...]
        z = zp_ref[...]
        scale = ps_ref[...]
        sg = _sigmoid(z)
        silu = z * sg
        dmixed = (dp * scale * silu).astype(BF16)
        dmx_ref[...] = dmixed
        dps_ref[...] += jnp.sum(dp * mixed * silu, axis=0, keepdims=True)
        dzp_ref[...] = (dp * mixed * scale * (sg * (1.0 + z * (1.0 - sg)))).astype(BF16)
        dpooled = _dot(dmixed, w_ref[...], NT)
        du = _window_sum(dpooled * _inv_count(g, s), g, s, ahead=True) - dpooled
        du_ref[...] = du.astype(BF16)

    blk = pl.BlockSpec((s, GROUP_DIM), lambda g, b: (b, g))
    sds = lambda dt: jax.ShapeDtypeStruct((t, D_MODEL), dt)
    return _pcall(
        body, name="pool_bwd", grid=(N_GROUPS, nb),
        in_specs=[blk, blk, blk, pl.BlockSpec((None, GROUP_DIM, GROUP_DIM), lambda g, b: (g, 0, 0)),
                  pl.BlockSpec((1, GROUP_DIM), lambda g, b: (0, g))],
        out_specs=[blk, blk, blk, pl.BlockSpec((1, GROUP_DIM), lambda g, b: (0, g))],
        out_shape=[sds(BF16), sds(BF16), sds(BF16), jax.ShapeDtypeStruct((1, D_MODEL), F32)],
        dims=("parallel", "arbitrary"))(dpool, mixed, zp, wpool, ps)


def _pool_wgrad(pooled, dmixed, *, tk=1024):
    t = pooled.shape[0]
    tk = min(tk, t)
    nk = t // tk

    def body(a_ref, b_ref, o_ref):
        @pl.when(pl.program_id(1) == 0)
        def _():
            o_ref[...] = jnp.zeros_like(o_ref)

        o_ref[...] += _dot(a_ref[...], b_ref[...], TN)

    blk = pl.BlockSpec((tk, GROUP_DIM), lambda g, k: (k, g))
    return _pcall(
        body, name="pool_wgrad", grid=(N_GROUPS, nk), in_specs=[blk, blk],
        out_specs=pl.BlockSpec((None, GROUP_DIM, GROUP_DIM), lambda g, k: (g, 0, 0)),
        out_shape=jax.ShapeDtypeStruct((N_GROUPS, GROUP_DIM, GROUP_DIM), F32),
        dims=("parallel", "arbitrary"))(pooled, dmixed)


def _out_block(h, o, za, pool, p, wo, wpg, wpe, gpost, *, tm=256):
    t = h.shape[0]

    def body(h_ref, o_ref, za_ref, pool_ref, p_ref, wo_ref, wpg_ref, wpe_ref, g_ref,
             h2_ref, mix_ref, h1_ref, gate_ref, pe_ref, cat_ref):
        z = za_ref[...]
        a = (o_ref[...] * (z * _sigmoid(z))).astype(BF16)
        pool_v = pool_ref[...]
        cat_ref[:, :D_MODEL] = a
        cat_ref[:, D_MODEL:] = pool_v
        mix = _dot(a, wo_ref[:D_MODEL, :]) + _dot(pool_v, wo_ref[D_MODEL:, :])
        r = lax.rsqrt(jnp.mean(mix * mix, axis=-1, keepdims=True) + EPS)
        h1 = h_ref[...] + mix * r * g_ref[...]
        gate = _sigmoid(_dot(h1.astype(BF16), wpg_ref[...]))
        pe = _dot(p_ref[...].astype(BF16), wpe_ref[...])
        mix_ref[...] = mix
        h1_ref[...] = h1
        gate_ref[...] = gate
        pe_ref[...] = pe
        h2_ref[...] = h1 + gate * pe

    row = lambda i: (i, 0)
    const = lambda i: (0, 0)
    blk = pl.BlockSpec((tm, D_MODEL), row)
    sds = jax.ShapeDtypeStruct((t, D_MODEL), F32)
    return _pcall(
        body, name="out_block", grid=(t // tm,),
        in_specs=[blk, blk, blk, blk, pl.BlockSpec((tm, PLE_DIM), row),
                  pl.BlockSpec((MIX, D_MODEL), const), pl.BlockSpec((D_MODEL, D_MODEL), const),
                  pl.BlockSpec((PLE_DIM, D_MODEL), const), pl.BlockSpec((1, D_MODEL), const)],
        out_specs=[blk] * 5 + [pl.BlockSpec((tm, MIX), row)],
        out_shape=[sds] * 5 + [jax.ShapeDtypeStruct((t, MIX), BF16)],
        dims=("parallel",))(h, o, za, pool, p, wo, wpg, wpe, gpost)


def _loss_grad(y, target, *, tm=512):
    t = y.shape[0]

    def body(y_ref, t_ref, dy_ref, loss_ref):
        @pl.when(pl.program_id(0) == 0)
        def _():
            loss_ref[...] = jnp.zeros_like(loss_ref)

        err = y_ref[...] - t_ref[...]
        dy_ref[...] = err * (1.0 / D_MODEL)
        part = jnp.sum(jnp.sum(err * err, axis=1, keepdims=True), axis=0, keepdims=True)
        loss_ref[...] += part * (0.5 / D_MODEL)

    blk = pl.BlockSpec((tm, D_MODEL), lambda i: (i, 0))
    return _pcall(
        body, name="loss_grad", grid=(t // tm,), in_specs=[blk, blk],
        out_specs=[blk, pl.BlockSpec((8, LANES), lambda i: (0, 0))],
        out_shape=[jax.ShapeDtypeStruct((t, D_MODEL), F32), jax.ShapeDtypeStruct((8, LANES), F32)],
        dims=("arbitrary",))(y, target)


def _out_block_bwd(dh2, gate, pe, mix, o, za, wpg, wo, gpost, gsum, *, tm=256):
    t = dh2.shape[0]

    def body(dh2_ref, gate_ref, pe_ref, mix_ref, o_ref, za_ref, wpg_ref, wo_ref, g_ref, gs_ref,
             dh1_ref, dgp_ref, dpe_ref, dmix_ref, do_ref, dza_ref, dpool_ref, delta_ref, dg_ref):
        @pl.when(pl.program_id(0) == 0)
        def _():
            dg_ref[...] = jnp.zeros_like(dg_ref)

        dh2 = dh2_ref[...]
        gate = gate_ref[...]
        dpe_ref[...] = (dh2 * gate).astype(BF16)
        dgp = (dh2 * pe_ref[...] * gate * (1.0 - gate)).astype(BF16)
        dgp_ref[...] = dgp
        dh1 = dh2 + _dot(dgp, wpg_ref[...], NT)
        dh1_ref[...] = dh1
        mix = mix_ref[...]
        r = lax.rsqrt(jnp.mean(mix * mix, axis=-1, keepdims=True) + EPS)
        dg_ref[...] += jnp.sum(dh1 * mix * r, axis=0, keepdims=True)
        a = dh1 * g_ref[...]
        dmix = (r * a - mix * (r * r * r) * jnp.mean(a * mix, axis=-1, keepdims=True)).astype(BF16)
        dmix_ref[...] = dmix
        dattn = _dot(dmix, wo_ref[:D_MODEL, :], NT)
        dpool_ref[...] = _dot(dmix, wo_ref[D_MODEL:, :], NT)
        z = za_ref[...]
        sg = _sigmoid(z)
        o = o_ref[...]
        do = (dattn * (z * sg)).astype(BF16)
        do_ref[...] = do
        dza_ref[...] = (dattn * o * (sg * (1.0 + z * (1.0 - sg)))).astype(BF16)
        prod = do.astype(F32) * o
        hi = prod.astype(BF16)
        rest = (prod - hi.astype(F32)).astype(BF16)
        delta_ref[...] = _dot(hi, gs_ref[...]) + _dot(rest, gs_ref[...])

    row = lambda i: (i, 0)
    const = lambda i: (0, 0)
    blk = pl.BlockSpec((tm, D_MODEL), row)
    sds = lambda dt: jax.ShapeDtypeStruct((t, D_MODEL), dt)
    return _pcall(
        body, name="out_block_bwd", grid=(t // tm,),
        in_specs=[blk] * 6 + [pl.BlockSpec((D_MODEL, D_MODEL), const), pl.BlockSpec((MIX, D_MODEL), const),
                              pl.BlockSpec((1, D_MODEL), const), pl.BlockSpec((D_MODEL, LANES), const)],
        out_specs=[blk] * 7 + [pl.BlockSpec((tm, LANES), row), pl.BlockSpec((1, D_MODEL), const)],
        out_shape=[sds(F32)] + [sds(BF16)] * 5 + [sds(F32), jax.ShapeDtypeStruct((t, LANES), F32),
                                                 jax.ShapeDtypeStruct((1, D_MODEL), F32)],
        dims=("arbitrary",))(dh2, gate, pe, mix, o, za, wpg, wo, gpost, gsum)


def _in_proj_bwd(dsegs, dfl, wm, wf, h, dh1, gpre, *, tm=512):
    t = h.shape[0]

    def body(*refs):
        seg_refs = refs[:N_SEG]
        dfl_ref, wm_ref, wf_ref, h_ref, dh1_ref, g_ref, dh_ref, dg_ref, acc_ref = refs[N_SEG:]
        i = pl.program_id(0)
        j = pl.program_id(1)

        @pl.when(jnp.logical_and(i == 0, j == 0))
        def _():
            dg_ref[...] = jnp.zeros_like(dg_ref)

        @pl.when(j == 0)
        def _():
            acc_ref[...] = _dot(dfl_ref[...].astype(BF16), wf_ref[...], NT)

        for pair in range(N_SEG // 2):
            @pl.when(j == pair)
            def _(ref_a=seg_refs[2 * pair], ref_b=seg_refs[2 * pair + 1]):
                acc_ref[...] += (_dot(ref_a[...].astype(BF16), wm_ref[:, :D_MODEL], NT)
                                 + _dot(ref_b[...].astype(BF16), wm_ref[:, D_MODEL:], NT))

        @pl.when(j == N_SEG // 2 - 1)
        def _():
            dhn = acc_ref[...]
            x = h_ref[...]
            r = lax.rsqrt(jnp.mean(x * x, axis=-1, keepdims=True) + EPS)
            dg_ref[...] += jnp.sum(dhn * x * r, axis=0, keepdims=True)
            a = dhn * g_ref[...]
            dh_ref[...] = dh1_ref[...] + r * a - x * (r * r * r) * jnp.mean(a * x, axis=-1, keepdims=True)

    row = lambda i, j: (i, 0)
    const = lambda i, j: (0, 0)
    blk = pl.BlockSpec((tm, D_MODEL), row)
    return _pcall(
        body, name="in_proj_bwd", grid=(t // tm, N_SEG // 2),
        in_specs=[blk] * N_SEG + [pl.BlockSpec((tm, LANES), row), pl.BlockSpec((D_MODEL, 2 * D_MODEL), lambda i, j: (0, j)),
                                  pl.BlockSpec((D_MODEL, LANES), const), blk, blk, pl.BlockSpec((1, D_MODEL), const)],
        out_specs=[blk, pl.BlockSpec((1, D_MODEL), const)],
        out_shape=[jax.ShapeDtypeStruct((t, D_MODEL), F32), jax.ShapeDtypeStruct((1, D_MODEL), F32)],
        scratch_shapes=[pltpu.VMEM((tm, D_MODEL), F32)],
        dims=("arbitrary", "arbitrary"))(*dsegs, dfl, wm, wf, h, dh1, gpre)


def _chip_peers(x, y):
    return [(1 - x, y), (x, 1 - y), (1 - x, 1 - y)]


def _ici_views(src, land, gather, x, y, c, pi, px, py):
    if gather:
        return src.at[c], land.at[pi], land.at[pi]
    return src.at[2 * px + py], land.at[2 * x + y], land.at[2 * px + py]


def _ici_start(srcs, lands, after, *, gather, name):
    n = len(srcs)

    def body(*refs):
        src, land = refs[:n], refs[n:2 * n]
        send, recv = refs[2 * n + 1], refs[2 * n + 2]
        token = refs[-1]
        x, y, c = lax.axis_index("x"), lax.axis_index("y"), lax.axis_index("c")
        for kk in range(n):
            for pi, (px, py) in enumerate(_chip_peers(x, y)):
                mine, there, _ = _ici_views(src[kk], land[kk], gather, x, y, c, pi, px, py)
                pltpu.make_async_remote_copy(src_ref=mine, dst_ref=there, send_sem=send.at[3 * kk + pi],
                                             recv_sem=recv.at[3 * kk + pi], device_id=(px, py, c),
                                             device_id_type=MESH).start()
        token[...] = jnp.zeros_like(token)

    hbm = lambda a: pltpu.with_memory_space_constraint(a, pltpu.HBM)
    outs = pl.pallas_call(
        body, name=name,
        out_shape=(pltpu.SemaphoreType.DMA((3 * n,)), pltpu.SemaphoreType.DMA((3 * n,)),
                   *[pltpu.HBM(a.shape, a.dtype) for a in srcs], *[pltpu.HBM(a.shape, a.dtype) for a in lands],
                   jax.ShapeDtypeStruct((8, LANES), F32)),
        in_specs=[HBM_SPEC] * (2 * n) + [ANY],
        out_specs=(SEM_SPEC, SEM_SPEC, *[HBM_SPEC] * (2 * n), pl.BlockSpec(memory_space=pltpu.VMEM)),
        input_output_aliases={i: 2 + i for i in range(2 * n)},
        compiler_params=pltpu.CompilerParams(has_side_effects=SIDE_EFFECT),
    )(*[hbm(a) for a in srcs], *[hbm(a) for a in lands], after)
    return outs[0], outs[1], list(outs[2:2 + n]), list(outs[2 + n:2 + 2 * n]), outs[-1]


def _ici_wait(send, recv, srcs, lands, after, *, gather, name):
    n = len(srcs)

    def body(*refs):
        src, land = refs[:n], refs[n:2 * n]
        send_ref, recv_ref = refs[2 * n], refs[2 * n + 1]
        x, y, c = lax.axis_index("x"), lax.axis_index("y"), lax.axis_index("c")
        for kk in range(n):
            for pi, (px, py) in enumerate(_chip_peers(x, y)):
                mine, _, here = _ici_views(src[kk], land[kk], gather, x, y, c, pi, px, py)
                cp = pltpu.make_async_remote_copy(src_ref=mine, dst_ref=here, send_sem=send_ref.at[3 * kk + pi],
                                                  recv_sem=recv_ref.at[3 * kk + pi], device_id=(px, py, c),
                                                  device_id_type=MESH)
                cp.wait_send()
                cp.wait_recv()

    outs = pl.pallas_call(
        body, name=name,
        out_shape=[pltpu.HBM(a.shape, a.dtype) for a in srcs] + [pltpu.HBM(a.shape, a.dtype) for a in lands],
        in_specs=[HBM_SPEC] * (2 * n) + [SEM_SPEC, SEM_SPEC, ANY], out_specs=[HBM_SPEC] * (2 * n),
        input_output_aliases={i: i for i in range(2 * n)},
        compiler_params=pltpu.CompilerParams(has_side_effects=SIDE_EFFECT),
    )(*srcs, *lands, send, recv, after)
    return list(outs[:n]), list(outs[n:])


def _sibling_send(arrays, *, name):
    n = len(arrays)

    def body(*refs):
        ins, outs = refs[:n], refs[n:2 * n]
        send, recv = refs[2 * n:]
        sib = (lax.axis_index("x"), lax.axis_index("y"), 1 - lax.axis_index("c"))
        copies = [pltpu.make_async_remote_copy(src_ref=ins[kk], dst_ref=outs[kk], send_sem=send.at[kk],
                                               recv_sem=recv.at[kk], device_id=sib, device_id_type=MESH)
                  for kk in range(n)]
        for cp in copies:
            cp.start()
        for cp in copies:
            cp.wait_recv()
        for cp in copies:
            cp.wait_send()

    return _pcall(
        body, name=name, in_specs=[ANY] * n, out_specs=[ANY] * n,
        out_shape=[jax.ShapeDtypeStruct(a.shape, a.dtype) for a in arrays],
        scratch_shapes=[pltpu.SemaphoreType.DMA((n,)), pltpu.SemaphoreType.DMA((n,))])(*arrays)


def _all_gather_small(a, *, name):
    def body(a_ref, o_ref, send, recv, loc):
        x, y, c = lax.axis_index("x"), lax.axis_index("y"), lax.axis_index("c")
        me = 4 * x + 2 * y + c

        def peer(rel):
            fx, fy, fc = (rel >> 2) & 1, (rel >> 1) & 1, rel & 1
            px = (1 - x) if fx else x
            py = (1 - y) if fy else y
            pc = (1 - c) if fc else c
            return px, py, pc

        def remote(rel, slot):
            return pltpu.make_async_remote_copy(
                src_ref=a_ref, dst_ref=o_ref.at[slot], send_sem=send.at[rel - 1], recv_sem=recv.at[rel - 1],
                device_id=peer(rel), device_id_type=MESH)

        mine = pltpu.make_async_copy(a_ref, o_ref.at[me], loc)
        mine.start()
        sends = [remote(rel, me) for rel in range(1, 8)]
        for cp in sends:
            cp.start()
        for rel in range(1, 8):
            px, py, pc = peer(rel)
            remote(rel, 4 * px + 2 * py + pc).wait_recv()
        for cp in sends:
            cp.wait_send()
        mine.wait()

    return _pcall(
        body, name=name, in_specs=[ANY], out_specs=ANY,
        out_shape=jax.ShapeDtypeStruct((8,) + a.shape, a.dtype),
        scratch_shapes=[pltpu.SemaphoreType.DMA((7,)), pltpu.SemaphoreType.DMA((7,)), pltpu.SemaphoreType.DMA])(a)


def _to_bf16(a, *, name, tr=256):
    rows, cols = a.shape

    def body(a_ref, o_ref):
        o_ref[...] = a_ref[...].astype(BF16)

    blk = pl.BlockSpec((tr, cols), lambda i: (i, 0))
    return _pcall(body, name=name, grid=(rows // tr,), in_specs=[blk], out_specs=blk,
                  out_shape=jax.ShapeDtypeStruct(a.shape, BF16), dims=("parallel",))(a)


def _pair_sum(a, b, *, name, tr=256):
    _, rows, cols = a.shape
    tr = min(tr, rows)

    def body(a_ref, b_ref, o_ref):
        o_ref[...] = (a_ref[...].astype(F32) + b_ref[...].astype(F32)).astype(BF16)

    blk = pl.BlockSpec((None, tr, cols), lambda j, i: (j, i, 0))
    return _pcall(body, name=name, grid=(4, rows // tr), in_specs=[blk, blk], out_specs=blk,
                  out_shape=jax.ShapeDtypeStruct(a.shape, BF16), dims=("parallel", "parallel"))(a, b)


def _chip_sum(own, recv, chip, *, name, tr=256):
    _, rows, cols = own.shape
    tr = min(tr, rows)

    def body(chip_ref, own_ref, r1_ref, r2_ref, r3_ref, o_ref):
        acc = own_ref[...].astype(F32)
        for ref in (r1_ref, r2_ref, r3_ref):
            acc = acc + ref[...].astype(F32)
        o_ref[...] = acc

    def slot(step):
        return pl.BlockSpec((None, tr, cols), lambda i, chip_ref: ((chip_ref[0] + step) % 4, i, 0))

    return pl.pallas_call(
        body, name=name, out_shape=jax.ShapeDtypeStruct((rows, cols), F32),
        grid_spec=pltpu.PrefetchScalarGridSpec(
            num_scalar_prefetch=1, grid=(rows // tr,), in_specs=[slot(0), slot(1), slot(2), slot(3)],
            out_specs=pl.BlockSpec((tr, cols), lambda i, chip_ref: (i, 0))),
        compiler_params=pltpu.CompilerParams(dimension_semantics=("parallel",), vmem_limit_bytes=VMEM_LIMIT),
    )(chip, own, recv, recv, recv)


def _adamw_math(w, g, m, v):
    m = ADAM_B1 * m + (1.0 - ADAM_B1) * g
    v = ADAM_B2 * v + (1.0 - ADAM_B2) * (g * g)
    m_hat = m / (1.0 - ADAM_B1 ** ADAM_STEP)
    v_hat = v / (1.0 - ADAM_B2 ** ADAM_STEP)
    delta = -ADAM_LR * (m_hat / (jnp.sqrt(v_hat) + ADAM_EPS) + ADAM_WD * w)
    return delta, m, v


def _adamw(w, m, v, g, layer, prev, *, name, tr=128):
    rows, cols = g.shape
    tr = min(tr, rows)
    nblk = rows // tr

    def body(w_ref, m_ref, v_ref, gin_ref, *refs):
        g_ref, d_ref, mo_ref, vo_ref = refs[-4:]
        grad = gin_ref[...]
        delta, m_new, v_new = _adamw_math(w_ref[...], grad, m_ref[...], v_ref[...])
        g_ref[...] = grad
        d_ref[...] = delta
        mo_ref[...] = m_new
        vo_ref[...] = v_new

    blk = pl.BlockSpec((tr, cols), lambda i: (layer * nblk + i, 0))
    carried = [] if prev is None else list(prev)
    return pl.pallas_call(
        body, name=name, grid=(nblk,), out_shape=[jax.ShapeDtypeStruct(w.shape, F32)] * 4,
        in_specs=[blk] * 3 + [pl.BlockSpec((tr, cols), lambda i: (i, 0))] + [ANY] * len(carried), out_specs=[blk] * 4,
        input_output_aliases={4 + i: i for i in range(len(carried))},
        compiler_params=pltpu.CompilerParams(dimension_semantics=("parallel",), vmem_limit_bytes=VMEM_LIMIT),
    )(w, m, v, g, *carried)


def _adamw_small(w, m, v, parts, *, name):
    def body(w_ref, m_ref, v_ref, p_ref, g_ref, d_ref, mo_ref, vo_ref):
        g = p_ref[0]
        for dev in range(1, 8):
            g = g + p_ref[dev]
        delta, m_new, v_new = _adamw_math(w_ref[...], g, m_ref[...], v_ref[...])
        g_ref[...] = g
        d_ref[...] = delta
        mo_ref[...] = m_new
        vo_ref[...] = v_new

    return _pcall(body, name=name, out_shape=[jax.ShapeDtypeStruct(w.shape, F32)] * 4)(w, m, v, parts)


def _column_parts(segments, lo, hi):
    parts, start = [], 0
    for seg in segments:
        a, b = max(lo, start), min(hi, start + seg.shape[1])
        if a < b:
            parts.append(seg[:, a - start:b - start])
        start += seg.shape[1]
    return parts


def _rows_from_heads(a, nb, s):
    return a.reshape(nb, s, N_HEADS).transpose(0, 2, 1).reshape(nb, N_PAIRS, 2, s)


def _layer_fwd(h, p_l, wts, nb):
    t = h.shape[0]
    s = t // nb
    q, k, v, za, u, zp, fl, hn = _norm_in_proj(h, wts["gpre"], wts["wm"], wts["wf"])
    flt = fl[:, :N_HEADS].reshape(nb, s, N_HEADS).transpose(0, 2, 1)
    c = _gate_cumsum(flt, wts["bf"])
    o, lse = _attn_fwd(q, k, v, c.reshape(nb, N_PAIRS, 2, s), nb=nb)
    pool, pooled, mixed = _pool_fwd(u, zp, wts["wpool"], wts["ps"], nb=nb)
    h2, mix, h1, gate, pe, cat = _out_block(h, o, za, pool, p_l, wts["wo"], wts["wpg"], wts["wpe"], wts["gpost"])
    saved = dict(h=h, hn=hn, q=q, k=k, v=v, za=za, zp=zp, flt=flt, c=c, o=o, lse=lse, pooled=pooled, mixed=mixed,
                 mix=mix, h1=h1, gate=gate, pe=pe, cat=cat, p=p_l)
    return h2, saved


def _layer_bwd(dh2, sv, wts, gsum, nb):
    t = dh2.shape[0]
    s = t // nb
    dh1, dgp, dpe, dmix, do, dza, dpool, delta, dgpost = _out_block_bwd(
        dh2, sv["gate"], sv["pe"], sv["mix"], sv["o"], sv["za"], wts["wpg"], wts["wo"], wts["gpost"], gsum)
    g_wpe = _matmul(sv["p"], dpe, ta=True, name="wgrad_pe")
    g_wpg = _matmul(sv["h1"], dgp, ta=True, name="wgrad_pg")
    g_wo = _matmul(sv["cat"], dmix, ta=True, name="wgrad_out")

    delta4 = _rows_from_heads(delta[:, :N_HEADS], nb, s)
    dq, dk, dv, dcol, drow = _attn_bwd(sv["q"], sv["k"], sv["v"], do, sv["c"].reshape(nb, N_PAIRS, 2, s), sv["lse"],
                                       delta4, nb=nb)
    dflt, dbf = _gate_cumsum_bwd(dcol.reshape(nb, N_HEADS, s), drow.reshape(nb, N_HEADS, s), sv["flt"], wts["bf"])
    dfl = jnp.pad(dflt.transpose(0, 2, 1).reshape(t, N_HEADS), ((0, 0), (0, LANES - N_HEADS)))

    dmixed, du, dzp, dps = _pool_bwd(dpool, sv["mixed"], sv["zp"], wts["wpool"], wts["ps"], nb=nb)
    g_wpool = _pool_wgrad(sv["pooled"], dmixed)

    dsegs = (dq, dk, dv, dza, du, dzp)
    dh, dgpre = _in_proj_bwd(dsegs, dfl, wts["wm"], wts["wf"], sv["h"], dh1, wts["gpre"])
    g_segs = [_matmul(sv["hn"], dx, ta=True, name="wgrad_in") for dx in dsegs]
    g_wf = _matmul(sv["hn"], dfl, ta=True, name="wgrad_in_f")
    g_win = g_segs[:4] + [g_wf[:, :N_HEADS]] + g_segs[4:]
    grads = dict(w_in=g_win, w_out=g_wo, w_pg=g_wpg, w_pe=g_wpe, w_pool=g_wpool,
                 norm_pre=dgpre[0], norm_post=dgpost[0], pool_scale=dps[0], b_f=dbf[:, 0])
    return dh, grads


def kernel(x, p, norm_pre, norm_post, w_in, b_f, w_pool, pool_scale, w_out, w_pg, w_pe, loss_target, m_norm_pre, m_norm_post, m_w_in, m_b_f, m_w_pool, m_pool_scale, m_w_out, m_w_pg, m_w_pe, v_norm_pre, v_norm_post, v_w_in, v_b_f, v_w_pool, v_pool_scale, v_w_out, v_w_pg, v_w_pe):
    nb, s, _ = x.shape
    t = nb * s
    depth = w_in.shape[0]
    big = dict(w_in=(w_in, m_w_in, v_w_in), w_out=(w_out, m_w_out, v_w_out), w_pg=(w_pg, m_w_pg, v_w_pg),
               w_pe=(w_pe, m_w_pe, v_w_pe), w_pool=(w_pool, m_w_pool, v_w_pool))
    names = list(big)
    cols = {n: big[n][0].shape[-1] for n in names}
    chip = (2 * lax.axis_index("x") + lax.axis_index("y")).astype(jnp.int32).reshape(1)
    core = lax.axis_index("c")
    south = core == 0

    gathers = []
    token = jnp.zeros((8, LANES), F32)
    narrow = {n: big[n][0].astype(BF16) for n in names if n != "w_in"}
    narrow["w_in"] = _to_bf16(w_in.reshape(-1, cols["w_in"]), name="w_in_to_bf16").reshape(w_in.shape)
    for l in range(depth):
        halves = [narrow[n][l].reshape(2, -1, cols[n]) for n in names]
        lands = [lax.empty((3,) + a.shape[1:], a.dtype) for a in halves]
        send, recv, halves, lands, token = _ici_start(halves, lands, token, gather=True, name=f"gather_start_{l}")
        gathers.append((send, recv, halves, lands))

    def layer_weights(l, after, anchor):
        send, recv, halves, lands = gathers[l]
        halves, lands = _ici_wait(send, recv, halves, lands, after, gather=True, name=f"gather_wait_{l}")
        others = _sibling_send(lands, name="gather_pass_on")
        full = {}
        for n, own, land, other in zip(names, halves, lands, others):
            low = jnp.where(south, land, other)
            high = jnp.where(south, other, land)
            rel = [jnp.concatenate([low[pi], high[pi]], axis=0) for pi in range(3)]
            by_rel = jnp.stack([own.reshape(-1, cols[n]), rel[1], rel[0], rel[2]])
            full[n] = [lax.dynamic_index_in_dim(by_rel, j ^ chip[0], 0, keepdims=False) for j in range(4)]
        win = full["w_in"]
        wm = jnp.concatenate(_column_parts(win, 0, F_OFF) + _column_parts(win, F_OFF + N_HEADS, IN_COLS), axis=1)
        wf = jnp.pad(jnp.concatenate(_column_parts(win, F_OFF, F_OFF + N_HEADS), axis=1), ((0, 0), (0, LANES - N_HEADS)))
        return dict(
            wm=wm, wf=wf, wo=jnp.concatenate(full["w_out"], axis=0), wpg=jnp.concatenate(full["w_pg"], axis=0),
            wpe=jnp.concatenate(full["w_pe"], axis=1),
            wpool=jnp.stack(full["w_pool"]).reshape(4, N_GROUPS, GROUP_DIM // 4, GROUP_DIM).transpose(1, 0, 2, 3).reshape(
                N_GROUPS, GROUP_DIM, GROUP_DIM),
            gpre=norm_pre[l][None] + anchor, gpost=norm_post[l][None], ps=pool_scale[l][None], bf=b_f[l][:, None])

    h = x.reshape(t, D_MODEL)
    saved, layers = [], []
    after = token
    for l in range(depth):
        layers.append(layer_weights(l, after, token[0, 0]))
        h, sv = _layer_fwd(h, p[l].reshape(t, PLE_DIM), layers[l], nb)
        saved.append(sv)
        after = h
    dh, loss_blk = _loss_grad(h, loss_target.reshape(t, D_MODEL))
    loss = lax.psum(loss_blk[0, 0], ("x", "y", "c"))
    gsum = (jnp.arange(D_MODEL)[:, None] // HEAD_DIM == jnp.arange(LANES)[None, :]).astype(BF16)

    def pieces(g, name):
        if name == "w_in":
            by_chip = jnp.stack([jnp.concatenate(_column_parts(g, j * cols[name], (j + 1) * cols[name]), axis=1)
                                 for j in range(4)])
        elif name == "w_pe":
            by_chip = jnp.stack([g[:, j * cols[name]:(j + 1) * cols[name]] for j in range(4)])
        elif name == "w_pool":
            by_chip = g.reshape(N_GROUPS, 4, GROUP_DIM // 4, GROUP_DIM).transpose(1, 0, 2, 3)
        else:
            by_chip = g
        return by_chip.reshape(4, 2, -1, cols[name]).transpose(1, 0, 2, 3).astype(BF16)

    def scatter_start(l, g):
        mine, sent = [], []
        for n in names:
            halves = pieces(g[n], n)
            mine.append(lax.dynamic_index_in_dim(halves, core, 0, keepdims=False))
            sent.append(lax.dynamic_index_in_dim(halves, 1 - core, 0, keepdims=False))
        theirs = _sibling_send(sent, name="presum_exchange")
        sums = [_pair_sum(a, b, name="presum_" + n) for n, a, b in zip(names, mine, theirs)]
        lands = [lax.empty(a.shape, a.dtype) for a in sums]
        return _ici_start(sums, lands, token, gather=False, name=f"scatter_start_{l}")

    def scatter_finish(l, state, after):
        send, recv, sums, lands, _ = state
        sums, lands = _ici_wait(send, recv, sums, lands, after, gather=False, name=f"scatter_wait_{l}")
        mine = [_chip_sum(a, r, chip, name="chip_sum_" + n) for n, a, r in zip(names, sums, lands)]
        theirs = _sibling_send(mine, name="halves_exchange")
        return {n: jnp.concatenate([jnp.where(south, a, b), jnp.where(south, b, a)], axis=0)
                for n, a, b in zip(names, mine, theirs)}

    grads = [None] * depth
    big_out = {n: None for n in names}

    def update(l, state, after):
        reduced = scatter_finish(l, state, after)
        for n in names:
            w, m, v = big[n]
            big_out[n] = _adamw(w.reshape(-1, cols[n]), m.reshape(-1, cols[n]), v.reshape(-1, cols[n]), reduced[n], l,
                                big_out[n], name="adamw_" + n)

    pending = None
    for l in reversed(range(depth)):
        wts = layers[l]
        if pending is not None:
            wts = dict(wts, gpost=wts["gpost"] + pending[1][4][0, 0])
        dh, grads[l] = _layer_bwd(dh, saved[l], wts, gsum, nb)
        state = scatter_start(l, grads[l])
        if pending is not None:
            update(pending[0], pending[1], state[4])
        pending = (l, state)
    grad_x = dh.reshape(nb, s, D_MODEL)

    small = dict(norm_pre=(norm_pre, m_norm_pre, v_norm_pre), norm_post=(norm_post, m_norm_post, v_norm_post),
                 pool_scale=(pool_scale, m_pool_scale, v_pool_scale), b_f=(b_f, m_b_f, v_b_f))

    def pack(get):
        rows = []
        for n in small:
            a = get(n)
            rows.append(jnp.pad(a, ((0, 0), (0, D_MODEL - a.shape[1]))))
        return jnp.concatenate(rows, axis=0)

    parts = _all_gather_small(pack(lambda n: jnp.stack([grads[l][n] for l in range(depth)])), name="gather_small")
    small_packed = _adamw_small(pack(lambda n: small[n][0]), pack(lambda n: small[n][1]), pack(lambda n: small[n][2]),
                                parts, name="adamw_small")
    small_out = {}
    for i, n in enumerate(small):
        width = small[n][0].shape[1]
        small_out[n] = [o[depth * i:depth * (i + 1), :width] for o in small_packed]

    update(pending[0], pending[1], big_out[names[-1]][0] if depth > 1 else small_packed[0])
    big_out = {n: [o.reshape(big[n][0].shape) for o in outs] for n, outs in big_out.items()}

    order =["norm_pre", "norm_post", "w_in", "b_f", "w_pool", "pool_scale", "w_out", "w_pg", "w_pe"]
    result = [loss, grad_x]
    for kind in range(4):
        for n in order:
            result.append(big_out[n][kind] if n in big_out else small_out[n][kind])
    return tuple(result)
```

```python
import functools

import jax
import jax.numpy as jnp
from jax import lax
from jax.experimental import pallas as pl
from jax.experimental.pallas import tpu as pltpu

F32 = jnp.float32
BF16 = jnp.bfloat16

D_MODEL = 1024
N_HEADS = 16
HEAD_DIM = 64
PAIR = 2 * HEAD_DIM
N_PAIRS = N_HEADS // 2
N_GROUPS = 4
GROUP_DIM = 256
PLE_DIM = 256
MIX = 2 * D_MODEL
N_SEG = 6
F_OFF = 4 * D_MODEL
IN_COLS = N_SEG * D_MODEL + N_HEADS
LANES = 128
EPS = 1e-6
SCALE = 0.125
NEG = -1e30

ADAM_LR = 0.001
ADAM_B1 = 0.9
ADAM_B2 = 0.999
ADAM_EPS = 1e-08
ADAM_WD = 0.01
ADAM_STEP = 10

VMEM_LIMIT = 56 * 1024 * 1024
MESH = pl.DeviceIdType.MESH
ANY = pl.BlockSpec(memory_space=pl.ANY)
HBM_SPEC = pl.BlockSpec(memory_space=pltpu.HBM)
SEM_SPEC = pl.BlockSpec(memory_space=pltpu.SEMAPHORE)
SIDE_EFFECT = pltpu.SideEffectType.DATAFLOW_SIDE_EFFECTING

NT = (((1,), (1,)), ((), ()))
TN = (((0,), (0,)), ((), ()))
NN = (((1,), (0,)), ((), ()))


def _pcall(body, *, name, out_shape, grid=(), in_specs=None, out_specs=None, scratch_shapes=(), dims=None):
    kw = {}
    if in_specs is not None:
        kw["in_specs"] = in_specs
    if out_specs is not None:
        kw["out_specs"] = out_specs
    return pl.pallas_call(
        body, name=name, out_shape=out_shape, grid=grid, scratch_shapes=list(scratch_shapes),
        compiler_params=pltpu.CompilerParams(dimension_semantics=dims, vmem_limit_bytes=VMEM_LIMIT), **kw)


def _dot(a, b, dn=NN):
    return lax.dot_general(a, b, dn, preferred_element_type=F32)


def _sigmoid(x):
    return 1.0 / (1.0 + jnp.exp(-x))


def _matmul(a, b, *, ta=False, name, tm=512, tn=1024, tk=2048):
    if ta:
        kdim, m = a.shape
    else:
        m, kdim = a.shape
    n = b.shape[1]
    tm, tn, tk = min(tm, m), min(tn, n), min(tk, kdim)
    nk = kdim // tk

    def body(a_ref, b_ref, o_ref, acc_ref):
        k = pl.program_id(2)

        @pl.when(k == 0)
        def _():
            acc_ref[...] = jnp.zeros_like(acc_ref)

        acc_ref[...] += _dot(a_ref[...].astype(BF16), b_ref[...].astype(BF16), TN if ta else NN)

        @pl.when(k == nk - 1)
        def _():
            o_ref[...] = acc_ref[...].astype(BF16)

    a_spec = pl.BlockSpec((tk, tm), lambda i, j, k: (k, i)) if ta else pl.BlockSpec((tm, tk), lambda i, j, k: (i, k))
    return _pcall(
        body, name=name, out_shape=jax.ShapeDtypeStruct((m, n), BF16), grid=(m // tm, n // tn, nk),
        in_specs=[a_spec, pl.BlockSpec((tk, tn), lambda i, j, k: (k, j))],
        out_specs=pl.BlockSpec((tm, tn), lambda i, j, k: (i, j)),
        scratch_shapes=[pltpu.VMEM((tm, tn), F32)], dims=("parallel", "parallel", "arbitrary"))(a, b)


def _norm_in_proj(h, g, wm, wf, *, tm=512):
    t = h.shape[0]

    def body(h_ref, g_ref, wm_ref, wf_ref, q_ref, k_ref, v_ref, za_ref, u_ref, zp_ref, fl_ref, hn_ref):
        j = pl.program_id(1)

        @pl.when(j == 0)
        def _():
            x = h_ref[...]
            r = lax.rsqrt(jnp.mean(x * x, axis=-1, keepdims=True) + EPS)
            hn = (x * r * g_ref[...]).astype(BF16)
            hn_ref[...] = hn
            fl_ref[...] = _dot(hn, wf_ref[...])

        y = _dot(hn_ref[...], wm_ref[...])
        for s, (ref_a, ref_b) in enumerate(((q_ref, k_ref), (v_ref, za_ref), (u_ref, zp_ref))):
            @pl.when(j == s)
            def _(ref_a=ref_a, ref_b=ref_b):
                ref_a[...] = y[:, :D_MODEL].astype(ref_a.dtype)
                ref_b[...] = y[:, D_MODEL:].astype(ref_b.dtype)

    row = lambda i, j: (i, 0)
    blk = pl.BlockSpec((tm, D_MODEL), row)
    sds = lambda dt: jax.ShapeDtypeStruct((t, D_MODEL), dt)
    return _pcall(
        body, name="norm_in_proj", grid=(t // tm, N_SEG // 2),
        in_specs=[blk, pl.BlockSpec((1, D_MODEL), lambda i, j: (0, 0)),
                  pl.BlockSpec((D_MODEL, 2 * D_MODEL), lambda i, j: (0, j)),
                  pl.BlockSpec((D_MODEL, LANES), lambda i, j: (0, 0))],
        out_specs=[blk] * 6 + [pl.BlockSpec((tm, LANES), row), blk],
        out_shape=[sds(BF16)] * 3 + [sds(F32)] * 3 + [jax.ShapeDtypeStruct((t, LANES), F32), sds(BF16)],
        dims=("parallel", "arbitrary"))(h, g, wm, wf)


def _gate_cumsum(flt, bias):
    b, hh, s = flt.shape

    def body(fl_ref, b_ref, c_ref):
        x = fl_ref[...] + b_ref[...]
        acc = jnp.minimum(x, 0.0) - jnp.log(1.0 + jnp.exp(-jnp.abs(x)))
        idx = lax.broadcasted_iota(jnp.int32, x.shape, 1)
        sh = 1
        while sh < s:
            acc = acc + jnp.where(idx >= sh, pltpu.roll(acc, sh, 1), 0.0)
            sh *= 2
        c_ref[...] = acc

    return _pcall(
        body, name="gate_cumsum", grid=(b,),
        in_specs=[pl.BlockSpec((None, hh, s), lambda i: (i, 0, 0)), pl.BlockSpec((hh, 1), lambda i: (0, 0))],
        out_specs=pl.BlockSpec((None, hh, s), lambda i: (i, 0, 0)),
        out_shape=jax.ShapeDtypeStruct((b, hh, s), F32), dims=("parallel",))(flt, bias)


def _gate_cumsum_bwd(dcol, drow, flt, bias):
    b, hh, s = flt.shape

    def body(d_ref, dr_ref, fl_ref, b_ref, dfl_ref, db_ref):
        @pl.when(pl.program_id(0) == 0)
        def _():
            db_ref[...] = jnp.zeros_like(db_ref)

        x = fl_ref[...] + b_ref[...]
        acc = dr_ref[...] - d_ref[...]
        idx = lax.broadcasted_iota(jnp.int32, x.shape, 1)
        sh = 1
        while sh < s:
            acc = acc + jnp.where(idx + sh < s, pltpu.roll(acc, s - sh, 1), 0.0)
            sh *= 2
        e = jnp.exp(-jnp.abs(x))
        sig_neg = jnp.where(x >= 0.0, e, 1.0) / (1.0 + e)
        dfl = acc * sig_neg
        dfl_ref[...] = dfl
        db_ref[...] += jnp.sum(dfl, axis=1, keepdims=True)

    return _pcall(
        body, name="gate_cumsum_bwd", grid=(b,),
        in_specs=[pl.BlockSpec((None, hh, s), lambda i: (i, 0, 0))] * 3 + [pl.BlockSpec((hh, 1), lambda i: (0, 0))],
        out_specs=[pl.BlockSpec((None, hh, s), lambda i: (i, 0, 0)), pl.BlockSpec((hh, 1), lambda i: (0, 0))],
        out_shape=[jax.ShapeDtypeStruct((b, hh, s), F32), jax.ShapeDtypeStruct((hh, 1), F32)],
        dims=("arbitrary",))(dcol, drow, flt, bias)


def _attn_fwd(q, k, v, c4, *, nb, tq=512, rc=32, diag=256):
    t = q.shape[0]
    s = t // nb
    tq = min(tq, s)
    nq = s // tq
    tk = tq
    diag = min(diag, tq)

    def body(q_ref, k_ref, v_ref, c_ref, o_ref, lse_ref, qh_scr, s_scr, p_scr, m_scr, acc_scr):
        qi = pl.program_id(2)
        lo = lax.broadcasted_iota(jnp.int32, (tq, PAIR), 1) < HEAD_DIM
        q2 = q_ref[...] * SCALE
        zero = jnp.zeros_like(q2)
        qh_scr[0] = jnp.where(lo, q2, zero)
        qh_scr[1] = jnp.where(lo, zero, q2)
        m_scr[...] = jnp.full(m_scr.shape, NEG, F32)
        acc_scr[...] = jnp.zeros(acc_scr.shape, F32)
        whole = [(0, tq, tk, False)]
        diagonal = [(r0, r0 + diag, r0 + diag, True) for r0 in range(0, tq, diag)]

        def block(kj, bands):
            off = pl.multiple_of(kj * tk, tk)
            k2 = k_ref[pl.ds(off, tk), :]
            v2 = v_ref[pl.ds(off, tk), :]
            one = jnp.ones_like(v2)
            va = (jnp.where(lo, v2, one), jnp.where(lo, one, v2))
            crow = c_ref[:, pl.ds(off, tk)]
            for hd in range(2):
                for r0, r1, nc, _ in bands:
                    s_scr[hd, r0:r1, :nc] = _dot(qh_scr[hd, r0:r1, :], k2[:nc], NT)
            for hd in range(2):
                for r0, r1, nc, masked in bands:
                    row = lax.broadcasted_iota(jnp.int32, (rc, nc), 0)
                    col = lax.broadcasted_iota(jnp.int32, (rc, nc), 1)
                    for r in range(r0, r1, rc):
                        sc = s_scr[hd, r:r + rc, :nc] - crow[hd:hd + 1, :nc]
                        if masked:
                            sc = jnp.where(row + r >= col, sc, NEG)
                        m_old = m_scr[hd, r:r + rc, :]
                        m_new = jnp.maximum(m_old, jnp.max(sc, axis=1, keepdims=True))
                        for cb in range(0, nc, LANES):
                            p_scr[hd, r:r + rc, cb:cb + LANES] = jnp.exp(sc[:, cb:cb + LANES] - m_new).astype(BF16)
                        m_scr[hd, r:r + rc, :] = m_new
                        acc_scr[hd, r:r + rc, :] = acc_scr[hd, r:r + rc, :] * jnp.exp(m_old - m_new)
                    acc_scr[hd, r0:r1, :] += _dot(p_scr[hd, r0:r1, :nc], va[hd][:nc])

        def below(kj, carry):
            block(kj, whole)
            return carry

        lax.fori_loop(0, qi, below, 0)
        block(qi, diagonal)
        a0, a1 = acc_scr[0], acc_scr[1]
        den = jnp.where(lo, pltpu.roll(a0, HEAD_DIM, 1), pltpu.roll(a1, HEAD_DIM, 1))
        o_ref[...] = jnp.where(lo, a0, a1) / den
        lse_t = (jnp.where(lo, m_scr[0], m_scr[1]) + jnp.log(den)).T
        lse_ref[0:1, :] = lse_t[0:1, :]
        lse_ref[1:2, :] = lse_t[HEAD_DIM:HEAD_DIM + 1, :]

    qspec = pl.BlockSpec((tq, PAIR), lambda b, hp, i: (b * nq + i, hp))
    kvspec = pl.BlockSpec((s, PAIR), lambda b, hp, i: (b, hp))
    return _pcall(
        body, name="attn_fwd", grid=(nb, N_PAIRS, nq),
        in_specs=[qspec, kvspec, kvspec, pl.BlockSpec((None, None, 2, s), lambda b, hp, i: (b, hp, 0, 0))],
        out_specs=[qspec, pl.BlockSpec((None, None, 2, tq), lambda b, hp, i: (b, hp, 0, i))],
        out_shape=[jax.ShapeDtypeStruct((t, D_MODEL), F32), jax.ShapeDtypeStruct((nb, N_PAIRS, 2, s), F32)],
        scratch_shapes=[pltpu.VMEM((2, tq, PAIR), BF16), pltpu.VMEM((2, tq, tk), F32), pltpu.VMEM((2, tq, tk), BF16),
                        pltpu.VMEM((2, tq, LANES), F32), pltpu.VMEM((2, tq, PAIR), F32)],
        dims=("parallel", "parallel", "arbitrary"))(q, k, v, c4)


def _attn_bwd(q, k, v, do, c4, lse4, delta4, *, nb, tk=512, rc=16, diag=256):
    t = q.shape[0]
    s = t // nb
    tk = min(tk, s)
    nk = s // tk
    tq = tk
    diag = min(diag, tk)

    def body(q_ref, do_ref, k_ref, v_ref, c_ref, lse_ref, dl_ref, dq_ref, dk_ref, dv_ref, dc_ref, dr_ref,
             kz_scr, vz_scr, ko_scr, crep_scr, st_scr, dp_scr, pt_scr, ds_scr, dk_scr, dv_scr, dq_scr, dr_scr):
        kj = pl.program_id(2)

        @pl.when(kj == 0)
        def _():
            dq_scr[...] = jnp.zeros_like(dq_scr)
            dr_scr[...] = jnp.zeros_like(dr_scr)

        lo = lax.broadcasted_iota(jnp.int32, (tk, PAIR), 1) < HEAD_DIM
        k2 = k_ref[...]
        v2 = v_ref[...]
        zero = jnp.zeros_like(k2)
        one = jnp.ones_like(k2)
        kz_scr[0] = jnp.where(lo, k2, zero)
        kz_scr[1] = jnp.where(lo, zero, k2)
        vz_scr[0] = jnp.where(lo, v2, zero)
        vz_scr[1] = jnp.where(lo, zero, v2)
        ko_scr[0] = jnp.where(lo, k2, one)
        ko_scr[1] = jnp.where(lo, one, k2)
        for hd in range(2):
            crep_scr[hd] = jnp.broadcast_to(c_ref[hd:hd + 1, :], (LANES, tk)).T
        dk_scr[...] = jnp.zeros(dk_scr.shape, F32)
        dv_scr[...] = jnp.zeros(dv_scr.shape, F32)
        row = lax.broadcasted_iota(jnp.int32, (rc, LANES), 0)
        col = lax.broadcasted_iota(jnp.int32, (rc, LANES), 1)

        whole = [(0, tk, 0, False)]
        diagonal = [(r0, r0 + diag, r0, True) for r0 in range(0, tk, diag)]

        def block(qi, bands):
            off = pl.multiple_of(qi * tq, tq)
            q2 = q_ref[pl.ds(off, tq), :] * SCALE
            do2 = do_ref[pl.ds(off, tq), :]
            lse = lse_ref[:, pl.ds(off, tq)]
            dl = dl_ref[:, pl.ds(off, tq)]
            qo = (jnp.where(lo, q2, jnp.ones_like(q2)), jnp.where(lo, jnp.ones_like(q2), q2))
            for hd in range(2):
                for r0, r1, c0, _ in bands:
                    st_scr[hd, r0:r1, c0:] = _dot(kz_scr[hd, r0:r1, :], q2[c0:], NT)
                    dp_scr[hd, r0:r1, c0:] = _dot(vz_scr[hd, r0:r1, :], do2[c0:], NT)
            for r0, r1, c0, masked in bands:
                dqs = []
                for hd in range(2):
                    for r in range(r0, r1, rc):
                        c_rep = crep_scr[hd, r:r + rc, :]
                        for cb in range(c0, tq, LANES):
                            pt = jnp.exp(st_scr[hd, r:r + rc, cb:cb + LANES] - c_rep - lse[hd:hd + 1, cb:cb + LANES])
                            if masked and cb < c0 + diag:
                                pt = jnp.where(col + cb >= row + r, pt, 0.0)
                            dst = pt * (dp_scr[hd, r:r + rc, cb:cb + LANES] - dl[hd:hd + 1, cb:cb + LANES])
                            pt_scr[hd, r:r + rc, cb:cb + LANES] = pt.astype(BF16)
                            ds_scr[hd, r:r + rc, cb:cb + LANES] = dst.astype(BF16)
                    dv_scr[hd, r0:r1, :] += _dot(pt_scr[hd, r0:r1, c0:], do2[c0:])
                    dk_scr[hd, r0:r1, :] += _dot(ds_scr[hd, r0:r1, c0:], qo[hd][c0:])
                    dqs.append(_dot(ds_scr[hd, r0:r1, c0:], ko_scr[hd, r0:r1, :], TN))
                dq_scr[pl.ds(off + c0, tq - c0), :] += jnp.where(lo[c0:], dqs[0], dqs[1]) * SCALE
                dr_scr[pl.ds(off + c0, tq - c0), :] += jnp.where(lo[c0:], dqs[1], dqs[0])

        block(kj, diagonal)

        def below(qi, carry):
            block(qi, whole)
            return carry

        lax.fori_loop(kj + 1, nk, below, 0)
        dk_ref[...] = jnp.where(lo, dk_scr[0], dk_scr[1]).astype(BF16)
        dv_ref[...] = jnp.where(lo, dv_scr[0], dv_scr[1]).astype(BF16)
        dc_t = jnp.where(lo, dk_scr[1], dk_scr[0]).T
        dc_ref[0:1, :] = dc_t[HEAD_DIM:HEAD_DIM + 1, :]
        dc_ref[1:2, :] = dc_t[0:1, :]

        @pl.when(kj == nk - 1)
        def _():
            dq_ref[...] = dq_scr[...].astype(BF16)
            for r in range(0, s, tq):
                dr_t = dr_scr[r:r + tq, :].T
                dr_ref[0:1, r:r + tq] = dr_t[HEAD_DIM:HEAD_DIM + 1, :]
                dr_ref[1:2, r:r + tq] = dr_t[0:1, :]

    full = pl.BlockSpec((s, PAIR), lambda b, hp, j: (b, hp))
    kblk = pl.BlockSpec((tk, PAIR), lambda b, hp, j: (b * nk + j, hp))
    rows = pl.BlockSpec((None, None, 2, s), lambda b, hp, j: (b, hp, 0, 0))
    krows = pl.BlockSpec((None, None, 2, tk), lambda b, hp, j: (b, hp, 0, j))
    sds = lambda dt: jax.ShapeDtypeStruct((t, D_MODEL), dt)
    rows_sds = jax.ShapeDtypeStruct((nb, N_PAIRS, 2, s), F32)
    pair_bf = pltpu.VMEM((2, tk, PAIR), BF16)
    pair_f = pltpu.VMEM((2, tk, PAIR), F32)
    return _pcall(
        body, name="attn_bwd", grid=(nb, N_PAIRS, nk),
        in_specs=[full, full, kblk, kblk, krows, rows, rows],
        out_specs=[full, kblk, kblk, krows, rows],
        out_shape=[sds(BF16), sds(BF16), sds(BF16), rows_sds, rows_sds],
        scratch_shapes=[pair_bf, pair_bf, pair_bf, pair_f, pltpu.VMEM((2, tk, tq), F32), pltpu.VMEM((2, tk, tq), F32),
                        pltpu.VMEM((2, tk, tq), BF16), pltpu.VMEM((2, tk, tq), BF16), pair_f, pair_f,
                        pltpu.VMEM((s, PAIR), F32), pltpu.VMEM((s, PAIR), F32)],
        dims=("parallel", "parallel", "arbitrary"))(q, do, k, v, c4, lse4, delta4)


def _window_sum(x, g, s, *, ahead):
    row = lax.broadcasted_iota(jnp.int32, x.shape, 0)
    for step in range(N_GROUPS):
        sh = 1 << step
        if ahead:
            moved = jnp.where(row + sh < s, pltpu.roll(x, s - sh, 0), 0.0)
        else:
            moved = jnp.where(row >= sh, pltpu.roll(x, sh, 0), 0.0)
        x = jnp.where(step <= g, x + moved, x)
    return x


def _inv_count(g, s):
    pos = lax.broadcasted_iota(jnp.int32, (s, 1), 0) + 1
    return 1.0 / jnp.minimum(pos, 2 << g).astype(F32)


def _pool_fwd(u, zp, wpool, ps, *, nb):
    t = u.shape[0]
    s = t // nb

    def body(u_ref, zp_ref, w_ref, ps_ref, out_ref, pooled_ref, mixed_ref):
        g = pl.program_id(1)
        x = u_ref[...]
        pooled = (_window_sum(x, g, s, ahead=False) * _inv_count(g, s) - x).astype(BF16)
        mixed = _dot(pooled, w_ref[...])
        z = zp_ref[...]
        pooled_ref[...] = pooled
        mixed_ref[...] = mixed
        out_ref[...] = (mixed * ps_ref[...] * (z * _sigmoid(z))).astype(BF16)

    blk = pl.BlockSpec((s, GROUP_DIM), lambda b, g: (b, g))
    sds = lambda dt: jax.ShapeDtypeStruct((t, D_MODEL), dt)
    return _pcall(
        body, name="pool_fwd", grid=(nb, N_GROUPS),
        in_specs=[blk, blk, pl.BlockSpec((None, GROUP_DIM, GROUP_DIM), lambda b, g: (g, 0, 0)),
                  pl.BlockSpec((1, GROUP_DIM), lambda b, g: (0, g))],
        out_specs=[blk, blk, blk], out_shape=[sds(BF16), sds(BF16), sds(F32)],
        dims=("parallel", "parallel"))(u, zp, wpool, ps)


def _pool_bwd(dpool, mixed, zp, wpool, ps, *, nb):
    t = dpool.shape[0]
    s = t // nb

    def body(dp_ref, mx_ref, zp_ref, w_ref, ps_ref, dmx_ref, du_ref, dzp_ref, dps_ref):
        g = pl.program_id(0)

        @pl.when(pl.program_id(1) == 0)
        def _():
            dps_ref[...] = jnp.zeros_like(dps_ref)

        dp = dp_ref[...]
        mixed = mx_ref[...]
        z = zp_ref[...]
        scale = ps_ref[...]
        sg = _sigmoid(z)
        silu = z * sg
        dmixed = (dp * scale * silu).astype(BF16)
        dmx_ref[...] = dmixed
        dps_ref[...] += jnp.sum(dp * mixed * silu, axis=0, keepdims=True)
        dzp_ref[...] = (dp * mixed * scale * (sg * (1.0 + z * (1.0 - sg)))).astype(BF16)
        dpooled = _dot(dmixed, w_ref[...], NT)
        du = _window_sum(dpooled * _inv_count(g, s), g, s, ahead=True) - dpooled
        du_ref[...] = du.astype(BF16)

    blk = pl.BlockSpec((s, GROUP_DIM), lambda g, b: (b, g))
    sds = lambda dt: jax.ShapeDtypeStruct((t, D_MODEL), dt)
    return _pcall(
        body, name="pool_bwd", grid=(N_GROUPS, nb),
        in_specs=[blk, blk, blk, pl.BlockSpec((None, GROUP_DIM, GROUP_DIM), lambda g, b: (g, 0, 0)),
                  pl.BlockSpec((1, GROUP_DIM), lambda g, b: (0, g))],
        out_specs=[blk, blk, blk, pl.BlockSpec((1, GROUP_DIM), lambda g, b: (0, g))],
        out_shape=[sds(BF16), sds(BF16), sds(BF16), jax.ShapeDtypeStruct((1, D_MODEL), F32)],
        dims=("parallel", "arbitrary"))(dpool, mixed, zp, wpool, ps)


def _pool_wgrad(pooled, dmixed, *, tk=1024):
    t = pooled.shape[0]
    tk = min(tk, t)
    nk = t // tk

    def body(a_ref, b_ref, o_ref):
        @pl.when(pl.program_id(1) == 0)
        def _():
            o_ref[...] = jnp.zeros_like(o_ref)

        o_ref[...] += _dot(a_ref[...], b_ref[...], TN)

    blk = pl.BlockSpec((tk, GROUP_DIM), lambda g, k: (k, g))
    return _pcall(
        body, name="pool_wgrad", grid=(N_GROUPS, nk), in_specs=[blk, blk],
        out_specs=pl.BlockSpec((None, GROUP_DIM, GROUP_DIM), lambda g, k: (g, 0, 0)),
        out_shape=jax.ShapeDtypeStruct((N_GROUPS, GROUP_DIM, GROUP_DIM), F32),
        dims=("parallel", "arbitrary"))(pooled, dmixed)


def _out_block(h, o, za, pool, p, wo, wpg, wpe, gpost, *, tm=256):
    t = h.shape[0]

    def body(h_ref, o_ref, za_ref, pool_ref, p_ref, wo_ref, wpg_ref, wpe_ref, g_ref,
             h2_ref, mix_ref, h1_ref, gate_ref, pe_ref, cat_ref):
        z = za_ref[...]
        a = (o_ref[...] * (z * _sigmoid(z))).astype(BF16)
        pool_v = pool_ref[...]
        cat_ref[:, :D_MODEL] = a
        cat_ref[:, D_MODEL:] = pool_v
        mix = _dot(a, wo_ref[:D_MODEL, :]) + _dot(pool_v, wo_ref[D_MODEL:, :])
        r = lax.rsqrt(jnp.mean(mix * mix, axis=-1, keepdims=True) + EPS)
        h1 = h_ref[...] + mix * r * g_ref[...]
        gate = _sigmoid(_dot(h1.astype(BF16), wpg_ref[...]))
        pe = _dot(p_ref[...].astype(BF16), wpe_ref[...])
        mix_ref[...] = mix
        h1_ref[...] = h1
        gate_ref[...] = gate
        pe_ref[...] = pe
        h2_ref[...] = h1 + gate * pe

    row = lambda i: (i, 0)
    const = lambda i: (0, 0)
    blk = pl.BlockSpec((tm, D_MODEL), row)
    sds = jax.ShapeDtypeStruct((t, D_MODEL), F32)
    return _pcall(
        body, name="out_block", grid=(t // tm,),
        in_specs=[blk, blk, blk, blk, pl.BlockSpec((tm, PLE_DIM), row),
                  pl.BlockSpec((MIX, D_MODEL), const), pl.BlockSpec((D_MODEL, D_MODEL), const),
                  pl.BlockSpec((PLE_DIM, D_MODEL), const), pl.BlockSpec((1, D_MODEL), const)],
        out_specs=[blk] * 5 + [pl.BlockSpec((tm, MIX), row)],
        out_shape=[sds] * 5 + [jax.ShapeDtypeStruct((t, MIX), BF16)],
        dims=("parallel",))(h, o, za, pool, p, wo, wpg, wpe, gpost)


def _loss_grad(y, target, *, tm=512):
    t = y.shape[0]

    def body(y_ref, t_ref, dy_ref, loss_ref):
        @pl.when(pl.program_id(0) == 0)
        def _():
            loss_ref[...] = jnp.zeros_like(loss_ref)

        err = y_ref[...] - t_ref[...]
        dy_ref[...] = err * (1.0 / D_MODEL)
        part = jnp.sum(jnp.sum(err * err, axis=1, keepdims=True), axis=0, keepdims=True)
        loss_ref[...] += part * (0.5 / D_MODEL)

    blk = pl.BlockSpec((tm, D_MODEL), lambda i: (i, 0))
    return _pcall(
        body, name="loss_grad", grid=(t // tm,), in_specs=[blk, blk],
        out_specs=[blk, pl.BlockSpec((8, LANES), lambda i: (0, 0))],
        out_shape=[jax.ShapeDtypeStruct((t, D_MODEL), F32), jax.ShapeDtypeStruct((8, LANES), F32)],
        dims=("arbitrary",))(y, target)


def _out_block_bwd(dh2, gate, pe, mix, o, za, wpg, wo, gpost, gsum, *, tm=256):
    t = dh2.shape[0]

    def body(dh2_ref, gate_ref, pe_ref, mix_ref, o_ref, za_ref, wpg_ref, wo_ref, g_ref, gs_ref,
             dh1_ref, dgp_ref, dpe_ref, dmix_ref, do_ref, dza_ref, dpool_ref, delta_ref, dg_ref):
        @pl.when(pl.program_id(0) == 0)
        def _():
            dg_ref[...] = jnp.zeros_like(dg_ref)

        dh2 = dh2_ref[...]
        gate = gate_ref[...]
        dpe_ref[...] = (dh2 * gate).astype(BF16)
        dgp = (dh2 * pe_ref[...] * gate * (1.0 - gate)).astype(BF16)
        dgp_ref[...] = dgp
        dh1 = dh2 + _dot(dgp, wpg_ref[...], NT)
        dh1_ref[...] = dh1
        mix = mix_ref[...]
        r = lax.rsqrt(jnp.mean(mix * mix, axis=-1, keepdims=True) + EPS)
        dg_ref[...] += jnp.sum(dh1 * mix * r, axis=0, keepdims=True)
        a = dh1 * g_ref[...]
        dmix = (r * a - mix * (r * r * r) * jnp.mean(a * mix, axis=-1, keepdims=True)).astype(BF16)
        dmix_ref[...] = dmix
        dattn = _dot(dmix, wo_ref[:D_MODEL, :], NT)
        dpool_ref[...] = _dot(dmix, wo_ref[D_MODEL:, :], NT)
        z = za_ref[...]
        sg = _sigmoid(z)
        o = o_ref[...]
        do = (dattn * (z * sg)).astype(BF16)
        do_ref[...] = do
        dza_ref[...] = (dattn * o * (sg * (1.0 + z * (1.0 - sg)))).astype(BF16)
        prod = do.astype(F32) * o
        hi = prod.astype(BF16)
        rest = (prod - hi.astype(F32)).astype(BF16)
        delta_ref[...] = _dot(hi, gs_ref[...]) + _dot(rest, gs_ref[...])

    row = lambda i: (i, 0)
    const = lambda i: (0, 0)
    blk = pl.BlockSpec((tm, D_MODEL), row)
    sds = lambda dt: jax.ShapeDtypeStruct((t, D_MODEL), dt)
    return _pcall(
        body, name="out_block_bwd", grid=(t // tm,),
        in_specs=[blk] * 6 + [pl.BlockSpec((D_MODEL, D_MODEL), const), pl.BlockSpec((MIX, D_MODEL), const),
                              pl.BlockSpec((1, D_MODEL), const), pl.BlockSpec((D_MODEL, LANES), const)],
        out_specs=[blk] * 7 + [pl.BlockSpec((tm, LANES), row), pl.BlockSpec((1, D_MODEL), const)],
        out_shape=[sds(F32)] + [sds(BF16)] * 5 + [sds(F32), jax.ShapeDtypeStruct((t, LANES), F32),
                                                 jax.ShapeDtypeStruct((1, D_MODEL), F32)],
        dims=("arbitrary",))(dh2, gate, pe, mix, o, za, wpg, wo, gpost, gsum)


def _in_proj_bwd(dsegs, dfl, wm, wf, h, dh1, gpre, *, tm=512):
    t = h.shape[0]

    def body(*refs):
        seg_refs = refs[:N_SEG]
        dfl_ref, wm_ref, wf_ref, h_ref, dh1_ref, g_ref, dh_ref, dg_ref, acc_ref = refs[N_SEG:]
        i = pl.program_id(0)
        j = pl.program_id(1)

        @pl.when(jnp.logical_and(i == 0, j == 0))
        def _():
            dg_ref[...] = jnp.zeros_like(dg_ref)

        @pl.when(j == 0)
        def _():
            acc_ref[...] = _dot(dfl_ref[...].astype(BF16), wf_ref[...], NT)

        for pair in range(N_SEG // 2):
            @pl.when(j == pair)
            def _(ref_a=seg_refs[2 * pair], ref_b=seg_refs[2 * pair + 1]):
                acc_ref[...] += (_dot(ref_a[...].astype(BF16), wm_ref[:, :D_MODEL], NT)
                                 + _dot(ref_b[...].astype(BF16), wm_ref[:, D_MODEL:], NT))

        @pl.when(j == N_SEG // 2 - 1)
        def _():
            dhn = acc_ref[...]
            x = h_ref[...]
            r = lax.rsqrt(jnp.mean(x * x, axis=-1, keepdims=True) + EPS)
            dg_ref[...] += jnp.sum(dhn * x * r, axis=0, keepdims=True)
            a = dhn * g_ref[...]
            dh_ref[...] = dh1_ref[...] + r * a - x * (r * r * r) * jnp.mean(a * x, axis=-1, keepdims=True)

    row = lambda i, j: (i, 0)
    const = lambda i, j: (0, 0)
    blk = pl.BlockSpec((tm, D_MODEL), row)
    return _pcall(
        body, name="in_proj_bwd", grid=(t // tm, N_SEG // 2),
        in_specs=[blk] * N_SEG + [pl.BlockSpec((tm, LANES), row), pl.BlockSpec((D_MODEL, 2 * D_MODEL), lambda i, j: (0, j)),
                                  pl.BlockSpec((D_MODEL, LANES), const), blk, blk, pl.BlockSpec((1, D_MODEL), const)],
        out_specs=[blk, pl.BlockSpec((1, D_MODEL), const)],
        out_shape=[jax.ShapeDtypeStruct((t, D_MODEL), F32), jax.ShapeDtypeStruct((1, D_MODEL), F32)],
        scratch_shapes=[pltpu.VMEM((tm, D_MODEL), F32)],
        dims=("arbitrary", "arbitrary"))(*dsegs, dfl, wm, wf, h, dh1, gpre)


def _chip_peers(x, y):
    return [(1 - x, y), (x, 1 - y), (1 - x, 1 - y)]


def _ici_views(src, land, gather, x, y, c, pi, px, py):
    if gather:
        return src.at[c], land.at[pi], land.at[pi]
    return src.at[2 * px + py], land.at[2 * x + y], land.at[2 * px + py]


def _ici_start(srcs, lands, after, *, gather, name):
    n = len(srcs)

    def body(*refs):
        src, land = refs[:n], refs[n:2 * n]
        send, recv = refs[2 * n + 1], refs[2 * n + 2]
        token = refs[-1]
        x, y, c = lax.axis_index("x"), lax.axis_index("y"), lax.axis_index("c")
        for kk in range(n):
            for pi, (px, py) in enumerate(_chip_peers(x, y)):
                mine, there, _ = _ici_views(src[kk], land[kk], gather, x, y, c, pi, px, py)
                pltpu.make_async_remote_copy(src_ref=mine, dst_ref=there, send_sem=send.at[3 * kk + pi],
                                             recv_sem=recv.at[3 * kk + pi], device_id=(px, py, c),
                                             device_id_type=MESH).start()
        token[...] = jnp.zeros_like(token)

    hbm = lambda a: pltpu.with_memory_space_constraint(a, pltpu.HBM)
    outs = pl.pallas_call(
        body, name=name,
        out_shape=(pltpu.SemaphoreType.DMA((3 * n,)), pltpu.SemaphoreType.DMA((3 * n,)),
                   *[pltpu.HBM(a.shape, a.dtype) for a in srcs], *[pltpu.HBM(a.shape, a.dtype) for a in lands],
                   jax.ShapeDtypeStruct((8, LANES), F32)),
        in_specs=[HBM_SPEC] * (2 * n) + [ANY],
        out_specs=(SEM_SPEC, SEM_SPEC, *[HBM_SPEC] * (2 * n), pl.BlockSpec(memory_space=pltpu.VMEM)),
        input_output_aliases={i: 2 + i for i in range(2 * n)},
        compiler_params=pltpu.CompilerParams(has_side_effects=SIDE_EFFECT),
    )(*[hbm(a) for a in srcs], *[hbm(a) for a in lands], after)
    return outs[0], outs[1], list(outs[2:2 + n]), list(outs[2 + n:2 + 2 * n]), outs[-1]


def _ici_wait(send, recv, srcs, lands, after, *, gather, name):
    n = len(srcs)

    def body(*refs):
        src, land = refs[:n], refs[n:2 * n]
        send_ref, recv_ref = refs[2 * n], refs[2 * n + 1]
        x, y, c = lax.axis_index("x"), lax.axis_index("y"), lax.axis_index("c")
        for kk in range(n):
            for pi, (px, py) in enumerate(_chip_peers(x, y)):
                mine, _, here = _ici_views(src[kk], land[kk], gather, x, y, c, pi, px, py)
                cp = pltpu.make_async_remote_copy(src_ref=mine, dst_ref=here, send_sem=send_ref.at[3 * kk + pi],
                                                  recv_sem=recv_ref.at[3 * kk + pi], device_id=(px, py, c),
                                                  device_id_type=MESH)
                cp.wait_send()
                cp.wait_recv()

    outs = pl.pallas_call(
        body, name=name,
        out_shape=[pltpu.HBM(a.shape, a.dtype) for a in srcs] + [pltpu.HBM(a.shape, a.dtype) for a in lands],
        in_specs=[HBM_SPEC] * (2 * n) + [SEM_SPEC, SEM_SPEC] + [ANY] * len(after), out_specs=[HBM_SPEC] * (2 * n),
        input_output_aliases={i: i for i in range(2 * n)},
        compiler_params=pltpu.CompilerParams(has_side_effects=SIDE_EFFECT),
    )(*srcs, *lands, send, recv, *after)
    return list(outs[:n]), list(outs[n:])


def _sibling_send(arrays, *, name):
    n = len(arrays)

    def body(*refs):
        ins, outs = refs[:n], refs[n:2 * n]
        send, recv = refs[2 * n:]
        sib = (lax.axis_index("x"), lax.axis_index("y"), 1 - lax.axis_index("c"))
        copies = [pltpu.make_async_remote_copy(src_ref=ins[kk], dst_ref=outs[kk], send_sem=send.at[kk],
                                               recv_sem=recv.at[kk], device_id=sib, device_id_type=MESH)
                  for kk in range(n)]
        for cp in copies:
            cp.start()
        for cp in copies:
            cp.wait_recv()
        for cp in copies:
            cp.wait_send()

    return _pcall(
        body, name=name, in_specs=[ANY] * n, out_specs=[ANY] * n,
        out_shape=[jax.ShapeDtypeStruct(a.shape, a.dtype) for a in arrays],
        scratch_shapes=[pltpu.SemaphoreType.DMA((n,)), pltpu.SemaphoreType.DMA((n,))])(*arrays)


def _all_gather_small(a, *, name):
    def body(a_ref, o_ref, send, recv, loc):
        x, y, c = lax.axis_index("x"), lax.axis_index("y"), lax.axis_index("c")
        me = 4 * x + 2 * y + c

        def peer(rel):
            fx, fy, fc = (rel >> 2) & 1, (rel >> 1) & 1, rel & 1
            px = (1 - x) if fx else x
            py = (1 - y) if fy else y
            pc = (1 - c) if fc else c
            return px, py, pc

        def remote(rel, slot):
            return pltpu.make_async_remote_copy(
                src_ref=a_ref, dst_ref=o_ref.at[slot], send_sem=send.at[rel - 1], recv_sem=recv.at[rel - 1],
                device_id=peer(rel), device_id_type=MESH)

        mine = pltpu.make_async_copy(a_ref, o_ref.at[me], loc)
        mine.start()
        sends = [remote(rel, me) for rel in range(1, 8)]
        for cp in sends:
            cp.start()
        for rel in range(1, 8):
            px, py, pc = peer(rel)
            remote(rel, 4 * px + 2 * py + pc).wait_recv()
        for cp in sends:
            cp.wait_send()
        mine.wait()

    return _pcall(
        body, name=name, in_specs=[ANY], out_specs=ANY,
        out_shape=jax.ShapeDtypeStruct((8,) + a.shape, a.dtype),
        scratch_shapes=[pltpu.SemaphoreType.DMA((7,)), pltpu.SemaphoreType.DMA((7,)), pltpu.SemaphoreType.DMA])(a)


def _to_bf16(a, *, name, tr=256):
    rows, cols = a.shape

    def body(a_ref, o_ref):
        o_ref[...] = a_ref[...].astype(BF16)

    blk = pl.BlockSpec((tr, cols), lambda i: (i, 0))
    return _pcall(body, name=name, grid=(rows // tr,), in_specs=[blk], out_specs=blk,
                  out_shape=jax.ShapeDtypeStruct(a.shape, BF16), dims=("parallel",))(a)


def _pair_sum(a, b, *, name, tr=256):
    _, rows, cols = a.shape
    tr = min(tr, rows)

    def body(a_ref, b_ref, o_ref):
        o_ref[...] = (a_ref[...].astype(F32) + b_ref[...].astype(F32)).astype(BF16)

    blk = pl.BlockSpec((None, tr, cols), lambda j, i: (j, i, 0))
    return _pcall(body, name=name, grid=(4, rows // tr), in_specs=[blk, blk], out_specs=blk,
                  out_shape=jax.ShapeDtypeStruct(a.shape, BF16), dims=("parallel", "parallel"))(a, b)


def _chip_sum(own, recv, chip, *, name, tr=256):
    _, rows, cols = own.shape
    tr = min(tr, rows)

    def body(chip_ref, own_ref, r1_ref, r2_ref, r3_ref, o_ref):
        acc = own_ref[...].astype(F32)
        for ref in (r1_ref, r2_ref, r3_ref):
            acc = acc + ref[...].astype(F32)
        o_ref[...] = acc

    def slot(step):
        return pl.BlockSpec((None, tr, cols), lambda i, chip_ref: ((chip_ref[0] + step) % 4, i, 0))

    return pl.pallas_call(
        body, name=name, out_shape=jax.ShapeDtypeStruct((rows, cols), F32),
        grid_spec=pltpu.PrefetchScalarGridSpec(
            num_scalar_prefetch=1, grid=(rows // tr,), in_specs=[slot(0), slot(1), slot(2), slot(3)],
            out_specs=pl.BlockSpec((tr, cols), lambda i, chip_ref: (i, 0))),
        compiler_params=pltpu.CompilerParams(dimension_semantics=("parallel",), vmem_limit_bytes=VMEM_LIMIT),
    )(chip, own, recv, recv, recv)


def _adamw_math(w, g, m, v):
    m = ADAM_B1 * m + (1.0 - ADAM_B1) * g
    v = ADAM_B2 * v + (1.0 - ADAM_B2) * (g * g)
    m_hat = m / (1.0 - ADAM_B1 ** ADAM_STEP)
    v_hat = v / (1.0 - ADAM_B2 ** ADAM_STEP)
    delta = -ADAM_LR * (m_hat / (jnp.sqrt(v_hat) + ADAM_EPS) + ADAM_WD * w)
    return delta, m, v


def _adamw(w, m, v, g, layer, prev, *, name, tr=128):
    rows, cols = g.shape
    tr = min(tr, rows)
    nblk = rows // tr

    def body(w_ref, m_ref, v_ref, gin_ref, *refs):
        g_ref, d_ref, mo_ref, vo_ref = refs[-4:]
        grad = gin_ref[...]
        delta, m_new, v_new = _adamw_math(w_ref[...], grad, m_ref[...], v_ref[...])
        g_ref[...] = grad
        d_ref[...] = delta
        mo_ref[...] = m_new
        vo_ref[...] = v_new

    blk = pl.BlockSpec((tr, cols), lambda i: (layer * nblk + i, 0))
    carried = [] if prev is None else list(prev)
    return pl.pallas_call(
        body, name=name, grid=(nblk,), out_shape=[jax.ShapeDtypeStruct(w.shape, F32)] * 4,
        in_specs=[blk] * 3 + [pl.BlockSpec((tr, cols), lambda i: (i, 0))] + [ANY] * len(carried), out_specs=[blk] * 4,
        input_output_aliases={4 + i: i for i in range(len(carried))},
        compiler_params=pltpu.CompilerParams(dimension_semantics=("parallel",), vmem_limit_bytes=VMEM_LIMIT),
    )(w, m, v, g, *carried)


def _adamw_small(w, m, v, parts, *, name):
    def body(w_ref, m_ref, v_ref, p_ref, g_ref, d_ref, mo_ref, vo_ref):
        g = p_ref[0]
        for dev in range(1, 8):
            g = g + p_ref[dev]
        delta, m_new, v_new = _adamw_math(w_ref[...], g, m_ref[...], v_ref[...])
        g_ref[...] = g
        d_ref[...] = delta
        mo_ref[...] = m_new
        vo_ref[...] = v_new

    return _pcall(body, name=name, out_shape=[jax.ShapeDtypeStruct(w.shape, F32)] * 4)(w, m, v, parts)


def _column_parts(segments, lo, hi):
    parts, start = [], 0
    for seg in segments:
        a, b = max(lo, start), min(hi, start + seg.shape[1])
        if a < b:
            parts.append(seg[:, a - start:b - start])
        start += seg.shape[1]
    return parts


def _rows_from_heads(a, nb, s):
    return a.reshape(nb, s, N_HEADS).transpose(0, 2, 1).reshape(nb, N_PAIRS, 2, s)


def _layer_fwd(h, p_l, wts, nb):
    t = h.shape[0]
    s = t // nb
    q, k, v, za, u, zp, fl, hn = _norm_in_proj(h, wts["gpre"], wts["wm"], wts["wf"])
    flt = fl[:, :N_HEADS].reshape(nb, s, N_HEADS).transpose(0, 2, 1)
    c = _gate_cumsum(flt, wts["bf"])
    o, lse = _attn_fwd(q, k, v, c.reshape(nb, N_PAIRS, 2, s), nb=nb)
    pool, pooled, mixed = _pool_fwd(u, zp, wts["wpool"], wts["ps"], nb=nb)
    h2, mix, h1, gate, pe, cat = _out_block(h, o, za, pool, p_l, wts["wo"], wts["wpg"], wts["wpe"], wts["gpost"])
    saved = dict(h=h, hn=hn, q=q, k=k, v=v, za=za, zp=zp, flt=flt, c=c, o=o, lse=lse, pooled=pooled, mixed=mixed,
                 mix=mix, h1=h1, gate=gate, pe=pe, cat=cat, p=p_l)
    return h2, saved


def _layer_bwd(dh2, sv, wts, gsum, nb):
    t = dh2.shape[0]
    s = t // nb
    dh1, dgp, dpe, dmix, do, dza, dpool, delta, dgpost = _out_block_bwd(
        dh2, sv["gate"], sv["pe"], sv["mix"], sv["o"], sv["za"], wts["wpg"], wts["wo"], wts["gpost"], gsum)
    g_wpe = _matmul(sv["p"], dpe, ta=True, name="wgrad_pe")
    g_wpg = _matmul(sv["h1"], dgp, ta=True, name="wgrad_pg")
    g_wo = _matmul(sv["cat"], dmix, ta=True, name="wgrad_out")

    delta4 = _rows_from_heads(delta[:, :N_HEADS], nb, s)
    dq, dk, dv, dcol, drow = _attn_bwd(sv["q"], sv["k"], sv["v"], do, sv["c"].reshape(nb, N_PAIRS, 2, s), sv["lse"],
                                       delta4, nb=nb)
    dflt, dbf = _gate_cumsum_bwd(dcol.reshape(nb, N_HEADS, s), drow.reshape(nb, N_HEADS, s), sv["flt"], wts["bf"])
    dfl = jnp.pad(dflt.transpose(0, 2, 1).reshape(t, N_HEADS), ((0, 0), (0, LANES - N_HEADS)))

    dmixed, du, dzp, dps = _pool_bwd(dpool, sv["mixed"], sv["zp"], wts["wpool"], wts["ps"], nb=nb)
    g_wpool = _pool_wgrad(sv["pooled"], dmixed)

    dsegs = (dq, dk, dv, dza, du, dzp)
    dh, dgpre = _in_proj_bwd(dsegs, dfl, wts["wm"], wts["wf"], sv["h"], dh1, wts["gpre"])
    g_segs = [_matmul(sv["hn"], dx, ta=True, name="wgrad_in") for dx in dsegs]
    g_wf = _matmul(sv["hn"], dfl, ta=True, name="wgrad_in_f")
    g_win = g_segs[:4] + [g_wf[:, :N_HEADS]] + g_segs[4:]
    grads = dict(w_in=g_win, w_out=g_wo, w_pg=g_wpg, w_pe=g_wpe, w_pool=g_wpool,
                 norm_pre=dgpre[0], norm_post=dgpost[0], pool_scale=dps[0], b_f=dbf[:, 0])
    return dh, grads


def kernel(x, p, norm_pre, norm_post, w_in, b_f, w_pool, pool_scale, w_out, w_pg, w_pe, loss_target, m_norm_pre, m_norm_post, m_w_in, m_b_f, m_w_pool, m_pool_scale, m_w_out, m_w_pg, m_w_pe, v_norm_pre, v_norm_post, v_w_in, v_b_f, v_w_pool, v_pool_scale, v_w_out, v_w_pg, v_w_pe):
    nb, s, _ = x.shape
    t = nb * s
    depth = w_in.shape[0]
    big = dict(w_in=(w_in, m_w_in, v_w_in), w_out=(w_out, m_w_out, v_w_out), w_pg=(w_pg, m_w_pg, v_w_pg),
               w_pe=(w_pe, m_w_pe, v_w_pe), w_pool=(w_pool, m_w_pool, v_w_pool))
    names = list(big)
    cols = {n: big[n][0].shape[-1] for n in names}
    chip = (2 * lax.axis_index("x") + lax.axis_index("y")).astype(jnp.int32).reshape(1)
    core = lax.axis_index("c")
    south = core == 0

    gathers = []
    token = jnp.zeros((8, LANES), F32)
    narrow = {n: big[n][0].astype(BF16) for n in names if n != "w_in"}
    narrow["w_in"] = _to_bf16(w_in.reshape(-1, cols["w_in"]), name="w_in_to_bf16").reshape(w_in.shape)
    for l in range(depth):
        halves = [narrow[n][l].reshape(2, -1, cols[n]) for n in names]
        lands = [lax.empty((3,) + a.shape[1:], a.dtype) for a in halves]
        send, recv, halves, lands, token = _ici_start(halves, lands, token, gather=True, name=f"gather_start_{l}")
        gathers.append((send, recv, halves, lands))

    def layer_weights(l, after, anchor):
        send, recv, halves, lands = gathers[l]
        halves, lands = _ici_wait(send, recv, halves, lands, after, gather=True, name=f"gather_wait_{l}")
        others = _sibling_send(lands, name="gather_pass_on")
        full = {}
        for n, own, land, other in zip(names, halves, lands, others):
            low = jnp.where(south, land, other)
            high = jnp.where(south, other, land)
            rel = [jnp.concatenate([low[pi], high[pi]], axis=0) for pi in range(3)]
            by_rel = jnp.stack([own.reshape(-1, cols[n]), rel[1], rel[0], rel[2]])
            full[n] = [lax.dynamic_index_in_dim(by_rel, j ^ chip[0], 0, keepdims=False) for j in range(4)]
        win = full["w_in"]
        wm = jnp.concatenate(_column_parts(win, 0, F_OFF) + _column_parts(win, F_OFF + N_HEADS, IN_COLS), axis=1)
        wf = jnp.pad(jnp.concatenate(_column_parts(win, F_OFF, F_OFF + N_HEADS), axis=1), ((0, 0), (0, LANES - N_HEADS)))
        return dict(
            wm=wm, wf=wf, wo=jnp.concatenate(full["w_out"], axis=0), wpg=jnp.concatenate(full["w_pg"], axis=0),
            wpe=jnp.concatenate(full["w_pe"], axis=1),
            wpool=jnp.stack(full["w_pool"]).reshape(4, N_GROUPS, GROUP_DIM // 4, GROUP_DIM).transpose(1, 0, 2, 3).reshape(
                N_GROUPS, GROUP_DIM, GROUP_DIM),
            gpre=norm_pre[l][None] + anchor, gpost=norm_post[l][None], ps=pool_scale[l][None], bf=b_f[l][:, None])

    h = x.reshape(t, D_MODEL)
    saved, layers = [], []
    m_w_in, v_w_in = m_w_in + token[0, 0], v_w_in + token[0, 0]
    big["w_in"] = (w_in, m_w_in, v_w_in)
    after = [token, m_w_in.reshape(-1, cols["w_in"]), v_w_in.reshape(-1, cols["w_in"])]
    for l in range(depth):
        layers.append(layer_weights(l, after, token[0, 0]))
        h, sv = _layer_fwd(h, p[l].reshape(t, PLE_DIM), layers[l], nb)
        saved.append(sv)
        after = [h]
    dh, loss_blk = _loss_grad(h, loss_target.reshape(t, D_MODEL))
    loss = lax.psum(loss_blk[0, 0], ("x", "y", "c"))
    gsum = (jnp.arange(D_MODEL)[:, None] // HEAD_DIM == jnp.arange(LANES)[None, :]).astype(BF16)

    def pieces(g, name):
        if name == "w_in":
            by_chip = jnp.stack([jnp.concatenate(_column_parts(g, j * cols[name], (j + 1) * cols[name]), axis=1)
                                 for j in range(4)])
        elif name == "w_pe":
            by_chip = jnp.stack([g[:, j * cols[name]:(j + 1) * cols[name]] for j in range(4)])
        elif name == "w_pool":
            by_chip = g.reshape(N_GROUPS, 4, GROUP_DIM // 4, GROUP_DIM).transpose(1, 0, 2, 3)
        else:
            by_chip = g
        return by_chip.reshape(4, 2, -1, cols[name]).transpose(1, 0, 2, 3).astype(BF16)

    def scatter_start(l, g):
        mine, sent = [], []
        for n in names:
            halves = pieces(g[n], n)
            mine.append(lax.dynamic_index_in_dim(halves, core, 0, keepdims=False))
            sent.append(lax.dynamic_index_in_dim(halves, 1 - core, 0, keepdims=False))
        theirs = _sibling_send(sent, name="presum_exchange")
        sums = [_pair_sum(a, b, name="presum_" + n) for n, a, b in zip(names, mine, theirs)]
        lands = [lax.empty(a.shape, a.dtype) for a in sums]
        return _ici_start(sums, lands, token, gather=False, name=f"scatter_start_{l}")

    def scatter_finish(l, state, after):
        send, recv, sums, lands, _ = state
        sums, lands = _ici_wait(send, recv, sums, lands, [after], gather=False, name=f"scatter_wait_{l}")
        mine = [_chip_sum(a, r, chip, name="chip_sum_" + n) for n, a, r in zip(names, sums, lands)]
        theirs = _sibling_send(mine, name="halves_exchange")
        return {n: jnp.concatenate([jnp.where(south, a, b), jnp.where(south, b, a)], axis=0)
                for n, a, b in zip(names, mine, theirs)}

    grads = [None] * depth
    big_out = {n: None for n in names}

    def update(l, state, after):
        reduced = scatter_finish(l, state, after)
        for n in names:
            w, m, v = big[n]
            big_out[n] = _adamw(w.reshape(-1, cols[n]), m.reshape(-1, cols[n]), v.reshape(-1, cols[n]), reduced[n], l,
                                big_out[n], name="adamw_" + n)

    pending = None
    for l in reversed(range(depth)):
        wts = layers[l]
        if pending is not None:
            wts = dict(wts, gpost=wts["gpost"] + pending[1][4][0, 0])
        dh, grads[l] = _layer_bwd(dh, saved[l], wts, gsum, nb)
        state = scatter_start(l, grads[l])
        if pending is not None:
            update(pending[0], pending[1], state[4])
        pending = (l, state)
    grad_x = dh.reshape(nb, s, D_MODEL)

    small = dict(norm_pre=(norm_pre, m_norm_pre, v_norm_pre), norm_post=(norm_post, m_norm_post, v_norm_post),
                 pool_scale=(pool_scale, m_pool_scale, v_pool_scale), b_f=(b_f, m_b_f, v_b_f))

    def pack(get):
        rows = []
        for n in small:
            a = get(n)
            rows.append(jnp.pad(a, ((0, 0), (0, D_MODEL - a.shape[1]))))
        return jnp.concatenate(rows, axis=0)

    parts = _all_gather_small(pack(lambda n: jnp.stack([grads[l][n] for l in range(depth)])), name="gather_small")
    small_packed = _adamw_small(pack(lambda n: small[n][0]), pack(lambda n: small[n][1]), pack(lambda n: small[n][2]),
                                parts, name="adamw_small")
    small_out = {}
    for i, n in enumerate(small):
        width = small[n][0].shape[1]
        small_out[n] = [o[depth * i:depth * (i + 1), :width] for o in small_packed]

    update(pending[0], pending[1], big_out[names[-1]][0] if depth > 1 else small_packed[0])
    big_out = {n: [o.reshape(big[n][0].shape) for o in outs] for n, outs in big_out.items()}

    order =["norm_pre", "norm_post", "w_in", "b_f", "w_pool", "pool_scale", "w_out", "w_pg", "w_pe"]
    result = [loss, grad_x]
    for kind in range(4):
        for n in order:
            result.append(big_out[n][kind] if n in big_out else small_out[n][kind])
    return tuple(result)
```

```python
import functools

import jax
import jax.numpy as jnp
from jax import lax
from jax.experimental import pallas as pl
from jax.experimental.pallas import tpu as pltpu

F32 = jnp.float32
BF16 = jnp.bfloat16

D_MODEL = 1024
N_HEADS = 16
HEAD_DIM = 64
PAIR = 2 * HEAD_DIM
N_PAIRS = N_HEADS // 2
N_GROUPS = 4
GROUP_DIM = 256
PLE_DIM = 256
MIX = 2 * D_MODEL
N_SEG = 6
F_OFF = 4 * D_MODEL
IN_COLS = N_SEG * D_MODEL + N_HEADS
LANES = 128
EPS = 1e-6
SCALE = 0.125
NEG = -1e30

ADAM_LR = 0.001
ADAM_B1 = 0.9
ADAM_B2 = 0.999
ADAM_EPS = 1e-08
ADAM_WD = 0.01
ADAM_STEP = 10

VMEM_LIMIT = 56 * 1024 * 1024
MESH = pl.DeviceIdType.MESH
ANY = pl.BlockSpec(memory_space=pl.ANY)
HBM_SPEC = pl.BlockSpec(memory_space=pltpu.HBM)
SEM_SPEC = pl.BlockSpec(memory_space=pltpu.SEMAPHORE)
SIDE_EFFECT = pltpu.SideEffectType.DATAFLOW_SIDE_EFFECTING

NT = (((1,), (1,)), ((), ()))
TN = (((0,), (0,)), ((), ()))
NN = (((1,), (0,)), ((), ()))


def _pcall(body, *, name, out_shape, grid=(), in_specs=None, out_specs=None, scratch_shapes=(), dims=None):
    kw = {}
    if in_specs is not None:
        kw["in_specs"] = in_specs
    if out_specs is not None:
        kw["out_specs"] = out_specs
    return pl.pallas_call(
        body, name=name, out_shape=out_shape, grid=grid, scratch_shapes=list(scratch_shapes),
        compiler_params=pltpu.CompilerParams(dimension_semantics=dims, vmem_limit_bytes=VMEM_LIMIT), **kw)


def _dot(a, b, dn=NN):
    return lax.dot_general(a, b, dn, preferred_element_type=F32)


def _sigmoid(x):
    return 1.0 / (1.0 + jnp.exp(-x))


def _matmul(a, b, *, ta=False, name, tm=512, tn=1024, tk=2048):
    if ta:
        kdim, m = a.shape
    else:
        m, kdim = a.shape
    n = b.shape[1]
    tm, tn, tk = min(tm, m), min(tn, n), min(tk, kdim)
    nk = kdim // tk

    def body(a_ref, b_ref, o_ref, acc_ref):
        k = pl.program_id(2)

        @pl.when(k == 0)
        def _():
            acc_ref[...] = jnp.zeros_like(acc_ref)

        acc_ref[...] += _dot(a_ref[...].astype(BF16), b_ref[...].astype(BF16), TN if ta else NN)

        @pl.when(k == nk - 1)
        def _():
            o_ref[...] = acc_ref[...].astype(BF16)

    a_spec = pl.BlockSpec((tk, tm), lambda i, j, k: (k, i)) if ta else pl.BlockSpec((tm, tk), lambda i, j, k: (i, k))
    return _pcall(
        body, name=name, out_shape=jax.ShapeDtypeStruct((m, n), BF16), grid=(m // tm, n // tn, nk),
        in_specs=[a_spec, pl.BlockSpec((tk, tn), lambda i, j, k: (k, j))],
        out_specs=pl.BlockSpec((tm, tn), lambda i, j, k: (i, j)),
        scratch_shapes=[pltpu.VMEM((tm, tn), F32)], dims=("parallel", "parallel", "arbitrary"))(a, b)


def _norm_in_proj(h, g, wm, wf, *, tm=512):
    t = h.shape[0]

    def body(h_ref, g_ref, wm_ref, wf_ref, q_ref, k_ref, v_ref, za_ref, u_ref, zp_ref, fl_ref, hn_ref):
        j = pl.program_id(1)

        @pl.when(j == 0)
        def _():
            x = h_ref[...]
            r = lax.rsqrt(jnp.mean(x * x, axis=-1, keepdims=True) + EPS)
            hn = (x * r * g_ref[...]).astype(BF16)
            hn_ref[...] = hn
            fl_ref[...] = _dot(hn, wf_ref[...])

        y = _dot(hn_ref[...], wm_ref[...])
        for s, (ref_a, ref_b) in enumerate(((q_ref, k_ref), (v_ref, za_ref), (u_ref, zp_ref))):
            @pl.when(j == s)
            def _(ref_a=ref_a, ref_b=ref_b):
                ref_a[...] = y[:, :D_MODEL].astype(ref_a.dtype)
                ref_b[...] = y[:, D_MODEL:].astype(ref_b.dtype)

    row = lambda i, j: (i, 0)
    blk = pl.BlockSpec((tm, D_MODEL), row)
    sds = lambda dt: jax.ShapeDtypeStruct((t, D_MODEL), dt)
    return _pcall(
        body, name="norm_in_proj", grid=(t // tm, N_SEG // 2),
        in_specs=[blk, pl.BlockSpec((1, D_MODEL), lambda i, j: (0, 0)),
                  pl.BlockSpec((D_MODEL, 2 * D_MODEL), lambda i, j: (0, j)),
                  pl.BlockSpec((D_MODEL, LANES), lambda i, j: (0, 0))],
        out_specs=[blk] * 6 + [pl.BlockSpec((tm, LANES), row), blk],
        out_shape=[sds(BF16)] * 3 + [sds(F32)] * 3 + [jax.ShapeDtypeStruct((t, LANES), F32), sds(BF16)],
        dims=("parallel", "arbitrary"))(h, g, wm, wf)


def _gate_cumsum(flt, bias):
    b, hh, s = flt.shape

    def body(fl_ref, b_ref, c_ref):
        x = fl_ref[...] + b_ref[...]
        acc = jnp.minimum(x, 0.0) - jnp.log(1.0 + jnp.exp(-jnp.abs(x)))
        idx = lax.broadcasted_iota(jnp.int32, x.shape, 1)
        sh = 1
        while sh < s:
            acc = acc + jnp.where(idx >= sh, pltpu.roll(acc, sh, 1), 0.0)
            sh *= 2
        c_ref[...] = acc

    return _pcall(
        body, name="gate_cumsum", grid=(b,),
        in_specs=[pl.BlockSpec((None, hh, s), lambda i: (i, 0, 0)), pl.BlockSpec((hh, 1), lambda i: (0, 0))],
        out_specs=pl.BlockSpec((None, hh, s), lambda i: (i, 0, 0)),
        out_shape=jax.ShapeDtypeStruct((b, hh, s), F32), dims=("parallel",))(flt, bias)


def _gate_cumsum_bwd(dcol, drow, flt, bias):
    b, hh, s = flt.shape

    def body(d_ref, dr_ref, fl_ref, b_ref, dfl_ref, db_ref):
        @pl.when(pl.program_id(0) == 0)
        def _():
            db_ref[...] = jnp.zeros_like(db_ref)

        x = fl_ref[...] + b_ref[...]
        acc = dr_ref[...] - d_ref[...]
        idx = lax.broadcasted_iota(jnp.int32, x.shape, 1)
        sh = 1
        while sh < s:
            acc = acc + jnp.where(idx + sh < s, pltpu.roll(acc, s - sh, 1), 0.0)
            sh *= 2
        e = jnp.exp(-jnp.abs(x))
        sig_neg = jnp.where(x >= 0.0, e, 1.0) / (1.0 + e)
        dfl = acc * sig_neg
        dfl_ref[...] = dfl
        db_ref[...] += jnp.sum(dfl, axis=1, keepdims=True)

    return _pcall(
        body, name="gate_cumsum_bwd", grid=(b,),
        in_specs=[pl.BlockSpec((None, hh, s), lambda i: (i, 0, 0))] * 3 + [pl.BlockSpec((hh, 1), lambda i: (0, 0))],
        out_specs=[pl.BlockSpec((None, hh, s), lambda i: (i, 0, 0)), pl.BlockSpec((hh, 1), lambda i: (0, 0))],
        out_shape=[jax.ShapeDtypeStruct((b, hh, s), F32), jax.ShapeDtypeStruct((hh, 1), F32)],
        dims=("arbitrary",))(dcol, drow, flt, bias)


def _attn_fwd(q, k, v, c4, *, nb, tq=512, rc=32, diag=256):
    t = q.shape[0]
    s = t // nb
    tq = min(tq, s)
    nq = s // tq
    tk = tq
    diag = min(diag, tq)

    def body(q_ref, k_ref, v_ref, c_ref, o_ref, lse_ref, qh_scr, s_scr, p_scr, m_scr, acc_scr):
        qi = pl.program_id(2)
        lo = lax.broadcasted_iota(jnp.int32, (tq, PAIR), 1) < HEAD_DIM
        q2 = q_ref[...] * SCALE
        zero = jnp.zeros_like(q2)
        qh_scr[0] = jnp.where(lo, q2, zero)
        qh_scr[1] = jnp.where(lo, zero, q2)
        m_scr[...] = jnp.full(m_scr.shape, NEG, F32)
        acc_scr[...] = jnp.zeros(acc_scr.shape, F32)
        whole = [(0, tq, tk, False)]
        diagonal = [(r0, r0 + diag, r0 + diag, True) for r0 in range(0, tq, diag)]

        def block(kj, bands):
            off = pl.multiple_of(kj * tk, tk)
            k2 = k_ref[pl.ds(off, tk), :]
            v2 = v_ref[pl.ds(off, tk), :]
            one = jnp.ones_like(v2)
            va = (jnp.where(lo, v2, one), jnp.where(lo, one, v2))
            crow = c_ref[:, pl.ds(off, tk)]
            for hd in range(2):
                for r0, r1, nc, _ in bands:
                    s_scr[hd, r0:r1, :nc] = _dot(qh_scr[hd, r0:r1, :], k2[:nc], NT)
            for hd in range(2):
                for r0, r1, nc, masked in bands:
                    row = lax.broadcasted_iota(jnp.int32, (rc, nc), 0)
                    col = lax.broadcasted_iota(jnp.int32, (rc, nc), 1)
                    for r in range(r0, r1, rc):
                        sc = s_scr[hd, r:r + rc, :nc] - crow[hd:hd + 1, :nc]
                        if masked:
                            sc = jnp.where(row + r >= col, sc, NEG)
                        m_old = m_scr[hd, r:r + rc, :]
                        m_new = jnp.maximum(m_old, jnp.max(sc, axis=1, keepdims=True))
                        for cb in range(0, nc, LANES):
                            p_scr[hd, r:r + rc, cb:cb + LANES] = jnp.exp(sc[:, cb:cb + LANES] - m_new).astype(BF16)
                        m_scr[hd, r:r + rc, :] = m_new
                        acc_scr[hd, r:r + rc, :] = acc_scr[hd, r:r + rc, :] * jnp.exp(m_old - m_new)
                    acc_scr[hd, r0:r1, :] += _dot(p_scr[hd, r0:r1, :nc], va[hd][:nc])

        def below(kj, carry):
            block(kj, whole)
            return carry

        lax.fori_loop(0, qi, below, 0)
        block(qi, diagonal)
        a0, a1 = acc_scr[0], acc_scr[1]
        den = jnp.where(lo, pltpu.roll(a0, HEAD_DIM, 1), pltpu.roll(a1, HEAD_DIM, 1))
        o_ref[...] = jnp.where(lo, a0, a1) / den
        lse_t = (jnp.where(lo, m_scr[0], m_scr[1]) + jnp.log(den)).T
        lse_ref[0:1, :] = lse_t[0:1, :]
        lse_ref[1:2, :] = lse_t[HEAD_DIM:HEAD_DIM + 1, :]

    qspec = pl.BlockSpec((tq, PAIR), lambda b, hp, i: (b * nq + i, hp))
    kvspec = pl.BlockSpec((s, PAIR), lambda b, hp, i: (b, hp))
    return _pcall(
        body, name="attn_fwd", grid=(nb, N_PAIRS, nq),
        in_specs=[qspec, kvspec, kvspec, pl.BlockSpec((None, None, 2, s), lambda b, hp, i: (b, hp, 0, 0))],
        out_specs=[qspec, pl.BlockSpec((None, None, 2, tq), lambda b, hp, i: (b, hp, 0, i))],
        out_shape=[jax.ShapeDtypeStruct((t, D_MODEL), F32), jax.ShapeDtypeStruct((nb, N_PAIRS, 2, s), F32)],
        scratch_shapes=[pltpu.VMEM((2, tq, PAIR), BF16), pltpu.VMEM((2, tq, tk), F32), pltpu.VMEM((2, tq, tk), BF16),
                        pltpu.VMEM((2, tq, LANES), F32), pltpu.VMEM((2, tq, PAIR), F32)],
        dims=("parallel", "parallel", "arbitrary"))(q, k, v, c4)


def _attn_bwd(q, k, v, do, c4, lse4, delta4, *, nb, tk=512, rc=16, diag=256):
    t = q.shape[0]
    s = t // nb
    tk = min(tk, s)
    nk = s // tk
    tq = tk
    diag = min(diag, tk)

    def body(q_ref, do_ref, k_ref, v_ref, c_ref, lse_ref, dl_ref, dq_ref, dk_ref, dv_ref, dc_ref, dr_ref,
             kz_scr, vz_scr, ko_scr, crep_scr, st_scr, dp_scr, pt_scr, ds_scr, dk_scr, dv_scr, dq_scr, dr_scr):
        kj = pl.program_id(2)

        @pl.when(kj == 0)
        def _():
            dq_scr[...] = jnp.zeros_like(dq_scr)
            dr_scr[...] = jnp.zeros_like(dr_scr)

        lo = lax.broadcasted_iota(jnp.int32, (tk, PAIR), 1) < HEAD_DIM
        k2 = k_ref[...]
        v2 = v_ref[...]
        zero = jnp.zeros_like(k2)
        one = jnp.ones_like(k2)
        kz_scr[0] = jnp.where(lo, k2, zero)
        kz_scr[1] = jnp.where(lo, zero, k2)
        vz_scr[0] = jnp.where(lo, v2, zero)
        vz_scr[1] = jnp.where(lo, zero, v2)
        ko_scr[0] = jnp.where(lo, k2, one)
        ko_scr[1] = jnp.where(lo, one, k2)
        for hd in range(2):
            crep_scr[hd] = jnp.broadcast_to(c_ref[hd:hd + 1, :], (LANES, tk)).T
        dk_scr[...] = jnp.zeros(dk_scr.shape, F32)
        dv_scr[...] = jnp.zeros(dv_scr.shape, F32)
        row = lax.broadcasted_iota(jnp.int32, (rc, LANES), 0)
        col = lax.broadcasted_iota(jnp.int32, (rc, LANES), 1)

        whole = [(0, tk, 0, False)]
        diagonal = [(r0, r0 + diag, r0, True) for r0 in range(0, tk, diag)]

        def block(qi, bands):
            off = pl.multiple_of(qi * tq, tq)
            q2 = q_ref[pl.ds(off, tq), :] * SCALE
            do2 = do_ref[pl.ds(off, tq), :]
            lse = lse_ref[:, pl.ds(off, tq)]
            dl = dl_ref[:, pl.ds(off, tq)]
            qo = (jnp.where(lo, q2, jnp.ones_like(q2)), jnp.where(lo, jnp.ones_like(q2), q2))
            for hd in range(2):
                for r0, r1, c0, _ in bands:
                    st_scr[hd, r0:r1, c0:] = _dot(kz_scr[hd, r0:r1, :], q2[c0:], NT)
                    dp_scr[hd, r0:r1, c0:] = _dot(vz_scr[hd, r0:r1, :], do2[c0:], NT)
            for r0, r1, c0, masked in bands:
                dqs = []
                for hd in range(2):
                    for r in range(r0, r1, rc):
                        c_rep = crep_scr[hd, r:r + rc, :]
                        for cb in range(c0, tq, LANES):
                            pt = jnp.exp(st_scr[hd, r:r + rc, cb:cb + LANES] - c_rep - lse[hd:hd + 1, cb:cb + LANES])
                            if masked and cb < c0 + diag:
                                pt = jnp.where(col + cb >= row + r, pt, 0.0)
                            dst = pt * (dp_scr[hd, r:r + rc, cb:cb + LANES] - dl[hd:hd + 1, cb:cb + LANES])
                            pt_scr[hd, r:r + rc, cb:cb + LANES] = pt.astype(BF16)
                            ds_scr[hd, r:r + rc, cb:cb + LANES] = dst.astype(BF16)
                    dv_scr[hd, r0:r1, :] += _dot(pt_scr[hd, r0:r1, c0:], do2[c0:])
                    dk_scr[hd, r0:r1, :] += _dot(ds_scr[hd, r0:r1, c0:], qo[hd][c0:])
                    dqs.append(_dot(ds_scr[hd, r0:r1, c0:], ko_scr[hd, r0:r1, :], TN))
                dq_scr[pl.ds(off + c0, tq - c0), :] += jnp.where(lo[c0:], dqs[0], dqs[1]) * SCALE
                dr_scr[pl.ds(off + c0, tq - c0), :] += jnp.where(lo[c0:], dqs[1], dqs[0])

        block(kj, diagonal)

        def below(qi, carry):
            block(qi, whole)
            return carry

        lax.fori_loop(kj + 1, nk, below, 0)
        dk_ref[...] = jnp.where(lo, dk_scr[0], dk_scr[1]).astype(BF16)
        dv_ref[...] = jnp.where(lo, dv_scr[0], dv_scr[1]).astype(BF16)
        dc_t = jnp.where(lo, dk_scr[1], dk_scr[0]).T
        dc_ref[0:1, :] = dc_t[HEAD_DIM:HEAD_DIM + 1, :]
        dc_ref[1:2, :] = dc_t[0:1, :]

        @pl.when(kj == nk - 1)
        def _():
            dq_ref[...] = dq_scr[...].astype(BF16)
            for r in range(0, s, tq):
                dr_t = dr_scr[r:r + tq, :].T
                dr_ref[0:1, r:r + tq] = dr_t[HEAD_DIM:HEAD_DIM + 1, :]
                dr_ref[1:2, r:r + tq] = dr_t[0:1, :]

    full = pl.BlockSpec((s, PAIR), lambda b, hp, j: (b, hp))
    kblk = pl.BlockSpec((tk, PAIR), lambda b, hp, j: (b * nk + j, hp))
    rows = pl.BlockSpec((None, None, 2, s), lambda b, hp, j: (b, hp, 0, 0))
    krows = pl.BlockSpec((None, None, 2, tk), lambda b, hp, j: (b, hp, 0, j))
    sds = lambda dt: jax.ShapeDtypeStruct((t, D_MODEL), dt)
    rows_sds = jax.ShapeDtypeStruct((nb, N_PAIRS, 2, s), F32)
    pair_bf = pltpu.VMEM((2, tk, PAIR), BF16)
    pair_f = pltpu.VMEM((2, tk, PAIR), F32)
    return _pcall(
        body, name="attn_bwd", grid=(nb, N_PAIRS, nk),
        in_specs=[full, full, kblk, kblk, krows, rows, rows],
        out_specs=[full, kblk, kblk, krows, rows],
        out_shape=[sds(BF16), sds(BF16), sds(BF16), rows_sds, rows_sds],
        scratch_shapes=[pair_bf, pair_bf, pair_bf, pair_f, pltpu.VMEM((2, tk, tq), F32), pltpu.VMEM((2, tk, tq), F32),
                        pltpu.VMEM((2, tk, tq), BF16), pltpu.VMEM((2, tk, tq), BF16), pair_f, pair_f,
                        pltpu.VMEM((s, PAIR), F32), pltpu.VMEM((s, PAIR), F32)],
        dims=("parallel", "parallel", "arbitrary"))(q, do, k, v, c4, lse4, delta4)


def _window_sum(x, g, s, *, ahead):
    row = lax.broadcasted_iota(jnp.int32, x.shape, 0)
    for step in range(N_GROUPS):
        sh = 1 << step
        if ahead:
            moved = jnp.where(row + sh < s, pltpu.roll(x, s - sh, 0), 0.0)
        else:
            moved = jnp.where(row >= sh, pltpu.roll(x, sh, 0), 0.0)
        x = jnp.where(step <= g, x + moved, x)
    return x


def _inv_count(g, s):
    pos = lax.broadcasted_iota(jnp.int32, (s, 1), 0) + 1
    return 1.0 / jnp.minimum(pos, 2 << g).astype(F32)


def _pool_fwd(u, zp, wpool, ps, *, nb):
    t = u.shape[0]
    s = t // nb

    def body(u_ref, zp_ref, w_ref, ps_ref, out_ref, pooled_ref, mixed_ref):
        g = pl.program_id(1)
        x = u_ref[...]
        pooled = (_window_sum(x, g, s, ahead=False) * _inv_count(g, s) - x).astype(BF16)
        mixed = _dot(pooled, w_ref[...])
        z = zp_ref[...]
        pooled_ref[...] = pooled
        mixed_ref[...] = mixed
        out_ref[...] = (mixed * ps_ref[...] * (z * _sigmoid(z))).astype(BF16)

    blk = pl.BlockSpec((s, GROUP_DIM), lambda b, g: (b, g))
    sds = lambda dt: jax.ShapeDtypeStruct((t, D_MODEL), dt)
    return _pcall(
        body, name="pool_fwd", grid=(nb, N_GROUPS),
        in_specs=[blk, blk, pl.BlockSpec((None, GROUP_DIM, GROUP_DIM), lambda b, g: (g, 0, 0)),
                  pl.BlockSpec((1, GROUP_DIM), lambda b, g: (0, g))],
        out_specs=[blk, blk, blk], out_shape=[sds(BF16), sds(BF16), sds(F32)],
        dims=("parallel", "parallel"))(u, zp, wpool, ps)


def _pool_bwd(dpool, mixed, zp, wpool, ps, *, nb):
    t = dpool.shape[0]
    s = t // nb

    def body(dp_ref, mx_ref, zp_ref, w_ref, ps_ref, dmx_ref, du_ref, dzp_ref, dps_ref):
        g = pl.program_id(0)

        @pl.when(pl.program_id(1) == 0)
        def _():
            dps_ref[...] = jnp.zeros_like(dps_ref)

        dp = dp_ref[...]
        mixed = mx_ref[...]
        z = zp_ref[...]
        scale = ps_ref[...]
        sg = _sigmoid(z)
        silu = z * sg
        dmixed = (dp * scale * silu).astype(BF16)
        dmx_ref[...] = dmixed
        dps_ref[...] += jnp.sum(dp * mixed * silu, axis=0, keepdims=True)
        dzp_ref[...] = (dp * mixed * scale * (sg * (1.0 + z * (1.0 - sg)))).astype(BF16)
        dpooled = _dot(dmixed, w_ref[...], NT)
        du = _window_sum(dpooled * _inv_count(g, s), g, s, ahead=True) - dpooled
        du_ref[...] = du.astype(BF16)

    blk = pl.BlockSpec((s, GROUP_DIM), lambda g, b: (b, g))
    sds = lambda dt: jax.ShapeDtypeStruct((t, D_MODEL), dt)
    return _pcall(
        body, name="pool_bwd", grid=(N_GROUPS, nb),
        in_specs=[blk, blk, blk, pl.BlockSpec((None, GROUP_DIM, GROUP_DIM), lambda g, b: (g, 0, 0)),
                  pl.BlockSpec((1, GROUP_DIM), lambda g, b: (0, g))],
        out_specs=[blk, blk, blk, pl.BlockSpec((1, GROUP_DIM), lambda g, b: (0, g))],
        out_shape=[sds(BF16), sds(BF16), sds(BF16), jax.ShapeDtypeStruct((1, D_MODEL), F32)],
        dims=("parallel", "arbitrary"))(dpool, mixed, zp, wpool, ps)


def _pool_wgrad(pooled, dmixed, *, tk=1024):
    t = pooled.shape[0]
    tk = min(tk, t)
    nk = t // tk

    def body(a_ref, b_ref, o_ref):
        @pl.when(pl.program_id(1) == 0)
        def _():
            o_ref[...] = jnp.zeros_like(o_ref)

        o_ref[...] += _dot(a_ref[...], b_ref[...], TN)

    blk = pl.BlockSpec((tk, GROUP_DIM), lambda g, k: (k, g))
    return _pcall(
        body, name="pool_wgrad", grid=(N_GROUPS, nk), in_specs=[blk, blk],
        out_specs=pl.BlockSpec((None, GROUP_DIM, GROUP_DIM), lambda g, k: (g, 0, 0)),
        out_shape=jax.ShapeDtypeStruct((N_GROUPS, GROUP_DIM, GROUP_DIM), F32),
        dims=("parallel", "arbitrary"))(pooled, dmixed)


def _out_block(h, o, za, pool, p, wo, wpg, wpe, gpost, *, tm=256):
    t = h.shape[0]

    def body(h_ref, o_ref, za_ref, pool_ref, p_ref, wo_ref, wpg_ref, wpe_ref, g_ref,
             h2_ref, mix_ref, h1_ref, gate_ref, pe_ref, cat_ref):
        z = za_ref[...]
        a = (o_ref[...] * (z * _sigmoid(z))).astype(BF16)
        pool_v = pool_ref[...]
        cat_ref[:, :D_MODEL] = a
        cat_ref[:, D_MODEL:] = pool_v
        mix = _dot(a, wo_ref[:D_MODEL, :]) + _dot(pool_v, wo_ref[D_MODEL:, :])
        r = lax.rsqrt(jnp.mean(mix * mix, axis=-1, keepdims=True) + EPS)
        h1 = h_ref[...] + mix * r * g_ref[...]
        gate = _sigmoid(_dot(h1.astype(BF16), wpg_ref[...]))
        pe = _dot(p_ref[...].astype(BF16), wpe_ref[...])
        mix_ref[...] = mix
        h1_ref[...] = h1
        gate_ref[...] = gate
        pe_ref[...] = pe
        h2_ref[...] = h1 + gate * pe

    row = lambda i: (i, 0)
    const = lambda i: (0, 0)
    blk = pl.BlockSpec((tm, D_MODEL), row)
    sds = jax.ShapeDtypeStruct((t, D_MODEL), F32)
    return _pcall(
        body, name="out_block", grid=(t // tm,),
        in_specs=[blk, blk, blk, blk, pl.BlockSpec((tm, PLE_DIM), row),
                  pl.BlockSpec((MIX, D_MODEL), const), pl.BlockSpec((D_MODEL, D_MODEL), const),
                  pl.BlockSpec((PLE_DIM, D_MODEL), const), pl.BlockSpec((1, D_MODEL), const)],
        out_specs=[blk] * 5 + [pl.BlockSpec((tm, MIX), row)],
        out_shape=[sds] * 5 + [jax.ShapeDtypeStruct((t, MIX), BF16)],
        dims=("parallel",))(h, o, za, pool, p, wo, wpg, wpe, gpost)


def _loss_grad(y, target, *, tm=512):
    t = y.shape[0]

    def body(y_ref, t_ref, dy_ref, loss_ref):
        @pl.when(pl.program_id(0) == 0)
        def _():
            loss_ref[...] = jnp.zeros_like(loss_ref)

        err = y_ref[...] - t_ref[...]
        dy_ref[...] = err * (1.0 / D_MODEL)
        part = jnp.sum(jnp.sum(err * err, axis=1, keepdims=True), axis=0, keepdims=True)
        loss_ref[...] += part * (0.5 / D_MODEL)

    blk = pl.BlockSpec((tm, D_MODEL), lambda i: (i, 0))
    return _pcall(
        body, name="loss_grad", grid=(t // tm,), in_specs=[blk, blk],
        out_specs=[blk, pl.BlockSpec((8, LANES), lambda i: (0, 0))],
        out_shape=[jax.ShapeDtypeStruct((t, D_MODEL), F32), jax.ShapeDtypeStruct((8, LANES), F32)],
        dims=("arbitrary",))(y, target)


def _out_block_bwd(dh2, gate, pe, mix, o, za, wpg, wo, gpost, gsum, *, tm=256):
    t = dh2.shape[0]

    def body(dh2_ref, gate_ref, pe_ref, mix_ref, o_ref, za_ref, wpg_ref, wo_ref, g_ref, gs_ref,
             dh1_ref, dgp_ref, dpe_ref, dmix_ref, do_ref, dza_ref, dpool_ref, delta_ref, dg_ref):
        @pl.when(pl.program_id(0) == 0)
        def _():
            dg_ref[...] = jnp.zeros_like(dg_ref)

        dh2 = dh2_ref[...]
        gate = gate_ref[...]
        dpe_ref[...] = (dh2 * gate).astype(BF16)
        dgp = (dh2 * pe_ref[...] * gate * (1.0 - gate)).astype(BF16)
        dgp_ref[...] = dgp
        dh1 = dh2 + _dot(dgp, wpg_ref[...], NT)
        dh1_ref[...] = dh1
        mix = mix_ref[...]
        r = lax.rsqrt(jnp.mean(mix * mix, axis=-1, keepdims=True) + EPS)
        dg_ref[...] += jnp.sum(dh1 * mix * r, axis=0, keepdims=True)
        a = dh1 * g_ref[...]
        dmix = (r * a - mix * (r * r * r) * jnp.mean(a * mix, axis=-1, keepdims=True)).astype(BF16)
        dmix_ref[...] = dmix
        dattn = _dot(dmix, wo_ref[:D_MODEL, :], NT)
        dpool_ref[...] = _dot(dmix, wo_ref[D_MODEL:, :], NT)
        z = za_ref[...]
        sg = _sigmoid(z)
        o = o_ref[...]
        do = (dattn * (z * sg)).astype(BF16)
        do_ref[...] = do
        dza_ref[...] = (dattn * o * (sg * (1.0 + z * (1.0 - sg)))).astype(BF16)
        prod = do.astype(F32) * o
        hi = prod.astype(BF16)
        rest = (prod - hi.astype(F32)).astype(BF16)
        delta_ref[...] = _dot(hi, gs_ref[...]) + _dot(rest, gs_ref[...])

    row = lambda i: (i, 0)
    const = lambda i: (0, 0)
    blk = pl.BlockSpec((tm, D_MODEL), row)
    sds = lambda dt: jax.ShapeDtypeStruct((t, D_MODEL), dt)
    return _pcall(
        body, name="out_block_bwd", grid=(t // tm,),
        in_specs=[blk] * 6 + [pl.BlockSpec((D_MODEL, D_MODEL), const), pl.BlockSpec((MIX, D_MODEL), const),
                              pl.BlockSpec((1, D_MODEL), const), pl.BlockSpec((D_MODEL, LANES), const)],
        out_specs=[blk] * 7 + [pl.BlockSpec((tm, LANES), row), pl.BlockSpec((1, D_MODEL), const)],
        out_shape=[sds(F32)] + [sds(BF16)] * 5 + [sds(F32), jax.ShapeDtypeStruct((t, LANES), F32),
                                                 jax.ShapeDtypeStruct((1, D_MODEL), F32)],
        dims=("arbitrary",))(dh2, gate, pe, mix, o, za, wpg, wo, gpost, gsum)


def _in_proj_bwd(dsegs, dfl, wm, wf, h, dh1, gpre, *, tm=512):
    t = h.shape[0]

    def body(*refs):
        seg_refs = refs[:N_SEG]
        dfl_ref, wm_ref, wf_ref, h_ref, dh1_ref, g_ref, dh_ref, dg_ref, acc_ref = refs[N_SEG:]
        i = pl.program_id(0)
        j = pl.program_id(1)

        @pl.when(jnp.logical_and(i == 0, j == 0))
        def _():
            dg_ref[...] = jnp.zeros_like(dg_ref)

        @pl.when(j == 0)
        def _():
            acc_ref[...] = _dot(dfl_ref[...].astype(BF16), wf_ref[...], NT)

        for pair in range(N_SEG // 2):
            @pl.when(j == pair)
            def _(ref_a=seg_refs[2 * pair], ref_b=seg_refs[2 * pair + 1]):
                acc_ref[...] += (_dot(ref_a[...].astype(BF16), wm_ref[:, :D_MODEL], NT)
                                 + _dot(ref_b[...].astype(BF16), wm_ref[:, D_MODEL:], NT))

        @pl.when(j == N_SEG // 2 - 1)
        def _():
            dhn = acc_ref[...]
            x = h_ref[...]
            r = lax.rsqrt(jnp.mean(x * x, axis=-1, keepdims=True) + EPS)
            dg_ref[...] += jnp.sum(dhn * x * r, axis=0, keepdims=True)
            a = dhn * g_ref[...]
            dh_ref[...] = dh1_ref[...] + r * a - x * (r * r * r) * jnp.mean(a * x, axis=-1, keepdims=True)

    row = lambda i, j: (i, 0)
    const = lambda i, j: (0, 0)
    blk = pl.BlockSpec((tm, D_MODEL), row)
    return _pcall(
        body, name="in_proj_bwd", grid=(t // tm, N_SEG // 2),
        in_specs=[blk] * N_SEG + [pl.BlockSpec((tm, LANES), row), pl.BlockSpec((D_MODEL, 2 * D_MODEL), lambda i, j: (0, j)),
                                  pl.BlockSpec((D_MODEL, LANES), const), blk, blk, pl.BlockSpec((1, D_MODEL), const)],
        out_specs=[blk, pl.BlockSpec((1, D_MODEL), const)],
        out_shape=[jax.ShapeDtypeStruct((t, D_MODEL), F32), jax.ShapeDtypeStruct((1, D_MODEL), F32)],
        scratch_shapes=[pltpu.VMEM((tm, D_MODEL), F32)],
        dims=("arbitrary", "arbitrary"))(*dsegs, dfl, wm, wf, h, dh1, gpre)


def _chip_peers(x, y):
    return [(1 - x, y), (x, 1 - y), (1 - x, 1 - y)]


def _ici_views(src, land, gather, x, y, c, pi, px, py):
    if gather:
        return src.at[c], land.at[pi], land.at[pi]
    return src.at[2 * px + py], land.at[2 * x + y], land.at[2 * px + py]


def _ici_start(srcs, lands, after, *, gather, name):
    n = len(srcs)

    def body(*refs):
        src, land = refs[:n], refs[n:2 * n]
        send, recv = refs[2 * n + 1], refs[2 * n + 2]
        token = refs[-1]
        x, y, c = lax.axis_index("x"), lax.axis_index("y"), lax.axis_index("c")
        for kk in range(n):
            for pi, (px, py) in enumerate(_chip_peers(x, y)):
                mine, there, _ = _ici_views(src[kk], land[kk], gather, x, y, c, pi, px, py)
                pltpu.make_async_remote_copy(src_ref=mine, dst_ref=there, send_sem=send.at[3 * kk + pi],
                                             recv_sem=recv.at[3 * kk + pi], device_id=(px, py, c),
                                             device_id_type=MESH).start()
        token[...] = jnp.zeros_like(token)

    hbm = lambda a: pltpu.with_memory_space_constraint(a, pltpu.HBM)
    outs = pl.pallas_call(
        body, name=name,
        out_shape=(pltpu.SemaphoreType.DMA((3 * n,)), pltpu.SemaphoreType.DMA((3 * n,)),
                   *[pltpu.HBM(a.shape, a.dtype) for a in srcs], *[pltpu.HBM(a.shape, a.dtype) for a in lands],
                   jax.ShapeDtypeStruct((8, LANES), F32)),
        in_specs=[HBM_SPEC] * (2 * n) + [ANY],
        out_specs=(SEM_SPEC, SEM_SPEC, *[HBM_SPEC] * (2 * n), pl.BlockSpec(memory_space=pltpu.VMEM)),
        input_output_aliases={i: 2 + i for i in range(2 * n)},
        compiler_params=pltpu.CompilerParams(has_side_effects=SIDE_EFFECT),
    )(*[hbm(a) for a in srcs], *[hbm(a) for a in lands], after)
    return outs[0], outs[1], list(outs[2:2 + n]), list(outs[2 + n:2 + 2 * n]), outs[-1]


def _ici_wait(send, recv, srcs, lands, after, *, gather, name):
    n = len(srcs)

    def body(*refs):
        src, land = refs[:n], refs[n:2 * n]
        send_ref, recv_ref = refs[2 * n], refs[2 * n + 1]
        x, y, c = lax.axis_index("x"), lax.axis_index("y"), lax.axis_index("c")
        for kk in range(n):
            for pi, (px, py) in enumerate(_chip_peers(x, y)):
                mine, _, here = _ici_views(src[kk], land[kk], gather, x, y, c, pi, px, py)
                cp = pltpu.make_async_remote_copy(src_ref=mine, dst_ref=here, send_sem=send_ref.at[3 * kk + pi],
                                                  recv_sem=recv_ref.at[3 * kk + pi], device_id=(px, py, c),
                                                  device_id_type=MESH)
                cp.wait_send()
                cp.wait_recv()

    outs = pl.pallas_call(
        body, name=name,
        out_shape=[pltpu.HBM(a.shape, a.dtype) for a in srcs] + [pltpu.HBM(a.shape, a.dtype) for a in lands],
        in_specs=[HBM_SPEC] * (2 * n) + [SEM_SPEC, SEM_SPEC] + [ANY] * len(after), out_specs=[HBM_SPEC] * (2 * n),
        input_output_aliases={i: i for i in range(2 * n)},
        compiler_params=pltpu.CompilerParams(has_side_effects=SIDE_EFFECT),
    )(*srcs, *lands, send, recv, *after)
    return list(outs[:n]), list(outs[n:])


def _sibling_send(arrays, *, name):
    n = len(arrays)

    def body(*refs):
        ins, outs = refs[:n], refs[n:2 * n]
        send, recv = refs[2 * n:]
        sib = (lax.axis_index("x"), lax.axis_index("y"), 1 - lax.axis_index("c"))
        copies = [pltpu.make_async_remote_copy(src_ref=ins[kk], dst_ref=outs[kk], send_sem=send.at[kk],
                                               recv_sem=recv.at[kk], device_id=sib, device_id_type=MESH)
                  for kk in range(n)]
        for cp in copies:
            cp.start()
        for cp in copies:
            cp.wait_recv()
        for cp in copies:
            cp.wait_send()

    return _pcall(
        body, name=name, in_specs=[ANY] * n, out_specs=[ANY] * n,
        out_shape=[jax.ShapeDtypeStruct(a.shape, a.dtype) for a in arrays],
        scratch_shapes=[pltpu.SemaphoreType.DMA((n,)), pltpu.SemaphoreType.DMA((n,))])(*arrays)


def _all_gather_small(a, *, name):
    def body(a_ref, o_ref, send, recv, loc):
        x, y, c = lax.axis_index("x"), lax.axis_index("y"), lax.axis_index("c")
        me = 4 * x + 2 * y + c

        def peer(rel):
            fx, fy, fc = (rel >> 2) & 1, (rel >> 1) & 1, rel & 1
            px = (1 - x) if fx else x
            py = (1 - y) if fy else y
            pc = (1 - c) if fc else c
            return px, py, pc

        def remote(rel, slot):
            return pltpu.make_async_remote_copy(
                src_ref=a_ref, dst_ref=o_ref.at[slot], send_sem=send.at[rel - 1], recv_sem=recv.at[rel - 1],
                device_id=peer(rel), device_id_type=MESH)

        mine = pltpu.make_async_copy(a_ref, o_ref.at[me], loc)
        mine.start()
        sends = [remote(rel, me) for rel in range(1, 8)]
        for cp in sends:
            cp.start()
        for rel in range(1, 8):
            px, py, pc = peer(rel)
            remote(rel, 4 * px + 2 * py + pc).wait_recv()
        for cp in sends:
            cp.wait_send()
        mine.wait()

    return _pcall(
        body, name=name, in_specs=[ANY], out_specs=ANY,
        out_shape=jax.ShapeDtypeStruct((8,) + a.shape, a.dtype),
        scratch_shapes=[pltpu.SemaphoreType.DMA((7,)), pltpu.SemaphoreType.DMA((7,)), pltpu.SemaphoreType.DMA])(a)


def _to_bf16(a, *, name, tr=256):
    rows, cols = a.shape

    def body(a_ref, o_ref):
        o_ref[...] = a_ref[...].astype(BF16)

    blk = pl.BlockSpec((tr, cols), lambda i: (i, 0))
    return _pcall(body, name=name, grid=(rows // tr,), in_specs=[blk], out_specs=blk,
                  out_shape=jax.ShapeDtypeStruct(a.shape, BF16), dims=("parallel",))(a)


def _pair_sum(a, b, *, name, tr=256):
    _, rows, cols = a.shape
    tr = min(tr, rows)

    def body(a_ref, b_ref, o_ref):
        o_ref[...] = (a_ref[...].astype(F32) + b_ref[...].astype(F32)).astype(BF16)

    blk = pl.BlockSpec((None, tr, cols), lambda j, i: (j, i, 0))
    return _pcall(body, name=name, grid=(4, rows // tr), in_specs=[blk, blk], out_specs=blk,
                  out_shape=jax.ShapeDtypeStruct(a.shape, BF16), dims=("parallel", "parallel"))(a, b)


def _chip_sum(own, recv, chip, *, name, tr=256):
    _, rows, cols = own.shape
    tr = min(tr, rows)

    def body(chip_ref, own_ref, r1_ref, r2_ref, r3_ref, o_ref):
        acc = own_ref[...].astype(F32)
        for ref in (r1_ref, r2_ref, r3_ref):
            acc = acc + ref[...].astype(F32)
        o_ref[...] = acc

    def slot(step):
        return pl.BlockSpec((None, tr, cols), lambda i, chip_ref: ((chip_ref[0] + step) % 4, i, 0))

    return pl.pallas_call(
        body, name=name, out_shape=jax.ShapeDtypeStruct((rows, cols), F32),
        grid_spec=pltpu.PrefetchScalarGridSpec(
            num_scalar_prefetch=1, grid=(rows // tr,), in_specs=[slot(0), slot(1), slot(2), slot(3)],
            out_specs=pl.BlockSpec((tr, cols), lambda i, chip_ref: (i, 0))),
        compiler_params=pltpu.CompilerParams(dimension_semantics=("parallel",), vmem_limit_bytes=VMEM_LIMIT),
    )(chip, own, recv, recv, recv)


def _adamw_math(w, g, m, v):
    m = ADAM_B1 * m + (1.0 - ADAM_B1) * g
    v = ADAM_B2 * v + (1.0 - ADAM_B2) * (g * g)
    m_hat = m / (1.0 - ADAM_B1 ** ADAM_STEP)
    v_hat = v / (1.0 - ADAM_B2 ** ADAM_STEP)
    delta = -ADAM_LR * (m_hat / (jnp.sqrt(v_hat) + ADAM_EPS) + ADAM_WD * w)
    return delta, m, v


def _adamw(w, m, v, g, layer, prev, *, name, tr=128):
    rows, cols = g.shape
    tr = min(tr, rows)
    nblk = rows // tr

    def body(w_ref, m_ref, v_ref, gin_ref, *refs):
        g_ref, d_ref, mo_ref, vo_ref = refs[-4:]
        grad = gin_ref[...]
        delta, m_new, v_new = _adamw_math(w_ref[...], grad, m_ref[...], v_ref[...])
        g_ref[...] = grad
        d_ref[...] = delta
        mo_ref[...] = m_new
        vo_ref[...] = v_new

    blk = pl.BlockSpec((tr, cols), lambda i: (layer * nblk + i, 0))
    carried = [] if prev is None else list(prev)
    return pl.pallas_call(
        body, name=name, grid=(nblk,), out_shape=[jax.ShapeDtypeStruct(w.shape, F32)] * 4,
        in_specs=[blk] * 3 + [pl.BlockSpec((tr, cols), lambda i: (i, 0))] + [ANY] * len(carried), out_specs=[blk] * 4,
        input_output_aliases={4 + i: i for i in range(len(carried))},
        compiler_params=pltpu.CompilerParams(dimension_semantics=("parallel",), vmem_limit_bytes=VMEM_LIMIT),
    )(w, m, v, g, *carried)


def _adamw_small(w, m, v, parts, *, name):
    def body(w_ref, m_ref, v_ref, p_ref, g_ref, d_ref, mo_ref, vo_ref):
        g = p_ref[0]
        for dev in range(1, 8):
            g = g + p_ref[dev]
        delta, m_new, v_new = _adamw_math(w_ref[...], g, m_ref[...], v_ref[...])
        g_ref[...] = g
        d_ref[...] = delta
        mo_ref[...] = m_new
        vo_ref[...] = v_new

    return _pcall(body, name=name, out_shape=[jax.ShapeDtypeStruct(w.shape, F32)] * 4)(w, m, v, parts)


def _column_parts(segments, lo, hi):
    parts, start = [], 0
    for seg in segments:
        a, b = max(lo, start), min(hi, start + seg.shape[1])
        if a < b:
            parts.append(seg[:, a - start:b - start])
        start += seg.shape[1]
    return parts


def _rows_from_heads(a, nb, s):
    return a.reshape(nb, s, N_HEADS).transpose(0, 2, 1).reshape(nb, N_PAIRS, 2, s)


def _layer_fwd(h, p_l, wts, nb):
    t = h.shape[0]
    s = t // nb
    q, k, v, za, u, zp, fl, hn = _norm_in_proj(h, wts["gpre"], wts["wm"], wts["wf"])
    flt = fl[:, :N_HEADS].reshape(nb, s, N_HEADS).transpose(0, 2, 1)
    c = _gate_cumsum(flt, wts["bf"])
    o, lse = _attn_fwd(q, k, v, c.reshape(nb, N_PAIRS, 2, s), nb=nb)
    pool, pooled, mixed = _pool_fwd(u, zp, wts["wpool"], wts["ps"], nb=nb)
    h2, mix, h1, gate, pe, cat = _out_block(h, o, za, pool, p_l, wts["wo"], wts["wpg"], wts["wpe"], wts["gpost"])
    saved = dict(h=h, hn=hn, q=q, k=k, v=v, za=za, zp=zp, flt=flt, c=c, o=o, lse=lse, pooled=pooled, mixed=mixed,
                 mix=mix, h1=h1, gate=gate, pe=pe, cat=cat, p=p_l)
    return h2, saved


def _layer_bwd(dh2, sv, wts, gsum, nb):
    t = dh2.shape[0]
    s = t // nb
    dh1, dgp, dpe, dmix, do, dza, dpool, delta, dgpost = _out_block_bwd(
        dh2, sv["gate"], sv["pe"], sv["mix"], sv["o"], sv["za"], wts["wpg"], wts["wo"], wts["gpost"], gsum)
    g_wpe = _matmul(sv["p"], dpe, ta=True, name="wgrad_pe")
    g_wpg = _matmul(sv["h1"], dgp, ta=True, name="wgrad_pg")
    g_wo = _matmul(sv["cat"], dmix, ta=True, name="wgrad_out")

    delta4 = _rows_from_heads(delta[:, :N_HEADS], nb, s)
    dq, dk, dv, dcol, drow = _attn_bwd(sv["q"], sv["k"], sv["v"], do, sv["c"].reshape(nb, N_PAIRS, 2, s), sv["lse"],
                                       delta4, nb=nb)
    dflt, dbf = _gate_cumsum_bwd(dcol.reshape(nb, N_HEADS, s), drow.reshape(nb, N_HEADS, s), sv["flt"], wts["bf"])
    dfl = jnp.pad(dflt.transpose(0, 2, 1).reshape(t, N_HEADS), ((0, 0), (0, LANES - N_HEADS)))

    dmixed, du, dzp, dps = _pool_bwd(dpool, sv["mixed"], sv["zp"], wts["wpool"], wts["ps"], nb=nb)
    g_wpool = _pool_wgrad(sv["pooled"], dmixed)

    dsegs = (dq, dk, dv, dza, du, dzp)
    dh, dgpre = _in_proj_bwd(dsegs, dfl, wts["wm"], wts["wf"], sv["h"], dh1, wts["gpre"])
    g_segs = [_matmul(sv["hn"], dx, ta=True, name="wgrad_in") for dx in dsegs]
    g_wf = _matmul(sv["hn"], dfl, ta=True, name="wgrad_in_f")
    g_win = g_segs[:4] + [g_wf[:, :N_HEADS]] + g_segs[4:]
    grads = dict(w_in=g_win, w_out=g_wo, w_pg=g_wpg, w_pe=g_wpe, w_pool=g_wpool,
                 norm_pre=dgpre[0], norm_post=dgpost[0], pool_scale=dps[0], b_f=dbf[:, 0])
    return dh, grads


def kernel(x, p, norm_pre, norm_post, w_in, b_f, w_pool, pool_scale, w_out, w_pg, w_pe, loss_target, m_norm_pre, m_norm_post, m_w_in, m_b_f, m_w_pool, m_pool_scale, m_w_out, m_w_pg, m_w_pe, v_norm_pre, v_norm_post, v_w_in, v_b_f, v_w_pool, v_pool_scale, v_w_out, v_w_pg, v_w_pe):
    nb, s, _ = x.shape
    t = nb * s
    depth = w_in.shape[0]
    big = dict(w_in=(w_in, m_w_in, v_w_in), w_out=(w_out, m_w_out, v_w_out), w_pg=(w_pg, m_w_pg, v_w_pg),
               w_pe=(w_pe, m_w_pe, v_w_pe), w_pool=(w_pool, m_w_pool, v_w_pool))
    names = list(big)
    cols = {n: big[n][0].shape[-1] for n in names}
    chip = (2 * lax.axis_index("x") + lax.axis_index("y")).astype(jnp.int32).reshape(1)
    core = lax.axis_index("c")
    south = core == 0

    gathers = []
    token = jnp.zeros((8, LANES), F32)
    narrow = {n: big[n][0].astype(BF16) for n in names if n != "w_in"}
    narrow["w_in"] = _to_bf16(w_in.reshape(-1, cols["w_in"]), name="w_in_to_bf16").reshape(w_in.shape)
    for l in range(depth):
        halves = [narrow[n][l].reshape(2, -1, cols[n]) for n in names]
        lands = [lax.empty((3,) + a.shape[1:], a.dtype) for a in halves]
        send, recv, halves, lands, token = _ici_start(halves, lands, token, gather=True, name=f"gather_start_{l}")
        gathers.append((send, recv, halves, lands))

    def layer_weights(l, after, anchor):
        send, recv, halves, lands = gathers[l]
        halves, lands = _ici_wait(send, recv, halves, lands, after, gather=True, name=f"gather_wait_{l}")
        others = _sibling_send(lands, name="gather_pass_on")
        full = {}
        for n, own, land, other in zip(names, halves, lands, others):
            low = jnp.where(south, land, other)
            high = jnp.where(south, other, land)
            rel = [jnp.concatenate([low[pi], high[pi]], axis=0) for pi in range(3)]
            by_rel = jnp.stack([own.reshape(-1, cols[n]), rel[1], rel[0], rel[2]])
            full[n] = [lax.dynamic_index_in_dim(by_rel, j ^ chip[0], 0, keepdims=False) for j in range(4)]
        win = full["w_in"]
        wm = jnp.concatenate(_column_parts(win, 0, F_OFF) + _column_parts(win, F_OFF + N_HEADS, IN_COLS), axis=1)
        wf = jnp.pad(jnp.concatenate(_column_parts(win, F_OFF, F_OFF + N_HEADS), axis=1), ((0, 0), (0, LANES - N_HEADS)))
        return dict(
            wm=wm, wf=wf, wo=jnp.concatenate(full["w_out"], axis=0), wpg=jnp.concatenate(full["w_pg"], axis=0),
            wpe=jnp.concatenate(full["w_pe"], axis=1),
            wpool=jnp.stack(full["w_pool"]).reshape(4, N_GROUPS, GROUP_DIM // 4, GROUP_DIM).transpose(1, 0, 2, 3).reshape(
                N_GROUPS, GROUP_DIM, GROUP_DIM),
            gpre=norm_pre[l][None] + anchor, gpost=norm_post[l][None], ps=pool_scale[l][None], bf=b_f[l][:, None])

    h = x.reshape(t, D_MODEL)
    saved, layers = [], []
    m_w_in, v_w_in = m_w_in + token[0, 0], v_w_in + token[0, 0]
    big["w_in"] = (w_in, m_w_in, v_w_in)
    after = [token, m_w_in.reshape(-1, cols["w_in"]), v_w_in.reshape(-1, cols["w_in"])]
    for l in range(depth):
        layers.append(layer_weights(l, after, token[0, 0]))
        h, sv = _layer_fwd(h, p[l].reshape(t, PLE_DIM), layers[l], nb)
        saved.append(sv)
        after = [h]
    dh, loss_blk = _loss_grad(h, loss_target.reshape(t, D_MODEL))
    loss = lax.psum(loss_blk[0, 0], ("x", "y", "c"))
    gsum = (jnp.arange(D_MODEL)[:, None] // HEAD_DIM == jnp.arange(LANES)[None, :]).astype(BF16)

    def pieces(g, name):
        if name == "w_in":
            by_chip = jnp.stack([jnp.concatenate(_column_parts(g, j * cols[name], (j + 1) * cols[name]), axis=1)
                                 for j in range(4)])
        elif name == "w_pe":
            by_chip = jnp.stack([g[:, j * cols[name]:(j + 1) * cols[name]] for j in range(4)])
        elif name == "w_pool":
            by_chip = g.reshape(N_GROUPS, 4, GROUP_DIM // 4, GROUP_DIM).transpose(1, 0, 2, 3)
        else:
            by_chip = g
        return by_chip.reshape(4, 2, -1, cols[name]).transpose(1, 0, 2, 3).astype(BF16)

    def scatter_start(l, g):
        mine, sent = [], []
        for n in names:
            halves = pieces(g[n], n)
            mine.append(lax.dynamic_index_in_dim(halves, core, 0, keepdims=False))
            sent.append(lax.dynamic_index_in_dim(halves, 1 - core, 0, keepdims=False))
        theirs = _sibling_send(sent, name="presum_exchange")
        sums = [_pair_sum(a, b, name="presum_" + n) for n, a, b in zip(names, mine, theirs)]
        lands = [lax.empty(a.shape, a.dtype) for a in sums]
        return _ici_start(sums, lands, token, gather=False, name=f"scatter_start_{l}")

    def scatter_finish(l, state, after):
        send, recv, sums, lands, _ = state
        sums, lands = _ici_wait(send, recv, sums, lands, after, gather=False, name=f"scatter_wait_{l}")
        mine = [_chip_sum(a, r, chip, name="chip_sum_" + n) for n, a, r in zip(names, sums, lands)]
        theirs = _sibling_send(mine, name="halves_exchange")
        return {n: jnp.concatenate([jnp.where(south, a, b), jnp.where(south, b, a)], axis=0)
                for n, a, b in zip(names, mine, theirs)}

    grads = [None] * depth
    big_out = {n: None for n in names}

    def update(l, state, after):
        reduced = scatter_finish(l, state, after)
        for n in names:
            w, m, v = big[n]
            big_out[n] = _adamw(w.reshape(-1, cols[n]), m.reshape(-1, cols[n]), v.reshape(-1, cols[n]), reduced[n], l,
                                big_out[n], name="adamw_" + n)

    pending = None
    for l in reversed(range(depth)):
        wts = layers[l]
        if pending is not None:
            wts = dict(wts, gpost=wts["gpost"] + pending[1][4][0, 0])
        dh, grads[l] = _layer_bwd(dh, saved[l], wts, gsum, nb)
        state = scatter_start(l, grads[l])
        if pending is not None:
            update(pending[0], pending[1], [state[4]])
        pending = (l, state)
    grad_x = dh.reshape(nb, s, D_MODEL)

    small = dict(norm_pre=(norm_pre, m_norm_pre, v_norm_pre), norm_post=(norm_post, m_norm_post, v_norm_post),
                 pool_scale=(pool_scale, m_pool_scale, v_pool_scale), b_f=(b_f, m_b_f, v_b_f))

    def pack(get):
        rows = []
        for n in small:
            a = get(n)
            rows.append(jnp.pad(a, ((0, 0), (0, D_MODEL - a.shape[1]))))
        return jnp.concatenate(rows, axis=0)

    parts = _all_gather_small(pack(lambda n: jnp.stack([grads[l][n] for l in range(depth)])), name="gather_small")
    small_packed = _adamw_small(pack(lambda n: small[n][0]), pack(lambda n: small[n][1]), pack(lambda n: small[n][2]),
                                parts, name="adamw_small")
    small_out = {}
    for i, n in enumerate(small):
        width = small[n][0].shape[1]
        small_out[n] = [o[depth * i:depth * (i + 1), :width] for o in small_packed]

    update(pending[0], pending[1], [outs[0] for outs in big_out.values() if outs is not None] + [small_packed[0]])
    big_out = {n: [o.reshape(big[n][0].shape) for o in outs] for n, outs in big_out.items()}

    order =["norm_pre", "norm_post", "w_in", "b_f", "w_pool", "pool_scale", "w_out", "w_pg", "w_pe"]
    result = [loss, grad_x]
    for kind in range(4):
        for n in order:
            result.append(big_out[n][kind] if n in big_out else small_out[n][kind])
    return tuple(result)
```

```python
import functools

import jax
import jax.numpy as jnp
from jax import lax
from jax.experimental import pallas as pl
from jax.experimental.pallas import tpu as pltpu

F32 = jnp.float32
BF16 = jnp.bfloat16

D_MODEL = 1024
N_HEADS = 16
HEAD_DIM = 64
PAIR = 2 * HEAD_DIM
N_PAIRS = N_HEADS // 2
N_GROUPS = 4
GROUP_DIM = 256
PLE_DIM = 256
MIX = 2 * D_MODEL
N_SEG = 6
SEG_STEP = 3
F_OFF = 4 * D_MODEL
IN_COLS = N_SEG * D_MODEL + N_HEADS
LANES = 128
EPS = 1e-6
SCALE = 0.125
NEG = -1e30

ADAM_LR = 0.001
ADAM_B1 = 0.9
ADAM_B2 = 0.999
ADAM_EPS = 1e-08
ADAM_WD = 0.01
ADAM_STEP = 10

VMEM_LIMIT = 56 * 1024 * 1024
MESH = pl.DeviceIdType.MESH
ANY = pl.BlockSpec(memory_space=pl.ANY)
HBM_SPEC = pl.BlockSpec(memory_space=pltpu.HBM)
SEM_SPEC = pl.BlockSpec(memory_space=pltpu.SEMAPHORE)
SIDE_EFFECT = pltpu.SideEffectType.DATAFLOW_SIDE_EFFECTING

NT = (((1,), (1,)), ((), ()))
TN = (((0,), (0,)), ((), ()))
NN = (((1,), (0,)), ((), ()))


def _pcall(body, *, name, out_shape, grid=(), in_specs=None, out_specs=None, scratch_shapes=(), dims=None):
    kw = {}
    if in_specs is not None:
        kw["in_specs"] = in_specs
    if out_specs is not None:
        kw["out_specs"] = out_specs
    return pl.pallas_call(
        body, name=name, out_shape=out_shape, grid=grid, scratch_shapes=list(scratch_shapes),
        compiler_params=pltpu.CompilerParams(dimension_semantics=dims, vmem_limit_bytes=VMEM_LIMIT), **kw)


def _dot(a, b, dn=NN):
    return lax.dot_general(a, b, dn, preferred_element_type=F32)


def _sigmoid(x):
    return 1.0 / (1.0 + jnp.exp(-x))


def _matmul(a, b, *, ta=False, name, tm=512, tn=1024, tk=2048):
    if ta:
        kdim, m = a.shape
    else:
        m, kdim = a.shape
    n = b.shape[1]
    tm, tn, tk = min(tm, m), min(tn, n), min(tk, kdim)
    nk = kdim // tk

    def body(a_ref, b_ref, o_ref, acc_ref):
        k = pl.program_id(2)

        @pl.when(k == 0)
        def _():
            acc_ref[...] = jnp.zeros_like(acc_ref)

        acc_ref[...] += _dot(a_ref[...].astype(BF16), b_ref[...].astype(BF16), TN if ta else NN)

        @pl.when(k == nk - 1)
        def _():
            o_ref[...] = acc_ref[...].astype(BF16)

    a_spec = pl.BlockSpec((tk, tm), lambda i, j, k: (k, i)) if ta else pl.BlockSpec((tm, tk), lambda i, j, k: (i, k))
    return _pcall(
        body, name=name, out_shape=jax.ShapeDtypeStruct((m, n), BF16), grid=(m // tm, n // tn, nk),
        in_specs=[a_spec, pl.BlockSpec((tk, tn), lambda i, j, k: (k, j))],
        out_specs=pl.BlockSpec((tm, tn), lambda i, j, k: (i, j)),
        scratch_shapes=[pltpu.VMEM((tm, tn), F32)], dims=("parallel", "parallel", "arbitrary"))(a, b)


def _norm_in_proj(h, g, wm, wf, *, tm=512):
    t = h.shape[0]

    def body(h_ref, g_ref, wm_ref, wf_ref, q_ref, k_ref, v_ref, za_ref, u_ref, zp_ref, fl_ref, hn_ref):
        j = pl.program_id(1)

        @pl.when(j == 0)
        def _():
            x = h_ref[...]
            r = lax.rsqrt(jnp.mean(x * x, axis=-1, keepdims=True) + EPS)
            hn = (x * r * g_ref[...]).astype(BF16)
            hn_ref[...] = hn
            fl_ref[...] = _dot(hn, wf_ref[...])

        y = _dot(hn_ref[...], wm_ref[...])
        outs = (q_ref, k_ref, v_ref, za_ref, u_ref, zp_ref)
        for s in range(N_SEG // SEG_STEP):
            @pl.when(j == s)
            def _(s=s):
                for i, ref in enumerate(outs[SEG_STEP * s:SEG_STEP * (s + 1)]):
                    ref[...] = y[:, i * D_MODEL:(i + 1) * D_MODEL].astype(ref.dtype)

    row = lambda i, j: (i, 0)
    blk = pl.BlockSpec((tm, D_MODEL), row)
    sds = lambda dt: jax.ShapeDtypeStruct((t, D_MODEL), dt)
    return _pcall(
        body, name="norm_in_proj", grid=(t // tm, N_SEG // SEG_STEP),
        in_specs=[blk, pl.BlockSpec((1, D_MODEL), lambda i, j: (0, 0)),
                  pl.BlockSpec((D_MODEL, SEG_STEP * D_MODEL), lambda i, j: (0, j)),
                  pl.BlockSpec((D_MODEL, LANES), lambda i, j: (0, 0))],
        out_specs=[blk] * 6 + [pl.BlockSpec((tm, LANES), row), blk],
        out_shape=[sds(BF16)] * 3 + [sds(F32)] * 3 + [jax.ShapeDtypeStruct((t, LANES), F32), sds(BF16)],
        dims=("parallel", "arbitrary"))(h, g, wm, wf)


def _gate_cumsum(flt, bias):
    b, hh, s = flt.shape

    def body(fl_ref, b_ref, c_ref):
        x = fl_ref[...] + b_ref[...]
        acc = jnp.minimum(x, 0.0) - jnp.log(1.0 + jnp.exp(-jnp.abs(x)))
        idx = lax.broadcasted_iota(jnp.int32, x.shape, 1)
        sh = 1
        while sh < s:
            acc = acc + jnp.where(idx >= sh, pltpu.roll(acc, sh, 1), 0.0)
            sh *= 2
        c_ref[...] = acc

    return _pcall(
        body, name="gate_cumsum", grid=(b,),
        in_specs=[pl.BlockSpec((None, hh, s), lambda i: (i, 0, 0)), pl.BlockSpec((hh, 1), lambda i: (0, 0))],
        out_specs=pl.BlockSpec((None, hh, s), lambda i: (i, 0, 0)),
        out_shape=jax.ShapeDtypeStruct((b, hh, s), F32), dims=("parallel",))(flt, bias)


def _gate_cumsum_bwd(dcol, drow, flt, bias):
    b, hh, s = flt.shape

    def body(d_ref, dr_ref, fl_ref, b_ref, dfl_ref, db_ref):
        @pl.when(pl.program_id(0) == 0)
        def _():
            db_ref[...] = jnp.zeros_like(db_ref)

        x = fl_ref[...] + b_ref[...]
        acc = dr_ref[...] - d_ref[...]
        idx = lax.broadcasted_iota(jnp.int32, x.shape, 1)
        sh = 1
        while sh < s:
            acc = acc + jnp.where(idx + sh < s, pltpu.roll(acc, s - sh, 1), 0.0)
            sh *= 2
        e = jnp.exp(-jnp.abs(x))
        sig_neg = jnp.where(x >= 0.0, e, 1.0) / (1.0 + e)
        dfl = acc * sig_neg
        dfl_ref[...] = dfl
        db_ref[...] += jnp.sum(dfl, axis=1, keepdims=True)

    return _pcall(
        body, name="gate_cumsum_bwd", grid=(b,),
        in_specs=[pl.BlockSpec((None, hh, s), lambda i: (i, 0, 0))] * 3 + [pl.BlockSpec((hh, 1), lambda i: (0, 0))],
        out_specs=[pl.BlockSpec((None, hh, s), lambda i: (i, 0, 0)), pl.BlockSpec((hh, 1), lambda i: (0, 0))],
        out_shape=[jax.ShapeDtypeStruct((b, hh, s), F32), jax.ShapeDtypeStruct((hh, 1), F32)],
        dims=("arbitrary",))(dcol, drow, flt, bias)


def _attn_fwd(q, k, v, c4, *, nb, tq=512, rc=32, diag=256):
    t = q.shape[0]
    s = t // nb
    tq = min(tq, s)
    nq = s // tq
    tk = tq
    diag = min(diag, tq)

    def body(q_ref, k_ref, v_ref, c_ref, o_ref, lse_ref, qh_scr, s_scr, p_scr, m_scr, acc_scr):
        qi = pl.program_id(2)
        lo = lax.broadcasted_iota(jnp.int32, (tq, PAIR), 1) < HEAD_DIM
        q2 = q_ref[...] * SCALE
        zero = jnp.zeros_like(q2)
        qh_scr[0] = jnp.where(lo, q2, zero)
        qh_scr[1] = jnp.where(lo, zero, q2)
        m_scr[...] = jnp.full(m_scr.shape, NEG, F32)
        acc_scr[...] = jnp.zeros(acc_scr.shape, F32)
        whole = [(0, tq, tk, False)]
        diagonal = [(r0, r0 + diag, r0 + diag, True) for r0 in range(0, tq, diag)]

        def block(kj, bands):
            off = pl.multiple_of(kj * tk, tk)
            k2 = k_ref[pl.ds(off, tk), :]
            v2 = v_ref[pl.ds(off, tk), :]
            one = jnp.ones_like(v2)
            va = (jnp.where(lo, v2, one), jnp.where(lo, one, v2))
            crow = c_ref[:, pl.ds(off, tk)]
            for hd in range(2):
                for r0, r1, nc, _ in bands:
                    s_scr[hd, r0:r1, :nc] = _dot(qh_scr[hd, r0:r1, :], k2[:nc], NT)
            for hd in range(2):
                for r0, r1, nc, masked in bands:
                    row = lax.broadcasted_iota(jnp.int32, (rc, nc), 0)
                    col = lax.broadcasted_iota(jnp.int32, (rc, nc), 1)
                    for r in range(r0, r1, rc):
                        sc = s_scr[hd, r:r + rc, :nc] - crow[hd:hd + 1, :nc]
                        if masked:
                            sc = jnp.where(row + r >= col, sc, NEG)
                        m_old = m_scr[hd, r:r + rc, :]
                        m_new = jnp.maximum(m_old, jnp.max(sc, axis=1, keepdims=True))
                        for cb in range(0, nc, LANES):
                            p_scr[hd, r:r + rc, cb:cb + LANES] = jnp.exp(sc[:, cb:cb + LANES] - m_new).astype(BF16)
                        m_scr[hd, r:r + rc, :] = m_new
                        acc_scr[hd, r:r + rc, :] = acc_scr[hd, r:r + rc, :] * jnp.exp(m_old - m_new)
                    acc_scr[hd, r0:r1, :] += _dot(p_scr[hd, r0:r1, :nc], va[hd][:nc])

        def below(kj, carry):
            block(kj, whole)
            return carry

        lax.fori_loop(0, qi, below, 0)
        block(qi, diagonal)
        a0, a1 = acc_scr[0], acc_scr[1]
        den = jnp.where(lo, pltpu.roll(a0, HEAD_DIM, 1), pltpu.roll(a1, HEAD_DIM, 1))
        o_ref[...] = jnp.where(lo, a0, a1) / den
        lse_t = (jnp.where(lo, m_scr[0], m_scr[1]) + jnp.log(den)).T
        lse_ref[0:1, :] = lse_t[0:1, :]
        lse_ref[1:2, :] = lse_t[HEAD_DIM:HEAD_DIM + 1, :]

    qspec = pl.BlockSpec((tq, PAIR), lambda b, hp, i: (b * nq + i, hp))
    kvspec = pl.BlockSpec((s, PAIR), lambda b, hp, i: (b, hp))
    return _pcall(
        body, name="attn_fwd", grid=(nb, N_PAIRS, nq),
        in_specs=[qspec, kvspec, kvspec, pl.BlockSpec((None, None, 2, s), lambda b, hp, i: (b, hp, 0, 0))],
        out_specs=[qspec, pl.BlockSpec((None, None, 2, tq), lambda b, hp, i: (b, hp, 0, i))],
        out_shape=[jax.ShapeDtypeStruct((t, D_MODEL), F32), jax.ShapeDtypeStruct((nb, N_PAIRS, 2, s), F32)],
        scratch_shapes=[pltpu.VMEM((2, tq, PAIR), BF16), pltpu.VMEM((2, tq, tk), F32), pltpu.VMEM((2, tq, tk), BF16),
                        pltpu.VMEM((2, tq, LANES), F32), pltpu.VMEM((2, tq, PAIR), F32)],
        dims=("parallel", "parallel", "arbitrary"))(q, k, v, c4)


def _attn_bwd(q, k, v, do, c4, lse4, delta4, *, nb, tk=512, rc=16, diag=256):
    t = q.shape[0]
    s = t // nb
    tk = min(tk, s)
    nk = s // tk
    tq = tk
    diag = min(diag, tk)

    def body(q_ref, do_ref, k_ref, v_ref, c_ref, lse_ref, dl_ref, dq_ref, dk_ref, dv_ref, dc_ref, dr_ref,
             kz_scr, vz_scr, ko_scr, crep_scr, st_scr, dp_scr, pt_scr, ds_scr, dk_scr, dv_scr, dq_scr, dr_scr):
        kj = pl.program_id(2)

        @pl.when(kj == 0)
        def _():
            dq_scr[...] = jnp.zeros_like(dq_scr)
            dr_scr[...] = jnp.zeros_like(dr_scr)

        lo = lax.broadcasted_iota(jnp.int32, (tk, PAIR), 1) < HEAD_DIM
        k2 = k_ref[...]
        v2 = v_ref[...]
        zero = jnp.zeros_like(k2)
        one = jnp.ones_like(k2)
        kz_scr[0] = jnp.where(lo, k2, zero)
        kz_scr[1] = jnp.where(lo, zero, k2)
        vz_scr[0] = jnp.where(lo, v2, zero)
        vz_scr[1] = jnp.where(lo, zero, v2)
        ko_scr[0] = jnp.where(lo, k2, one)
        ko_scr[1] = jnp.where(lo, one, k2)
        for hd in range(2):
            crep_scr[hd] = jnp.broadcast_to(c_ref[hd:hd + 1, :], (LANES, tk)).T
        dk_scr[...] = jnp.zeros(dk_scr.shape, F32)
        dv_scr[...] = jnp.zeros(dv_scr.shape, F32)
        row = lax.broadcasted_iota(jnp.int32, (rc, LANES), 0)
        col = lax.broadcasted_iota(jnp.int32, (rc, LANES), 1)

        whole = [(0, tk, 0, False)]
        diagonal = [(r0, r0 + diag, r0, True) for r0 in range(0, tk, diag)]

        def block(qi, bands):
            off = pl.multiple_of(qi * tq, tq)
            q2 = q_ref[pl.ds(off, tq), :] * SCALE
            do2 = do_ref[pl.ds(off, tq), :]
            lse = lse_ref[:, pl.ds(off, tq)]
            dl = dl_ref[:, pl.ds(off, tq)]
            qo = (jnp.where(lo, q2, jnp.ones_like(q2)), jnp.where(lo, jnp.ones_like(q2), q2))
            for hd in range(2):
                for r0, r1, c0, _ in bands:
                    st_scr[hd, r0:r1, c0:] = _dot(kz_scr[hd, r0:r1, :], q2[c0:], NT)
                    dp_scr[hd, r0:r1, c0:] = _dot(vz_scr[hd, r0:r1, :], do2[c0:], NT)
            for r0, r1, c0, masked in bands:
                dqs = []
                for hd in range(2):
                    for r in range(r0, r1, rc):
                        c_rep = crep_scr[hd, r:r + rc, :]
                        for cb in range(c0, tq, LANES):
                            pt = jnp.exp(st_scr[hd, r:r + rc, cb:cb + LANES] - c_rep - lse[hd:hd + 1, cb:cb + LANES])
                            if masked and cb < c0 + diag:
                                pt = jnp.where(col + cb >= row + r, pt, 0.0)
                            dst = pt * (dp_scr[hd, r:r + rc, cb:cb + LANES] - dl[hd:hd + 1, cb:cb + LANES])
                            pt_scr[hd, r:r + rc, cb:cb + LANES] = pt.astype(BF16)
                            ds_scr[hd, r:r + rc, cb:cb + LANES] = dst.astype(BF16)
                    dv_scr[hd, r0:r1, :] += _dot(pt_scr[hd, r0:r1, c0:], do2[c0:])
                    dk_scr[hd, r0:r1, :] += _dot(ds_scr[hd, r0:r1, c0:], qo[hd][c0:])
                    dqs.append(_dot(ds_scr[hd, r0:r1, c0:], ko_scr[hd, r0:r1, :], TN))
                dq_scr[pl.ds(off + c0, tq - c0), :] += jnp.where(lo[c0:], dqs[0], dqs[1]) * SCALE
                dr_scr[pl.ds(off + c0, tq - c0), :] += jnp.where(lo[c0:], dqs[1], dqs[0])

        block(kj, diagonal)

        def below(qi, carry):
            block(qi, whole)
            return carry

        lax.fori_loop(kj + 1, nk, below, 0)
        dk_ref[...] = jnp.where(lo, dk_scr[0], dk_scr[1]).astype(BF16)
        dv_ref[...] = jnp.where(lo, dv_scr[0], dv_scr[1]).astype(BF16)
        dc_t = jnp.where(lo, dk_scr[1], dk_scr[0]).T
        dc_ref[0:1, :] = dc_t[HEAD_DIM:HEAD_DIM + 1, :]
        dc_ref[1:2, :] = dc_t[0:1, :]

        @pl.when(kj == nk - 1)
        def _():
            dq_ref[...] = dq_scr[...].astype(BF16)
            for r in range(0, s, tq):
                dr_t = dr_scr[r:r + tq, :].T
                dr_ref[0:1, r:r + tq] = dr_t[HEAD_DIM:HEAD_DIM + 1, :]
                dr_ref[1:2, r:r + tq] = dr_t[0:1, :]

    full = pl.BlockSpec((s, PAIR), lambda b, hp, j: (b, hp))
    kblk = pl.BlockSpec((tk, PAIR), lambda b, hp, j: (b * nk + j, hp))
    rows = pl.BlockSpec((None, None, 2, s), lambda b, hp, j: (b, hp, 0, 0))
    krows = pl.BlockSpec((None, None, 2, tk), lambda b, hp, j: (b, hp, 0, j))
    sds = lambda dt: jax.ShapeDtypeStruct((t, D_MODEL), dt)
    rows_sds = jax.ShapeDtypeStruct((nb, N_PAIRS, 2, s), F32)
    pair_bf = pltpu.VMEM((2, tk, PAIR), BF16)
    pair_f = pltpu.VMEM((2, tk, PAIR), F32)
    return _pcall(
        body, name="attn_bwd", grid=(nb, N_PAIRS, nk),
        in_specs=[full, full, kblk, kblk, krows, rows, rows],
        out_specs=[full, kblk, kblk, krows, rows],
        out_shape=[sds(BF16), sds(BF16), sds(BF16), rows_sds, rows_sds],
        scratch_shapes=[pair_bf, pair_bf, pair_bf, pair_f, pltpu.VMEM((2, tk, tq), F32), pltpu.VMEM((2, tk, tq), F32),
                        pltpu.VMEM((2, tk, tq), BF16), pltpu.VMEM((2, tk, tq), BF16), pair_f, pair_f,
                        pltpu.VMEM((s, PAIR), F32), pltpu.VMEM((s, PAIR), F32)],
        dims=("parallel", "parallel", "arbitrary"))(q, do, k, v, c4, lse4, delta4)


def _window_sum(x, g, s, *, ahead):
    row = lax.broadcasted_iota(jnp.int32, x.shape, 0)
    for step in range(N_GROUPS):
        sh = 1 << step
        if ahead:
            moved = jnp.where(row + sh < s, pltpu.roll(x, s - sh, 0), 0.0)
        else:
            moved = jnp.where(row >= sh, pltpu.roll(x, sh, 0), 0.0)
        x = jnp.where(step <= g, x + moved, x)
    return x


def _inv_count(g, s):
    pos = lax.broadcasted_iota(jnp.int32, (s, 1), 0) + 1
    return 1.0 / jnp.minimum(pos, 2 << g).astype(F32)


def _pool_fwd(u, zp, wpool, ps, *, nb):
    t = u.shape[0]
    s = t // nb

    def body(u_ref, zp_ref, w_ref, ps_ref, out_ref, pooled_ref, mixed_ref):
        g = pl.program_id(1)
        x = u_ref[...]
        pooled = (_window_sum(x, g, s, ahead=False) * _inv_count(g, s) - x).astype(BF16)
        mixed = _dot(pooled, w_ref[...])
        z = zp_ref[...]
        pooled_ref[...] = pooled
        mixed_ref[...] = mixed
        out_ref[...] = (mixed * ps_ref[...] * (z * _sigmoid(z))).astype(BF16)

    blk = pl.BlockSpec((s, GROUP_DIM), lambda b, g: (b, g))
    sds = lambda dt: jax.ShapeDtypeStruct((t, D_MODEL), dt)
    return _pcall(
        body, name="pool_fwd", grid=(nb, N_GROUPS),
        in_specs=[blk, blk, pl.BlockSpec((None, GROUP_DIM, GROUP_DIM), lambda b, g: (g, 0, 0)),
                  pl.BlockSpec((1, GROUP_DIM), lambda b, g: (0, g))],
        out_specs=[blk, blk, blk], out_shape=[sds(BF16), sds(BF16), sds(F32)],
        dims=("parallel", "parallel"))(u, zp, wpool, ps)


def _pool_bwd(dpool, mixed, zp, wpool, ps, *, nb):
    t = dpool.shape[0]
    s = t // nb

    def body(dp_ref, mx_ref, zp_ref, w_ref, ps_ref, dmx_ref, du_ref, dzp_ref, dps_ref):
        g = pl.program_id(0)

        @pl.when(pl.program_id(1) == 0)
        def _():
            dps_ref[...] = jnp.zeros_like(dps_ref)

        dp = dp_ref[...]
        mixed = mx_ref[...]
        z = zp_ref[...]
        scale = ps_ref[...]
        sg = _sigmoid(z)
        silu = z * sg
        dmixed = (dp * scale * silu).astype(BF16)
        dmx_ref[...] = dmixed
        dps_ref[...] += jnp.sum(dp * mixed * silu, axis=0, keepdims=True)
        dzp_ref[...] = (dp * mixed * scale * (sg * (1.0 + z * (1.0 - sg)))).astype(BF16)
        dpooled = _dot(dmixed, w_ref[...], NT)
        du = _window_sum(dpooled * _inv_count(g, s), g, s, ahead=True) - dpooled
        du_ref[...] = du.astype(BF16)

    blk = pl.BlockSpec((s, GROUP_DIM), lambda g, b: (b, g))
    sds = lambda dt: jax.ShapeDtypeStruct((t, D_MODEL), dt)
    return _pcall(
        body, name="pool_bwd", grid=(N_GROUPS, nb),
        in_specs=[blk, blk, blk, pl.BlockSpec((None, GROUP_DIM, GROUP_DIM), lambda g, b: (g, 0, 0)),
                  pl.BlockSpec((1, GROUP_DIM), lambda g, b: (0, g))],
        out_specs=[blk, blk, blk, pl.BlockSpec((1, GROUP_DIM), lambda g, b: (0, g))],
        out_shape=[sds(BF16), sds(BF16), sds(BF16), jax.ShapeDtypeStruct((1, D_MODEL), F32)],
        dims=("parallel", "arbitrary"))(dpool, mixed, zp, wpool, ps)


def _pool_wgrad(pooled, dmixed, *, tk=1024):
    t = pooled.shape[0]
    tk = min(tk, t)
    nk = t // tk

    def body(a_ref, b_ref, o_ref):
        @pl.when(pl.program_id(1) == 0)
        def _():
            o_ref[...] = jnp.zeros_like(o_ref)

        o_ref[...] += _dot(a_ref[...], b_ref[...], TN)

    blk = pl.BlockSpec((tk, GROUP_DIM), lambda g, k: (k, g))
    return _pcall(
        body, name="pool_wgrad", grid=(N_GROUPS, nk), in_specs=[blk, blk],
        out_specs=pl.BlockSpec((None, GROUP_DIM, GROUP_DIM), lambda g, k: (g, 0, 0)),
        out_shape=jax.ShapeDtypeStruct((N_GROUPS, GROUP_DIM, GROUP_DIM), F32),
        dims=("parallel", "arbitrary"))(pooled, dmixed)


def _out_block(h, o, za, pool, p, wo, wpg, wpe, gpost, *, tm=256):
    t = h.shape[0]

    def body(h_ref, o_ref, za_ref, pool_ref, p_ref, wo_ref, wpg_ref, wpe_ref, g_ref,
             h2_ref, mix_ref, h1_ref, gate_ref, pe_ref, cat_ref):
        z = za_ref[...]
        a = (o_ref[...] * (z * _sigmoid(z))).astype(BF16)
        pool_v = pool_ref[...]
        cat_ref[:, :D_MODEL] = a
        cat_ref[:, D_MODEL:] = pool_v
        mix = _dot(a, wo_ref[:D_MODEL, :]) + _dot(pool_v, wo_ref[D_MODEL:, :])
        r = lax.rsqrt(jnp.mean(mix * mix, axis=-1, keepdims=True) + EPS)
        h1 = h_ref[...] + mix * r * g_ref[...]
        gate = _sigmoid(_dot(h1.astype(BF16), wpg_ref[...]))
        pe = _dot(p_ref[...].astype(BF16), wpe_ref[...])
        mix_ref[...] = mix
        h1_ref[...] = h1
        gate_ref[...] = gate
        pe_ref[...] = pe
        h2_ref[...] = h1 + gate * pe

    row = lambda i: (i, 0)
    const = lambda i: (0, 0)
    blk = pl.BlockSpec((tm, D_MODEL), row)
    sds = jax.ShapeDtypeStruct((t, D_MODEL), F32)
    return _pcall(
        body, name="out_block", grid=(t // tm,),
        in_specs=[blk, blk, blk, blk, pl.BlockSpec((tm, PLE_DIM), row),
                  pl.BlockSpec((MIX, D_MODEL), const), pl.BlockSpec((D_MODEL, D_MODEL), const),
                  pl.BlockSpec((PLE_DIM, D_MODEL), const), pl.BlockSpec((1, D_MODEL), const)],
        out_specs=[blk] * 5 + [pl.BlockSpec((tm, MIX), row)],
        out_shape=[sds] * 5 + [jax.ShapeDtypeStruct((t, MIX), BF16)],
        dims=("parallel",))(h, o, za, pool, p, wo, wpg, wpe, gpost)


def _loss_grad(y, target, *, tm=512):
    t = y.shape[0]

    def body(y_ref, t_ref, dy_ref, loss_ref):
        @pl.when(pl.program_id(0) == 0)
        def _():
            loss_ref[...] = jnp.zeros_like(loss_ref)

        err = y_ref[...] - t_ref[...]
        dy_ref[...] = err * (1.0 / D_MODEL)
        part = jnp.sum(jnp.sum(err * err, axis=1, keepdims=True), axis=0, keepdims=True)
        loss_ref[...] += part * (0.5 / D_MODEL)

    blk = pl.BlockSpec((tm, D_MODEL), lambda i: (i, 0))
    return _pcall(
        body, name="loss_grad", grid=(t // tm,), in_specs=[blk, blk],
        out_specs=[blk, pl.BlockSpec((8, LANES), lambda i: (0, 0))],
        out_shape=[jax.ShapeDtypeStruct((t, D_MODEL), F32), jax.ShapeDtypeStruct((8, LANES), F32)],
        dims=("arbitrary",))(y, target)


def _out_block_bwd(dh2, gate, pe, mix, o, za, wpg, wo, gpost, gsum, *, tm=256):
    t = dh2.shape[0]

    def body(dh2_ref, gate_ref, pe_ref, mix_ref, o_ref, za_ref, wpg_ref, wo_ref, g_ref, gs_ref,
             dh1_ref, dgp_ref, dpe_ref, dmix_ref, do_ref, dza_ref, dpool_ref, delta_ref, dg_ref):
        @pl.when(pl.program_id(0) == 0)
        def _():
            dg_ref[...] = jnp.zeros_like(dg_ref)

        dh2 = dh2_ref[...]
        gate = gate_ref[...]
        dpe_ref[...] = (dh2 * gate).astype(BF16)
        dgp = (dh2 * pe_ref[...] * gate * (1.0 - gate)).astype(BF16)
        dgp_ref[...] = dgp
        dh1 = dh2 + _dot(dgp, wpg_ref[...], NT)
        dh1_ref[...] = dh1
        mix = mix_ref[...]
        r = lax.rsqrt(jnp.mean(mix * mix, axis=-1, keepdims=True) + EPS)
        dg_ref[...] += jnp.sum(dh1 * mix * r, axis=0, keepdims=True)
        a = dh1 * g_ref[...]
        dmix = (r * a - mix * (r * r * r) * jnp.mean(a * mix, axis=-1, keepdims=True)).astype(BF16)
        dmix_ref[...] = dmix
        dattn = _dot(dmix, wo_ref[:D_MODEL, :], NT)
        dpool_ref[...] = _dot(dmix, wo_ref[D_MODEL:, :], NT)
        z = za_ref[...]
        sg = _sigmoid(z)
        o = o_ref[...]
        do = (dattn * (z * sg)).astype(BF16)
        do_ref[...] = do
        dza_ref[...] = (dattn * o * (sg * (1.0 + z * (1.0 - sg)))).astype(BF16)
        prod = do.astype(F32) * o
        hi = prod.astype(BF16)
        rest = (prod - hi.astype(F32)).astype(BF16)
        delta_ref[...] = _dot(hi, gs_ref[...]) + _dot(rest, gs_ref[...])

    row = lambda i: (i, 0)
    const = lambda i: (0, 0)
    blk = pl.BlockSpec((tm, D_MODEL), row)
    sds = lambda dt: jax.ShapeDtypeStruct((t, D_MODEL), dt)
    return _pcall(
        body, name="out_block_bwd", grid=(t // tm,),
        in_specs=[blk] * 6 + [pl.BlockSpec((D_MODEL, D_MODEL), const), pl.BlockSpec((MIX, D_MODEL), const),
                              pl.BlockSpec((1, D_MODEL), const), pl.BlockSpec((D_MODEL, LANES), const)],
        out_specs=[blk] * 7 + [pl.BlockSpec((tm, LANES), row), pl.BlockSpec((1, D_MODEL), const)],
        out_shape=[sds(F32)] + [sds(BF16)] * 5 + [sds(F32), jax.ShapeDtypeStruct((t, LANES), F32),
                                                 jax.ShapeDtypeStruct((1, D_MODEL), F32)],
        dims=("arbitrary",))(dh2, gate, pe, mix, o, za, wpg, wo, gpost, gsum)


def _in_proj_bwd(dsegs, dfl, wm, wf, h, dh1, gpre, *, tm=512):
    t = h.shape[0]

    def body(*refs):
        seg_refs = refs[:N_SEG]
        dfl_ref, wm_ref, wf_ref, h_ref, dh1_ref, g_ref, dh_ref, dg_ref, acc_ref = refs[N_SEG:]
        i = pl.program_id(0)
        j = pl.program_id(1)

        @pl.when(jnp.logical_and(i == 0, j == 0))
        def _():
            dg_ref[...] = jnp.zeros_like(dg_ref)

        @pl.when(j == 0)
        def _():
            acc_ref[...] = _dot(dfl_ref[...].astype(BF16), wf_ref[...], NT)

        for s in range(N_SEG // SEG_STEP):
            @pl.when(j == s)
            def _(s=s):
                part = None
                for i, ref in enumerate(seg_refs[SEG_STEP * s:SEG_STEP * (s + 1)]):
                    term = _dot(ref[...].astype(BF16), wm_ref[:, i * D_MODEL:(i + 1) * D_MODEL], NT)
                    part = term if part is None else part + term
                acc_ref[...] += part

        @pl.when(j == N_SEG // SEG_STEP - 1)
        def _():
            dhn = acc_ref[...]
            x = h_ref[...]
            r = lax.rsqrt(jnp.mean(x * x, axis=-1, keepdims=True) + EPS)
            dg_ref[...] += jnp.sum(dhn * x * r, axis=0, keepdims=True)
            a = dhn * g_ref[...]
            dh_ref[...] = dh1_ref[...] + r * a - x * (r * r * r) * jnp.mean(a * x, axis=-1, keepdims=True)

    row = lambda i, j: (i, 0)
    const = lambda i, j: (0, 0)
    blk = pl.BlockSpec((tm, D_MODEL), row)
    return _pcall(
        body, name="in_proj_bwd", grid=(t // tm, N_SEG // SEG_STEP),
        in_specs=[blk] * N_SEG + [pl.BlockSpec((tm, LANES), row), pl.BlockSpec((D_MODEL, SEG_STEP * D_MODEL), lambda i, j: (0, j)),
                                  pl.BlockSpec((D_MODEL, LANES), const), blk, blk, pl.BlockSpec((1, D_MODEL), const)],
        out_specs=[blk, pl.BlockSpec((1, D_MODEL), const)],
        out_shape=[jax.ShapeDtypeStruct((t, D_MODEL), F32), jax.ShapeDtypeStruct((1, D_MODEL), F32)],
        scratch_shapes=[pltpu.VMEM((tm, D_MODEL), F32)],
        dims=("arbitrary", "arbitrary"))(*dsegs, dfl, wm, wf, h, dh1, gpre)


def _chip_peers(x, y):
    return [(1 - x, y), (x, 1 - y), (1 - x, 1 - y)]


def _ici_views(src, land, gather, x, y, c, pi, px, py):
    if gather:
        return src.at[c], land.at[pi], land.at[pi]
    return src.at[2 * px + py], land.at[2 * x + y], land.at[2 * px + py]


def _ici_start(srcs, lands, after, *, gather, name):
    n = len(srcs)

    def body(*refs):
        src, land = refs[:n], refs[n:2 * n]
        send, recv = refs[2 * n + 1], refs[2 * n + 2]
        token = refs[-1]
        x, y, c = lax.axis_index("x"), lax.axis_index("y"), lax.axis_index("c")
        for kk in range(n):
            for pi, (px, py) in enumerate(_chip_peers(x, y)):
                mine, there, _ = _ici_views(src[kk], land[kk], gather, x, y, c, pi, px, py)
                pltpu.make_async_remote_copy(src_ref=mine, dst_ref=there, send_sem=send.at[3 * kk + pi],
                                             recv_sem=recv.at[3 * kk + pi], device_id=(px, py, c),
                                             device_id_type=MESH).start()
        token[...] = jnp.zeros_like(token)

    hbm = lambda a: pltpu.with_memory_space_constraint(a, pltpu.HBM)
    outs = pl.pallas_call(
        body, name=name,
        out_shape=(pltpu.SemaphoreType.DMA((3 * n,)), pltpu.SemaphoreType.DMA((3 * n,)),
                   *[pltpu.HBM(a.shape, a.dtype) for a in srcs], *[pltpu.HBM(a.shape, a.dtype) for a in lands],
                   jax.ShapeDtypeStruct((8, LANES), F32)),
        in_specs=[HBM_SPEC] * (2 * n) + [ANY],
        out_specs=(SEM_SPEC, SEM_SPEC, *[HBM_SPEC] * (2 * n), pl.BlockSpec(memory_space=pltpu.VMEM)),
        input_output_aliases={i: 2 + i for i in range(2 * n)},
        compiler_params=pltpu.CompilerParams(has_side_effects=SIDE_EFFECT),
    )(*[hbm(a) for a in srcs], *[hbm(a) for a in lands], after)
    return outs[0], outs[1], list(outs[2:2 + n]), list(outs[2 + n:2 + 2 * n]), outs[-1]


def _ici_wait(send, recv, srcs, lands, after, *, gather, name):
    n = len(srcs)

    def body(*refs):
        src, land = refs[:n], refs[n:2 * n]
        send_ref, recv_ref = refs[2 * n], refs[2 * n + 1]
        x, y, c = lax.axis_index("x"), lax.axis_index("y"), lax.axis_index("c")
        for kk in range(n):
            for pi, (px, py) in enumerate(_chip_peers(x, y)):
                mine, _, here = _ici_views(src[kk], land[kk], gather, x, y, c, pi, px, py)
                cp = pltpu.make_async_remote_copy(src_ref=mine, dst_ref=here, send_sem=send_ref.at[3 * kk + pi],
                                                  recv_sem=recv_ref.at[3 * kk + pi], device_id=(px, py, c),
                                                  device_id_type=MESH)
                cp.wait_send()
                cp.wait_recv()

    outs = pl.pallas_call(
        body, name=name,
        out_shape=[pltpu.HBM(a.shape, a.dtype) for a in srcs] + [pltpu.HBM(a.shape, a.dtype) for a in lands],
        in_specs=[HBM_SPEC] * (2 * n) + [SEM_SPEC, SEM_SPEC] + [ANY] * len(after), out_specs=[HBM_SPEC] * (2 * n),
        input_output_aliases={i: i for i in range(2 * n)},
        compiler_params=pltpu.CompilerParams(has_side_effects=SIDE_EFFECT),
    )(*srcs, *lands, send, recv, *after)
    return list(outs[:n]), list(outs[n:])


def _sibling_send(arrays, *, name):
    n = len(arrays)

    def body(*refs):
        ins, outs = refs[:n], refs[n:2 * n]
        send, recv = refs[2 * n:]
        sib = (lax.axis_index("x"), lax.axis_index("y"), 1 - lax.axis_index("c"))
        copies = [pltpu.make_async_remote_copy(src_ref=ins[kk], dst_ref=outs[kk], send_sem=send.at[kk],
                                               recv_sem=recv.at[kk], device_id=sib, device_id_type=MESH)
                  for kk in range(n)]
        for cp in copies:
            cp.start()
        for cp in copies:
            cp.wait_recv()
        for cp in copies:
            cp.wait_send()

    return _pcall(
        body, name=name, in_specs=[ANY] * n, out_specs=[ANY] * n,
        out_shape=[jax.ShapeDtypeStruct(a.shape, a.dtype) for a in arrays],
        scratch_shapes=[pltpu.SemaphoreType.DMA((n,)), pltpu.SemaphoreType.DMA((n,))])(*arrays)


def _all_gather_small(a, *, name):
    def body(a_ref, o_ref, send, recv, loc):
        x, y, c = lax.axis_index("x"), lax.axis_index("y"), lax.axis_index("c")
        me = 4 * x + 2 * y + c

        def peer(rel):
            fx, fy, fc = (rel >> 2) & 1, (rel >> 1) & 1, rel & 1
            px = (1 - x) if fx else x
            py = (1 - y) if fy else y
            pc = (1 - c) if fc else c
            return px, py, pc

        def remote(rel, slot):
            return pltpu.make_async_remote_copy(
                src_ref=a_ref, dst_ref=o_ref.at[slot], send_sem=send.at[rel - 1], recv_sem=recv.at[rel - 1],
                device_id=peer(rel), device_id_type=MESH)

        mine = pltpu.make_async_copy(a_ref, o_ref.at[me], loc)
        mine.start()
        sends = [remote(rel, me) for rel in range(1, 8)]
        for cp in sends:
            cp.start()
        for rel in range(1, 8):
            px, py, pc = peer(rel)
            remote(rel, 4 * px + 2 * py + pc).wait_recv()
        for cp in sends:
            cp.wait_send()
        mine.wait()

    return _pcall(
        body, name=name, in_specs=[ANY], out_specs=ANY,
        out_shape=jax.ShapeDtypeStruct((8,) + a.shape, a.dtype),
        scratch_shapes=[pltpu.SemaphoreType.DMA((7,)), pltpu.SemaphoreType.DMA((7,)), pltpu.SemaphoreType.DMA])(a)


def _to_bf16(a, *, name, tr=256):
    rows, cols = a.shape

    def body(a_ref, o_ref):
        o_ref[...] = a_ref[...].astype(BF16)

    blk = pl.BlockSpec((tr, cols), lambda i: (i, 0))
    return _pcall(body, name=name, grid=(rows // tr,), in_specs=[blk], out_specs=blk,
                  out_shape=jax.ShapeDtypeStruct(a.shape, BF16), dims=("parallel",))(a)


def _pair_sum(a, b, *, name, tr=256):
    _, rows, cols = a.shape
    tr = min(tr, rows)

    def body(a_ref, b_ref, o_ref):
        o_ref[...] = (a_ref[...].astype(F32) + b_ref[...].astype(F32)).astype(BF16)

    blk = pl.BlockSpec((None, tr, cols), lambda j, i: (j, i, 0))
    return _pcall(body, name=name, grid=(4, rows // tr), in_specs=[blk, blk], out_specs=blk,
                  out_shape=jax.ShapeDtypeStruct(a.shape, BF16), dims=("parallel", "parallel"))(a, b)


def _chip_sum(own, recv, chip, *, name, tr=256):
    _, rows, cols = own.shape
    tr = min(tr, rows)

    def body(chip_ref, own_ref, r1_ref, r2_ref, r3_ref, o_ref):
        acc = own_ref[...].astype(F32)
        for ref in (r1_ref, r2_ref, r3_ref):
            acc = acc + ref[...].astype(F32)
        o_ref[...] = acc

    def slot(step):
        return pl.BlockSpec((None, tr, cols), lambda i, chip_ref: ((chip_ref[0] + step) % 4, i, 0))

    return pl.pallas_call(
        body, name=name, out_shape=jax.ShapeDtypeStruct((rows, cols), F32),
        grid_spec=pltpu.PrefetchScalarGridSpec(
            num_scalar_prefetch=1, grid=(rows // tr,), in_specs=[slot(0), slot(1), slot(2), slot(3)],
            out_specs=pl.BlockSpec((tr, cols), lambda i, chip_ref: (i, 0))),
        compiler_params=pltpu.CompilerParams(dimension_semantics=("parallel",), vmem_limit_bytes=VMEM_LIMIT),
    )(chip, own, recv, recv, recv)


def _adamw_math(w, g, m, v):
    m = ADAM_B1 * m + (1.0 - ADAM_B1) * g
    v = ADAM_B2 * v + (1.0 - ADAM_B2) * (g * g)
    m_hat = m / (1.0 - ADAM_B1 ** ADAM_STEP)
    v_hat = v / (1.0 - ADAM_B2 ** ADAM_STEP)
    delta = -ADAM_LR * (m_hat / (jnp.sqrt(v_hat) + ADAM_EPS) + ADAM_WD * w)
    return delta, m, v


def _adamw(w, m, v, g, layer, prev, *, name, tr=128):
    rows, cols = g.shape
    tr = min(tr, rows)
    nblk = rows // tr

    def body(w_ref, m_ref, v_ref, gin_ref, *refs):
        g_ref, d_ref, mo_ref, vo_ref = refs[-4:]
        grad = gin_ref[...]
        delta, m_new, v_new = _adamw_math(w_ref[...], grad, m_ref[...], v_ref[...])
        g_ref[...] = grad
        d_ref[...] = delta
        mo_ref[...] = m_new
        vo_ref[...] = v_new

    blk = pl.BlockSpec((tr, cols), lambda i: (layer * nblk + i, 0))
    carried = [] if prev is None else list(prev)
    return pl.pallas_call(
        body, name=name, grid=(nblk,), out_shape=[jax.ShapeDtypeStruct(w.shape, F32)] * 4,
        in_specs=[blk] * 3 + [pl.BlockSpec((tr, cols), lambda i: (i, 0))] + [ANY] * len(carried), out_specs=[blk] * 4,
        input_output_aliases={4 + i: i for i in range(len(carried))},
        compiler_params=pltpu.CompilerParams(dimension_semantics=("parallel",), vmem_limit_bytes=VMEM_LIMIT),
    )(w, m, v, g, *carried)


def _adamw_small(w, m, v, parts, *, name):
    def body(w_ref, m_ref, v_ref, p_ref, g_ref, d_ref, mo_ref, vo_ref):
        g = p_ref[0]
        for dev in range(1, 8):
            g = g + p_ref[dev]
        delta, m_new, v_new = _adamw_math(w_ref[...], g, m_ref[...], v_ref[...])
        g_ref[...] = g
        d_ref[...] = delta
        mo_ref[...] = m_new
        vo_ref[...] = v_new

    return _pcall(body, name=name, out_shape=[jax.ShapeDtypeStruct(w.shape, F32)] * 4)(w, m, v, parts)


def _column_parts(segments, lo, hi):
    parts, start = [], 0
    for seg in segments:
        a, b = max(lo, start), min(hi, start + seg.shape[1])
        if a < b:
            parts.append(seg[:, a - start:b - start])
        start += seg.shape[1]
    return parts


def _rows_from_heads(a, nb, s):
    return a.reshape(nb, s, N_HEADS).transpose(0, 2, 1).reshape(nb, N_PAIRS, 2, s)


def _layer_fwd(h, p_l, wts, nb):
    t = h.shape[0]
    s = t // nb
    q, k, v, za, u, zp, fl, hn = _norm_in_proj(h, wts["gpre"], wts["wm"], wts["wf"])
    flt = fl[:, :N_HEADS].reshape(nb, s, N_HEADS).transpose(0, 2, 1)
    c = _gate_cumsum(flt, wts["bf"])
    o, lse = _attn_fwd(q, k, v, c.reshape(nb, N_PAIRS, 2, s), nb=nb)
    pool, pooled, mixed = _pool_fwd(u, zp, wts["wpool"], wts["ps"], nb=nb)
    h2, mix, h1, gate, pe, cat = _out_block(h, o, za, pool, p_l, wts["wo"], wts["wpg"], wts["wpe"], wts["gpost"])
    saved = dict(h=h, hn=hn, q=q, k=k, v=v, za=za, zp=zp, flt=flt, c=c, o=o, lse=lse, pooled=pooled, mixed=mixed,
                 mix=mix, h1=h1, gate=gate, pe=pe, cat=cat, p=p_l)
    return h2, saved


def _layer_bwd(dh2, sv, wts, gsum, nb):
    t = dh2.shape[0]
    s = t // nb
    dh1, dgp, dpe, dmix, do, dza, dpool, delta, dgpost = _out_block_bwd(
        dh2, sv["gate"], sv["pe"], sv["mix"], sv["o"], sv["za"], wts["wpg"], wts["wo"], wts["gpost"], gsum)
    g_wpe = _matmul(sv["p"], dpe, ta=True, name="wgrad_pe")
    g_wpg = _matmul(sv["h1"], dgp, ta=True, name="wgrad_pg")
    g_wo = _matmul(sv["cat"], dmix, ta=True, name="wgrad_out")

    delta4 = _rows_from_heads(delta[:, :N_HEADS], nb, s)
    dq, dk, dv, dcol, drow = _attn_bwd(sv["q"], sv["k"], sv["v"], do, sv["c"].reshape(nb, N_PAIRS, 2, s), sv["lse"],
                                       delta4, nb=nb)
    dflt, dbf = _gate_cumsum_bwd(dcol.reshape(nb, N_HEADS, s), drow.reshape(nb, N_HEADS, s), sv["flt"], wts["bf"])
    dfl = jnp.pad(dflt.transpose(0, 2, 1).reshape(t, N_HEADS), ((0, 0), (0, LANES - N_HEADS)))

    dmixed, du, dzp, dps = _pool_bwd(dpool, sv["mixed"], sv["zp"], wts["wpool"], wts["ps"], nb=nb)
    g_wpool = _pool_wgrad(sv["pooled"], dmixed)

    dsegs = (dq, dk, dv, dza, du, dzp)
    dh, dgpre = _in_proj_bwd(dsegs, dfl, wts["wm"], wts["wf"], sv["h"], dh1, wts["gpre"])
    g_segs = [_matmul(sv["hn"], dx, ta=True, name="wgrad_in") for dx in dsegs]
    g_wf = _matmul(sv["hn"], dfl, ta=True, name="wgrad_in_f")
    g_win = g_segs[:4] + [g_wf[:, :N_HEADS]] + g_segs[4:]
    grads = dict(w_in=g_win, w_out=g_wo, w_pg=g_wpg, w_pe=g_wpe, w_pool=g_wpool,
                 norm_pre=dgpre[0], norm_post=dgpost[0], pool_scale=dps[0], b_f=dbf[:, 0])
    return dh, grads


def kernel(x, p, norm_pre, norm_post, w_in, b_f, w_pool, pool_scale, w_out, w_pg, w_pe, loss_target, m_norm_pre, m_norm_post, m_w_in, m_b_f, m_w_pool, m_pool_scale, m_w_out, m_w_pg, m_w_pe, v_norm_pre, v_norm_post, v_w_in, v_b_f, v_w_pool, v_pool_scale, v_w_out, v_w_pg, v_w_pe):
    nb, s, _ = x.shape
    t = nb * s
    depth = w_in.shape[0]
    big = dict(w_in=(w_in, m_w_in, v_w_in), w_out=(w_out, m_w_out, v_w_out), w_pg=(w_pg, m_w_pg, v_w_pg),
               w_pe=(w_pe, m_w_pe, v_w_pe), w_pool=(w_pool, m_w_pool, v_w_pool))
    names = list(big)
    cols = {n: big[n][0].shape[-1] for n in names}
    chip = (2 * lax.axis_index("x") + lax.axis_index("y")).astype(jnp.int32).reshape(1)
    core = lax.axis_index("c")
    south = core == 0

    gathers = []
    token = jnp.zeros((8, LANES), F32)
    narrow = {n: big[n][0].astype(BF16) for n in names if n != "w_in"}
    narrow["w_in"] = _to_bf16(w_in.reshape(-1, cols["w_in"]), name="w_in_to_bf16").reshape(w_in.shape)
    for l in range(depth):
        halves = [narrow[n][l].reshape(2, -1, cols[n]) for n in names]
        lands = [lax.empty((3,) + a.shape[1:], a.dtype) for a in halves]
        send, recv, halves, lands, token = _ici_start(halves, lands, token, gather=True, name=f"gather_start_{l}")
        gathers.append((send, recv, halves, lands))

    def layer_weights(l, after, anchor):
        send, recv, halves, lands = gathers[l]
        halves, lands = _ici_wait(send, recv, halves, lands, after, gather=True, name=f"gather_wait_{l}")
        others = _sibling_send(lands, name="gather_pass_on")
        full = {}
        for n, own, land, other in zip(names, halves, lands, others):
            low = jnp.where(south, land, other)
            high = jnp.where(south, other, land)
            rel = [jnp.concatenate([low[pi], high[pi]], axis=0) for pi in range(3)]
            by_rel = jnp.stack([own.reshape(-1, cols[n]), rel[1], rel[0], rel[2]])
            full[n] = [lax.dynamic_index_in_dim(by_rel, j ^ chip[0], 0, keepdims=False) for j in range(4)]
        win = full["w_in"]
        wm = jnp.concatenate(_column_parts(win, 0, F_OFF) + _column_parts(win, F_OFF + N_HEADS, IN_COLS), axis=1)
        wf = jnp.pad(jnp.concatenate(_column_parts(win, F_OFF, F_OFF + N_HEADS), axis=1), ((0, 0), (0, LANES - N_HEADS)))
        return dict(
            wm=wm, wf=wf, wo=jnp.concatenate(full["w_out"], axis=0), wpg=jnp.concatenate(full["w_pg"], axis=0),
            wpe=jnp.concatenate(full["w_pe"], axis=1),
            wpool=jnp.stack(full["w_pool"]).reshape(4, N_GROUPS, GROUP_DIM // 4, GROUP_DIM).transpose(1, 0, 2, 3).reshape(
                N_GROUPS, GROUP_DIM, GROUP_DIM),
            gpre=norm_pre[l][None] + anchor, gpost=norm_post[l][None], ps=pool_scale[l][None], bf=b_f[l][:, None])

    h = x.reshape(t, D_MODEL)
    saved, layers = [], []
    m_w_in, v_w_in = m_w_in + token[0, 0], v_w_in + token[0, 0]
    big["w_in"] = (w_in, m_w_in, v_w_in)
    p_layers = [p[l].reshape(t, PLE_DIM) + token[0, 0] for l in range(depth)]
    after = [token, m_w_in.reshape(-1, cols["w_in"]), v_w_in.reshape(-1, cols["w_in"])] + p_layers
    for l in range(depth):
        layers.append(layer_weights(l, after, token[0, 0]))
        h, sv = _layer_fwd(h, p_layers[l], layers[l], nb)
        saved.append(sv)
        after = [h]
    dh, loss_blk = _loss_grad(h, loss_target.reshape(t, D_MODEL))
    loss = lax.psum(loss_blk[0, 0], ("x", "y", "c"))
    gsum = (jnp.arange(D_MODEL)[:, None] // HEAD_DIM == jnp.arange(LANES)[None, :]).astype(BF16)

    def pieces(g, name):
        if name == "w_in":
            by_chip = jnp.stack([jnp.concatenate(_column_parts(g, j * cols[name], (j + 1) * cols[name]), axis=1)
                                 for j in range(4)])
        elif name == "w_pe":
            by_chip = jnp.stack([g[:, j * cols[name]:(j + 1) * cols[name]] for j in range(4)])
        elif name == "w_pool":
            by_chip = g.reshape(N_GROUPS, 4, GROUP_DIM // 4, GROUP_DIM).transpose(1, 0, 2, 3)
        else:
            by_chip = g
        return by_chip.reshape(4, 2, -1, cols[name]).transpose(1, 0, 2, 3).astype(BF16)

    def scatter_start(l, g):
        mine, sent = [], []
        for n in names:
            halves = pieces(g[n], n)
            mine.append(lax.dynamic_index_in_dim(halves, core, 0, keepdims=False))
            sent.append(lax.dynamic_index_in_dim(halves, 1 - core, 0, keepdims=False))
        theirs = _sibling_send(sent, name="presum_exchange")
        sums = [_pair_sum(a, b, name="presum_" + n) for n, a, b in zip(names, mine, theirs)]
        lands = [lax.empty(a.shape, a.dtype) for a in sums]
        return _ici_start(sums, lands, token, gather=False, name=f"scatter_start_{l}")

    def scatter_finish(l, state, after):
        send, recv, sums, lands, _ = state
        sums, lands = _ici_wait(send, recv, sums, lands, after, gather=False, name=f"scatter_wait_{l}")
        mine = [_chip_sum(a, r, chip, name="chip_sum_" + n) for n, a, r in zip(names, sums, lands)]
        theirs = _sibling_send(mine, name="halves_exchange")
        return {n: jnp.concatenate([jnp.where(south, a, b), jnp.where(south, b, a)], axis=0)
                for n, a, b in zip(names, mine, theirs)}

    grads = [None] * depth
    big_out = {n: None for n in names}

    def update(l, state, after):
        reduced = scatter_finish(l, state, after)
        for n in names:
            w, m, v = big[n]
            big_out[n] = _adamw(w.reshape(-1, cols[n]), m.reshape(-1, cols[n]), v.reshape(-1, cols[n]), reduced[n], l,
                                big_out[n], name="adamw_" + n)

    pending = None
    for l in reversed(range(depth)):
        wts = layers[l]
        if pending is not None:
            wts = dict(wts, gpost=wts["gpost"] + pending[1][4][0, 0])
        dh, grads[l] = _layer_bwd(dh, saved[l], wts, gsum, nb)
        state = scatter_start(l, grads[l])
        if pending is not None:
            update(pending[0], pending[1], [state[4]])
        pending = (l, state)
    grad_x = dh.reshape(nb, s, D_MODEL)

    small = dict(norm_pre=(norm_pre, m_norm_pre, v_norm_pre), norm_post=(norm_post, m_norm_post, v_norm_post),
                 pool_scale=(pool_scale, m_pool_scale, v_pool_scale), b_f=(b_f, m_b_f, v_b_f))

    def pack(get):
        rows = []
        for n in small:
            a = get(n)
            rows.append(jnp.pad(a, ((0, 0), (0, D_MODEL - a.shape[1]))))
        return jnp.concatenate(rows, axis=0)

    parts = _all_gather_small(pack(lambda n: jnp.stack([grads[l][n] for l in range(depth)])), name="gather_small")
    small_packed = _adamw_small(pack(lambda n: small[n][0]), pack(lambda n: small[n][1]), pack(lambda n: small[n][2]),
                                parts, name="adamw_small")
    small_out = {}
    for i, n in enumerate(small):
        width = small[n][0].shape[1]
        small_out[n] = [o[depth * i:depth * (i + 1), :width] for o in small_packed]

    update(pending[0], pending[1], [outs[0] for outs in big_out.values() if outs is not None] + [small_packed[0]])
    big_out = {n: [o.reshape(big[n][0].shape) for o in outs] for n, outs in big_out.items()}

    order =["norm_pre", "norm_post", "w_in", "b_f", "w_pool", "pool_scale", "w_out", "w_pg", "w_pe"]
    result = [loss, grad_x]
    for kind in range(4):
        for n in order:
            result.append(big_out[n][kind] if n in big_out else small_out[n][kind])
    return tuple(result)
```

```python
import functools

import jax
import jax.numpy as jnp
from jax import lax
from jax.experimental import pallas as pl
from jax.experimental.pallas import tpu as pltpu

F32 = jnp.float32
BF16 = jnp.bfloat16

D_MODEL = 1024
N_HEADS = 16
HEAD_DIM = 64
PAIR = 2 * HEAD_DIM
N_PAIRS = N_HEADS // 2
N_GROUPS = 4
GROUP_DIM = 256
PLE_DIM = 256
MIX = 2 * D_MODEL
N_SEG = 6
SEG_STEP = 3
F_OFF = 4 * D_MODEL
IN_COLS = N_SEG * D_MODEL + N_HEADS
LANES = 128
EPS = 1e-6
SCALE = 0.125
NEG = -1e30

ADAM_LR = 0.001
ADAM_B1 = 0.9
ADAM_B2 = 0.999
ADAM_EPS = 1e-08
ADAM_WD = 0.01
ADAM_STEP = 10

VMEM_LIMIT = 56 * 1024 * 1024
MESH = pl.DeviceIdType.MESH
ANY = pl.BlockSpec(memory_space=pl.ANY)
HBM_SPEC = pl.BlockSpec(memory_space=pltpu.HBM)
SEM_SPEC = pl.BlockSpec(memory_space=pltpu.SEMAPHORE)
SIDE_EFFECT = pltpu.SideEffectType.DATAFLOW_SIDE_EFFECTING

NT = (((1,), (1,)), ((), ()))
TN = (((0,), (0,)), ((), ()))
NN = (((1,), (0,)), ((), ()))


def _pcall(body, *, name, out_shape, grid=(), in_specs=None, out_specs=None, scratch_shapes=(), dims=None):
    kw = {}
    if in_specs is not None:
        kw["in_specs"] = in_specs
    if out_specs is not None:
        kw["out_specs"] = out_specs
    return pl.pallas_call(
        body, name=name, out_shape=out_shape, grid=grid, scratch_shapes=list(scratch_shapes),
        compiler_params=pltpu.CompilerParams(dimension_semantics=dims, vmem_limit_bytes=VMEM_LIMIT), **kw)


def _dot(a, b, dn=NN):
    return lax.dot_general(a, b, dn, preferred_element_type=F32)


def _sigmoid(x):
    return 1.0 / (1.0 + jnp.exp(-x))


def _matmul(a, b, *, ta=False, name, tm=512, tn=1024, tk=2048):
    if ta:
        kdim, m = a.shape
    else:
        m, kdim = a.shape
    n = b.shape[1]
    tm, tn, tk = min(tm, m), min(tn, n), min(tk, kdim)
    nk = kdim // tk

    def body(a_ref, b_ref, o_ref, acc_ref):
        k = pl.program_id(2)

        @pl.when(k == 0)
        def _():
            acc_ref[...] = jnp.zeros_like(acc_ref)

        acc_ref[...] += _dot(a_ref[...].astype(BF16), b_ref[...].astype(BF16), TN if ta else NN)

        @pl.when(k == nk - 1)
        def _():
            o_ref[...] = acc_ref[...].astype(BF16)

    a_spec = pl.BlockSpec((tk, tm), lambda i, j, k: (k, i)) if ta else pl.BlockSpec((tm, tk), lambda i, j, k: (i, k))
    return _pcall(
        body, name=name, out_shape=jax.ShapeDtypeStruct((m, n), BF16), grid=(m // tm, n // tn, nk),
        in_specs=[a_spec, pl.BlockSpec((tk, tn), lambda i, j, k: (k, j))],
        out_specs=pl.BlockSpec((tm, tn), lambda i, j, k: (i, j)),
        scratch_shapes=[pltpu.VMEM((tm, tn), F32)], dims=("parallel", "parallel", "arbitrary"))(a, b)


def _norm_in_proj(h, g, wm, wf, *, tm=512):
    t = h.shape[0]

    def body(h_ref, g_ref, wm_ref, wf_ref, q_ref, k_ref, v_ref, za_ref, u_ref, zp_ref, fl_ref, hn_ref):
        j = pl.program_id(1)

        @pl.when(j == 0)
        def _():
            x = h_ref[...]
            r = lax.rsqrt(jnp.mean(x * x, axis=-1, keepdims=True) + EPS)
            hn = (x * r * g_ref[...]).astype(BF16)
            hn_ref[...] = hn
            fl_ref[...] = _dot(hn, wf_ref[...])

        y = _dot(hn_ref[...], wm_ref[...])
        outs = (q_ref, k_ref, v_ref, za_ref, u_ref, zp_ref)
        for s in range(N_SEG // SEG_STEP):
            @pl.when(j == s)
            def _(s=s):
                for i, ref in enumerate(outs[SEG_STEP * s:SEG_STEP * (s + 1)]):
                    ref[...] = y[:, i * D_MODEL:(i + 1) * D_MODEL].astype(ref.dtype)

    row = lambda i, j: (i, 0)
    blk = pl.BlockSpec((tm, D_MODEL), row)
    sds = lambda dt: jax.ShapeDtypeStruct((t, D_MODEL), dt)
    return _pcall(
        body, name="norm_in_proj", grid=(t // tm, N_SEG // SEG_STEP),
        in_specs=[blk, pl.BlockSpec((1, D_MODEL), lambda i, j: (0, 0)),
                  pl.BlockSpec((D_MODEL, SEG_STEP * D_MODEL), lambda i, j: (0, j)),
                  pl.BlockSpec((D_MODEL, LANES), lambda i, j: (0, 0))],
        out_specs=[blk] * 6 + [pl.BlockSpec((tm, LANES), row), blk],
        out_shape=[sds(BF16)] * 3 + [sds(F32)] * 3 + [jax.ShapeDtypeStruct((t, LANES), F32), sds(BF16)],
        dims=("parallel", "arbitrary"))(h, g, wm, wf)


def _gate_cumsum(flt, bias):
    b, hh, s = flt.shape

    def body(fl_ref, b_ref, c_ref):
        x = fl_ref[...] + b_ref[...]
        acc = jnp.minimum(x, 0.0) - jnp.log(1.0 + jnp.exp(-jnp.abs(x)))
        idx = lax.broadcasted_iota(jnp.int32, x.shape, 1)
        sh = 1
        while sh < s:
            acc = acc + jnp.where(idx >= sh, pltpu.roll(acc, sh, 1), 0.0)
            sh *= 2
        c_ref[...] = acc

    return _pcall(
        body, name="gate_cumsum", grid=(b,),
        in_specs=[pl.BlockSpec((None, hh, s), lambda i: (i, 0, 0)), pl.BlockSpec((hh, 1), lambda i: (0, 0))],
        out_specs=pl.BlockSpec((None, hh, s), lambda i: (i, 0, 0)),
        out_shape=jax.ShapeDtypeStruct((b, hh, s), F32), dims=("parallel",))(flt, bias)


def _gate_cumsum_bwd(dcol, drow, flt, bias):
    b, hh, s = flt.shape

    def body(d_ref, dr_ref, fl_ref, b_ref, dfl_ref, db_ref):
        @pl.when(pl.program_id(0) == 0)
        def _():
            db_ref[...] = jnp.zeros_like(db_ref)

        x = fl_ref[...] + b_ref[...]
        acc = dr_ref[...] - d_ref[...]
        idx = lax.broadcasted_iota(jnp.int32, x.shape, 1)
        sh = 1
        while sh < s:
            acc = acc + jnp.where(idx + sh < s, pltpu.roll(acc, s - sh, 1), 0.0)
            sh *= 2
        e = jnp.exp(-jnp.abs(x))
        sig_neg = jnp.where(x >= 0.0, e, 1.0) / (1.0 + e)
        dfl = acc * sig_neg
        dfl_ref[...] = dfl
        db_ref[...] += jnp.sum(dfl, axis=1, keepdims=True)

    return _pcall(
        body, name="gate_cumsum_bwd", grid=(b,),
        in_specs=[pl.BlockSpec((None, hh, s), lambda i: (i, 0, 0))] * 3 + [pl.BlockSpec((hh, 1), lambda i: (0, 0))],
        out_specs=[pl.BlockSpec((None, hh, s), lambda i: (i, 0, 0)), pl.BlockSpec((hh, 1), lambda i: (0, 0))],
        out_shape=[jax.ShapeDtypeStruct((b, hh, s), F32), jax.ShapeDtypeStruct((hh, 1), F32)],
        dims=("arbitrary",))(dcol, drow, flt, bias)


def _attn_fwd(q, k, v, c4, *, nb, tq=512, rc=32, diag=256):
    t = q.shape[0]
    s = t // nb
    tq = min(tq, s)
    nq = s // tq
    tk = tq
    diag = min(diag, tq)

    def body(q_ref, k_ref, v_ref, c_ref, o_ref, lse_ref, qh_scr, s_scr, p_scr, m_scr, acc_scr):
        qi = pl.program_id(2)
        lo = lax.broadcasted_iota(jnp.int32, (tq, PAIR), 1) < HEAD_DIM
        q2 = q_ref[...] * SCALE
        zero = jnp.zeros_like(q2)
        qh_scr[0] = jnp.where(lo, q2, zero)
        qh_scr[1] = jnp.where(lo, zero, q2)
        m_scr[...] = jnp.full(m_scr.shape, NEG, F32)
        acc_scr[...] = jnp.zeros(acc_scr.shape, F32)
        whole = [(0, tq, tk, False)]
        diagonal = [(r0, r0 + diag, r0 + diag, True) for r0 in range(0, tq, diag)]

        def block(kj, bands):
            off = pl.multiple_of(kj * tk, tk)
            k2 = k_ref[pl.ds(off, tk), :]
            v2 = v_ref[pl.ds(off, tk), :]
            one = jnp.ones_like(v2)
            va = (jnp.where(lo, v2, one), jnp.where(lo, one, v2))
            crow = c_ref[:, pl.ds(off, tk)]
            for hd in range(2):
                for r0, r1, nc, _ in bands:
                    s_scr[hd, r0:r1, :nc] = _dot(qh_scr[hd, r0:r1, :], k2[:nc], NT)
            for hd in range(2):
                for r0, r1, nc, masked in bands:
                    row = lax.broadcasted_iota(jnp.int32, (rc, nc), 0)
                    col = lax.broadcasted_iota(jnp.int32, (rc, nc), 1)
                    for r in range(r0, r1, rc):
                        sc = s_scr[hd, r:r + rc, :nc] - crow[hd:hd + 1, :nc]
                        if masked:
                            sc = jnp.where(row + r >= col, sc, NEG)
                        m_old = m_scr[hd, r:r + rc, :]
                        m_new = jnp.maximum(m_old, jnp.max(sc, axis=1, keepdims=True))
                        for cb in range(0, nc, LANES):
                            p_scr[hd, r:r + rc, cb:cb + LANES] = jnp.exp(sc[:, cb:cb + LANES] - m_new).astype(BF16)
                        m_scr[hd, r:r + rc, :] = m_new
                        acc_scr[hd, r:r + rc, :] = acc_scr[hd, r:r + rc, :] * jnp.exp(m_old - m_new)
                    acc_scr[hd, r0:r1, :] += _dot(p_scr[hd, r0:r1, :nc], va[hd][:nc])

        def below(kj, carry):
            block(kj, whole)
            return carry

        lax.fori_loop(0, qi, below, 0)
        block(qi, diagonal)
        a0, a1 = acc_scr[0], acc_scr[1]
        den = jnp.where(lo, pltpu.roll(a0, HEAD_DIM, 1), pltpu.roll(a1, HEAD_DIM, 1))
        o_ref[...] = jnp.where(lo, a0, a1) / den
        lse_t = (jnp.where(lo, m_scr[0], m_scr[1]) + jnp.log(den)).T
        lse_ref[0:1, :] = lse_t[0:1, :]
        lse_ref[1:2, :] = lse_t[HEAD_DIM:HEAD_DIM + 1, :]

    qspec = pl.BlockSpec((tq, PAIR), lambda b, hp, i: (b * nq + i, hp))
    kvspec = pl.BlockSpec((s, PAIR), lambda b, hp, i: (b, hp))
    return _pcall(
        body, name="attn_fwd", grid=(nb, N_PAIRS, nq),
        in_specs=[qspec, kvspec, kvspec, pl.BlockSpec((None, None, 2, s), lambda b, hp, i: (b, hp, 0, 0))],
        out_specs=[qspec, pl.BlockSpec((None, None, 2, tq), lambda b, hp, i: (b, hp, 0, i))],
        out_shape=[jax.ShapeDtypeStruct((t, D_MODEL), F32), jax.ShapeDtypeStruct((nb, N_PAIRS, 2, s), F32)],
        scratch_shapes=[pltpu.VMEM((2, tq, PAIR), BF16), pltpu.VMEM((2, tq, tk), F32), pltpu.VMEM((2, tq, tk), BF16),
                        pltpu.VMEM((2, tq, LANES), F32), pltpu.VMEM((2, tq, PAIR), F32)],
        dims=("parallel", "parallel", "arbitrary"))(q, k, v, c4)


def _attn_bwd(q, k, v, do, c4, lse4, delta4, *, nb, tk=512, rc=16, diag=256):
    t = q.shape[0]
    s = t // nb
    tk = min(tk, s)
    nk = s // tk
    tq = tk
    diag = min(diag, tk)

    def body(q_ref, do_ref, k_ref, v_ref, c_ref, lse_ref, dl_ref, dq_ref, dk_ref, dv_ref, dc_ref, dr_ref,
             kz_scr, vz_scr, ko_scr, crep_scr, st_scr, dp_scr, pt_scr, ds_scr, dk_scr, dv_scr, dq_scr, dr_scr):
        kj = pl.program_id(2)

        @pl.when(kj == 0)
        def _():
            dq_scr[...] = jnp.zeros_like(dq_scr)
            dr_scr[...] = jnp.zeros_like(dr_scr)

        lo = lax.broadcasted_iota(jnp.int32, (tk, PAIR), 1) < HEAD_DIM
        k2 = k_ref[...]
        v2 = v_ref[...]
        zero = jnp.zeros_like(k2)
        one = jnp.ones_like(k2)
        kz_scr[0] = jnp.where(lo, k2, zero)
        kz_scr[1] = jnp.where(lo, zero, k2)
        vz_scr[0] = jnp.where(lo, v2, zero)
        vz_scr[1] = jnp.where(lo, zero, v2)
        ko_scr[0] = jnp.where(lo, k2, one)
        ko_scr[1] = jnp.where(lo, one, k2)
        for hd in range(2):
            crep_scr[hd] = jnp.broadcast_to(c_ref[hd:hd + 1, :], (LANES, tk)).T
        dk_scr[...] = jnp.zeros(dk_scr.shape, F32)
        dv_scr[...] = jnp.zeros(dv_scr.shape, F32)
        row = lax.broadcasted_iota(jnp.int32, (rc, LANES), 0)
        col = lax.broadcasted_iota(jnp.int32, (rc, LANES), 1)

        whole = [(0, tk, 0, False)]
        diagonal = [(r0, r0 + diag, r0, True) for r0 in range(0, tk, diag)]

        def block(qi, bands):
            off = pl.multiple_of(qi * tq, tq)
            q2 = q_ref[pl.ds(off, tq), :] * SCALE
            do2 = do_ref[pl.ds(off, tq), :]
            lse = lse_ref[:, pl.ds(off, tq)]
            dl = dl_ref[:, pl.ds(off, tq)]
            qo = (jnp.where(lo, q2, jnp.ones_like(q2)), jnp.where(lo, jnp.ones_like(q2), q2))
            for hd in range(2):
                for r0, r1, c0, _ in bands:
                    st_scr[hd, r0:r1, c0:] = _dot(kz_scr[hd, r0:r1, :], q2[c0:], NT)
                    dp_scr[hd, r0:r1, c0:] = _dot(vz_scr[hd, r0:r1, :], do2[c0:], NT)
            for r0, r1, c0, masked in bands:
                dqs = []
                for hd in range(2):
                    for r in range(r0, r1, rc):
                        c_rep = crep_scr[hd, r:r + rc, :]
                        for cb in range(c0, tq, LANES):
                            pt = jnp.exp(st_scr[hd, r:r + rc, cb:cb + LANES] - c_rep - lse[hd:hd + 1, cb:cb + LANES])
                            if masked and cb < c0 + diag:
                                pt = jnp.where(col + cb >= row + r, pt, 0.0)
                            dst = pt * (dp_scr[hd, r:r + rc, cb:cb + LANES] - dl[hd:hd + 1, cb:cb + LANES])
                            pt_scr[hd, r:r + rc, cb:cb + LANES] = pt.astype(BF16)
                            ds_scr[hd, r:r + rc, cb:cb + LANES] = dst.astype(BF16)
                    dv_scr[hd, r0:r1, :] += _dot(pt_scr[hd, r0:r1, c0:], do2[c0:])
                    dk_scr[hd, r0:r1, :] += _dot(ds_scr[hd, r0:r1, c0:], qo[hd][c0:])
                    dqs.append(_dot(ds_scr[hd, r0:r1, c0:], ko_scr[hd, r0:r1, :], TN))
                dq_scr[pl.ds(off + c0, tq - c0), :] += jnp.where(lo[c0:], dqs[0], dqs[1]) * SCALE
                dr_scr[pl.ds(off + c0, tq - c0), :] += jnp.where(lo[c0:], dqs[1], dqs[0])

        block(kj, diagonal)

        def below(qi, carry):
            block(qi, whole)
            return carry

        lax.fori_loop(kj + 1, nk, below, 0)
        dk_ref[...] = jnp.where(lo, dk_scr[0], dk_scr[1]).astype(BF16)
        dv_ref[...] = jnp.where(lo, dv_scr[0], dv_scr[1]).astype(BF16)
        dc_t = jnp.where(lo, dk_scr[1], dk_scr[0]).T
        dc_ref[0:1, :] = dc_t[HEAD_DIM:HEAD_DIM + 1, :]
        dc_ref[1:2, :] = dc_t[0:1, :]

        @pl.when(kj == nk - 1)
        def _():
            dq_ref[...] = dq_scr[...].astype(BF16)
            for r in range(0, s, tq):
                dr_t = dr_scr[r:r + tq, :].T
                dr_ref[0:1, r:r + tq] = dr_t[HEAD_DIM:HEAD_DIM + 1, :]
                dr_ref[1:2, r:r + tq] = dr_t[0:1, :]

    full = pl.BlockSpec((s, PAIR), lambda b, hp, j: (b, hp))
    kblk = pl.BlockSpec((tk, PAIR), lambda b, hp, j: (b * nk + j, hp))
    rows = pl.BlockSpec((None, None, 2, s), lambda b, hp, j: (b, hp, 0, 0))
    krows = pl.BlockSpec((None, None, 2, tk), lambda b, hp, j: (b, hp, 0, j))
    sds = lambda dt: jax.ShapeDtypeStruct((t, D_MODEL), dt)
    rows_sds = jax.ShapeDtypeStruct((nb, N_PAIRS, 2, s), F32)
    pair_bf = pltpu.VMEM((2, tk, PAIR), BF16)
    pair_f = pltpu.VMEM((2, tk, PAIR), F32)
    return _pcall(
        body, name="attn_bwd", grid=(nb, N_PAIRS, nk),
        in_specs=[full, full, kblk, kblk, krows, rows, rows],
        out_specs=[full, kblk, kblk, krows, rows],
        out_shape=[sds(BF16), sds(BF16), sds(BF16), rows_sds, rows_sds],
        scratch_shapes=[pair_bf, pair_bf, pair_bf, pair_f, pltpu.VMEM((2, tk, tq), F32), pltpu.VMEM((2, tk, tq), F32),
                        pltpu.VMEM((2, tk, tq), BF16), pltpu.VMEM((2, tk, tq), BF16), pair_f, pair_f,
                        pltpu.VMEM((s, PAIR), F32), pltpu.VMEM((s, PAIR), F32)],
        dims=("parallel", "parallel", "arbitrary"))(q, do, k, v, c4, lse4, delta4)


def _window_sum(x, g, s, *, ahead):
    row = lax.broadcasted_iota(jnp.int32, x.shape, 0)
    for step in range(N_GROUPS):
        sh = 1 << step
        if ahead:
            moved = jnp.where(row + sh < s, pltpu.roll(x, s - sh, 0), 0.0)
        else:
            moved = jnp.where(row >= sh, pltpu.roll(x, sh, 0), 0.0)
        x = jnp.where(step <= g, x + moved, x)
    return x


def _inv_count(g, s):
    pos = lax.broadcasted_iota(jnp.int32, (s, 1), 0) + 1
    return 1.0 / jnp.minimum(pos, 2 << g).astype(F32)


def _pool_fwd(u, zp, wpool, ps, *, nb):
    t = u.shape[0]
    s = t // nb

    def body(u_ref, zp_ref, w_ref, ps_ref, out_ref, pooled_ref, mixed_ref):
        g = pl.program_id(1)
        x = u_ref[...]
        pooled = (_window_sum(x, g, s, ahead=False) * _inv_count(g, s) - x).astype(BF16)
        mixed = _dot(pooled, w_ref[...])
        z = zp_ref[...]
        pooled_ref[...] = pooled
        mixed_ref[...] = mixed
        out_ref[...] = (mixed * ps_ref[...] * (z * _sigmoid(z))).astype(BF16)

    blk = pl.BlockSpec((s, GROUP_DIM), lambda b, g: (b, g))
    sds = lambda dt: jax.ShapeDtypeStruct((t, D_MODEL), dt)
    return _pcall(
        body, name="pool_fwd", grid=(nb, N_GROUPS),
        in_specs=[blk, blk, pl.BlockSpec((None, GROUP_DIM, GROUP_DIM), lambda b, g: (g, 0, 0)),
                  pl.BlockSpec((1, GROUP_DIM), lambda b, g: (0, g))],
        out_specs=[blk, blk, blk], out_shape=[sds(BF16), sds(BF16), sds(F32)],
        dims=("parallel", "parallel"))(u, zp, wpool, ps)


def _pool_bwd(dpool, mixed, zp, wpool, ps, *, nb):
    t = dpool.shape[0]
    s = t // nb

    def body(dp_ref, mx_ref, zp_ref, w_ref, ps_ref, dmx_ref, du_ref, dzp_ref, dps_ref):
        g = pl.program_id(0)

        @pl.when(pl.program_id(1) == 0)
        def _():
            dps_ref[...] = jnp.zeros_like(dps_ref)

        dp = dp_ref[...]
        mixed = mx_ref[...]
        z = zp_ref[...]
        scale = ps_ref[...]
        sg = _sigmoid(z)
        silu = z * sg
        dmixed = (dp * scale * silu).astype(BF16)
        dmx_ref[...] = dmixed
        dps_ref[...] += jnp.sum(dp * mixed * silu, axis=0, keepdims=True)
        dzp_ref[...] = (dp * mixed * scale * (sg * (1.0 + z * (1.0 - sg)))).astype(BF16)
        dpooled = _dot(dmixed, w_ref[...], NT)
        du = _window_sum(dpooled * _inv_count(g, s), g, s, ahead=True) - dpooled
        du_ref[...] = du.astype(BF16)

    blk = pl.BlockSpec((s, GROUP_DIM), lambda g, b: (b, g))
    sds = lambda dt: jax.ShapeDtypeStruct((t, D_MODEL), dt)
    return _pcall(
        body, name="pool_bwd", grid=(N_GROUPS, nb),
        in_specs=[blk, blk, blk, pl.BlockSpec((None, GROUP_DIM, GROUP_DIM), lambda g, b: (g, 0, 0)),
                  pl.BlockSpec((1, GROUP_DIM), lambda g, b: (0, g))],
        out_specs=[blk, blk, blk, pl.BlockSpec((1, GROUP_DIM), lambda g, b: (0, g))],
        out_shape=[sds(BF16), sds(BF16), sds(BF16), jax.ShapeDtypeStruct((1, D_MODEL), F32)],
        dims=("parallel", "arbitrary"))(dpool, mixed, zp, wpool, ps)


def _pool_wgrad(pooled, dmixed, *, tk=1024):
    t = pooled.shape[0]
    tk = min(tk, t)
    nk = t // tk

    def body(a_ref, b_ref, o_ref):
        @pl.when(pl.program_id(1) == 0)
        def _():
            o_ref[...] = jnp.zeros_like(o_ref)

        o_ref[...] += _dot(a_ref[...], b_ref[...], TN)

    blk = pl.BlockSpec((tk, GROUP_DIM), lambda g, k: (k, g))
    return _pcall(
        body, name="pool_wgrad", grid=(N_GROUPS, nk), in_specs=[blk, blk],
        out_specs=pl.BlockSpec((None, GROUP_DIM, GROUP_DIM), lambda g, k: (g, 0, 0)),
        out_shape=jax.ShapeDtypeStruct((N_GROUPS, GROUP_DIM, GROUP_DIM), F32),
        dims=("parallel", "arbitrary"))(pooled, dmixed)


def _out_block(h, o, za, pool, p, wo, wpg, wpe, gpost, *, tm=256):
    t = h.shape[0]

    def body(h_ref, o_ref, za_ref, pool_ref, p_ref, wo_ref, wpg_ref, wpe_ref, g_ref,
             h2_ref, mix_ref, h1_ref, gate_ref, pe_ref, cat_ref):
        z = za_ref[...]
        a = (o_ref[...] * (z * _sigmoid(z))).astype(BF16)
        pool_v = pool_ref[...]
        cat_ref[:, :D_MODEL] = a
        cat_ref[:, D_MODEL:] = pool_v
        mix = _dot(a, wo_ref[:D_MODEL, :]) + _dot(pool_v, wo_ref[D_MODEL:, :])
        r = lax.rsqrt(jnp.mean(mix * mix, axis=-1, keepdims=True) + EPS)
        h1 = h_ref[...] + mix * r * g_ref[...]
        gate = _sigmoid(_dot(h1.astype(BF16), wpg_ref[...]))
        pe = _dot(p_ref[...].astype(BF16), wpe_ref[...])
        mix_ref[...] = mix
        h1_ref[...] = h1.astype(BF16)
        gate_ref[...] = gate
        pe_ref[...] = pe
        h2_ref[...] = h1 + gate * pe

    row = lambda i: (i, 0)
    const = lambda i: (0, 0)
    blk = pl.BlockSpec((tm, D_MODEL), row)
    sds = jax.ShapeDtypeStruct((t, D_MODEL), F32)
    return _pcall(
        body, name="out_block", grid=(t // tm,),
        in_specs=[blk, blk, blk, blk, pl.BlockSpec((tm, PLE_DIM), row),
                  pl.BlockSpec((MIX, D_MODEL), const), pl.BlockSpec((D_MODEL, D_MODEL), const),
                  pl.BlockSpec((PLE_DIM, D_MODEL), const), pl.BlockSpec((1, D_MODEL), const)],
        out_specs=[blk] * 5 + [pl.BlockSpec((tm, MIX), row)],
        out_shape=[sds, sds, jax.ShapeDtypeStruct((t, D_MODEL), BF16), sds, sds, jax.ShapeDtypeStruct((t, MIX), BF16)],
        dims=("parallel",))(h, o, za, pool, p, wo, wpg, wpe, gpost)


def _loss_grad(y, target, *, tm=512):
    t = y.shape[0]

    def body(y_ref, t_ref, dy_ref, loss_ref):
        @pl.when(pl.program_id(0) == 0)
        def _():
            loss_ref[...] = jnp.zeros_like(loss_ref)

        err = y_ref[...] - t_ref[...]
        dy_ref[...] = err * (1.0 / D_MODEL)
        part = jnp.sum(jnp.sum(err * err, axis=1, keepdims=True), axis=0, keepdims=True)
        loss_ref[...] += part * (0.5 / D_MODEL)

    blk = pl.BlockSpec((tm, D_MODEL), lambda i: (i, 0))
    return _pcall(
        body, name="loss_grad", grid=(t // tm,), in_specs=[blk, blk],
        out_specs=[blk, pl.BlockSpec((8, LANES), lambda i: (0, 0))],
        out_shape=[jax.ShapeDtypeStruct((t, D_MODEL), F32), jax.ShapeDtypeStruct((8, LANES), F32)],
        dims=("arbitrary",))(y, target)


def _out_block_bwd(dh2, gate, pe, mix, o, za, wpg, wo, gpost, gsum, *, tm=256):
    t = dh2.shape[0]

    def body(dh2_ref, gate_ref, pe_ref, mix_ref, o_ref, za_ref, wpg_ref, wo_ref, g_ref, gs_ref,
             dh1_ref, dgp_ref, dpe_ref, dmix_ref, do_ref, dza_ref, dpool_ref, delta_ref, dg_ref):
        @pl.when(pl.program_id(0) == 0)
        def _():
            dg_ref[...] = jnp.zeros_like(dg_ref)

        dh2 = dh2_ref[...]
        gate = gate_ref[...]
        dpe_ref[...] = (dh2 * gate).astype(BF16)
        dgp = (dh2 * pe_ref[...] * gate * (1.0 - gate)).astype(BF16)
        dgp_ref[...] = dgp
        dh1 = dh2 + _dot(dgp, wpg_ref[...], NT)
        dh1_ref[...] = dh1
        mix = mix_ref[...]
        r = lax.rsqrt(jnp.mean(mix * mix, axis=-1, keepdims=True) + EPS)
        dg_ref[...] += jnp.sum(dh1 * mix * r, axis=0, keepdims=True)
        a = dh1 * g_ref[...]
        dmix = (r * a - mix * (r * r * r) * jnp.mean(a * mix, axis=-1, keepdims=True)).astype(BF16)
        dmix_ref[...] = dmix
        dattn = _dot(dmix, wo_ref[:D_MODEL, :], NT)
        dpool_ref[...] = _dot(dmix, wo_ref[D_MODEL:, :], NT)
        z = za_ref[...]
        sg = _sigmoid(z)
        o = o_ref[...]
        do = (dattn * (z * sg)).astype(BF16)
        do_ref[...] = do
        dza_ref[...] = (dattn * o * (sg * (1.0 + z * (1.0 - sg)))).astype(BF16)
        prod = do.astype(F32) * o
        hi = prod.astype(BF16)
        rest = (prod - hi.astype(F32)).astype(BF16)
        delta_ref[...] = _dot(hi, gs_ref[...]) + _dot(rest, gs_ref[...])

    row = lambda i: (i, 0)
    const = lambda i: (0, 0)
    blk = pl.BlockSpec((tm, D_MODEL), row)
    sds = lambda dt: jax.ShapeDtypeStruct((t, D_MODEL), dt)
    return _pcall(
        body, name="out_block_bwd", grid=(t // tm,),
        in_specs=[blk] * 6 + [pl.BlockSpec((D_MODEL, D_MODEL), const), pl.BlockSpec((MIX, D_MODEL), const),
                              pl.BlockSpec((1, D_MODEL), const), pl.BlockSpec((D_MODEL, LANES), const)],
        out_specs=[blk] * 7 + [pl.BlockSpec((tm, LANES), row), pl.BlockSpec((1, D_MODEL), const)],
        out_shape=[sds(F32)] + [sds(BF16)] * 5 + [sds(F32), jax.ShapeDtypeStruct((t, LANES), F32),
                                                 jax.ShapeDtypeStruct((1, D_MODEL), F32)],
        dims=("arbitrary",))(dh2, gate, pe, mix, o, za, wpg, wo, gpost, gsum)


def _in_proj_bwd(dsegs, dfl, wm, wf, h, dh1, gpre, *, tm=512):
    t = h.shape[0]

    def body(*refs):
        seg_refs = refs[:N_SEG]
        dfl_ref, wm_ref, wf_ref, h_ref, dh1_ref, g_ref, dh_ref, dg_ref, acc_ref = refs[N_SEG:]
        i = pl.program_id(0)
        j = pl.program_id(1)

        @pl.when(jnp.logical_and(i == 0, j == 0))
        def _():
            dg_ref[...] = jnp.zeros_like(dg_ref)

        @pl.when(j == 0)
        def _():
            acc_ref[...] = _dot(dfl_ref[...].astype(BF16), wf_ref[...], NT)

        for s in range(N_SEG // SEG_STEP):
            @pl.when(j == s)
            def _(s=s):
                part = None
                for i, ref in enumerate(seg_refs[SEG_STEP * s:SEG_STEP * (s + 1)]):
                    term = _dot(ref[...].astype(BF16), wm_ref[:, i * D_MODEL:(i + 1) * D_MODEL], NT)
                    part = term if part is None else part + term
                acc_ref[...] += part

        @pl.when(j == N_SEG // SEG_STEP - 1)
        def _():
            dhn = acc_ref[...]
            x = h_ref[...]
            r = lax.rsqrt(jnp.mean(x * x, axis=-1, keepdims=True) + EPS)
            dg_ref[...] += jnp.sum(dhn * x * r, axis=0, keepdims=True)
            a = dhn * g_ref[...]
            dh_ref[...] = dh1_ref[...] + r * a - x * (r * r * r) * jnp.mean(a * x, axis=-1, keepdims=True)

    row = lambda i, j: (i, 0)
    const = lambda i, j: (0, 0)
    blk = pl.BlockSpec((tm, D_MODEL), row)
    return _pcall(
        body, name="in_proj_bwd", grid=(t // tm, N_SEG // SEG_STEP),
        in_specs=[blk] * N_SEG + [pl.BlockSpec((tm, LANES), row), pl.BlockSpec((D_MODEL, SEG_STEP * D_MODEL), lambda i, j: (0, j)),
                                  pl.BlockSpec((D_MODEL, LANES), const), blk, blk, pl.BlockSpec((1, D_MODEL), const)],
        out_specs=[blk, pl.BlockSpec((1, D_MODEL), const)],
        out_shape=[jax.ShapeDtypeStruct((t, D_MODEL), F32), jax.ShapeDtypeStruct((1, D_MODEL), F32)],
        scratch_shapes=[pltpu.VMEM((tm, D_MODEL), F32)],
        dims=("arbitrary", "arbitrary"))(*dsegs, dfl, wm, wf, h, dh1, gpre)


def _chip_peers(x, y):
    return [(1 - x, y), (x, 1 - y), (1 - x, 1 - y)]


def _ici_views(src, land, gather, x, y, c, pi, px, py):
    if gather:
        return src.at[c], land.at[pi], land.at[pi]
    return src.at[2 * px + py], land.at[2 * x + y], land.at[2 * px + py]


def _ici_start(srcs, lands, after, *, gather, name):
    n = len(srcs)

    def body(*refs):
        src, land = refs[:n], refs[n:2 * n]
        send, recv = refs[2 * n + 1], refs[2 * n + 2]
        token = refs[-1]
        x, y, c = lax.axis_index("x"), lax.axis_index("y"), lax.axis_index("c")
        for kk in range(n):
            for pi, (px, py) in enumerate(_chip_peers(x, y)):
                mine, there, _ = _ici_views(src[kk], land[kk], gather, x, y, c, pi, px, py)
                pltpu.make_async_remote_copy(src_ref=mine, dst_ref=there, send_sem=send.at[3 * kk + pi],
                                             recv_sem=recv.at[3 * kk + pi], device_id=(px, py, c),
                                             device_id_type=MESH).start()
        token[...] = jnp.zeros_like(token)

    hbm = lambda a: pltpu.with_memory_space_constraint(a, pltpu.HBM)
    outs = pl.pallas_call(
        body, name=name,
        out_shape=(pltpu.SemaphoreType.DMA((3 * n,)), pltpu.SemaphoreType.DMA((3 * n,)),
                   *[pltpu.HBM(a.shape, a.dtype) for a in srcs], *[pltpu.HBM(a.shape, a.dtype) for a in lands],
                   jax.ShapeDtypeStruct((8, LANES), F32)),
        in_specs=[HBM_SPEC] * (2 * n) + [ANY],
        out_specs=(SEM_SPEC, SEM_SPEC, *[HBM_SPEC] * (2 * n), pl.BlockSpec(memory_space=pltpu.VMEM)),
        input_output_aliases={i: 2 + i for i in range(2 * n)},
        compiler_params=pltpu.CompilerParams(has_side_effects=SIDE_EFFECT),
    )(*[hbm(a) for a in srcs], *[hbm(a) for a in lands], after)
    return outs[0], outs[1], list(outs[2:2 + n]), list(outs[2 + n:2 + 2 * n]), outs[-1]


def _ici_wait(send, recv, srcs, lands, after, *, gather, name):
    n = len(srcs)

    def body(*refs):
        src, land = refs[:n], refs[n:2 * n]
        send_ref, recv_ref = refs[2 * n], refs[2 * n + 1]
        x, y, c = lax.axis_index("x"), lax.axis_index("y"), lax.axis_index("c")
        for kk in range(n):
            for pi, (px, py) in enumerate(_chip_peers(x, y)):
                mine, _, here = _ici_views(src[kk], land[kk], gather, x, y, c, pi, px, py)
                cp = pltpu.make_async_remote_copy(src_ref=mine, dst_ref=here, send_sem=send_ref.at[3 * kk + pi],
                                                  recv_sem=recv_ref.at[3 * kk + pi], device_id=(px, py, c),
                                                  device_id_type=MESH)
                cp.wait_send()
                cp.wait_recv()

    outs = pl.pallas_call(
        body, name=name,
        out_shape=[pltpu.HBM(a.shape, a.dtype) for a in srcs] + [pltpu.HBM(a.shape, a.dtype) for a in lands],
        in_specs=[HBM_SPEC] * (2 * n) + [SEM_SPEC, SEM_SPEC] + [ANY] * len(after), out_specs=[HBM_SPEC] * (2 * n),
        input_output_aliases={i: i for i in range(2 * n)},
        compiler_params=pltpu.CompilerParams(has_side_effects=SIDE_EFFECT),
    )(*srcs, *lands, send, recv, *after)
    return list(outs[:n]), list(outs[n:])


def _sibling_send(arrays, *, name):
    n = len(arrays)

    def body(*refs):
        ins, outs = refs[:n], refs[n:2 * n]
        send, recv = refs[2 * n:]
        sib = (lax.axis_index("x"), lax.axis_index("y"), 1 - lax.axis_index("c"))
        copies = [pltpu.make_async_remote_copy(src_ref=ins[kk], dst_ref=outs[kk], send_sem=send.at[kk],
                                               recv_sem=recv.at[kk], device_id=sib, device_id_type=MESH)
                  for kk in range(n)]
        for cp in copies:
            cp.start()
        for cp in copies:
            cp.wait_recv()
        for cp in copies:
            cp.wait_send()

    return _pcall(
        body, name=name, in_specs=[ANY] * n, out_specs=[ANY] * n,
        out_shape=[jax.ShapeDtypeStruct(a.shape, a.dtype) for a in arrays],
        scratch_shapes=[pltpu.SemaphoreType.DMA((n,)), pltpu.SemaphoreType.DMA((n,))])(*arrays)


def _all_gather_small(a, *, name):
    def body(a_ref, o_ref, send, recv, loc):
        x, y, c = lax.axis_index("x"), lax.axis_index("y"), lax.axis_index("c")
        me = 4 * x + 2 * y + c

        def peer(rel):
            fx, fy, fc = (rel >> 2) & 1, (rel >> 1) & 1, rel & 1
            px = (1 - x) if fx else x
            py = (1 - y) if fy else y
            pc = (1 - c) if fc else c
            return px, py, pc

        def remote(rel, slot):
            return pltpu.make_async_remote_copy(
                src_ref=a_ref, dst_ref=o_ref.at[slot], send_sem=send.at[rel - 1], recv_sem=recv.at[rel - 1],
                device_id=peer(rel), device_id_type=MESH)

        mine = pltpu.make_async_copy(a_ref, o_ref.at[me], loc)
        mine.start()
        sends = [remote(rel, me) for rel in range(1, 8)]
        for cp in sends:
            cp.start()
        for rel in range(1, 8):
            px, py, pc = peer(rel)
            remote(rel, 4 * px + 2 * py + pc).wait_recv()
        for cp in sends:
            cp.wait_send()
        mine.wait()

    return _pcall(
        body, name=name, in_specs=[ANY], out_specs=ANY,
        out_shape=jax.ShapeDtypeStruct((8,) + a.shape, a.dtype),
        scratch_shapes=[pltpu.SemaphoreType.DMA((7,)), pltpu.SemaphoreType.DMA((7,)), pltpu.SemaphoreType.DMA])(a)


def _to_bf16(a, *, name, tr=256):
    rows, cols = a.shape

    def body(a_ref, o_ref):
        o_ref[...] = a_ref[...].astype(BF16)

    blk = pl.BlockSpec((tr, cols), lambda i: (i, 0))
    return _pcall(body, name=name, grid=(rows // tr,), in_specs=[blk], out_specs=blk,
                  out_shape=jax.ShapeDtypeStruct(a.shape, BF16), dims=("parallel",))(a)


def _pair_sum(a, b, *, name, tr=256):
    _, rows, cols = a.shape
    tr = min(tr, rows)

    def body(a_ref, b_ref, o_ref):
        o_ref[...] = (a_ref[...].astype(F32) + b_ref[...].astype(F32)).astype(BF16)

    blk = pl.BlockSpec((None, tr, cols), lambda j, i: (j, i, 0))
    return _pcall(body, name=name, grid=(4, rows // tr), in_specs=[blk, blk], out_specs=blk,
                  out_shape=jax.ShapeDtypeStruct(a.shape, BF16), dims=("parallel", "parallel"))(a, b)


def _chip_sum(own, recv, chip, *, name, tr=256):
    _, rows, cols = own.shape
    tr = min(tr, rows)

    def body(chip_ref, own_ref, r1_ref, r2_ref, r3_ref, o_ref):
        acc = own_ref[...].astype(F32)
        for ref in (r1_ref, r2_ref, r3_ref):
            acc = acc + ref[...].astype(F32)
        o_ref[...] = acc

    def slot(step):
        return pl.BlockSpec((None, tr, cols), lambda i, chip_ref: ((chip_ref[0] + step) % 4, i, 0))

    return pl.pallas_call(
        body, name=name, out_shape=jax.ShapeDtypeStruct((rows, cols), F32),
        grid_spec=pltpu.PrefetchScalarGridSpec(
            num_scalar_prefetch=1, grid=(rows // tr,), in_specs=[slot(0), slot(1), slot(2), slot(3)],
            out_specs=pl.BlockSpec((tr, cols), lambda i, chip_ref: (i, 0))),
        compiler_params=pltpu.CompilerParams(dimension_semantics=("parallel",), vmem_limit_bytes=VMEM_LIMIT),
    )(chip, own, recv, recv, recv)


def _adamw_math(w, g, m, v):
    m = ADAM_B1 * m + (1.0 - ADAM_B1) * g
    v = ADAM_B2 * v + (1.0 - ADAM_B2) * (g * g)
    m_hat = m / (1.0 - ADAM_B1 ** ADAM_STEP)
    v_hat = v / (1.0 - ADAM_B2 ** ADAM_STEP)
    delta = -ADAM_LR * (m_hat / (jnp.sqrt(v_hat) + ADAM_EPS) + ADAM_WD * w)
    return delta, m, v


def _adamw(w, m, v, g, layer, prev, *, name, tr=128):
    rows, cols = g.shape
    tr = min(tr, rows)
    nblk = rows // tr

    def body(w_ref, m_ref, v_ref, gin_ref, *refs):
        g_ref, d_ref, mo_ref, vo_ref = refs[-4:]
        grad = gin_ref[...]
        delta, m_new, v_new = _adamw_math(w_ref[...], grad, m_ref[...], v_ref[...])
        g_ref[...] = grad
        d_ref[...] = delta
        mo_ref[...] = m_new
        vo_ref[...] = v_new

    blk = pl.BlockSpec((tr, cols), lambda i: (layer * nblk + i, 0))
    carried = [] if prev is None else list(prev)
    return pl.pallas_call(
        body, name=name, grid=(nblk,), out_shape=[jax.ShapeDtypeStruct(w.shape, F32)] * 4,
        in_specs=[blk] * 3 + [pl.BlockSpec((tr, cols), lambda i: (i, 0))] + [ANY] * len(carried), out_specs=[blk] * 4,
        input_output_aliases={4 + i: i for i in range(len(carried))},
        compiler_params=pltpu.CompilerParams(dimension_semantics=("parallel",), vmem_limit_bytes=VMEM_LIMIT),
    )(w, m, v, g, *carried)


def _adamw_small(w, m, v, parts, *, name):
    def body(w_ref, m_ref, v_ref, p_ref, g_ref, d_ref, mo_ref, vo_ref):
        g = p_ref[0]
        for dev in range(1, 8):
            g = g + p_ref[dev]
        delta, m_new, v_new = _adamw_math(w_ref[...], g, m_ref[...], v_ref[...])
        g_ref[...] = g
        d_ref[...] = delta
        mo_ref[...] = m_new
        vo_ref[...] = v_new

    return _pcall(body, name=name, out_shape=[jax.ShapeDtypeStruct(w.shape, F32)] * 4)(w, m, v, parts)


def _column_parts(segments, lo, hi):
    parts, start = [], 0
    for seg in segments:
        a, b = max(lo, start), min(hi, start + seg.shape[1])
        if a < b:
            parts.append(seg[:, a - start:b - start])
        start += seg.shape[1]
    return parts


def _rows_from_heads(a, nb, s):
    return a.reshape(nb, s, N_HEADS).transpose(0, 2, 1).reshape(nb, N_PAIRS, 2, s)


def _layer_fwd(h, p_l, wts, nb):
    t = h.shape[0]
    s = t // nb
    q, k, v, za, u, zp, fl, hn = _norm_in_proj(h, wts["gpre"], wts["wm"], wts["wf"])
    flt = fl[:, :N_HEADS].reshape(nb, s, N_HEADS).transpose(0, 2, 1)
    c = _gate_cumsum(flt, wts["bf"])
    o, lse = _attn_fwd(q, k, v, c.reshape(nb, N_PAIRS, 2, s), nb=nb)
    pool, pooled, mixed = _pool_fwd(u, zp, wts["wpool"], wts["ps"], nb=nb)
    h2, mix, h1, gate, pe, cat = _out_block(h, o, za, pool, p_l, wts["wo"], wts["wpg"], wts["wpe"], wts["gpost"])
    saved = dict(h=h, hn=hn, q=q, k=k, v=v, za=za, zp=zp, flt=flt, c=c, o=o, lse=lse, pooled=pooled, mixed=mixed,
                 mix=mix, h1=h1, gate=gate, pe=pe, cat=cat, p=p_l)
    return h2, saved


def _layer_bwd(dh2, sv, wts, gsum, nb):
    t = dh2.shape[0]
    s = t // nb
    dh1, dgp, dpe, dmix, do, dza, dpool, delta, dgpost = _out_block_bwd(
        dh2, sv["gate"], sv["pe"], sv["mix"], sv["o"], sv["za"], wts["wpg"], wts["wo"], wts["gpost"], gsum)
    g_wpe = _matmul(sv["p"], dpe, ta=True, name="wgrad_pe")
    g_wpg = _matmul(sv["h1"], dgp, ta=True, name="wgrad_pg")
    g_wo = _matmul(sv["cat"], dmix, ta=True, name="wgrad_out")

    delta4 = _rows_from_heads(delta[:, :N_HEADS], nb, s)
    dq, dk, dv, dcol, drow = _attn_bwd(sv["q"], sv["k"], sv["v"], do, sv["c"].reshape(nb, N_PAIRS, 2, s), sv["lse"],
                                       delta4, nb=nb)
    dflt, dbf = _gate_cumsum_bwd(dcol.reshape(nb, N_HEADS, s), drow.reshape(nb, N_HEADS, s), sv["flt"], wts["bf"])
    dfl = jnp.pad(dflt.transpose(0, 2, 1).reshape(t, N_HEADS), ((0, 0), (0, LANES - N_HEADS)))

    dmixed, du, dzp, dps = _pool_bwd(dpool, sv["mixed"], sv["zp"], wts["wpool"], wts["ps"], nb=nb)
    g_wpool = _pool_wgrad(sv["pooled"], dmixed)

    dsegs = (dq, dk, dv, dza, du, dzp)
    dh, dgpre = _in_proj_bwd(dsegs, dfl, wts["wm"], wts["wf"], sv["h"], dh1, wts["gpre"])
    g_segs = [_matmul(sv["hn"], dx, ta=True, name="wgrad_in") for dx in dsegs]
    g_wf = _matmul(sv["hn"], dfl, ta=True, name="wgrad_in_f")
    g_win = g_segs[:4] + [g_wf[:, :N_HEADS]] + g_segs[4:]
    grads = dict(w_in=g_win, w_out=g_wo, w_pg=g_wpg, w_pe=g_wpe, w_pool=g_wpool,
                 norm_pre=dgpre[0], norm_post=dgpost[0], pool_scale=dps[0], b_f=dbf[:, 0])
    return dh, grads


def kernel(x, p, norm_pre, norm_post, w_in, b_f, w_pool, pool_scale, w_out, w_pg, w_pe, loss_target, m_norm_pre, m_norm_post, m_w_in, m_b_f, m_w_pool, m_pool_scale, m_w_out, m_w_pg, m_w_pe, v_norm_pre, v_norm_post, v_w_in, v_b_f, v_w_pool, v_pool_scale, v_w_out, v_w_pg, v_w_pe):
    nb, s, _ = x.shape
    t = nb * s
    depth = w_in.shape[0]
    big = dict(w_in=(w_in, m_w_in, v_w_in), w_out=(w_out, m_w_out, v_w_out), w_pg=(w_pg, m_w_pg, v_w_pg),
               w_pe=(w_pe, m_w_pe, v_w_pe), w_pool=(w_pool, m_w_pool, v_w_pool))
    names = list(big)
    cols = {n: big[n][0].shape[-1] for n in names}
    chip = (2 * lax.axis_index("x") + lax.axis_index("y")).astype(jnp.int32).reshape(1)
    core = lax.axis_index("c")
    south = core == 0

    gathers = []
    token = jnp.zeros((8, LANES), F32)
    narrow = {n: big[n][0].astype(BF16) for n in names if n != "w_in"}
    narrow["w_in"] = _to_bf16(w_in.reshape(-1, cols["w_in"]), name="w_in_to_bf16").reshape(w_in.shape)
    for l in range(depth):
        halves = [narrow[n][l].reshape(2, -1, cols[n]) for n in names]
        lands = [lax.empty((3,) + a.shape[1:], a.dtype) for a in halves]
        send, recv, halves, lands, token = _ici_start(halves, lands, token, gather=True, name=f"gather_start_{l}")
        gathers.append((send, recv, halves, lands))

    def layer_weights(l, after, anchor):
        send, recv, halves, lands = gathers[l]
        halves, lands = _ici_wait(send, recv, halves, lands, after, gather=True, name=f"gather_wait_{l}")
        others = _sibling_send(lands, name="gather_pass_on")
        full = {}
        for n, own, land, other in zip(names, halves, lands, others):
            low = jnp.where(south, land, other)
            high = jnp.where(south, other, land)
            rel = [jnp.concatenate([low[pi], high[pi]], axis=0) for pi in range(3)]
            by_rel = jnp.stack([own.reshape(-1, cols[n]), rel[1], rel[0], rel[2]])
            full[n] = [lax.dynamic_index_in_dim(by_rel, j ^ chip[0], 0, keepdims=False) for j in range(4)]
        win = full["w_in"]
        wm = jnp.concatenate(_column_parts(win, 0, F_OFF) + _column_parts(win, F_OFF + N_HEADS, IN_COLS), axis=1)
        wf = jnp.pad(jnp.concatenate(_column_parts(win, F_OFF, F_OFF + N_HEADS), axis=1), ((0, 0), (0, LANES - N_HEADS)))
        return dict(
            wm=wm, wf=wf, wo=jnp.concatenate(full["w_out"], axis=0), wpg=jnp.concatenate(full["w_pg"], axis=0),
            wpe=jnp.concatenate(full["w_pe"], axis=1),
            wpool=jnp.stack(full["w_pool"]).reshape(4, N_GROUPS, GROUP_DIM // 4, GROUP_DIM).transpose(1, 0, 2, 3).reshape(
                N_GROUPS, GROUP_DIM, GROUP_DIM),
            gpre=norm_pre[l][None] + anchor, gpost=norm_post[l][None], ps=pool_scale[l][None], bf=b_f[l][:, None])

    h = x.reshape(t, D_MODEL)
    saved, layers = [], []
    m_w_in, v_w_in = m_w_in + token[0, 0], v_w_in + token[0, 0]
    big["w_in"] = (w_in, m_w_in, v_w_in)
    p_layers = [p[l].reshape(t, PLE_DIM) + token[0, 0] for l in range(depth)]
    after = [token, m_w_in.reshape(-1, cols["w_in"]), v_w_in.reshape(-1, cols["w_in"])] + p_layers
    for l in range(depth):
        layers.append(layer_weights(l, after, token[0, 0]))
        h, sv = _layer_fwd(h, p_layers[l], layers[l], nb)
        saved.append(sv)
        after = [h]
    dh, loss_blk = _loss_grad(h, loss_target.reshape(t, D_MODEL))
    loss = lax.psum(loss_blk[0, 0], ("x", "y", "c"))
    gsum = (jnp.arange(D_MODEL)[:, None] // HEAD_DIM == jnp.arange(LANES)[None, :]).astype(BF16)

    def pieces(g, name):
        if name == "w_in":
            by_chip = jnp.stack([jnp.concatenate(_column_parts(g, j * cols[name], (j + 1) * cols[name]), axis=1)
                                 for j in range(4)])
        elif name == "w_pe":
            by_chip = jnp.stack([g[:, j * cols[name]:(j + 1) * cols[name]] for j in range(4)])
        elif name == "w_pool":
            by_chip = g.reshape(N_GROUPS, 4, GROUP_DIM // 4, GROUP_DIM).transpose(1, 0, 2, 3)
        else:
            by_chip = g
        return by_chip.reshape(4, 2, -1, cols[name]).transpose(1, 0, 2, 3).astype(BF16)

    def scatter_start(l, g):
        mine, sent = [], []
        for n in names:
            halves = pieces(g[n], n)
            mine.append(lax.dynamic_index_in_dim(halves, core, 0, keepdims=False))
            sent.append(lax.dynamic_index_in_dim(halves, 1 - core, 0, keepdims=False))
        theirs = _sibling_send(sent, name="presum_exchange")
        sums = [_pair_sum(a, b, name="presum_" + n) for n, a, b in zip(names, mine, theirs)]
        lands = [lax.empty(a.shape, a.dtype) for a in sums]
        return _ici_start(sums, lands, token, gather=False, name=f"scatter_start_{l}")

    def scatter_finish(l, state, after):
        send, recv, sums, lands, _ = state
        sums, lands = _ici_wait(send, recv, sums, lands, after, gather=False, name=f"scatter_wait_{l}")
        mine = [_chip_sum(a, r, chip, name="chip_sum_" + n) for n, a, r in zip(names, sums, lands)]
        theirs = _sibling_send(mine, name="halves_exchange")
        return {n: jnp.concatenate([jnp.where(south, a, b), jnp.where(south, b, a)], axis=0)
                for n, a, b in zip(names, mine, theirs)}

    grads = [None] * depth
    big_out = {n: None for n in names}

    def update(l, state, after):
        reduced = scatter_finish(l, state, after)
        for n in names:
            w, m, v = big[n]
            big_out[n] = _adamw(w.reshape(-1, cols[n]), m.reshape(-1, cols[n]), v.reshape(-1, cols[n]), reduced[n], l,
                                big_out[n], name="adamw_" + n)

    pending = None
    for l in reversed(range(depth)):
        wts = layers[l]
        if pending is not None:
            wts = dict(wts, gpost=wts["gpost"] + pending[1][4][0, 0])
        dh, grads[l] = _layer_bwd(dh, saved[l], wts, gsum, nb)
        state = scatter_start(l, grads[l])
        if pending is not None:
            update(pending[0], pending[1], [state[4]])
        pending = (l, state)
    grad_x = dh.reshape(nb, s, D_MODEL)

    small = dict(norm_pre=(norm_pre, m_norm_pre, v_norm_pre), norm_post=(norm_post, m_norm_post, v_norm_post),
                 pool_scale=(pool_scale, m_pool_scale, v_pool_scale), b_f=(b_f, m_b_f, v_b_f))

    def pack(get):
        rows = []
        for n in small:
            a = get(n)
            rows.append(jnp.pad(a, ((0, 0), (0, D_MODEL - a.shape[1]))))
        return jnp.concatenate(rows, axis=0)

    parts = _all_gather_small(pack(lambda n: jnp.stack([grads[l][n] for l in range(depth)])), name="gather_small")
    small_packed = _adamw_small(pack(lambda n: small[n][0]), pack(lambda n: small[n][1]), pack(lambda n: small[n][2]),
                                parts, name="adamw_small")
    small_out = {}
    for i, n in enumerate(small):
        width = small[n][0].shape[1]
        small_out[n] = [o[depth * i:depth * (i + 1), :width] for o in small_packed]

    update(pending[0], pending[1], [outs[0] for outs in big_out.values() if outs is not None] + [small_packed[0]])
    big_out = {n: [o.reshape(big[n][0].shape) for o in outs] for n, outs in big_out.items()}

    order =["norm_pre", "norm_post", "w_in", "b_f", "w_pool", "pool_scale", "w_out", "w_pg", "w_pe"]
    result = [loss, grad_x]
    for kind in range(4):
        for n in order:
            result.append(big_out[n][kind] if n in big_out else small_out[n][kind])
    return tuple(result)
```
